```python
import jax, jax.numpy as jnp
from jax import lax
import numpy as np

D_MODEL = 1024
BATCH = 32
SEQ = 2048
DEPTH = 2
DEC_BATCH = 8
DEC_SEQ = 2048
PAST_LEN = 128

GRID_W = 64
HEAD_DIM = 64
D_MIX = D_MODEL
N_GROUPS = 4
W_G = D_MIX // N_GROUPS
H_G = W_G // HEAD_DIM
KV_ATTN = H_G // 2
RWKV_W_LORA = 64
RWKV_A_LORA = 64
RWKV_G_LORA = 128
RWKV_GN_EPS = 64e-5
W_IN_RWKV = 3 * W_G + RWKV_W_LORA + RWKV_A_LORA + RWKV_G_LORA
W_IN_ATTN = W_G + 2 * KV_ATTN * HEAD_DIM
W_IN_MLSTM = 4 * W_G + 4 * H_G
W_IN_RET = 4 * W_G
D_IN = W_IN_RWKV + W_IN_ATTN + W_IN_MLSTM + W_IN_RET
D_FF = 2816
N_EXPERTS = 8
TOP_K = 2
N_DENSE = (DEPTH + 1) // 2
N_MOE = DEPTH // 2
BLOCK_Q = 128
CHUNK = 128
ROPE_THETA = 10000.0
NORM_EPS = 1e-6
HEAD_NORM_EPS = 1e-5
NEG_INF = -1e30

kernel_name = 'hybrid_bidir_rwkv7_gqa_mlstm_retention_moe'


def rms_norm(x, g, eps=NORM_EPS):
    xf = x.astype(jnp.float32)
    y = xf * lax.rsqrt(jnp.mean(xf * xf, -1, keepdims=True) + eps)
    return (y * g.astype(jnp.float32)).astype(x.dtype)


def head_group_norm(y, g, eps=HEAD_NORM_EPS):
    mu = jnp.mean(y, -1, keepdims=True)
    var = jnp.mean(jnp.square(y - mu), -1, keepdims=True)
    yn = (y - mu) * lax.rsqrt(var + eps)
    return yn.reshape(*y.shape[:-2], -1) * g


def heads(z):
    return z.reshape(*z.shape[:-1], -1, HEAD_DIM)


def split_cols(u, widths):
    offs = [int(o) for o in np.cumsum(widths)[:-1]]
    return jnp.split(u, offs, axis=-1)


def shift_prev(u):
    return jnp.pad(u[:, :-1], ((0, 0), (1, 0), (0, 0)))


def shift_next(u):
    return jnp.pad(u[:, 1:], ((0, 0), (0, 1), (0, 0)))


def axial_rope_angles(T):
    rows = T // GRID_W
    t = jnp.arange(rows * GRID_W)
    row = (t // GRID_W).astype(jnp.float32)
    col = (t % GRID_W).astype(jnp.float32)
    nf = HEAD_DIM // 4
    inv = ROPE_THETA ** (-jnp.arange(nf, dtype=jnp.float32) / nf)
    return row[:, None] * inv, col[:, None] * inv


def apply_axial_rope(x, ang_row, ang_col):
    xf = x.astype(jnp.float32)
    half = HEAD_DIM // 2

    def rot(z, ang):
        c = jnp.cos(ang)[:, None, :]
        s = jnp.sin(ang)[:, None, :]
        z1, z2 = z[..., :half // 2], z[..., half // 2:]
        return jnp.concatenate([z1 * c - z2 * s, z2 * c + z1 * s], -1)

    return jnp.concatenate([rot(xf[..., :half], ang_row), rot(xf[..., half:], ang_col)], -1)


def rwkv7_mixer(u, mu, w0, w2, a0, a2, g2, kk_scale, ka, rk, ln):
    B, T, _ = u.shape
    u = u.astype(jnp.float32)
    u = u + mu * (0.5 * (shift_prev(u) + shift_next(u)) - u)
    r, k, v, xw, xa, xg = split_cols(u, (W_G, W_G, W_G, RWKV_W_LORA, RWKV_A_LORA, RWKV_G_LORA))
    g = jax.nn.sigmoid(xg) @ g2
    lw = jnp.tanh(xw)
    a_lr = xa @ a2
    kk = heads(k * kk_scale)
    kk = kk * lax.rsqrt(jnp.sum(kk * kk, -1, keepdims=True) + 1e-12)
    rh, vh = heads(r), heads(v)

    def step(S, inp):
        r_t, w_t, k_t, v_t, kk_t, b_t = inp
        s_kk = jnp.einsum('bhvk,bhk->bhv', S, kk_t)
        S = S * w_t[:, :, None, :] - s_kk[..., None] * b_t[:, :, None, :] + v_t[..., None] * k_t[:, :, None, :]
        return S, jnp.einsum('bhvk,bhk->bhv', S, r_t)

    def direction(d, reverse):
        logw = -jnp.exp(-jax.nn.softplus(-(w0[d] + lw @ w2[d])) - 0.5)
        a = jax.nn.sigmoid(a0[d] + a_lr)
        kd = heads(k * (1.0 + (a - 1.0) * ka))
        xs = tuple(jnp.swapaxes(z, 0, 1) for z in (rh, jnp.exp(heads(logw)), kd, vh, kk, kk * heads(a)))
        s0 = jnp.zeros((B, H_G, HEAD_DIM, HEAD_DIM), jnp.float32)
        _, ys = lax.scan(step, s0, xs, reverse=reverse)
        bonus = jnp.sum(rh * kd * rk.reshape(H_G, HEAD_DIM), -1, keepdims=True) * vh
        return jnp.swapaxes(ys, 0, 1), bonus

    y_f, b_f = direction(0, False)
    y_b, b_b = direction(1, True)
    y = head_group_norm(y_f + y_b, ln, RWKV_GN_EPS) + (b_f + b_b).reshape(B, T, W_G)
    return y * g


def attention_mixer(q, k, v, q_gain, k_gain, ang_row, ang_col):
    B, T = q.shape[:2]
    G = H_G // KV_ATTN
    nq = T // BLOCK_Q
    q = apply_axial_rope(rms_norm(q, q_gain), ang_row, ang_col) * HEAD_DIM ** -0.5
    k = apply_axial_rope(rms_norm(k, k_gain), ang_row, ang_col)
    qb = q.reshape(B, nq, BLOCK_Q, KV_ATTN, G, HEAD_DIM).transpose(1, 0, 3, 4, 2, 5)
    kt = k.transpose(0, 2, 1, 3)
    vt = v.astype(jnp.float32).transpose(0, 2, 1, 3)

    def block(qi):
        p = jax.nn.softmax(jnp.einsum('bkgqd,bksd->bkgqs', qi, kt), axis=-1)
        return jnp.einsum('bkgqs,bksd->bkgqd', p, vt)

    o = lax.map(block, qb)
    return o.transpose(1, 0, 4, 2, 3, 5).reshape(B, T, W_G)


def mlstm_chunkwise(q, k, v, li, lf):
    B, H, T, d = q.shape
    nc = T // CHUNK
    q, k, v = (z.reshape(B, H, nc, CHUNK, d) for z in (q, k, v))
    li = li.reshape(B, H, nc, CHUNK)
    lf = lf.reshape(B, H, nc, CHUNK)
    b = jnp.cumsum(lf, -1)
    g = b[..., -1]
    a = g[..., None] - b + li
    causal = jnp.tril(jnp.ones((CHUNK, CHUNK), bool))
    dlog = jnp.where(causal, b[..., :, None] - b[..., None, :] + li[..., None, :], NEG_INF)

    def step(carry, xs):
        C, n, m = carry
        g_c, a_c, k_c, v_c = xs
        m_new = jnp.maximum(g_c + m, jnp.max(a_c, -1))
        dec = jnp.exp(g_c + m - m_new)
        w = jnp.exp(a_c - m_new[..., None])
        C_new = dec[..., None, None] * C + jnp.einsum('bhl,bhlv,bhlk->bhvk', w, v_c, k_c)
        n_new = dec[..., None] * n + jnp.einsum('bhl,bhlk->bhk', w, k_c)
        return (C_new, n_new, m_new), (C, n, m)

    init = (jnp.zeros((B, H, d, d), jnp.float32), jnp.zeros((B, H, d), jnp.float32), jnp.zeros((B, H), jnp.float32))
    mv = lambda z: jnp.moveaxis(z, 2, 0)
    _, (c_prev, n_prev, m_prev) = lax.scan(step, init, (mv(g), mv(a), mv(k), mv(v)))
    inter_log = b + jnp.moveaxis(m_prev, 0, 2)[..., None]
    m_t = jnp.maximum(inter_log, jnp.max(dlog, -1))
    sc = jnp.einsum('bhctd,bhcsd->bhcts', q, k) * jnp.exp(dlog - m_t[..., None])
    w_inter = jnp.exp(inter_log - m_t)
    num = jnp.einsum('bhcts,bhcsd->bhctd', sc, v) + w_inter[..., None] * jnp.einsum('cbhvk,bhctk->bhctv', c_prev, q)
    den = jnp.sum(sc, -1) + w_inter * jnp.einsum('cbhk,bhctk->bhct', n_prev, q)
    h = num / jnp.maximum(jnp.abs(den), jnp.exp(-m_t))[..., None]
    return h.reshape(B, H, T, d)


def mlstm_mixer(u, conv_w, conv_b, i_bias, f_bias, ln):
    B, T, _ = u.shape
    u = u.astype(jnp.float32)
    qk, v, o, gates = split_cols(u, (2 * W_G, W_G, W_G, 4 * H_G))
    qk = jax.nn.silu(conv_w[0] * shift_prev(qk) + conv_w[1] * qk + conv_w[2] * shift_next(qk) + conv_b)
    q, k = qk[..., :W_G], qk[..., W_G:]
    bht = lambda z: heads(z).transpose(0, 2, 1, 3)
    qh, kh, vh = bht(q), bht(k * HEAD_DIM ** -0.5), bht(v)
    gates = gates.reshape(B, T, 2, 2, H_G)
    li = (gates[:, :, 0] + i_bias).transpose(0, 2, 3, 1)
    lf = jax.nn.log_sigmoid(gates[:, :, 1] + f_bias).transpose(0, 2, 3, 1)
    fl = lambda z: jnp.flip(z, 2)
    h_f = mlstm_chunkwise(qh, kh, vh, li[:, 0], lf[:, 0])
    h_b = fl(mlstm_chunkwise(fl(qh), fl(kh), fl(vh), fl(li[:, 1]), fl(lf[:, 1])))
    h = (h_f + h_b).transpose(0, 2, 1, 3)
    return head_group_norm(h, ln) * jax.nn.sigmoid(o)


def retention_chunkwise(q, k, v, log_gamma):
    B, H, T, d = q.shape
    nc = T // CHUNK
    q, k, v = (z.reshape(B, H, nc, CHUNK, d) for z in (q, k, v))
    j = jnp.arange(CHUNK, dtype=jnp.float32)
    lg = log_gamma[:, None]
    causal = j[:, None] >= j[None, :]
    decay = jnp.where(causal, jnp.exp((j[:, None] - j[None, :]) * lg[:, :, None]), 0.0)
    k_in = k * jnp.exp((CHUNK - 1 - j) * lg)[None, :, None, :, None]
    kv = jnp.einsum('bhcsk,bhcsv->cbhkv', k_in, v)
    g_chunk = jnp.exp(CHUNK * log_gamma)[None, :, None, None]

    def step(R, kv_c):
        return g_chunk * R + kv_c, R

    _, r_prev = lax.scan(step, jnp.zeros((B, H, d, d), jnp.float32), kv)
    inter = jnp.einsum('bhctk,cbhkv->bhctv', q, r_prev) * jnp.exp((j + 1.0) * lg)[None, :, None, :, None]
    intra = jnp.einsum('bhcts,bhcsv->bhctv', jnp.einsum('bhctk,bhcsk->bhcts', q, k) * decay[None, :, None], v)
    return (intra + inter).reshape(B, H, T, d)


def retention_mixer(u, ang_row, ang_col, ln):
    q, k, v, g = split_cols(u.astype(jnp.float32), (W_G, W_G, W_G, W_G))
    qh = apply_axial_rope(heads(q), ang_row, ang_col).transpose(0, 2, 1, 3)
    kh = (apply_axial_rope(heads(k), ang_row, ang_col) * HEAD_DIM ** -0.5).transpose(0, 2, 1, 3)
    vh = heads(v).transpose(0, 2, 1, 3)
    expo = -5.0 - jnp.arange(2 * H_G, dtype=jnp.float32) / 2.0
    log_gamma = jnp.log1p(-jnp.exp2(expo))
    fl = lambda z: jnp.flip(z, 2)
    o_f = retention_chunkwise(qh, kh, vh, log_gamma[0::2])
    o_b = fl(retention_chunkwise(fl(qh), fl(kh), fl(vh), log_gamma[1::2]))
    o = (o_f + o_b).transpose(0, 2, 1, 3)
    return head_group_norm(o, ln) * jax.nn.silu(g)


def swiglu(x, wg, wu, wd):
    return (jax.nn.silu(x @ wg) * (x @ wu)) @ wd


def moe_swiglu(x, router, wg, wu, wd):
    B, T, D = x.shape
    xt = x.reshape(B * T, D)
    probs = jax.nn.softmax((xt @ router).astype(jnp.float32), axis=-1)
    top_p, top_i = lax.top_k(probs, TOP_K)
    top_p = top_p / jnp.sum(top_p, -1, keepdims=True)
    gates = jnp.einsum('nk,nke->ne', top_p, jax.nn.one_hot(top_i, N_EXPERTS, dtype=jnp.float32))
    y = jnp.zeros_like(xt)
    for e in range(N_EXPERTS):
        y = y + gates[:, e:e + 1].astype(xt.dtype) * swiglu(xt, wg[e], wu[e], wd[e])
    return y.reshape(B, T, D)


def trunk(x, p):
    B, T, _ = x.shape
    ang_row, ang_col = axial_rope_angles(T)
    for l in range(DEPTH):
        h = rms_norm(x, p['norm_mix'][l])
        u = h @ p['w_in'][l]
        u_a, u_b, u_c, u_d = split_cols(u, (W_IN_RWKV, W_IN_ATTN, W_IN_MLSTM, W_IN_RET))
        o_a = rwkv7_mixer(u_a, p['rwkv_mu'][l], p['rwkv_w0'][l], p['rwkv_w2'][l], p['rwkv_a0'][l],
                          p['rwkv_a2'][l], p['rwkv_g2'][l], p['rwkv_kk'][l], p['rwkv_ka'][l],
                          p['rwkv_rk'][l], p['rwkv_ln'][l])
        q, k, v = split_cols(u_b, (W_G, KV_ATTN * HEAD_DIM, KV_ATTN * HEAD_DIM))
        o_b = attention_mixer(heads(q), heads(k), heads(v), p['attn_q_norm'][l], p['attn_k_norm'][l], ang_row, ang_col)
        o_c = mlstm_mixer(u_c, p['mlstm_conv_w'][l], p['mlstm_conv_b'][l], p['mlstm_i_bias'][l],
                          p['mlstm_f_bias'][l], p['mlstm_ln'][l])
        o_d = retention_mixer(u_d, ang_row, ang_col, p['ret_ln'][l])
        mix = jnp.concatenate([o_a, o_b, o_c, o_d], -1).astype(x.dtype)
        x = x + mix @ p['w_out'][l]
        h = rms_norm(x, p['norm_ffn'][l])
        if l % 2 == 0:
            j = l // 2
            f = swiglu(h, p['ffn_w_gate'][j], p['ffn_w_up'][j], p['ffn_w_down'][j])
        else:
            j = l // 2
            f = moe_swiglu(h, p['moe_router'][j], p['moe_w_gate'][j], p['moe_w_up'][j], p['moe_w_down'][j])
        x = x + f
    return rms_norm(x, p['norm_final'])


def setup_inputs(seed: int = 0) -> dict:
    key = jax.random.key(seed)
    ks = iter(jax.random.split(key, 40))
    f32 = jnp.float32
    nrm = lambda shape, scale: scale * jax.random.normal(next(ks), shape, f32)
    uni = lambda shape, lo, hi: jax.random.uniform(next(ks), shape, f32, lo, hi)
    gain = lambda shape: 1.0 + nrm(shape, 0.05)
    return {
        'x_prompt': nrm((BATCH, SEQ, D_MODEL), 1.0),
        'x_sample': nrm((DEC_BATCH, DEC_SEQ, D_MODEL), 1.0),
        'norm_mix': gain((DEPTH, D_MODEL)),
        'norm_ffn': gain((DEPTH, D_MODEL)),
        'norm_final': gain((D_MODEL,)),
        'w_in': nrm((DEPTH, D_MODEL, D_IN), D_MODEL ** -0.5),
        'w_out': nrm((DEPTH, D_MIX, D_MODEL), D_MIX ** -0.5),
        'rwkv_mu': uni((DEPTH, W_IN_RWKV), 0.0, 1.0),
        'rwkv_w0': uni((DEPTH, 2, W_G), -3.0, 1.0),
        'rwkv_w2': nrm((DEPTH, 2, RWKV_W_LORA, W_G), 0.5 * RWKV_W_LORA ** -0.5),
        'rwkv_a0': nrm((DEPTH, 2, W_G), 0.1),
        'rwkv_a2': nrm((DEPTH, RWKV_A_LORA, W_G), RWKV_A_LORA ** -0.5),
        'rwkv_g2': nrm((DEPTH, RWKV_G_LORA, W_G), RWKV_G_LORA ** -0.5),
        'rwkv_kk': 0.85 + nrm((DEPTH, W_G), 0.05),
        'rwkv_ka': 1.0 + nrm((DEPTH, W_G), 0.05),
        'rwkv_rk': nrm((DEPTH, W_G), 0.1),
        'rwkv_ln': gain((DEPTH, W_G)),
        'attn_q_norm': gain((DEPTH, HEAD_DIM)),
        'attn_k_norm': gain((DEPTH, HEAD_DIM)),
        'mlstm_conv_w': nrm((DEPTH, 3, 2 * W_G), 0.5),
        'mlstm_conv_b': nrm((DEPTH, 2 * W_G), 0.02),
        'mlstm_i_bias': nrm((DEPTH, 2, H_G), 0.1),
        'mlstm_f_bias': uni((DEPTH, 2, H_G), 3.0, 6.0),
        'mlstm_ln': gain((DEPTH, W_G)),
        'ret_ln': gain((DEPTH, W_G)),
        'ffn_w_gate': nrm((N_DENSE, D_MODEL, D_FF), D_MODEL ** -0.5),
        'ffn_w_up': nrm((N_DENSE, D_MODEL, D_FF), D_MODEL ** -0.5),
        'ffn_w_down': nrm((N_DENSE, D_FF, D_MODEL), D_FF ** -0.5),
        'moe_router': nrm((N_MOE, D_MODEL, N_EXPERTS), D_MODEL ** -0.5),
        'moe_w_gate': nrm((N_MOE, N_EXPERTS, D_MODEL, D_FF), D_MODEL ** -0.5),
        'moe_w_up': nrm((N_MOE, N_EXPERTS, D_MODEL, D_FF), D_MODEL ** -0.5),
        'moe_w_down': nrm((N_MOE, N_EXPERTS, D_FF, D_MODEL), D_FF ** -0.5),
    }


def reference(x_prompt, x_sample, norm_mix, norm_ffn, norm_final, w_in, w_out, rwkv_mu, rwkv_w0, rwkv_w2,
              rwkv_a0, rwkv_a2, rwkv_g2, rwkv_kk, rwkv_ka, rwkv_rk, rwkv_ln, attn_q_norm, attn_k_norm,
              mlstm_conv_w, mlstm_conv_b, mlstm_i_bias, mlstm_f_bias, mlstm_ln, ret_ln, ffn_w_gate, ffn_w_up,
              ffn_w_down, moe_router, moe_w_gate, moe_w_up, moe_w_down):
    p = dict(norm_mix=norm_mix, norm_ffn=norm_ffn, norm_final=norm_final, w_in=w_in, w_out=w_out,
             rwkv_mu=rwkv_mu, rwkv_w0=rwkv_w0, rwkv_w2=rwkv_w2, rwkv_a0=rwkv_a0, rwkv_a2=rwkv_a2,
             rwkv_g2=rwkv_g2, rwkv_kk=rwkv_kk, rwkv_ka=rwkv_ka, rwkv_rk=rwkv_rk, rwkv_ln=rwkv_ln,
             attn_q_norm=attn_q_norm, attn_k_norm=attn_k_norm, mlstm_conv_w=mlstm_conv_w,
             mlstm_conv_b=mlstm_conv_b, mlstm_i_bias=mlstm_i_bias, mlstm_f_bias=mlstm_f_bias,
             mlstm_ln=mlstm_ln, ret_ln=ret_ln, ffn_w_gate=ffn_w_gate, ffn_w_up=ffn_w_up,
             ffn_w_down=ffn_w_down, moe_router=moe_router, moe_w_gate=moe_w_gate, moe_w_up=moe_w_up,
             moe_w_down=moe_w_down)
    y_prompt = trunk(x_prompt, p)
    y_sample = trunk(x_sample, p)
    return (y_prompt, y_sample)
```

```python
import functools
import math

import numpy as np
import jax
import jax.numpy as jnp
from jax import lax
from jax.experimental import pallas as pl
from jax.experimental.pallas import tpu as pltpu

F32 = jnp.float32
BF16 = jnp.bfloat16

D_MODEL = 1024
HEAD_DIM = 64
W_G = 256
H_G = 4
KV_ATTN = 2
D_FF = 2816
N_EXPERTS = 8
NORM_EPS = 1e-6
HEAD_NORM_EPS = 1e-5
RWKV_GN_EPS = 64e-5
NEG_INF = -1e30
ROPE_THETA = 10000.0
GRID_W = 64

LANES = 128
ROW_ALIGN = 8
VMEM_LIMIT_BYTES = 56 * 1024 * 1024

U_RWKV = 0
U_MLSTM = 1024
U_RET = 2048
U_ATTN = 3072
U_GATE = 3584
U_COLS = 3712

RWKV_CHUNK = 64
RWKV_BLOCK = 256
MIX_CHUNK = 128


def _cparams(sem):
    return pltpu.CompilerParams(dimension_semantics=sem, vmem_limit_bytes=VMEM_LIMIT_BYTES)


def _bdot(a, b, dims):
    return lax.dot_general(a, b, (dims, ((), ())), preferred_element_type=F32)


def _mm(a, b):
    return _bdot(a.astype(BF16), b.astype(BF16), ((1,), (0,)))


def _mm_nt(a, b):
    return _bdot(a.astype(BF16), b.astype(BF16), ((1,), (1,)))


def _mm_tn(a, b):
    return _bdot(a.astype(BF16), b.astype(BF16), ((0,), (0,)))


def _split2(a):
    hi = a.astype(BF16)
    lo = (a - hi.astype(F32)).astype(BF16)
    return hi, lo


def _split3(a):
    hi = a.astype(BF16)
    r = a - hi.astype(F32)
    mid = r.astype(BF16)
    lo = (r - mid.astype(F32)).astype(BF16)
    return hi, mid, lo


def _mm_l2(a, b_exact):
    hi, lo = _split2(a)
    return _bdot(hi, b_exact, ((1,), (0,))) + _bdot(lo, b_exact, ((1,), (0,)))


def _mm_l3(a, b_exact):
    h, m, l = _split3(a)
    return _bdot(h, b_exact, ((1,), (0,))) + _bdot(m, b_exact, ((1,), (0,))) + _bdot(l, b_exact, ((1,), (0,)))


def _mm_r3(a_exact, b):
    h, m, l = _split3(b)
    return _bdot(a_exact, h, ((1,), (0,))) + _bdot(a_exact, m, ((1,), (0,))) + _bdot(a_exact, l, ((1,), (0,)))


def _mm_x3(a, b):
    ah, al = _split2(a)
    bh, bl = _split2(b)
    d = ((1,), (0,))
    return _bdot(ah, bh, d) + _bdot(ah, bl, d) + _bdot(al, bh, d)


def _iota2(shape, axis):
    return lax.broadcasted_iota(jnp.int32, shape, axis)


def _head_mean_matrix(width):
    r = _iota2((width, width), 0) // HEAD_DIM
    c = _iota2((width, width), 1) // HEAD_DIM
    return jnp.where(r == c, 1.0 / HEAD_DIM, 0.0).astype(BF16)


def _head_mean(z, bd):
    return _mm_l2(z, bd)


def _sigmoid(x):
    return 1.0 / (1.0 + jnp.exp(-x))


def _silu(x):
    return x * _sigmoid(x)


def _log_sigmoid(x):
    return jnp.minimum(x, 0.0) - jnp.log(1.0 + jnp.exp(-jnp.abs(x)))


def _rms_norm_rows(x, gain):
    ms = jnp.mean(x * x, axis=-1, keepdims=True)
    return x * lax.rsqrt(ms + NORM_EPS) * gain


def _rope_swap(z):
    w = z.shape[-1]
    lane = _iota2(z.shape, z.ndim - 1)
    fwd = pltpu.roll(z, w - 16, z.ndim - 1)
    bwd = pltpu.roll(z, 16, z.ndim - 1)
    return jnp.where((lane % 32) < 16, fwd, bwd)


def _rope(z, cos, sin):
    return z * cos + _rope_swap(z) * sin


def _shift_rows(x, prev_row, next_row):
    n = x.shape[0]
    row = _iota2(x.shape, 0)
    prev = jnp.where(row == 0, prev_row, pltpu.roll(x, 1, 0))
    nxt = jnp.where(row == n - 1, next_row, pltpu.roll(x, n - 1, 0))
    return prev, nxt


def _inproj_kernel(x_ref, g_ref, w_ref, o_ref):
    h = _rms_norm_rows(x_ref[...], g_ref[...])
    o_ref[...] = jnp.dot(h.astype(BF16), w_ref[...], preferred_element_type=F32)


def _in_proj(x2, gain, w_pad, tm):
    n = x2.shape[0]
    return pl.pallas_call(
        _inproj_kernel,
        out_shape=jax.ShapeDtypeStruct((n, U_COLS), F32),
        grid=(n // tm,),
        in_specs=[
            pl.BlockSpec((tm, D_MODEL), lambda i: (i, 0)),
            pl.BlockSpec((1, D_MODEL), lambda i: (0, 0)),
            pl.BlockSpec((D_MODEL, U_COLS), lambda i: (0, 0), pipeline_mode=pl.Buffered(1)),
        ],
        out_specs=pl.BlockSpec((tm, U_COLS), lambda i: (i, 0)),
        compiler_params=_cparams(("parallel",)),
        name="in_proj",
    )(x2, gain, w_pad)


def _attn_kernel(u_ref, cos_ref, sin_ref, qg_ref, kg_ref, o_ref, q_scr, k_scr, v_scr, *, tq):
    t = u_ref.shape[1]
    u = u_ref[0]
    q = u[:, :W_G]
    k = u[:, W_G:W_G + KV_ATTN * HEAD_DIM]
    v = u[:, W_G + KV_ATTN * HEAD_DIM:]
    cos = cos_ref[...]
    sin = sin_ref[...]
    bd_q = _head_mean_matrix(W_G)
    bd_k = _head_mean_matrix(KV_ATTN * HEAD_DIM)
    qn = q * lax.rsqrt(_head_mean(q * q, bd_q) + NORM_EPS) * qg_ref[...]
    kn = k * lax.rsqrt(_head_mean(k * k, bd_k) + NORM_EPS) * kg_ref[...]
    kw = KV_ATTN * HEAD_DIM
    q_scr[...] = (_rope(qn, cos, sin) * HEAD_DIM ** -0.5).astype(BF16)
    k_scr[...] = _rope(kn, cos[:, :kw], sin[:, :kw]).astype(BF16)
    v_scr[...] = v.astype(BF16)
    group = H_G // KV_ATTN

    def q_tile(i, carry):
        rows = pl.ds(pl.multiple_of(i * tq, tq), tq)
        for j in range(KV_ATTN):
            kj = k_scr[:, j * HEAD_DIM:(j + 1) * HEAD_DIM]
            vj = v_scr[:, j * HEAD_DIM:(j + 1) * HEAD_DIM]
            for g in range(group):
                h = j * group + g
                qh = q_scr[rows, h * HEAD_DIM:(h + 1) * HEAD_DIM]
                s = _bdot(qh, kj, ((1,), (1,)))
                m = jnp.max(s, axis=-1, keepdims=True)
                p = jnp.exp(s - m)
                l = jnp.sum(p, axis=-1, keepdims=True)
                o = jnp.dot(p.astype(BF16), vj, preferred_element_type=F32) / l
                o_ref[0, rows, h * HEAD_DIM:(h + 1) * HEAD_DIM] = o
        return carry

    lax.fori_loop(0, t // tq, q_tile, 0)


def _attention(u3, cos, sin, q_gain, k_gain):
    b, t, _ = u3.shape
    tq = min(256, t)
    col = U_ATTN // 512
    return pl.pallas_call(
        functools.partial(_attn_kernel, tq=tq),
        out_shape=jax.ShapeDtypeStruct((b, t, W_G), F32),
        grid=(b,),
        in_specs=[
            pl.BlockSpec((1, t, 512), lambda i: (i, 0, col)),
            pl.BlockSpec((t, W_G), lambda i: (0, 0)),
            pl.BlockSpec((t, W_G), lambda i: (0, 0)),
            pl.BlockSpec((1, W_G), lambda i: (0, 0)),
            pl.BlockSpec((1, KV_ATTN * HEAD_DIM), lambda i: (0, 0)),
        ],
        out_specs=pl.BlockSpec((1, t, W_G), lambda i: (i, 0, 0)),
        scratch_shapes=[
            pltpu.VMEM((t, W_G), BF16),
            pltpu.VMEM((t, KV_ATTN * HEAD_DIM), BF16),
            pltpu.VMEM((t, KV_ATTN * HEAD_DIM), BF16),
        ],
        compiler_params=_cparams(("parallel",)),
        name="attention",
    )(u3, cos, sin, q_gain, k_gain)


def _ret_log_gamma(direction):
    return [math.log1p(-2.0 ** (-5.0 - (2 * h + direction) / 2.0)) for h in range(H_G)]


def _ret_kernel(uf_ref, ub_ref, cf_ref, sf_ref, cb_ref, sb_ref, of_ref, ob_ref, rf_scr, rb_scr):
    i = pl.program_id(1)
    c = MIX_CHUNK

    @pl.when(i == 0)
    def _():
        rf_scr[...] = jnp.zeros_like(rf_scr)
        rb_scr[...] = jnp.zeros_like(rb_scr)

    tt = _iota2((c, c), 0)
    ss = _iota2((c, c), 1)
    diff = (tt - ss).astype(F32)
    jcol = _iota2((c, HEAD_DIM), 0).astype(F32)

    uf = uf_ref[0]
    qf = _rope(uf[:, :W_G], cf_ref[...], sf_ref[...])
    kf = _rope(uf[:, W_G:2 * W_G], cf_ref[...], sf_ref[...]) * HEAD_DIM ** -0.5
    vf = uf[:, 2 * W_G:3 * W_G]
    lg_f = _ret_log_gamma(0)
    lg_b = _ret_log_gamma(1)
    for h in range(H_G):
        sl = slice(h * HEAD_DIM, (h + 1) * HEAD_DIM)
        q, k, v = qf[:, sl], kf[:, sl], vf[:, sl]
        decay = (jnp.where(tt >= ss, jnp.exp(diff * lg_f[h]), 0.0)
                 + jnp.where(ss >= tt, jnp.exp(-diff * lg_b[h]), 0.0))
        intra = _mm(_mm_nt(q, k) * decay, v)
        r_prev = rf_scr[h]
        inter = _mm(q * jnp.exp((jcol + 1.0) * lg_f[h]), r_prev)
        of_ref[0, :, sl] = intra + inter
        k_in = k * jnp.exp((c - 1.0 - jcol) * lg_f[h])
        rf_scr[h] = math.exp(c * lg_f[h]) * r_prev + _mm_tn(k_in, v)

    ub = ub_ref[0]
    qb = _rope(ub[:, :W_G], cb_ref[...], sb_ref[...])
    kb = _rope(ub[:, W_G:2 * W_G], cb_ref[...], sb_ref[...]) * HEAD_DIM ** -0.5
    vb = ub[:, 2 * W_G:3 * W_G]
    for h in range(H_G):
        sl = slice(h * HEAD_DIM, (h + 1) * HEAD_DIM)
        q, k, v = qb[:, sl], kb[:, sl], vb[:, sl]
        r_prev = rb_scr[h]
        ob_ref[0, :, sl] = _mm(q * jnp.exp((c - jcol) * lg_b[h]), r_prev)
        k_in = k * jnp.exp(jcol * lg_b[h])
        rb_scr[h] = math.exp(c * lg_b[h]) * r_prev + _mm_tn(k_in, v)


def _retention(u3, cos, sin):
    b, t, _ = u3.shape
    c = MIX_CHUNK
    nblk = t // c
    col = U_RET // 1024
    tab = lambda rev: pl.BlockSpec((c, W_G), (lambda bi, i: (nblk - 1 - i, 0)) if rev else (lambda bi, i: (i, 0)))
    return pl.pallas_call(
        _ret_kernel,
        out_shape=(jax.ShapeDtypeStruct((b, t, W_G), F32), jax.ShapeDtypeStruct((b, t, W_G), F32)),
        grid=(b, nblk),
        in_specs=[
            pl.BlockSpec((1, c, 1024), lambda bi, i: (bi, i, col)),
            pl.BlockSpec((1, c, 1024), lambda bi, i: (bi, nblk - 1 - i, col)),
            tab(False), tab(False), tab(True), tab(True),
        ],
        out_specs=(
            pl.BlockSpec((1, c, W_G), lambda bi, i: (bi, i, 0)),
            pl.BlockSpec((1, c, W_G), lambda bi, i: (bi, nblk - 1 - i, 0)),
        ),
        scratch_shapes=[pltpu.VMEM((H_G, HEAD_DIM, HEAD_DIM), F32), pltpu.VMEM((H_G, HEAD_DIM, HEAD_DIM), F32)],
        compiler_params=_cparams(("parallel", "arbitrary")),
        name="retention",
    )(u3, u3, cos, sin, cos, sin)


def _mlstm_chain(u_ref, up_ref, un_ref, g_ref, cw_ref, cb_ref, gb_ref, o_ref, st_scr, m_scr, blk, nblk, direction):
    c = MIX_CHUNK
    reverse = direction == 1
    u = u_ref[0]
    qk = u[:, :2 * W_G]
    prev_row = jnp.where(blk == 0, 0.0, up_ref[0][ROW_ALIGN - 1:ROW_ALIGN, :])
    next_row = jnp.where(blk == nblk - 1, 0.0, un_ref[0][0:1, :])
    prev, nxt = _shift_rows(qk, prev_row, next_row)
    cw = cw_ref[...]
    qk = _silu(cw[0:1] * prev + cw[1:2] * qk + cw[2:3] * nxt + cb_ref[...])
    qa = qk[:, :W_G]
    ka = qk[:, W_G:] * HEAD_DIM ** -0.5
    va = u[:, 2 * W_G:3 * W_G]

    x = g_ref[0] + gb_ref[...]
    xt = x.T
    lf_c = _log_sigmoid(x)
    lf_r = _log_sigmoid(xt)
    tt = _iota2((c, c), 0)
    ss = _iota2((c, c), 1)
    lower = jnp.where(ss <= tt, 1.0, 0.0).astype(BF16)
    upper = jnp.where(tt <= ss, 1.0, 0.0).astype(BF16)
    if reverse:
        b_c = _mm_r3(upper, lf_c)
        b_r = _mm_l3(lf_r, lower)
        mask = ss >= tt
    else:
        b_c = _mm_r3(lower, lf_c)
        b_r = _mm_l3(lf_r, upper)
        mask = ss <= tt
    g_all = jnp.sum(lf_c, axis=0, keepdims=True)
    ones = jnp.ones((c, HEAD_DIM), F32)

    for h in range(H_G):
        ci = direction * H_G + h
        cf = 2 * H_G + ci
        sl = slice(h * HEAD_DIM, (h + 1) * HEAD_DIM)
        q, k, v = qa[:, sl], ka[:, sl], va[:, sl]
        bc = b_c[:, cf:cf + 1]
        br = b_r[cf:cf + 1, :]
        li_r = xt[ci:ci + 1, :]
        li_c = x[:, ci:ci + 1]
        g = g_all[:, cf:cf + 1]
        m_prev = m_scr[ci:ci + 1, 0:1]
        state = st_scr[ci]

        dlog = jnp.where(mask, bc - br + li_r, NEG_INF)
        inter_log = bc + m_prev
        m_t = jnp.maximum(inter_log, jnp.max(dlog, axis=-1, keepdims=True))
        sc = _mm_nt(q, k) * jnp.exp(dlog - m_t)
        w_inter = jnp.exp(inter_log - m_t)
        v_aug = jnp.concatenate([v, ones], axis=1)
        res = _mm(sc, v_aug) + w_inter * _mm(q, state)
        den = jnp.maximum(jnp.abs(res[:, HEAD_DIM:]), jnp.exp(-m_t))
        o_ref[0, :, sl] = res[:, :HEAD_DIM] / den

        a_c = g - bc + li_c
        m_new = jnp.maximum(g + m_prev, jnp.max(a_c, axis=0, keepdims=True))
        dec = jnp.exp(g + m_prev - m_new)
        w = jnp.exp(a_c - m_new)
        st_scr[ci] = dec * state + _mm_tn(k * w, v_aug)
        m_scr[ci:ci + 1, :] = jnp.broadcast_to(m_new, (1, LANES))


def _mlstm_kernel(uf_ref, upf_ref, unf_ref, gf_ref, ub_ref, upb_ref, unb_ref, gbk_ref,
                  cw_ref, cb_ref, gb_ref, of_ref, ob_ref, st_scr, m_scr):
    i = pl.program_id(1)
    nblk = pl.num_programs(1)

    @pl.when(i == 0)
    def _():
        st_scr[...] = jnp.zeros_like(st_scr)
        m_scr[...] = jnp.zeros_like(m_scr)

    _mlstm_chain(uf_ref, upf_ref, unf_ref, gf_ref, cw_ref, cb_ref, gb_ref, of_ref, st_scr, m_scr, i, nblk, 0)
    _mlstm_chain(ub_ref, upb_ref, unb_ref, gbk_ref, cw_ref, cb_ref, gb_ref, ob_ref, st_scr, m_scr,
                 nblk - 1 - i, nblk, 1)


def _mlstm(u3, conv_w, conv_b, gate_bias):
    b, t, _ = u3.shape
    c = MIX_CHUNK
    nblk = t // c
    rpb = c // ROW_ALIGN
    n8 = t // ROW_ALIGN
    col = U_MLSTM // 1024
    hcol = U_MLSTM // 512
    gcol = U_GATE // LANES

    def specs(rev):
        blk = (lambda i: nblk - 1 - i) if rev else (lambda i: i)
        return [
            pl.BlockSpec((1, c, 1024), lambda bi, i: (bi, blk(i), col)),
            pl.BlockSpec((1, ROW_ALIGN, 512), lambda bi, i: (bi, jnp.maximum(blk(i) * rpb - 1, 0), hcol)),
            pl.BlockSpec((1, ROW_ALIGN, 512), lambda bi, i: (bi, jnp.minimum((blk(i) + 1) * rpb, n8 - 1), hcol)),
            pl.BlockSpec((1, c, LANES), lambda bi, i: (bi, blk(i), gcol)),
        ]

    const = lambda shape: pl.BlockSpec(shape, lambda bi, i: (0,) * len(shape))
    return pl.pallas_call(
        _mlstm_kernel,
        out_shape=(jax.ShapeDtypeStruct((b, t, W_G), F32), jax.ShapeDtypeStruct((b, t, W_G), F32)),
        grid=(b, nblk),
        in_specs=specs(False) + specs(True) + [const((3, 2 * W_G)), const((1, 2 * W_G)), const((1, LANES))],
        out_specs=(
            pl.BlockSpec((1, c, W_G), lambda bi, i: (bi, i, 0)),
            pl.BlockSpec((1, c, W_G), lambda bi, i: (bi, nblk - 1 - i, 0)),
        ),
        scratch_shapes=[pltpu.VMEM((2 * H_G, HEAD_DIM, LANES), F32), pltpu.VMEM((2 * H_G, LANES), F32)],
        compiler_params=_cparams(("parallel", "arbitrary")),
        name="mlstm",
    )(u3, u3, u3, u3, u3, u3, u3, u3, conv_w, conv_b, gate_bias)


def _rwkv_prep_kernel(u_ref, up_ref, un_ref, mu_ref, w0_ref, w2_ref, a0_ref, a2_ref, g2_ref, kks_ref, ka_ref, rk_ref,
                      r_ref, v_ref, kk_ref, g_ref, bonus_ref, lwf_ref, lwb_ref, kf_ref, kb_ref, bf_ref, bb_ref):
    i = pl.program_id(1)
    nblk = pl.num_programs(1)
    u = u_ref[0]
    prev_row = jnp.where(i == 0, 0.0, up_ref[0][ROW_ALIGN - 1:ROW_ALIGN, :])
    next_row = jnp.where(i == nblk - 1, 0.0, un_ref[0][0:1, :])
    prev, nxt = _shift_rows(u, prev_row, next_row)
    us = u + mu_ref[...] * (0.5 * (prev + nxt) - u)
    r = us[:, 0:W_G]
    k = us[:, W_G:2 * W_G]
    v = us[:, 2 * W_G:3 * W_G]
    xw = us[:, 3 * W_G:3 * W_G + 64]
    xa = us[:, 3 * W_G + 64:3 * W_G + 128]
    xg = us[:, 3 * W_G + 128:]
    bd = _head_mean_matrix(W_G)
    g = _mm(_sigmoid(xg), g2_ref[...])
    lw = jnp.tanh(xw)
    a_lr = _mm_x3(xa, a2_ref[...])
    kk = k * kks_ref[...]
    kk = kk * lax.rsqrt(_head_mean(kk * kk, bd) * HEAD_DIM + 1e-12)
    r_ref[0] = r
    v_ref[0] = v
    kk_ref[0] = kk
    g_ref[0] = g
    bonus = jnp.zeros_like(r)
    for d, (lw_ref, k_ref, b_ref) in enumerate(((lwf_ref, kf_ref, bf_ref), (lwb_ref, kb_ref, bb_ref))):
        z = w0_ref[d:d + 1, :] + _mm_x3(lw, w2_ref[d])
        lw_ref[0] = -_sigmoid(z) * math.exp(-0.5)
        a = _sigmoid(a0_ref[d:d + 1, :] + a_lr)
        kd = k * (1.0 + (a - 1.0) * ka_ref[...])
        k_ref[0] = kd
        b_ref[0] = kk * a
        bonus = bonus + _head_mean(r * kd * rk_ref[...], bd) * HEAD_DIM * v
    bonus_ref[0] = bonus


def _rwkv_prep(u3, mu, w0, w2, a0, a2, g2, kks, ka, rk):
    b, t, _ = u3.shape
    tb = min(RWKV_BLOCK, t)
    nblk = t // tb
    rpb = tb // ROW_ALIGN
    n8 = t // ROW_ALIGN
    const = lambda shape: pl.BlockSpec(shape, lambda bi, i: (0,) * len(shape))
    out = jax.ShapeDtypeStruct((b, t, W_G), F32)
    ospec = pl.BlockSpec((1, tb, W_G), lambda bi, i: (bi, i, 0))
    return pl.pallas_call(
        _rwkv_prep_kernel,
        out_shape=(out,) * 11,
        grid=(b, nblk),
        in_specs=[
            pl.BlockSpec((1, tb, 1024), lambda bi, i: (bi, i, 0)),
            pl.BlockSpec((1, ROW_ALIGN, 1024), lambda bi, i: (bi, jnp.maximum(i * rpb - 1, 0), 0)),
            pl.BlockSpec((1, ROW_ALIGN, 1024), lambda bi, i: (bi, jnp.minimum((i + 1) * rpb, n8 - 1), 0)),
            const((1, 1024)), const((2, W_G)), const((2, 64, W_G)), const((2, W_G)), const((64, W_G)),
            const((128, W_G)), const((1, W_G)), const((1, W_G)), const((1, W_G)),
        ],
        out_specs=(ospec,) * 11,
        compiler_params=_cparams(("parallel", "parallel")),
        name="rwkv_prep",
    )(u3, u3, u3, mu, w0, w2, a0, a2, g2, kks, ka, rk)


def _tri_inverse(lmat, n):
    r = _iota2((n, n), 0)
    c = _iota2((n, n), 1)
    eye = jnp.where(r == c, 1.0, 0.0)
    inv = eye + jnp.where((r // 2 == c // 2) & (r != c), lmat, 0.0)
    s = 2
    while s < n:
        off = jnp.where((r // (2 * s) == c // (2 * s)) & (r // s != c // s), -lmat, 0.0)
        inv = inv - _mm(_mm(inv, off), inv)
        s *= 2
    return inv


def _rwkv_chunk(r, k, v, kk, b, lw, state, reverse):
    c = r.shape[0]
    tt = _iota2((c, c), 0)
    ss = _iota2((c, c), 1)
    if reverse:
        tri = jnp.where(tt <= ss, 1.0, 0.0).astype(BF16)
        strict = ss > tt
        incl = ss >= tt
    else:
        tri = jnp.where(ss <= tt, 1.0, 0.0).astype(BF16)
        strict = ss < tt
        incl = ss <= tt
    cum_in = _mm_r3(tri, lw)
    cum_ex = cum_in - lw
    cum_all = jnp.sum(lw, axis=0, keepdims=True)
    e_neg = jnp.exp(-cum_in)
    at = -kk * jnp.exp(cum_ex)
    rt = r * jnp.exp(cum_in)
    bt = b * e_neg
    kt = k * e_neg
    z = jnp.concatenate([at, rt], axis=0)
    gm = jnp.concatenate([bt, kt], axis=0)
    p = _mm_nt(z, gm)
    l_ab = jnp.where(strict, p[:c, :c], 0.0)
    l_ak = jnp.where(strict, p[:c, c:], 0.0)
    m_rb = jnp.where(incl, p[c:, :c], 0.0)
    m_rk = jnp.where(incl, p[c:, c:], 0.0)
    zs = _mm_nt(z, state)
    u = _mm(_tri_inverse(l_ab, c), zs[:c] + _mm(l_ak, v))
    uv = jnp.concatenate([u, v], axis=0)
    y = zs[c:] + _mm(jnp.concatenate([m_rb, m_rk], axis=1), uv)
    e_end = jnp.exp(cum_all - cum_in)
    gs = jnp.concatenate([b * e_end, k * e_end], axis=0)
    new_state = state * jnp.exp(cum_all) + _mm_tn(uv, gs)
    return y, new_state


def _rwkv_core_kernel(rf_ref, kf_ref, vf_ref, kkf_ref, bf_ref, lwf_ref, rb_ref, kb_ref, vb_ref, kkb_ref, bb_ref,
                      lwb_ref, yf_ref, yb_ref, s_scr):
    i = pl.program_id(1)
    c = RWKV_CHUNK
    nch = rf_ref.shape[1] // c

    @pl.when(i == 0)
    def _():
        s_scr[...] = jnp.zeros_like(s_scr)

    def step(j, carry):
        for d, refs, y_ref in ((0, (rf_ref, kf_ref, vf_ref, kkf_ref, bf_ref, lwf_ref), yf_ref),
                               (1, (rb_ref, kb_ref, vb_ref, kkb_ref, bb_ref, lwb_ref), yb_ref)):
            cj = j if d == 0 else nch - 1 - j
            rows = pl.ds(pl.multiple_of(cj * c, c), c)
            r, k, v, kk, b, lw = (ref[0, rows, :] for ref in refs)
            ys = []
            for h in range(H_G):
                sl = slice(h * HEAD_DIM, (h + 1) * HEAD_DIM)
                y, s_new = _rwkv_chunk(r[:, sl], k[:, sl], v[:, sl], kk[:, sl], b[:, sl], lw[:, sl],
                                       s_scr[d * H_G + h], d == 1)
                s_scr[d * H_G + h] = s_new
                ys.append(y)
            y_ref[0, rows, :] = jnp.concatenate(ys, axis=1)
        return carry

    lax.fori_loop(0, nch, step, 0)


def _rwkv_core(r, v, kk, lwf, lwb, kf, kb, bf, bb):
    b, t, _ = r.shape
    tb = min(RWKV_BLOCK, t)
    nblk = t // tb
    fwd = pl.BlockSpec((1, tb, W_G), lambda bi, i: (bi, i, 0))
    bwd = pl.BlockSpec((1, tb, W_G), lambda bi, i: (bi, nblk - 1 - i, 0))
    out = jax.ShapeDtypeStruct((b, t, W_G), F32)
    return pl.pallas_call(
        _rwkv_core_kernel,
        out_shape=(out, out),
        grid=(b, nblk),
        in_specs=[fwd] * 6 + [bwd] * 6,
        out_specs=(fwd, bwd),
        scratch_shapes=[pltpu.VMEM((2 * H_G, HEAD_DIM, HEAD_DIM), F32)],
        compiler_params=_cparams(("parallel", "arbitrary")),
        name="rwkv_core",
    )(r, kf, v, kk, bf, lwf, r, kb, v, kk, bb, lwb)


def _group_norm(y, gain, eps, bd):
    mu = _head_mean(y, bd)
    d = y - mu
    var = _head_mean(d * d, bd)
    return d * lax.rsqrt(var + eps) * gain


def _outproj_kernel(x_ref, ryf_ref, ryb_ref, rbon_ref, rg_ref, at_ref, mhf_ref, mhb_ref, mo_ref, tof_ref, tob_ref,
                    tg_ref, rln_ref, mln_ref, tln_ref, w_ref, o_ref):
    bd = _head_mean_matrix(W_G)
    o_a = (_group_norm(ryf_ref[...] + ryb_ref[...], rln_ref[...], RWKV_GN_EPS, bd) + rbon_ref[...]) * rg_ref[...]
    o_c = _group_norm(mhf_ref[...] + mhb_ref[...], mln_ref[...], HEAD_NORM_EPS, bd) * _sigmoid(mo_ref[...])
    o_d = _group_norm(tof_ref[...] + tob_ref[...], tln_ref[...], HEAD_NORM_EPS, bd) * _silu(tg_ref[...])
    mix = jnp.concatenate([o_a, at_ref[...], o_c, o_d], axis=1).astype(BF16)
    o_ref[...] = x_ref[...] + jnp.dot(mix, w_ref[...], preferred_element_type=F32)


def _out_proj(x2, rw, at, ml, rt, u2, rln, mln, tln, w_out, tm):
    n = x2.shape[0]
    row = lambda w: pl.BlockSpec((tm, w), lambda i: (i, 0))
    ucol = lambda off: pl.BlockSpec((tm, W_G), lambda i: (i, off // W_G))
    const = lambda shape: pl.BlockSpec(shape, lambda i: (0, 0))
    return pl.pallas_call(
        _outproj_kernel,
        out_shape=jax.ShapeDtypeStruct((n, D_MODEL), F32),
        grid=(n // tm,),
        in_specs=[row(D_MODEL)] + [row(W_G)] * 7 + [ucol(U_MLSTM + 3 * W_G)] + [row(W_G)] * 2
                 + [ucol(U_RET + 3 * W_G)] + [const((1, W_G))] * 3
                 + [pl.BlockSpec((D_MODEL, D_MODEL), lambda i: (0, 0), pipeline_mode=pl.Buffered(1))],
        out_specs=row(D_MODEL),
        compiler_params=_cparams(("parallel",)),
        name="out_proj",
    )(x2, *rw, at, *ml, u2, *rt, u2, rln, mln, tln, w_out)


def _ffn_kernel(x_ref, g_ref, wg_ref, wu_ref, wd_ref, o_ref):
    x = x_ref[...]
    h = _rms_norm_rows(x, g_ref[...]).astype(BF16)
    a = jnp.dot(h, wg_ref[...], preferred_element_type=F32)
    b = jnp.dot(h, wu_ref[...], preferred_element_type=F32)
    z = (_silu(a) * b).astype(BF16)
    o_ref[...] = x + jnp.dot(z, wd_ref[...], preferred_element_type=F32)


def _ffn(x2, gain, wg, wu, wd, tm):
    n = x2.shape[0]
    res = lambda shape: pl.BlockSpec(shape, lambda i: (0, 0), pipeline_mode=pl.Buffered(1))
    return pl.pallas_call(
        _ffn_kernel,
        out_shape=jax.ShapeDtypeStruct((n, D_MODEL), F32),
        grid=(n // tm,),
        in_specs=[pl.BlockSpec((tm, D_MODEL), lambda i: (i, 0)), pl.BlockSpec((1, D_MODEL), lambda i: (0, 0)),
                  res((D_MODEL, D_FF)), res((D_MODEL, D_FF)), res((D_FF, D_MODEL))],
        out_specs=pl.BlockSpec((tm, D_MODEL), lambda i: (i, 0)),
        compiler_params=_cparams(("parallel",)),
        name="ffn",
    )(x2, gain, wg, wu, wd)


def _router_kernel(x_ref, g_ref, wr_ref, h_ref, gate_ref):
    h = _rms_norm_rows(x_ref[...], g_ref[...])
    h_ref[...] = h.astype(BF16)
    logits = _mm_x3(h, wr_ref[...])
    lane = _iota2(logits.shape, 1)
    logits = jnp.where(lane < N_EXPERTS, logits, NEG_INF)
    e = jnp.exp(logits - jnp.max(logits, axis=-1, keepdims=True))
    p = e / jnp.sum(e, axis=-1, keepdims=True)
    p = jnp.where(lane < N_EXPERTS, p, -1.0)
    m1 = jnp.max(p, axis=-1, keepdims=True)
    i1 = jnp.min(jnp.where(p == m1, lane, LANES), axis=-1, keepdims=True)
    p2 = jnp.where(lane == i1, -1.0, p)
    m2 = jnp.max(p2, axis=-1, keepdims=True)
    i2 = jnp.min(jnp.where(p2 == m2, lane, LANES), axis=-1, keepdims=True)
    tot = m1 + m2
    gate_ref[...] = jnp.where(lane == i1, m1 / tot, 0.0) + jnp.where(lane == i2, m2 / tot, 0.0)


def _router(x2, gain, wr_pad, tm):
    n = x2.shape[0]
    return pl.pallas_call(
        _router_kernel,
        out_shape=(jax.ShapeDtypeStruct((n, D_MODEL), BF16), jax.ShapeDtypeStruct((n, LANES), F32)),
        grid=(n // tm,),
        in_specs=[pl.BlockSpec((tm, D_MODEL), lambda i: (i, 0)), pl.BlockSpec((1, D_MODEL), lambda i: (0, 0)),
                  pl.BlockSpec((D_MODEL, LANES), lambda i: (0, 0))],
        out_specs=(pl.BlockSpec((tm, D_MODEL), lambda i: (i, 0)), pl.BlockSpec((tm, LANES), lambda i: (i, 0))),
        compiler_params=_cparams(("parallel",)),
        name="router",
    )(x2, gain, wr_pad)


def _moe_kernel(x_ref, h_ref, gate_ref, wg_ref, wu_ref, wd_ref, nf_ref, o_ref):
    e = pl.program_id(1)
    f = pl.program_id(2)
    last = (e == pl.num_programs(1) - 1) & (f == pl.num_programs(2) - 1)

    @pl.when((e == 0) & (f == 0))
    def _():
        o_ref[...] = x_ref[...]

    h = h_ref[...]
    a = jnp.dot(h, wg_ref[0], preferred_element_type=F32)
    b = jnp.dot(h, wu_ref[0], preferred_element_type=F32)
    z = (_silu(a) * b).astype(BF16)
    y = jnp.dot(z, wd_ref[0], preferred_element_type=F32)
    lane = _iota2(gate_ref.shape, 1)
    gate = jnp.sum(jnp.where(lane == e, gate_ref[...], 0.0), axis=-1, keepdims=True)
    o_ref[...] += gate * y

    @pl.when(last)
    def _():
        o_ref[...] = _rms_norm_rows(o_ref[...], nf_ref[...])


def _moe(x2, h, gates, wg, wu, wd, norm_final, tm, fsplit):
    n = x2.shape[0]
    fw = D_FF // fsplit
    return pl.pallas_call(
        _moe_kernel,
        out_shape=jax.ShapeDtypeStruct((n, D_MODEL), F32),
        grid=(n // tm, N_EXPERTS, fsplit),
        in_specs=[
            pl.BlockSpec((tm, D_MODEL), lambda i, e, f: (i, 0)),
            pl.BlockSpec((tm, D_MODEL), lambda i, e, f: (i, 0)),
            pl.BlockSpec((tm, LANES), lambda i, e, f: (i, 0)),
            pl.BlockSpec((1, D_MODEL, fw), lambda i, e, f: (e, 0, f)),
            pl.BlockSpec((1, D_MODEL, fw), lambda i, e, f: (e, 0, f)),
            pl.BlockSpec((1, fw, D_MODEL), lambda i, e, f: (e, f, 0)),
            pl.BlockSpec((1, D_MODEL), lambda i, e, f: (0, 0)),
        ],
        out_specs=pl.BlockSpec((tm, D_MODEL), lambda i, e, f: (i, 0)),
        compiler_params=_cparams(("parallel", "arbitrary", "arbitrary")),
        name="moe",
    )(x2, h, gates, wg, wu, wd, norm_final)


def _rope_tables(t):
    rows = t // GRID_W
    pos = np.arange(rows * GRID_W)
    row = (pos // GRID_W).astype(np.float32)
    col = (pos % GRID_W).astype(np.float32)
    nf = HEAD_DIM // 4
    inv = jnp.asarray(ROPE_THETA, F32) ** (-jnp.arange(nf, dtype=F32) / nf)
    ar = jnp.asarray(row)[:, None] * inv
    ac = jnp.asarray(col)[:, None] * inv
    cos = jnp.concatenate([jnp.cos(ar), jnp.cos(ar), jnp.cos(ac), jnp.cos(ac)], axis=-1)
    sin = jnp.concatenate([-jnp.sin(ar), jnp.sin(ar), -jnp.sin(ac), jnp.sin(ac)], axis=-1)
    return jnp.tile(cos, (1, H_G)), jnp.tile(sin, (1, H_G))


def _pad_w_in(w):
    a, b_, c, d = 1024, 512, 1040, 1024
    w_a, w_b, w_c, w_d = w[:, :a], w[:, a:a + b_], w[:, a + b_:a + b_ + c], w[:, a + b_ + c:]
    gates = jnp.pad(w_c[:, 1024:], ((0, 0), (0, LANES - 16)))
    return jnp.concatenate([w_a, w_c[:, :1024], w_d, w_b, gates], axis=1).astype(BF16)


def _row(v):
    return v.reshape(1, -1).astype(F32)


def _trunk(x, p):
    b, t, _ = x.shape
    n = b * t
    tm = 256 if n % 256 == 0 else n
    cos, sin = _rope_tables(t)
    x2 = x.reshape(n, D_MODEL)
    depth = p['w_in'].shape[0]
    for l in range(depth):
        u2 = _in_proj(x2, _row(p['norm_mix'][l]), _pad_w_in(p['w_in'][l]), tm)
        u3 = u2.reshape(b, t, U_COLS)
        (r, v, kk, g, bonus, lwf, lwb, kf, kb, bf, bb) = _rwkv_prep(
            u3, _row(p['rwkv_mu'][l]), p['rwkv_w0'][l], p['rwkv_w2'][l], p['rwkv_a0'][l], p['rwkv_a2'][l],
            p['rwkv_g2'][l], _row(p['rwkv_kk'][l]), _row(p['rwkv_ka'][l]), _row(p['rwkv_rk'][l]))
        yf, yb = _rwkv_core(r, v, kk, lwf, lwb, kf, kb, bf, bb)
        at = _attention(u3, cos, sin, _row(jnp.tile(p['attn_q_norm'][l], H_G)),
                        _row(jnp.tile(p['attn_k_norm'][l], KV_ATTN)))
        gate_bias = jnp.pad(jnp.concatenate([p['mlstm_i_bias'][l].reshape(-1), p['mlstm_f_bias'][l].reshape(-1)]),
                            (0, LANES - 4 * H_G))
        hf, hb = _mlstm(u3, p['mlstm_conv_w'][l], _row(p['mlstm_conv_b'][l]), _row(gate_bias))
        of, ob = _retention(u3, cos, sin)
        flat = lambda z: z.reshape(n, W_G)
        x2 = _out_proj(x2, tuple(map(flat, (yf, yb, bonus, g))), flat(at), tuple(map(flat, (hf, hb))),
                       tuple(map(flat, (of, ob))), u2, _row(p['rwkv_ln'][l]), _row(p['mlstm_ln'][l]),
                       _row(p['ret_ln'][l]), p['w_out'][l].astype(BF16), tm)
        j = l // 2
        if l % 2 == 0:
            x2 = _ffn(x2, _row(p['norm_ffn'][l]), p['ffn_w_gate'][j].astype(BF16), p['ffn_w_up'][j].astype(BF16),
                      p['ffn_w_down'][j].astype(BF16), tm)
            if l == depth - 1:
                raise NotImplementedError("final norm after a dense FFN layer")
        else:
            wr = jnp.pad(p['moe_router'][j], ((0, 0), (0, LANES - N_EXPERTS)))
            h, gates = _router(x2, _row(p['norm_ffn'][l]), wr, tm)
            if l != depth - 1:
                raise NotImplementedError("expert layer that is not the last layer")
            x2 = _moe(x2, h, gates, p['moe_w_gate'][j].astype(BF16), p['moe_w_up'][j].astype(BF16),
                      p['moe_w_down'][j].astype(BF16), _row(p['norm_final']), tm, 2)
    return x2.reshape(b, t, D_MODEL)


def kernel(x_prompt, x_sample, norm_mix, norm_ffn, norm_final, w_in, w_out, rwkv_mu, rwkv_w0, rwkv_w2,
           rwkv_a0, rwkv_a2, rwkv_g2, rwkv_kk, rwkv_ka, rwkv_rk, rwkv_ln, attn_q_norm, attn_k_norm,
           mlstm_conv_w, mlstm_conv_b, mlstm_i_bias, mlstm_f_bias, mlstm_ln, ret_ln, ffn_w_gate, ffn_w_up,
           ffn_w_down, moe_router, moe_w_gate, moe_w_up, moe_w_down):
    p = dict(norm_mix=norm_mix, norm_ffn=norm_ffn, norm_final=norm_final, w_in=w_in, w_out=w_out,
             rwkv_mu=rwkv_mu, rwkv_w0=rwkv_w0, rwkv_w2=rwkv_w2, rwkv_a0=rwkv_a0, rwkv_a2=rwkv_a2,
             rwkv_g2=rwkv_g2, rwkv_kk=rwkv_kk, rwkv_ka=rwkv_ka, rwkv_rk=rwkv_rk, rwkv_ln=rwkv_ln,
             attn_q_norm=attn_q_norm, attn_k_norm=attn_k_norm, mlstm_conv_w=mlstm_conv_w,
             mlstm_conv_b=mlstm_conv_b, mlstm_i_bias=mlstm_i_bias, mlstm_f_bias=mlstm_f_bias,
             mlstm_ln=mlstm_ln, ret_ln=ret_ln, ffn_w_gate=ffn_w_gate, ffn_w_up=ffn_w_up,
             ffn_w_down=ffn_w_down, moe_router=moe_router, moe_w_gate=moe_w_gate, moe_w_up=moe_w_up,
             moe_w_down=moe_w_down)
    nb = x_prompt.shape[0]
    y = _trunk(jnp.concatenate([x_prompt, x_sample], axis=0), p)
    return (y[:nb], y[nb:])
```

```python
import functools
import math

import numpy as np
import jax
import jax.numpy as jnp
from jax import lax
from jax.experimental import pallas as pl
from jax.experimental.pallas import tpu as pltpu

F32 = jnp.float32
BF16 = jnp.bfloat16

D_MODEL = 1024
HEAD_DIM = 64
W_G = 256
H_G = 4
KV_ATTN = 2
D_FF = 2816
N_EXPERTS = 8
NORM_EPS = 1e-6
HEAD_NORM_EPS = 1e-5
RWKV_GN_EPS = 64e-5
NEG_INF = -1e30
ROPE_THETA = 10000.0
GRID_W = 64

LANES = 128
ROW_ALIGN = 8
VMEM_LIMIT_BYTES = 56 * 1024 * 1024

U_RWKV = 0
U_MLSTM = 1024
U_RET = 2048
U_ATTN = 3072
U_GATE = 3584
U_COLS = 3712

RWKV_CHUNK = 64
RWKV_BLOCK = 256
RWKV_GROUP = 2
MIX_CHUNK = 128
EXPERT_TILE = 256


def _cparams(sem):
    return pltpu.CompilerParams(dimension_semantics=sem, vmem_limit_bytes=VMEM_LIMIT_BYTES)


def _bdot(a, b, dims):
    return lax.dot_general(a, b, (dims, ((), ())), preferred_element_type=F32)


def _mm(a, b):
    return _bdot(a.astype(BF16), b.astype(BF16), ((1,), (0,)))


def _mm_nt(a, b):
    return _bdot(a.astype(BF16), b.astype(BF16), ((1,), (1,)))


def _mm_tn(a, b):
    return _bdot(a.astype(BF16), b.astype(BF16), ((0,), (0,)))


def _split2(a):
    hi = a.astype(BF16)
    lo = (a - hi.astype(F32)).astype(BF16)
    return hi, lo


def _split3(a):
    hi = a.astype(BF16)
    r = a - hi.astype(F32)
    mid = r.astype(BF16)
    lo = (r - mid.astype(F32)).astype(BF16)
    return hi, mid, lo


def _mm_l2(a, b_exact):
    hi, lo = _split2(a)
    return _bdot(hi, b_exact, ((1,), (0,))) + _bdot(lo, b_exact, ((1,), (0,)))


def _mm_l3(a, b_exact):
    h, m, l = _split3(a)
    return _bdot(h, b_exact, ((1,), (0,))) + _bdot(m, b_exact, ((1,), (0,))) + _bdot(l, b_exact, ((1,), (0,)))


def _mm_r3(a_exact, b):
    h, m, l = _split3(b)
    return _bdot(a_exact, h, ((1,), (0,))) + _bdot(a_exact, m, ((1,), (0,))) + _bdot(a_exact, l, ((1,), (0,)))


def _mm_x3(a, b):
    ah, al = _split2(a)
    bh, bl = _split2(b)
    d = ((1,), (0,))
    return _bdot(ah, bh, d) + _bdot(ah, bl, d) + _bdot(al, bh, d)


def _iota2(shape, axis):
    return lax.broadcasted_iota(jnp.int32, shape, axis)


def _head_mean_matrix(width):
    r = _iota2((width, width), 0) // HEAD_DIM
    c = _iota2((width, width), 1) // HEAD_DIM
    return jnp.where(r == c, 1.0 / HEAD_DIM, 0.0).astype(BF16)


def _head_mean(z, bd):
    return _mm_l2(z, bd)


def _sigmoid(x):
    return 1.0 / (1.0 + jnp.exp(-x))


def _silu(x):
    return x * _sigmoid(x)


def _log_sigmoid(x):
    return jnp.minimum(x, 0.0) - jnp.log(1.0 + jnp.exp(-jnp.abs(x)))


def _rms_norm_rows(x, gain):
    ms = jnp.mean(x * x, axis=-1, keepdims=True)
    return x * lax.rsqrt(ms + NORM_EPS) * gain


def _rope_swap(z):
    w = z.shape[-1]
    lane = _iota2(z.shape, z.ndim - 1)
    fwd = pltpu.roll(z, w - 16, z.ndim - 1)
    bwd = pltpu.roll(z, 16, z.ndim - 1)
    return jnp.where((lane % 32) < 16, fwd, bwd)


def _rope(z, cos, sin):
    return z * cos + _rope_swap(z) * sin


def _shift_rows(x, prev_row, next_row):
    n = x.shape[0]
    row = _iota2(x.shape, 0)
    prev = jnp.where(row == 0, prev_row, pltpu.roll(x, 1, 0))
    nxt = jnp.where(row == n - 1, next_row, pltpu.roll(x, n - 1, 0))
    return prev, nxt


def _inproj_kernel(x_ref, g_ref, w_ref, o_ref):
    h = _rms_norm_rows(x_ref[...], g_ref[...])
    o_ref[...] = jnp.dot(h.astype(BF16), w_ref[...], preferred_element_type=F32)


def _in_proj(x2, gain, w_pad, tm):
    n = x2.shape[0]
    return pl.pallas_call(
        _inproj_kernel,
        out_shape=jax.ShapeDtypeStruct((n, U_COLS), F32),
        grid=(n // tm,),
        in_specs=[
            pl.BlockSpec((tm, D_MODEL), lambda i: (i, 0)),
            pl.BlockSpec((1, D_MODEL), lambda i: (0, 0)),
            pl.BlockSpec((D_MODEL, U_COLS), lambda i: (0, 0), pipeline_mode=pl.Buffered(1)),
        ],
        out_specs=pl.BlockSpec((tm, U_COLS), lambda i: (i, 0)),
        compiler_params=_cparams(("parallel",)),
        name="in_proj",
    )(x2, gain, w_pad)


def _attn_kernel(u_ref, cos_ref, sin_ref, qg_ref, kg_ref, o_ref, q_scr, k_scr, v_scr, *, tq):
    t = u_ref.shape[1]
    u = u_ref[0]
    q = u[:, :W_G]
    k = u[:, W_G:W_G + KV_ATTN * HEAD_DIM]
    v = u[:, W_G + KV_ATTN * HEAD_DIM:]
    cos = cos_ref[...]
    sin = sin_ref[...]
    bd_q = _head_mean_matrix(W_G)
    bd_k = _head_mean_matrix(KV_ATTN * HEAD_DIM)
    qn = q * lax.rsqrt(_head_mean(q * q, bd_q) + NORM_EPS) * qg_ref[...]
    kn = k * lax.rsqrt(_head_mean(k * k, bd_k) + NORM_EPS) * kg_ref[...]
    kw = KV_ATTN * HEAD_DIM
    q_scr[...] = (_rope(qn, cos, sin) * HEAD_DIM ** -0.5).astype(BF16)
    k_scr[...] = _rope(kn, cos[:, :kw], sin[:, :kw]).astype(BF16)
    v_scr[...] = v.astype(BF16)
    group = H_G // KV_ATTN

    def q_tile(i, carry):
        rows = pl.ds(pl.multiple_of(i * tq, tq), tq)
        for j in range(KV_ATTN):
            kj = k_scr[:, j * HEAD_DIM:(j + 1) * HEAD_DIM]
            vj = v_scr[:, j * HEAD_DIM:(j + 1) * HEAD_DIM]
            for g in range(group):
                h = j * group + g
                qh = q_scr[rows, h * HEAD_DIM:(h + 1) * HEAD_DIM]
                s = _bdot(qh, kj, ((1,), (1,)))
                m = jnp.max(s, axis=-1, keepdims=True)
                p = jnp.exp(s - m)
                l = jnp.sum(p, axis=-1, keepdims=True)
                o = jnp.dot(p.astype(BF16), vj, preferred_element_type=F32) / l
                o_ref[0, rows, h * HEAD_DIM:(h + 1) * HEAD_DIM] = o
        return carry

    lax.fori_loop(0, t // tq, q_tile, 0)


def _attention(u3, cos, sin, q_gain, k_gain):
    b, t, _ = u3.shape
    tq = min(256, t)
    col = U_ATTN // 512
    return pl.pallas_call(
        functools.partial(_attn_kernel, tq=tq),
        out_shape=jax.ShapeDtypeStruct((b, t, W_G), F32),
        grid=(b,),
        in_specs=[
            pl.BlockSpec((1, t, 512), lambda i: (i, 0, col)),
            pl.BlockSpec((t, W_G), lambda i: (0, 0)),
            pl.BlockSpec((t, W_G), lambda i: (0, 0)),
            pl.BlockSpec((1, W_G), lambda i: (0, 0)),
            pl.BlockSpec((1, KV_ATTN * HEAD_DIM), lambda i: (0, 0)),
        ],
        out_specs=pl.BlockSpec((1, t, W_G), lambda i: (i, 0, 0)),
        scratch_shapes=[
            pltpu.VMEM((t, W_G), BF16),
            pltpu.VMEM((t, KV_ATTN * HEAD_DIM), BF16),
            pltpu.VMEM((t, KV_ATTN * HEAD_DIM), BF16),
        ],
        compiler_params=_cparams(("parallel",)),
        name="attention",
    )(u3, cos, sin, q_gain, k_gain)


def _ret_log_gamma(direction):
    return [math.log1p(-2.0 ** (-5.0 - (2 * h + direction) / 2.0)) for h in range(H_G)]


def _ret_kernel(uf_ref, ub_ref, cf_ref, sf_ref, cb_ref, sb_ref, of_ref, ob_ref, rf_scr, rb_scr):
    i = pl.program_id(1)
    c = MIX_CHUNK

    @pl.when(i == 0)
    def _():
        rf_scr[...] = jnp.zeros_like(rf_scr)
        rb_scr[...] = jnp.zeros_like(rb_scr)

    tt = _iota2((c, c), 0)
    ss = _iota2((c, c), 1)
    diff = (tt - ss).astype(F32)
    jcol = _iota2((c, HEAD_DIM), 0).astype(F32)

    uf = uf_ref[0]
    qf = _rope(uf[:, :W_G], cf_ref[...], sf_ref[...])
    kf = _rope(uf[:, W_G:2 * W_G], cf_ref[...], sf_ref[...]) * HEAD_DIM ** -0.5
    vf = uf[:, 2 * W_G:3 * W_G]
    lg_f = _ret_log_gamma(0)
    lg_b = _ret_log_gamma(1)
    for h in range(H_G):
        sl = slice(h * HEAD_DIM, (h + 1) * HEAD_DIM)
        q, k, v = qf[:, sl], kf[:, sl], vf[:, sl]
        decay = (jnp.where(tt >= ss, jnp.exp(diff * lg_f[h]), 0.0)
                 + jnp.where(ss >= tt, jnp.exp(-diff * lg_b[h]), 0.0))
        intra = _mm(_mm_nt(q, k) * decay, v)
        r_prev = rf_scr[h]
        inter = _mm(q * jnp.exp((jcol + 1.0) * lg_f[h]), r_prev)
        of_ref[0, :, sl] = intra + inter
        k_in = k * jnp.exp((c - 1.0 - jcol) * lg_f[h])
        rf_scr[h] = math.exp(c * lg_f[h]) * r_prev + _mm_tn(k_in, v)

    ub = ub_ref[0]
    qb = _rope(ub[:, :W_G], cb_ref[...], sb_ref[...])
    kb = _rope(ub[:, W_G:2 * W_G], cb_ref[...], sb_ref[...]) * HEAD_DIM ** -0.5
    vb = ub[:, 2 * W_G:3 * W_G]
    for h in range(H_G):
        sl = slice(h * HEAD_DIM, (h + 1) * HEAD_DIM)
        q, k, v = qb[:, sl], kb[:, sl], vb[:, sl]
        r_prev = rb_scr[h]
        ob_ref[0, :, sl] = _mm(q * jnp.exp((c - jcol) * lg_b[h]), r_prev)
        k_in = k * jnp.exp(jcol * lg_b[h])
        rb_scr[h] = math.exp(c * lg_b[h]) * r_prev + _mm_tn(k_in, v)


def _retention(u3, cos, sin):
    b, t, _ = u3.shape
    c = MIX_CHUNK
    nblk = t // c
    col = U_RET // 1024
    tab = lambda rev: pl.BlockSpec((c, W_G), (lambda bi, i: (nblk - 1 - i, 0)) if rev else (lambda bi, i: (i, 0)))
    return pl.pallas_call(
        _ret_kernel,
        out_shape=(jax.ShapeDtypeStruct((b, t, W_G), F32), jax.ShapeDtypeStruct((b, t, W_G), F32)),
        grid=(b, nblk),
        in_specs=[
            pl.BlockSpec((1, c, 1024), lambda bi, i: (bi, i, col)),
            pl.BlockSpec((1, c, 1024), lambda bi, i: (bi, nblk - 1 - i, col)),
            tab(False), tab(False), tab(True), tab(True),
        ],
        out_specs=(
            pl.BlockSpec((1, c, W_G), lambda bi, i: (bi, i, 0)),
            pl.BlockSpec((1, c, W_G), lambda bi, i: (bi, nblk - 1 - i, 0)),
        ),
        scratch_shapes=[pltpu.VMEM((H_G, HEAD_DIM, HEAD_DIM), F32), pltpu.VMEM((H_G, HEAD_DIM, HEAD_DIM), F32)],
        compiler_params=_cparams(("parallel", "arbitrary")),
        name="retention",
    )(u3, u3, cos, sin, cos, sin)


def _mlstm_chain(u_ref, up_ref, un_ref, g_ref, cw_ref, cb_ref, gb_ref, o_ref, st_scr, m_scr, blk, nblk, direction):
    c = MIX_CHUNK
    reverse = direction == 1
    u = u_ref[0]
    qk = u[:, :2 * W_G]
    prev_row = jnp.where(blk == 0, 0.0, up_ref[0][ROW_ALIGN - 1:ROW_ALIGN, :])
    next_row = jnp.where(blk == nblk - 1, 0.0, un_ref[0][0:1, :])
    prev, nxt = _shift_rows(qk, prev_row, next_row)
    cw = cw_ref[...]
    qk = _silu(cw[0:1] * prev + cw[1:2] * qk + cw[2:3] * nxt + cb_ref[...])
    qa = qk[:, :W_G]
    ka = qk[:, W_G:] * HEAD_DIM ** -0.5
    va = u[:, 2 * W_G:3 * W_G]

    x = g_ref[0] + gb_ref[...]
    xt = x.T
    lf_c = _log_sigmoid(x)
    lf_r = _log_sigmoid(xt)
    tt = _iota2((c, c), 0)
    ss = _iota2((c, c), 1)
    lower = jnp.where(ss <= tt, 1.0, 0.0).astype(BF16)
    upper = jnp.where(tt <= ss, 1.0, 0.0).astype(BF16)
    if reverse:
        b_c = _mm_r3(upper, lf_c)
        b_r = _mm_l3(lf_r, lower)
        mask = ss >= tt
    else:
        b_c = _mm_r3(lower, lf_c)
        b_r = _mm_l3(lf_r, upper)
        mask = ss <= tt
    g_all = jnp.sum(lf_c, axis=0, keepdims=True)
    ones = jnp.ones((c, HEAD_DIM), F32)

    for h in range(H_G):
        ci = direction * H_G + h
        cf = 2 * H_G + ci
        sl = slice(h * HEAD_DIM, (h + 1) * HEAD_DIM)
        q, k, v = qa[:, sl], ka[:, sl], va[:, sl]
        bc = b_c[:, cf:cf + 1]
        br = b_r[cf:cf + 1, :]
        li_r = xt[ci:ci + 1, :]
        li_c = x[:, ci:ci + 1]
        g = g_all[:, cf:cf + 1]
        m_prev = m_scr[ci:ci + 1, 0:1]
        state = st_scr[ci]

        dlog = jnp.where(mask, bc - br + li_r, NEG_INF)
        inter_log = bc + m_prev
        m_t = jnp.maximum(inter_log, jnp.max(dlog, axis=-1, keepdims=True))
        sc = _mm_nt(q, k) * jnp.exp(dlog - m_t)
        w_inter = jnp.exp(inter_log - m_t)
        v_aug = jnp.concatenate([v, ones], axis=1)
        res = _mm(sc, v_aug) + w_inter * _mm(q, state)
        den = jnp.maximum(jnp.abs(res[:, HEAD_DIM:]), jnp.exp(-m_t))
        o_ref[0, :, sl] = res[:, :HEAD_DIM] / den

        a_c = g - bc + li_c
        m_new = jnp.maximum(g + m_prev, jnp.max(a_c, axis=0, keepdims=True))
        dec = jnp.exp(g + m_prev - m_new)
        w = jnp.exp(a_c - m_new)
        st_scr[ci] = dec * state + _mm_tn(k * w, v_aug)
        m_scr[ci:ci + 1, :] = jnp.broadcast_to(m_new, (1, LANES))


def _mlstm_kernel(uf_ref, upf_ref, unf_ref, gf_ref, ub_ref, upb_ref, unb_ref, gbk_ref,
                  cw_ref, cb_ref, gb_ref, of_ref, ob_ref, st_scr, m_scr):
    i = pl.program_id(1)
    nblk = pl.num_programs(1)

    @pl.when(i == 0)
    def _():
        st_scr[...] = jnp.zeros_like(st_scr)
        m_scr[...] = jnp.zeros_like(m_scr)

    _mlstm_chain(uf_ref, upf_ref, unf_ref, gf_ref, cw_ref, cb_ref, gb_ref, of_ref, st_scr, m_scr, i, nblk, 0)
    _mlstm_chain(ub_ref, upb_ref, unb_ref, gbk_ref, cw_ref, cb_ref, gb_ref, ob_ref, st_scr, m_scr,
                 nblk - 1 - i, nblk, 1)


def _mlstm(u3, conv_w, conv_b, gate_bias):
    b, t, _ = u3.shape
    c = MIX_CHUNK
    nblk = t // c
    rpb = c // ROW_ALIGN
    n8 = t // ROW_ALIGN
    col = U_MLSTM // 1024
    hcol = U_MLSTM // 512
    gcol = U_GATE // LANES

    def specs(rev):
        blk = (lambda i: nblk - 1 - i) if rev else (lambda i: i)
        return [
            pl.BlockSpec((1, c, 1024), lambda bi, i: (bi, blk(i), col)),
            pl.BlockSpec((1, ROW_ALIGN, 512), lambda bi, i: (bi, jnp.maximum(blk(i) * rpb - 1, 0), hcol)),
            pl.BlockSpec((1, ROW_ALIGN, 512), lambda bi, i: (bi, jnp.minimum((blk(i) + 1) * rpb, n8 - 1), hcol)),
            pl.BlockSpec((1, c, LANES), lambda bi, i: (bi, blk(i), gcol)),
        ]

    const = lambda shape: pl.BlockSpec(shape, lambda bi, i: (0,) * len(shape))
    return pl.pallas_call(
        _mlstm_kernel,
        out_shape=(jax.ShapeDtypeStruct((b, t, W_G), F32), jax.ShapeDtypeStruct((b, t, W_G), F32)),
        grid=(b, nblk),
        in_specs=specs(False) + specs(True) + [const((3, 2 * W_G)), const((1, 2 * W_G)), const((1, LANES))],
        out_specs=(
            pl.BlockSpec((1, c, W_G), lambda bi, i: (bi, i, 0)),
            pl.BlockSpec((1, c, W_G), lambda bi, i: (bi, nblk - 1 - i, 0)),
        ),
        scratch_shapes=[pltpu.VMEM((2 * H_G, HEAD_DIM, LANES), F32), pltpu.VMEM((2 * H_G, LANES), F32)],
        compiler_params=_cparams(("parallel", "arbitrary")),
        name="mlstm",
    )(u3, u3, u3, u3, u3, u3, u3, u3, conv_w, conv_b, gate_bias)


def _rwkv_prep_kernel(u_ref, up_ref, un_ref, mu_ref, w0_ref, w2_ref, a0_ref, a2_ref, g2_ref, kks_ref, ka_ref, rk_ref,
                      r_ref, v_ref, kk_ref, g_ref, bonus_ref, lwf_ref, lwb_ref, kf_ref, kb_ref, bf_ref, bb_ref):
    i = pl.program_id(1)
    nblk = pl.num_programs(1)
    u = u_ref[0]
    prev_row = jnp.where(i == 0, 0.0, up_ref[0][ROW_ALIGN - 1:ROW_ALIGN, :])
    next_row = jnp.where(i == nblk - 1, 0.0, un_ref[0][0:1, :])
    prev, nxt = _shift_rows(u, prev_row, next_row)
    us = u + mu_ref[...] * (0.5 * (prev + nxt) - u)
    r = us[:, 0:W_G]
    k = us[:, W_G:2 * W_G]
    v = us[:, 2 * W_G:3 * W_G]
    xw = us[:, 3 * W_G:3 * W_G + 64]
    xa = us[:, 3 * W_G + 64:3 * W_G + 128]
    xg = us[:, 3 * W_G + 128:]
    bd = _head_mean_matrix(W_G)
    g = _mm(_sigmoid(xg), g2_ref[...])
    lw = jnp.tanh(xw)
    a_lr = _mm_x3(xa, a2_ref[...])
    kk = k * kks_ref[...]
    kk = kk * lax.rsqrt(_head_mean(kk * kk, bd) * HEAD_DIM + 1e-12)
    r_ref[0] = r
    v_ref[0] = v
    kk_ref[0] = kk
    g_ref[0] = g
    bonus = jnp.zeros_like(r)
    for d, (lw_ref, k_ref, b_ref) in enumerate(((lwf_ref, kf_ref, bf_ref), (lwb_ref, kb_ref, bb_ref))):
        z = w0_ref[d:d + 1, :] + _mm_x3(lw, w2_ref[d])
        lw_ref[0] = -_sigmoid(z) * math.exp(-0.5)
        a = _sigmoid(a0_ref[d:d + 1, :] + a_lr)
        kd = k * (1.0 + (a - 1.0) * ka_ref[...])
        k_ref[0] = kd
        b_ref[0] = kk * a
        bonus = bonus + _head_mean(r * kd * rk_ref[...], bd) * HEAD_DIM * v
    bonus_ref[0] = bonus


def _rwkv_prep(u3, mu, w0, w2, a0, a2, g2, kks, ka, rk):
    b, t, _ = u3.shape
    tb = min(RWKV_BLOCK, t)
    nblk = t // tb
    rpb = tb // ROW_ALIGN
    n8 = t // ROW_ALIGN
    const = lambda shape: pl.BlockSpec(shape, lambda bi, i: (0,) * len(shape))
    out = jax.ShapeDtypeStruct((b, t, W_G), F32)
    ospec = pl.BlockSpec((1, tb, W_G), lambda bi, i: (bi, i, 0))
    return pl.pallas_call(
        _rwkv_prep_kernel,
        out_shape=(out,) * 11,
        grid=(b, nblk),
        in_specs=[
            pl.BlockSpec((1, tb, 1024), lambda bi, i: (bi, i, 0)),
            pl.BlockSpec((1, ROW_ALIGN, 1024), lambda bi, i: (bi, jnp.maximum(i * rpb - 1, 0), 0)),
            pl.BlockSpec((1, ROW_ALIGN, 1024), lambda bi, i: (bi, jnp.minimum((i + 1) * rpb, n8 - 1), 0)),
            const((1, 1024)), const((2, W_G)), const((2, 64, W_G)), const((2, W_G)), const((64, W_G)),
            const((128, W_G)), const((1, W_G)), const((1, W_G)), const((1, W_G)),
        ],
        out_specs=(ospec,) * 11,
        compiler_params=_cparams(("parallel", "parallel")),
        name="rwkv_prep",
    )(u3, u3, u3, mu, w0, w2, a0, a2, g2, kks, ka, rk)


def _tri_inverse_all(lmats, n):
    r = _iota2((n, n), 0)
    c = _iota2((n, n), 1)
    eye = jnp.where(r == c, 1.0, 0.0)
    pair = (r // 2 == c // 2) & (r != c)
    invs = [eye + jnp.where(pair, lm, 0.0) for lm in lmats]
    s = 2
    while s < n:
        sel = (r // (2 * s) == c // (2 * s)) & (r // s != c // s)
        offs = [jnp.where(sel, -lm, 0.0) for lm in lmats]
        xs = [_mm(inv, off) for inv, off in zip(invs, offs)]
        invs = [inv - _mm(x, inv) for inv, x in zip(invs, xs)]
        s *= 2
    return invs


def _rwkv_tile_terms(r, k, v, kk, b, lw, reverse):
    c = r.shape[0]
    tt = _iota2((c, c), 0)
    ss = _iota2((c, c), 1)
    tri = jnp.where((tt <= ss) if reverse else (ss <= tt), 1.0, 0.0).astype(BF16)
    cum_in = _mm_r3(tri, lw)
    cum_all = jnp.sum(lw, axis=0, keepdims=True)
    e_neg = jnp.exp(-cum_in)
    e_end = jnp.exp(cum_all - cum_in)
    return dict(at=-kk * jnp.exp(cum_in - lw), rt=r * jnp.exp(cum_in), bt=b * e_neg, kt=k * e_neg,
                gb=b * e_end, gk=k * e_end, v=v, e_all=jnp.exp(cum_all))


def _rwkv_chunk_terms(tiles, reverses):
    c = RWKV_CHUNK
    tt = _iota2((c, c), 0)
    ss = _iota2((c, c), 1)
    heads = [(ti, h) for ti in range(len(tiles)) for h in range(H_G)]
    sl = lambda h: slice(h * HEAD_DIM, (h + 1) * HEAD_DIM)
    get = lambda name: [tiles[ti][name][:, sl(h)] for ti, h in heads]
    at, rt, bt, kt, gb, gk, v = (get(nm) for nm in ("at", "rt", "bt", "kt", "gb", "gk", "v"))
    strict = [(ss > tt) if reverses[ti] else (ss < tt) for ti, _ in heads]
    incl = [(ss >= tt) if reverses[ti] else (ss <= tt) for ti, _ in heads]
    ps = [_mm_nt(jnp.concatenate([a, r_], axis=0), jnp.concatenate([b_, k_], axis=0))
          for a, r_, b_, k_ in zip(at, rt, bt, kt)]
    l_ab = [jnp.where(m, p[:c, :c], 0.0) for m, p in zip(strict, ps)]
    l_ak = [jnp.where(m, p[:c, c:], 0.0) for m, p in zip(strict, ps)]
    m_r = [jnp.concatenate([jnp.where(m, p[c:, :c], 0.0), jnp.where(m, p[c:, c:], 0.0)], axis=1)
           for m, p in zip(incl, ps)]
    lakv = [_mm(l, v_) for l, v_ in zip(l_ak, v)]
    invs = _tri_inverse_all(l_ab, c)
    tw = [_mm(inv, jnp.concatenate([a, lv], axis=1)) for inv, a, lv in zip(invs, at, lakv)]
    zeros = jnp.zeros((c, HEAD_DIM), F32)
    mm2 = [_mm(m, jnp.concatenate([t_, jnp.concatenate([zeros, v_], axis=1)], axis=0))
           for m, t_, v_ in zip(m_r, tw, v)]
    r1 = [r_ + m[:, :HEAD_DIM] for r_, m in zip(rt, mm2)]
    y0 = [m[:, HEAD_DIM:] for m in mm2]
    twg = [_mm_tn(t_, g_) for t_, g_ in zip(tw, gb)]
    vgk = [_mm_tn(v_, g_) for v_, g_ in zip(v, gk)]
    mlow = [x[:HEAD_DIM] for x in twg]
    nadd = [x[HEAD_DIM:] + y for x, y in zip(twg, vgk)]
    e_all = [tiles[ti]["e_all"][:, sl(h)] for ti, h in heads]
    return r1, y0, mlow, nadd, e_all


def _rwkv_core_kernel(rf_ref, kf_ref, vf_ref, kkf_ref, bf_ref, lwf_ref, rb_ref, kb_ref, vb_ref, kkb_ref, bb_ref,
                      lwb_ref, yf_ref, yb_ref, s_scr, *, group):
    i = pl.program_id(1)
    c = RWKV_CHUNK
    nch = rf_ref.shape[1] // c
    dirs = ((rf_ref, kf_ref, vf_ref, kkf_ref, bf_ref, lwf_ref), (rb_ref, kb_ref, vb_ref, kkb_ref, bb_ref, lwb_ref))
    y_refs = (yf_ref, yb_ref)

    @pl.when(i == 0)
    def _():
        s_scr[...] = jnp.zeros_like(s_scr)

    def step(j, states):
        tiles, reverses, rows = [], [], []
        for q in range(group):
            for d in range(2):
                cj = j * group + q
                cj = cj if d == 0 else nch - 1 - cj
                rw = pl.ds(pl.multiple_of(cj * c, c), c)
                tiles.append(_rwkv_tile_terms(*(ref[0, rw, :] for ref in dirs[d]), d == 1))
                reverses.append(d == 1)
                rows.append(rw)
        r1, y0, mlow, nadd, e_all = _rwkv_chunk_terms(tiles, reverses)
        states = list(states)
        for q in range(group):
            ys = [[], []]
            for d in range(2):
                for h in range(H_G):
                    n = (q * 2 + d) * H_G + h
                    s = states[d * H_G + h]
                    ys[d].append(_mm_nt(r1[n], s) + y0[n])
                    states[d * H_G + h] = s * e_all[n] + _mm(s, mlow[n]) + nadd[n]
            for d in range(2):
                y_refs[d][0, rows[q * 2 + d], :] = jnp.concatenate(ys[d], axis=1)
        return tuple(states)

    init = tuple(s_scr[n] for n in range(2 * H_G))
    if nch == group:
        states = step(0, init)
    else:
        states = lax.fori_loop(0, nch // group, step, init)
    for n in range(2 * H_G):
        s_scr[n] = states[n]


def _rwkv_core(r, v, kk, lwf, lwb, kf, kb, bf, bb):
    b, t, _ = r.shape
    tb = min(RWKV_BLOCK, t)
    nblk = t // tb
    fwd = pl.BlockSpec((1, tb, W_G), lambda bi, i: (bi, i, 0))
    bwd = pl.BlockSpec((1, tb, W_G), lambda bi, i: (bi, nblk - 1 - i, 0))
    out = jax.ShapeDtypeStruct((b, t, W_G), F32)
    return pl.pallas_call(
        functools.partial(_rwkv_core_kernel, group=min(RWKV_GROUP, tb // RWKV_CHUNK)),
        out_shape=(out, out),
        grid=(b, nblk),
        in_specs=[fwd] * 6 + [bwd] * 6,
        out_specs=(fwd, bwd),
        scratch_shapes=[pltpu.VMEM((2 * H_G, HEAD_DIM, HEAD_DIM), F32)],
        compiler_params=_cparams(("parallel", "arbitrary")),
        name="rwkv_core",
    )(r, kf, v, kk, bf, lwf, r, kb, v, kk, bb, lwb)


def _group_norm(y, gain, eps, bd):
    mu = _head_mean(y, bd)
    d = y - mu
    var = _head_mean(d * d, bd)
    return d * lax.rsqrt(var + eps) * gain


def _outproj_kernel(x_ref, ryf_ref, ryb_ref, rbon_ref, rg_ref, at_ref, mhf_ref, mhb_ref, mo_ref, tof_ref, tob_ref,
                    tg_ref, rln_ref, mln_ref, tln_ref, w_ref, o_ref):
    bd = _head_mean_matrix(W_G)
    o_a = (_group_norm(ryf_ref[...] + ryb_ref[...], rln_ref[...], RWKV_GN_EPS, bd) + rbon_ref[...]) * rg_ref[...]
    o_c = _group_norm(mhf_ref[...] + mhb_ref[...], mln_ref[...], HEAD_NORM_EPS, bd) * _sigmoid(mo_ref[...])
    o_d = _group_norm(tof_ref[...] + tob_ref[...], tln_ref[...], HEAD_NORM_EPS, bd) * _silu(tg_ref[...])
    mix = jnp.concatenate([o_a, at_ref[...], o_c, o_d], axis=1).astype(BF16)
    o_ref[...] = x_ref[...] + jnp.dot(mix, w_ref[...], preferred_element_type=F32)


def _out_proj(x2, rw, at, ml, rt, u2, rln, mln, tln, w_out, tm):
    n = x2.shape[0]
    row = lambda w: pl.BlockSpec((tm, w), lambda i: (i, 0))
    ucol = lambda off: pl.BlockSpec((tm, W_G), lambda i: (i, off // W_G))
    const = lambda shape: pl.BlockSpec(shape, lambda i: (0, 0))
    return pl.pallas_call(
        _outproj_kernel,
        out_shape=jax.ShapeDtypeStruct((n, D_MODEL), F32),
        grid=(n // tm,),
        in_specs=[row(D_MODEL)] + [row(W_G)] * 7 + [ucol(U_MLSTM + 3 * W_G)] + [row(W_G)] * 2
                 + [ucol(U_RET + 3 * W_G)] + [const((1, W_G))] * 3
                 + [pl.BlockSpec((D_MODEL, D_MODEL), lambda i: (0, 0), pipeline_mode=pl.Buffered(1))],
        out_specs=row(D_MODEL),
        compiler_params=_cparams(("parallel",)),
        name="out_proj",
    )(x2, *rw, at, *ml, u2, *rt, u2, rln, mln, tln, w_out)


def _ffn_kernel(x_ref, g_ref, wg_ref, wu_ref, wd_ref, o_ref):
    x = x_ref[...]
    h = _rms_norm_rows(x, g_ref[...]).astype(BF16)
    a = jnp.dot(h, wg_ref[...], preferred_element_type=F32)
    b = jnp.dot(h, wu_ref[...], preferred_element_type=F32)
    z = (_silu(a) * b).astype(BF16)
    o_ref[...] = x + jnp.dot(z, wd_ref[...], preferred_element_type=F32)


def _ffn(x2, gain, wg, wu, wd, tm):
    n = x2.shape[0]
    res = lambda shape: pl.BlockSpec(shape, lambda i: (0, 0), pipeline_mode=pl.Buffered(1))
    return pl.pallas_call(
        _ffn_kernel,
        out_shape=jax.ShapeDtypeStruct((n, D_MODEL), F32),
        grid=(n // tm,),
        in_specs=[pl.BlockSpec((tm, D_MODEL), lambda i: (i, 0)), pl.BlockSpec((1, D_MODEL), lambda i: (0, 0)),
                  res((D_MODEL, D_FF)), res((D_MODEL, D_FF)), res((D_FF, D_MODEL))],
        out_specs=pl.BlockSpec((tm, D_MODEL), lambda i: (i, 0)),
        compiler_params=_cparams(("parallel",)),
        name="ffn",
    )(x2, gain, wg, wu, wd)


def _router_kernel(x_ref, g_ref, wr_ref, h_ref, gate_ref, idx_ref):
    h = _rms_norm_rows(x_ref[...], g_ref[...])
    h_ref[...] = h.astype(BF16)
    logits = _mm_x3(h, wr_ref[...])
    lane = _iota2(logits.shape, 1)
    logits = jnp.where(lane < N_EXPERTS, logits, NEG_INF)
    e = jnp.exp(logits - jnp.max(logits, axis=-1, keepdims=True))
    p = e / jnp.sum(e, axis=-1, keepdims=True)
    p = jnp.where(lane < N_EXPERTS, p, -1.0)
    m1 = jnp.max(p, axis=-1, keepdims=True)
    i1 = jnp.min(jnp.where(p == m1, lane, LANES), axis=-1, keepdims=True)
    p2 = jnp.where(lane == i1, -1.0, p)
    m2 = jnp.max(p2, axis=-1, keepdims=True)
    i2 = jnp.min(jnp.where(p2 == m2, lane, LANES), axis=-1, keepdims=True)
    tot = m1 + m2
    gate_ref[...] = jnp.where(lane == 0, m1 / tot, jnp.where(lane == 1, m2 / tot, 0.0))
    idx_ref[...] = jnp.where(lane == 0, i1, jnp.where(lane == 1, i2, 0))


def _router(x2, gain, wr_pad, tm):
    n = x2.shape[0]
    return pl.pallas_call(
        _router_kernel,
        out_shape=(jax.ShapeDtypeStruct((n, D_MODEL), BF16), jax.ShapeDtypeStruct((n, LANES), F32),
                   jax.ShapeDtypeStruct((n, LANES), jnp.int32)),
        grid=(n // tm,),
        in_specs=[pl.BlockSpec((tm, D_MODEL), lambda i: (i, 0)), pl.BlockSpec((1, D_MODEL), lambda i: (0, 0)),
                  pl.BlockSpec((D_MODEL, LANES), lambda i: (0, 0))],
        out_specs=(pl.BlockSpec((tm, D_MODEL), lambda i: (i, 0)), pl.BlockSpec((tm, LANES), lambda i: (i, 0)),
                   pl.BlockSpec((tm, LANES), lambda i: (i, 0))),
        compiler_params=_cparams(("parallel",)),
        name="router",
    )(x2, gain, wr_pad)


def _expert_ffn_kernel(te_ref, nv_ref, xs_ref, wg_ref, wu_ref, wd_ref, o_ref):
    i = pl.program_id(0)

    @pl.when(i < nv_ref[0])
    def _():
        h = xs_ref[...]
        a = jnp.dot(h, wg_ref[0], preferred_element_type=F32)
        b = jnp.dot(h, wu_ref[0], preferred_element_type=F32)
        z = (_silu(a) * b).astype(BF16)
        o_ref[...] = jnp.dot(z, wd_ref[0], preferred_element_type=F32).astype(o_ref.dtype)

    @pl.when(i >= nv_ref[0])
    def _():
        o_ref[...] = jnp.zeros_like(o_ref)


def _expert_ffn(xs, tile_expert, n_valid, wg, wu, wd, tm):
    rows = xs.shape[0]
    wspec = lambda shape: pl.BlockSpec((1,) + shape, lambda i, te, nv: (te[i], 0, 0))
    return pl.pallas_call(
        _expert_ffn_kernel,
        out_shape=jax.ShapeDtypeStruct((rows, D_MODEL), BF16),
        grid_spec=pltpu.PrefetchScalarGridSpec(
            num_scalar_prefetch=2,
            grid=(rows // tm,),
            in_specs=[pl.BlockSpec((tm, D_MODEL), lambda i, te, nv: (i, 0)),
                      wspec((D_MODEL, D_FF)), wspec((D_MODEL, D_FF)), wspec((D_FF, D_MODEL))],
            out_specs=pl.BlockSpec((tm, D_MODEL), lambda i, te, nv: (i, 0)),
        ),
        compiler_params=_cparams(("arbitrary",)),
        name="expert_ffn",
    )(tile_expert, n_valid, xs, wg, wu, wd)


def _moe_combine_kernel(x_ref, y1_ref, y2_ref, gate_ref, nf_ref, o_ref):
    g = gate_ref[...]
    y = x_ref[...] + g[:, 0:1] * y1_ref[...].astype(F32) + g[:, 1:2] * y2_ref[...].astype(F32)
    o_ref[...] = _rms_norm_rows(y, nf_ref[...])


def _moe_combine(x2, y1, y2, gates, norm_final, tm):
    n = x2.shape[0]
    row = lambda w: pl.BlockSpec((tm, w), lambda i: (i, 0))
    return pl.pallas_call(
        _moe_combine_kernel,
        out_shape=jax.ShapeDtypeStruct((n, D_MODEL), F32),
        grid=(n // tm,),
        in_specs=[row(D_MODEL), row(D_MODEL), row(D_MODEL), row(LANES), pl.BlockSpec((1, D_MODEL), lambda i: (0, 0))],
        out_specs=row(D_MODEL),
        compiler_params=_cparams(("parallel",)),
        name="moe_combine",
    )(x2, y1, y2, gates, norm_final)


def _moe(x2, h, gates, idx, wg, wu, wd, norm_final, tm):
    n = x2.shape[0]
    tme = EXPERT_TILE
    n_tiles = (2 * n + N_EXPERTS * (tme - 1)) // tme + 1
    e_pair = idx[:, :2]
    onehot = (e_pair[:, :, None] == jnp.arange(N_EXPERTS)[None, None, :]).astype(jnp.int32)
    per_tok = onehot.sum(1)
    pos = jnp.cumsum(per_tok, axis=0) - per_tok
    cnt = per_tok.sum(0)
    padded = (cnt + tme - 1) // tme * tme
    group_end = jnp.cumsum(padded)
    group_off = group_end - padded
    dense_off = jnp.cumsum(cnt) - cnt
    slot = jnp.take_along_axis(group_off[None, :] + pos, e_pair, axis=1)
    order = jnp.argsort(e_pair.reshape(-1), stable=True)
    sorted_tok = (order // 2).astype(jnp.int32)
    tile_start = jnp.arange(n_tiles, dtype=jnp.int32) * tme
    tile_expert = jnp.minimum(jnp.searchsorted(group_end, tile_start, side='right'), N_EXPERTS - 1).astype(jnp.int32)
    n_valid = (group_end[-1:] // tme).astype(jnp.int32)
    row = jnp.arange(n_tiles * tme, dtype=jnp.int32)
    row_e = jnp.repeat(tile_expert, tme)
    rank = row - group_off[row_e]
    src = jnp.where(rank < cnt[row_e], sorted_tok[jnp.clip(dense_off[row_e] + rank, 0, 2 * n - 1)], 0)
    xs = jnp.take(h, src, axis=0)
    ys = _expert_ffn(xs, tile_expert, n_valid, wg, wu, wd, tme)
    y1 = jnp.take(ys, slot[:, 0], axis=0)
    y2 = jnp.take(ys, slot[:, 1], axis=0)
    return _moe_combine(x2, y1, y2, gates, norm_final, tm)


def _rope_tables(t):
    rows = t // GRID_W
    pos = np.arange(rows * GRID_W)
    row = (pos // GRID_W).astype(np.float32)
    col = (pos % GRID_W).astype(np.float32)
    nf = HEAD_DIM // 4
    inv = jnp.asarray(ROPE_THETA, F32) ** (-jnp.arange(nf, dtype=F32) / nf)
    ar = jnp.asarray(row)[:, None] * inv
    ac = jnp.asarray(col)[:, None] * inv
    cos = jnp.concatenate([jnp.cos(ar), jnp.cos(ar), jnp.cos(ac), jnp.cos(ac)], axis=-1)
    sin = jnp.concatenate([-jnp.sin(ar), jnp.sin(ar), -jnp.sin(ac), jnp.sin(ac)], axis=-1)
    return jnp.tile(cos, (1, H_G)), jnp.tile(sin, (1, H_G))


def _pad_w_in(w):
    a, b_, c, d = 1024, 512, 1040, 1024
    w_a, w_b, w_c, w_d = w[:, :a], w[:, a:a + b_], w[:, a + b_:a + b_ + c], w[:, a + b_ + c:]
    gates = jnp.pad(w_c[:, 1024:], ((0, 0), (0, LANES - 16)))
    return jnp.concatenate([w_a, w_c[:, :1024], w_d, w_b, gates], axis=1).astype(BF16)


def _row(v):
    return v.reshape(1, -1).astype(F32)


def _trunk(x, p):
    b, t, _ = x.shape
    n = b * t
    tm = 256 if n % 256 == 0 else n
    cos, sin = _rope_tables(t)
    x2 = x.reshape(n, D_MODEL)
    depth = p['w_in'].shape[0]
    for l in range(depth):
        u2 = _in_proj(x2, _row(p['norm_mix'][l]), _pad_w_in(p['w_in'][l]), tm)
        u3 = u2.reshape(b, t, U_COLS)
        (r, v, kk, g, bonus, lwf, lwb, kf, kb, bf, bb) = _rwkv_prep(
            u3, _row(p['rwkv_mu'][l]), p['rwkv_w0'][l], p['rwkv_w2'][l], p['rwkv_a0'][l], p['rwkv_a2'][l],
            p['rwkv_g2'][l], _row(p['rwkv_kk'][l]), _row(p['rwkv_ka'][l]), _row(p['rwkv_rk'][l]))
        yf, yb = _rwkv_core(r, v, kk, lwf, lwb, kf, kb, bf, bb)
        at = _attention(u3, cos, sin, _row(jnp.tile(p['attn_q_norm'][l], H_G)),
                        _row(jnp.tile(p['attn_k_norm'][l], KV_ATTN)))
        gate_bias = jnp.pad(jnp.concatenate([p['mlstm_i_bias'][l].reshape(-1), p['mlstm_f_bias'][l].reshape(-1)]),
                            (0, LANES - 4 * H_G))
        hf, hb = _mlstm(u3, p['mlstm_conv_w'][l], _row(p['mlstm_conv_b'][l]), _row(gate_bias))
        of, ob = _retention(u3, cos, sin)
        flat = lambda z: z.reshape(n, W_G)
        x2 = _out_proj(x2, tuple(map(flat, (yf, yb, bonus, g))), flat(at), tuple(map(flat, (hf, hb))),
                       tuple(map(flat, (of, ob))), u2, _row(p['rwkv_ln'][l]), _row(p['mlstm_ln'][l]),
                       _row(p['ret_ln'][l]), p['w_out'][l].astype(BF16), tm)
        j = l // 2
        if l % 2 == 0:
            x2 = _ffn(x2, _row(p['norm_ffn'][l]), p['ffn_w_gate'][j].astype(BF16), p['ffn_w_up'][j].astype(BF16),
                      p['ffn_w_down'][j].astype(BF16), tm)
            if l == depth - 1:
                raise NotImplementedError("final norm after a dense FFN layer")
        else:
            wr = jnp.pad(p['moe_router'][j], ((0, 0), (0, LANES - N_EXPERTS)))
            h, gates, idx = _router(x2, _row(p['norm_ffn'][l]), wr, tm)
            if l != depth - 1:
                raise NotImplementedError("expert layer that is not the last layer")
            x2 = _moe(x2, h, gates, idx, p['moe_w_gate'][j].astype(BF16), p['moe_w_up'][j].astype(BF16),
                      p['moe_w_down'][j].astype(BF16), _row(p['norm_final']), tm)
    return x2.reshape(b, t, D_MODEL)


def kernel(x_prompt, x_sample, norm_mix, norm_ffn, norm_final, w_in, w_out, rwkv_mu, rwkv_w0, rwkv_w2,
           rwkv_a0, rwkv_a2, rwkv_g2, rwkv_kk, rwkv_ka, rwkv_rk, rwkv_ln, attn_q_norm, attn_k_norm,
           mlstm_conv_w, mlstm_conv_b, mlstm_i_bias, mlstm_f_bias, mlstm_ln, ret_ln, ffn_w_gate, ffn_w_up,
           ffn_w_down, moe_router, moe_w_gate, moe_w_up, moe_w_down):
    p = dict(norm_mix=norm_mix, norm_ffn=norm_ffn, norm_final=norm_final, w_in=w_in, w_out=w_out,
             rwkv_mu=rwkv_mu, rwkv_w0=rwkv_w0, rwkv_w2=rwkv_w2, rwkv_a0=rwkv_a0, rwkv_a2=rwkv_a2,
             rwkv_g2=rwkv_g2, rwkv_kk=rwkv_kk, rwkv_ka=rwkv_ka, rwkv_rk=rwkv_rk, rwkv_ln=rwkv_ln,
             attn_q_norm=attn_q_norm, attn_k_norm=attn_k_norm, mlstm_conv_w=mlstm_conv_w,
             mlstm_conv_b=mlstm_conv_b, mlstm_i_bias=mlstm_i_bias, mlstm_f_bias=mlstm_f_bias,
             mlstm_ln=mlstm_ln, ret_ln=ret_ln, ffn_w_gate=ffn_w_gate, ffn_w_up=ffn_w_up,
             ffn_w_down=ffn_w_down, moe_router=moe_router, moe_w_gate=moe_w_gate, moe_w_up=moe_w_up,
             moe_w_down=moe_w_down)
    nb = x_prompt.shape[0]
    y = _trunk(jnp.concatenate([x_prompt, x_sample], axis=0), p)
    return (y[:nb], y[nb:])
```

```python
import functools
import math

import numpy as np
import jax
import jax.numpy as jnp
from jax import lax
from jax.experimental import pallas as pl
from jax.experimental.pallas import tpu as pltpu

F32 = jnp.float32
BF16 = jnp.bfloat16

D_MODEL = 1024
HEAD_DIM = 64
W_G = 256
H_G = 4
KV_ATTN = 2
D_FF = 2816
N_EXPERTS = 8
NORM_EPS = 1e-6
HEAD_NORM_EPS = 1e-5
RWKV_GN_EPS = 64e-5
NEG_INF = -1e30
ROPE_THETA = 10000.0
GRID_W = 64

LANES = 128
ROW_ALIGN = 8
VMEM_LIMIT_BYTES = 56 * 1024 * 1024

U_RWKV = 0
U_MLSTM = 1024
U_RET = 2048
U_ATTN = 3072
U_GATE = 3584
U_COLS = 3712

RWKV_CHUNK = 64
RWKV_BLOCK = 256
RWKV_GROUP = 4
MIX_CHUNK = 128
EXPERT_TILE = 256


def _cparams(sem):
    return pltpu.CompilerParams(dimension_semantics=sem, vmem_limit_bytes=VMEM_LIMIT_BYTES)


def _bdot(a, b, dims):
    return lax.dot_general(a, b, (dims, ((), ())), preferred_element_type=F32)


def _mm(a, b):
    return _bdot(a.astype(BF16), b.astype(BF16), ((1,), (0,)))


def _mm_nt(a, b):
    return _bdot(a.astype(BF16), b.astype(BF16), ((1,), (1,)))


def _mm_tn(a, b):
    return _bdot(a.astype(BF16), b.astype(BF16), ((0,), (0,)))


def _split2(a):
    hi = a.astype(BF16)
    lo = (a - hi.astype(F32)).astype(BF16)
    return hi, lo


def _split3(a):
    hi = a.astype(BF16)
    r = a - hi.astype(F32)
    mid = r.astype(BF16)
    lo = (r - mid.astype(F32)).astype(BF16)
    return hi, mid, lo


def _mm_l2(a, b_exact):
    hi, lo = _split2(a)
    return _bdot(hi, b_exact, ((1,), (0,))) + _bdot(lo, b_exact, ((1,), (0,)))


def _mm_l3(a, b_exact):
    h, m, l = _split3(a)
    return _bdot(h, b_exact, ((1,), (0,))) + _bdot(m, b_exact, ((1,), (0,))) + _bdot(l, b_exact, ((1,), (0,)))


def _mm_r3(a_exact, b):
    h, m, l = _split3(b)
    return _bdot(a_exact, h, ((1,), (0,))) + _bdot(a_exact, m, ((1,), (0,))) + _bdot(a_exact, l, ((1,), (0,)))


def _mm_x3(a, b):
    ah, al = _split2(a)
    bh, bl = _split2(b)
    d = ((1,), (0,))
    return _bdot(ah, bh, d) + _bdot(ah, bl, d) + _bdot(al, bh, d)


def _iota2(shape, axis):
    return lax.broadcasted_iota(jnp.int32, shape, axis)


def _head_mean_matrix(width):
    r = _iota2((width, width), 0) // HEAD_DIM
    c = _iota2((width, width), 1) // HEAD_DIM
    return jnp.where(r == c, 1.0 / HEAD_DIM, 0.0).astype(BF16)


def _head_mean(z, bd):
    return _mm_l2(z, bd)


def _sigmoid(x):
    return 1.0 / (1.0 + jnp.exp(-x))


def _silu(x):
    return x * _sigmoid(x)


def _log_sigmoid(x):
    return jnp.minimum(x, 0.0) - jnp.log(1.0 + jnp.exp(-jnp.abs(x)))


def _rms_norm_rows(x, gain):
    ms = jnp.mean(x * x, axis=-1, keepdims=True)
    return x * lax.rsqrt(ms + NORM_EPS) * gain


def _rope_swap(z):
    w = z.shape[-1]
    lane = _iota2(z.shape, z.ndim - 1)
    fwd = pltpu.roll(z, w - 16, z.ndim - 1)
    bwd = pltpu.roll(z, 16, z.ndim - 1)
    return jnp.where((lane % 32) < 16, fwd, bwd)


def _rope(z, cos, sin):
    return z * cos + _rope_swap(z) * sin


def _shift_rows(x, prev_row, next_row):
    n = x.shape[0]
    row = _iota2(x.shape, 0)
    prev = jnp.where(row == 0, prev_row, pltpu.roll(x, 1, 0))
    nxt = jnp.where(row == n - 1, next_row, pltpu.roll(x, n - 1, 0))
    return prev, nxt


def _part_tiles(parts, tm):
    return tuple(p.shape[0] // tm for p in parts)


def _part_specs(parts, tm, width):
    specs, start = [], 0
    for nt in _part_tiles(parts, tm):
        specs.append(pl.BlockSpec((tm, width), lambda i, s=start, nt=nt: (jnp.clip(i - s, 0, nt - 1), 0)))
        start += nt
    return specs


def _part_tile(i, refs, tiles):
    x = refs[0][...]
    start = tiles[0]
    for ref, nt in zip(refs[1:], tiles[1:]):
        x = jnp.where(i >= start, ref[...], x)
        start += nt
    return x


def _inproj_kernel(*refs, tiles):
    x_refs, (g_ref, w_ref, o_ref) = refs[:len(tiles)], refs[len(tiles):]
    h = _rms_norm_rows(_part_tile(pl.program_id(0), x_refs, tiles), g_ref[...])
    o_ref[...] = jnp.dot(h.astype(BF16), w_ref[...], preferred_element_type=F32)


def _in_proj(x_parts, gain, w_pad, tm):
    tiles = _part_tiles(x_parts, tm)
    n = sum(tiles) * tm
    return pl.pallas_call(
        functools.partial(_inproj_kernel, tiles=tiles),
        out_shape=jax.ShapeDtypeStruct((n, U_COLS), F32),
        grid=(n // tm,),
        in_specs=_part_specs(x_parts, tm, D_MODEL) + [
            pl.BlockSpec((1, D_MODEL), lambda i: (0, 0)),
            pl.BlockSpec((D_MODEL, U_COLS), lambda i: (0, 0), pipeline_mode=pl.Buffered(1)),
        ],
        out_specs=pl.BlockSpec((tm, U_COLS), lambda i: (i, 0)),
        compiler_params=_cparams(("arbitrary",)),
        name="in_proj",
    )(*x_parts, gain, w_pad)


def _attn_kernel(u_ref, cos_ref, sin_ref, qg_ref, kg_ref, o_ref, q_scr, k_scr, v_scr, *, tq):
    t = u_ref.shape[1]
    u = u_ref[0]
    q = u[:, :W_G]
    k = u[:, W_G:W_G + KV_ATTN * HEAD_DIM]
    v = u[:, W_G + KV_ATTN * HEAD_DIM:]
    cos = cos_ref[...]
    sin = sin_ref[...]
    bd_q = _head_mean_matrix(W_G)
    bd_k = _head_mean_matrix(KV_ATTN * HEAD_DIM)
    qn = q * lax.rsqrt(_head_mean(q * q, bd_q) + NORM_EPS) * qg_ref[...]
    kn = k * lax.rsqrt(_head_mean(k * k, bd_k) + NORM_EPS) * kg_ref[...]
    kw = KV_ATTN * HEAD_DIM
    q_scr[...] = (_rope(qn, cos, sin) * (HEAD_DIM ** -0.5 * math.log2(math.e))).astype(BF16)
    k_scr[...] = _rope(kn, cos[:, :kw], sin[:, :kw]).astype(BF16)
    ones = jnp.ones((t, HEAD_DIM), BF16)
    for j in range(KV_ATTN):
        vj = v[:, j * HEAD_DIM:(j + 1) * HEAD_DIM].astype(BF16)
        v_scr[:, j * LANES:(j + 1) * LANES] = jnp.concatenate([vj, ones], axis=1)
    group = H_G // KV_ATTN

    def q_tile(i, carry):
        rows = pl.ds(pl.multiple_of(i * tq, tq), tq)
        for j in range(KV_ATTN):
            kj = k_scr[:, j * HEAD_DIM:(j + 1) * HEAD_DIM]
            vj = v_scr[:, j * LANES:(j + 1) * LANES]
            for g in range(group):
                h = j * group + g
                qh = q_scr[rows, h * HEAD_DIM:(h + 1) * HEAD_DIM]
                s = _bdot(qh, kj, ((1,), (1,)))
                m = jnp.max(s, axis=-1, keepdims=True)
                p = jnp.exp2(s - m)
                r = jnp.dot(p.astype(BF16), vj, preferred_element_type=F32)
                o_ref[0, rows, h * HEAD_DIM:(h + 1) * HEAD_DIM] = r[:, :HEAD_DIM] / r[:, HEAD_DIM:]
        return carry

    lax.fori_loop(0, t // tq, q_tile, 0)


def _attention(u3, cos, sin, q_gain, k_gain):
    b, t, _ = u3.shape
    tq = min(256, t)
    col = U_ATTN // 512
    return pl.pallas_call(
        functools.partial(_attn_kernel, tq=tq),
        out_shape=jax.ShapeDtypeStruct((b, t, W_G), F32),
        grid=(b,),
        in_specs=[
            pl.BlockSpec((1, t, 512), lambda i: (i, 0, col)),
            pl.BlockSpec((t, W_G), lambda i: (0, 0)),
            pl.BlockSpec((t, W_G), lambda i: (0, 0)),
            pl.BlockSpec((1, W_G), lambda i: (0, 0)),
            pl.BlockSpec((1, KV_ATTN * HEAD_DIM), lambda i: (0, 0)),
        ],
        out_specs=pl.BlockSpec((1, t, W_G), lambda i: (i, 0, 0)),
        scratch_shapes=[
            pltpu.VMEM((t, W_G), BF16),
            pltpu.VMEM((t, KV_ATTN * HEAD_DIM), BF16),
            pltpu.VMEM((t, KV_ATTN * LANES), BF16),
        ],
        compiler_params=_cparams(("parallel",)),
        name="attention",
    )(u3, cos, sin, q_gain, k_gain)


def _ret_log_gamma(direction):
    return [math.log1p(-2.0 ** (-5.0 - (2 * h + direction) / 2.0)) for h in range(H_G)]


def _ret_kernel(uf_ref, ub_ref, cf_ref, sf_ref, cb_ref, sb_ref, of_ref, ob_ref, rf_scr, rb_scr):
    i = pl.program_id(1)
    c = MIX_CHUNK

    @pl.when(i == 0)
    def _():
        rf_scr[...] = jnp.zeros_like(rf_scr)
        rb_scr[...] = jnp.zeros_like(rb_scr)

    tt = _iota2((c, c), 0)
    ss = _iota2((c, c), 1)
    diff = (tt - ss).astype(F32)
    jcol = _iota2((c, HEAD_DIM), 0).astype(F32)

    uf = uf_ref[0]
    qf = _rope(uf[:, :W_G], cf_ref[...], sf_ref[...])
    kf = _rope(uf[:, W_G:2 * W_G], cf_ref[...], sf_ref[...]) * HEAD_DIM ** -0.5
    vf = uf[:, 2 * W_G:3 * W_G]
    lg_f = _ret_log_gamma(0)
    lg_b = _ret_log_gamma(1)
    for h in range(H_G):
        sl = slice(h * HEAD_DIM, (h + 1) * HEAD_DIM)
        q, k, v = qf[:, sl], kf[:, sl], vf[:, sl]
        decay = (jnp.where(tt >= ss, jnp.exp(diff * lg_f[h]), 0.0)
                 + jnp.where(ss >= tt, jnp.exp(-diff * lg_b[h]), 0.0))
        intra = _mm(_mm_nt(q, k) * decay, v)
        r_prev = rf_scr[h]
        inter = _mm(q * jnp.exp((jcol + 1.0) * lg_f[h]), r_prev)
        of_ref[0, :, sl] = intra + inter
        k_in = k * jnp.exp((c - 1.0 - jcol) * lg_f[h])
        rf_scr[h] = math.exp(c * lg_f[h]) * r_prev + _mm_tn(k_in, v)

    ub = ub_ref[0]
    qb = _rope(ub[:, :W_G], cb_ref[...], sb_ref[...])
    kb = _rope(ub[:, W_G:2 * W_G], cb_ref[...], sb_ref[...]) * HEAD_DIM ** -0.5
    vb = ub[:, 2 * W_G:3 * W_G]
    for h in range(H_G):
        sl = slice(h * HEAD_DIM, (h + 1) * HEAD_DIM)
        q, k, v = qb[:, sl], kb[:, sl], vb[:, sl]
        r_prev = rb_scr[h]
        ob_ref[0, :, sl] = _mm(q * jnp.exp((c - jcol) * lg_b[h]), r_prev)
        k_in = k * jnp.exp(jcol * lg_b[h])
        rb_scr[h] = math.exp(c * lg_b[h]) * r_prev + _mm_tn(k_in, v)


def _retention(u3, cos, sin):
    b, t, _ = u3.shape
    c = MIX_CHUNK
    nblk = t // c
    col = U_RET // 1024
    tab = lambda rev: pl.BlockSpec((c, W_G), (lambda bi, i: (nblk - 1 - i, 0)) if rev else (lambda bi, i: (i, 0)))
    return pl.pallas_call(
        _ret_kernel,
        out_shape=(jax.ShapeDtypeStruct((b, t, W_G), F32), jax.ShapeDtypeStruct((b, t, W_G), F32)),
        grid=(b, nblk),
        in_specs=[
            pl.BlockSpec((1, c, 1024), lambda bi, i: (bi, i, col)),
            pl.BlockSpec((1, c, 1024), lambda bi, i: (bi, nblk - 1 - i, col)),
            tab(False), tab(False), tab(True), tab(True),
        ],
        out_specs=(
            pl.BlockSpec((1, c, W_G), lambda bi, i: (bi, i, 0)),
            pl.BlockSpec((1, c, W_G), lambda bi, i: (bi, nblk - 1 - i, 0)),
        ),
        scratch_shapes=[pltpu.VMEM((H_G, HEAD_DIM, HEAD_DIM), F32), pltpu.VMEM((H_G, HEAD_DIM, HEAD_DIM), F32)],
        compiler_params=_cparams(("parallel", "arbitrary")),
        name="retention",
    )(u3, u3, cos, sin, cos, sin)


def _mlstm_tile(u_ref, up_ref, un_ref, g_ref, cw_ref, cb_ref, gb_ref, blk, nblk, direction):
    c = MIX_CHUNK
    reverse = direction == 1
    u = u_ref[0]
    qk = u[:, :2 * W_G]
    prev_row = jnp.where(blk == 0, 0.0, up_ref[0][ROW_ALIGN - 1:ROW_ALIGN, :])
    next_row = jnp.where(blk == nblk - 1, 0.0, un_ref[0][0:1, :])
    prev, nxt = _shift_rows(qk, prev_row, next_row)
    cw = cw_ref[...]
    qk = _silu(cw[0:1] * prev + cw[1:2] * qk + cw[2:3] * nxt + cb_ref[...])
    qa = qk[:, :W_G]
    ka = qk[:, W_G:] * HEAD_DIM ** -0.5
    va = u[:, 2 * W_G:3 * W_G]

    x = g_ref[0] + gb_ref[...]
    xt = x.T
    lf_c = _log_sigmoid(x)
    lf_r = _log_sigmoid(xt)
    tt = _iota2((c, c), 0)
    ss = _iota2((c, c), 1)
    lower = jnp.where(ss <= tt, 1.0, 0.0).astype(BF16)
    upper = jnp.where(tt <= ss, 1.0, 0.0).astype(BF16)
    if reverse:
        b_c = _mm_r3(upper, lf_c)
        b_r = _mm_l3(lf_r, lower)
        mask = ss >= tt
    else:
        b_c = _mm_r3(lower, lf_c)
        b_r = _mm_l3(lf_r, upper)
        mask = ss <= tt
    g_all = jnp.sum(lf_c, axis=0, keepdims=True)
    return dict(q=qa, k=ka, v=va, x=x, xt=xt, b_c=b_c, b_r=b_r, g_all=g_all, mask=mask)


def _mlstm_kernel(uf_ref, upf_ref, unf_ref, gf_ref, ub_ref, upb_ref, unb_ref, gbk_ref,
                  cw_ref, cb_ref, gb_ref, of_ref, ob_ref, st_scr, m_scr):
    i = pl.program_id(1)
    nblk = pl.num_programs(1)
    c = MIX_CHUNK

    @pl.when(i == 0)
    def _():
        st_scr[...] = jnp.zeros_like(st_scr)
        m_scr[...] = jnp.zeros_like(m_scr)

    tiles = (_mlstm_tile(uf_ref, upf_ref, unf_ref, gf_ref, cw_ref, cb_ref, gb_ref, i, nblk, 0),
             _mlstm_tile(ub_ref, upb_ref, unb_ref, gbk_ref, cw_ref, cb_ref, gb_ref, nblk - 1 - i, nblk, 1))
    o_refs = (of_ref, ob_ref)
    probs = [(d, h) for d in range(2) for h in range(H_G)]
    sl = lambda h: slice(h * HEAD_DIM, (h + 1) * HEAD_DIM)
    ci = lambda d, h: d * H_G + h
    cf = lambda d, h: 2 * H_G + d * H_G + h
    q = [tiles[d]["q"][:, sl(h)] for d, h in probs]
    k = [tiles[d]["k"][:, sl(h)] for d, h in probs]
    v = [tiles[d]["v"][:, sl(h)] for d, h in probs]
    bc = [tiles[d]["b_c"][:, cf(d, h):cf(d, h) + 1] for d, h in probs]
    br = [tiles[d]["b_r"][cf(d, h):cf(d, h) + 1, :] for d, h in probs]
    li_r = [tiles[d]["xt"][ci(d, h):ci(d, h) + 1, :] for d, h in probs]
    li_c = [tiles[d]["x"][:, ci(d, h):ci(d, h) + 1] for d, h in probs]
    g = [tiles[d]["g_all"][:, cf(d, h):cf(d, h) + 1] for d, h in probs]
    m_prev = [m_scr[ci(d, h):ci(d, h) + 1, 0:1] for d, h in probs]
    state = [st_scr[ci(d, h)] for d, h in probs]
    ones = jnp.ones((c, HEAD_DIM), F32)
    v_aug = [jnp.concatenate([v_, ones], axis=1) for v_ in v]

    qk = [_mm_nt(q_, k_) for q_, k_ in zip(q, k)]
    qs = [_mm(q_, s_) for q_, s_ in zip(q, state)]
    dlog = [jnp.where(tiles[d]["mask"], bc_ - br_ + li_, NEG_INF) for (d, _), bc_, br_, li_ in zip(probs, bc, br, li_r)]
    inter_log = [bc_ + m_ for bc_, m_ in zip(bc, m_prev)]
    m_t = [jnp.maximum(il, jnp.max(dl, axis=-1, keepdims=True)) for il, dl in zip(inter_log, dlog)]
    sc = [qk_ * jnp.exp(dl - mt) for qk_, dl, mt in zip(qk, dlog, m_t)]
    res = [_mm(sc_, va) + jnp.exp(il - mt) * qs_ for sc_, va, il, mt, qs_ in zip(sc, v_aug, inter_log, m_t, qs)]
    a_c = [g_ - bc_ + li_ for g_, bc_, li_ in zip(g, bc, li_c)]
    m_new = [jnp.maximum(g_ + m_, jnp.max(a_, axis=0, keepdims=True)) for g_, m_, a_ in zip(g, m_prev, a_c)]
    kw = [k_ * jnp.exp(a_ - mn) for k_, a_, mn in zip(k, a_c, m_new)]
    upd = [_mm_tn(kw_, va) for kw_, va in zip(kw, v_aug)]
    for n, (d, h) in enumerate(probs):
        den = jnp.maximum(jnp.abs(res[n][:, HEAD_DIM:]), jnp.exp(-m_t[n]))
        o_refs[d][0, :, sl(h)] = res[n][:, :HEAD_DIM] / den
    for n, (d, h) in enumerate(probs):
        st_scr[ci(d, h)] = jnp.exp(g[n] + m_prev[n] - m_new[n]) * state[n] + upd[n]
        m_scr[ci(d, h):ci(d, h) + 1, :] = jnp.broadcast_to(m_new[n], (1, LANES))


def _mlstm(u3, conv_w, conv_b, gate_bias):
    b, t, _ = u3.shape
    c = MIX_CHUNK
    nblk = t // c
    rpb = c // ROW_ALIGN
    n8 = t // ROW_ALIGN
    col = U_MLSTM // 1024
    hcol = U_MLSTM // 512
    gcol = U_GATE // LANES

    def specs(rev):
        blk = (lambda i: nblk - 1 - i) if rev else (lambda i: i)
        return [
            pl.BlockSpec((1, c, 1024), lambda bi, i: (bi, blk(i), col)),
            pl.BlockSpec((1, ROW_ALIGN, 512), lambda bi, i: (bi, jnp.maximum(blk(i) * rpb - 1, 0), hcol)),
            pl.BlockSpec((1, ROW_ALIGN, 512), lambda bi, i: (bi, jnp.minimum((blk(i) + 1) * rpb, n8 - 1), hcol)),
            pl.BlockSpec((1, c, LANES), lambda bi, i: (bi, blk(i), gcol)),
        ]

    const = lambda shape: pl.BlockSpec(shape, lambda bi, i: (0,) * len(shape))
    return pl.pallas_call(
        _mlstm_kernel,
        out_shape=(jax.ShapeDtypeStruct((b, t, W_G), F32), jax.ShapeDtypeStruct((b, t, W_G), F32)),
        grid=(b, nblk),
        in_specs=specs(False) + specs(True) + [const((3, 2 * W_G)), const((1, 2 * W_G)), const((1, LANES))],
        out_specs=(
            pl.BlockSpec((1, c, W_G), lambda bi, i: (bi, i, 0)),
            pl.BlockSpec((1, c, W_G), lambda bi, i: (bi, nblk - 1 - i, 0)),
        ),
        scratch_shapes=[pltpu.VMEM((2 * H_G, HEAD_DIM, LANES), F32), pltpu.VMEM((2 * H_G, LANES), F32)],
        compiler_params=_cparams(("parallel", "arbitrary")),
        name="mlstm",
    )(u3, u3, u3, u3, u3, u3, u3, u3, conv_w, conv_b, gate_bias)


def _rwkv_prep_kernel(u_ref, up_ref, un_ref, mu_ref, w0_ref, w2_ref, a0_ref, a2_ref, g2_ref, kks_ref, ka_ref, rk_ref,
                      r_ref, v_ref, kk_ref, g_ref, bonus_ref, lwf_ref, lwb_ref, kf_ref, kb_ref, bf_ref, bb_ref):
    i = pl.program_id(1)
    nblk = pl.num_programs(1)
    u = u_ref[0]
    prev_row = jnp.where(i == 0, 0.0, up_ref[0][ROW_ALIGN - 1:ROW_ALIGN, :])
    next_row = jnp.where(i == nblk - 1, 0.0, un_ref[0][0:1, :])
    prev, nxt = _shift_rows(u, prev_row, next_row)
    us = u + mu_ref[...] * (0.5 * (prev + nxt) - u)
    r = us[:, 0:W_G]
    k = us[:, W_G:2 * W_G]
    v = us[:, 2 * W_G:3 * W_G]
    xw = us[:, 3 * W_G:3 * W_G + 64]
    xa = us[:, 3 * W_G + 64:3 * W_G + 128]
    xg = us[:, 3 * W_G + 128:]
    bd = _head_mean_matrix(W_G)
    g = _mm(_sigmoid(xg), g2_ref[...])
    lw = jnp.tanh(xw)
    a_lr = _mm_x3(xa, a2_ref[...])
    kk = k * kks_ref[...]
    kk = kk * lax.rsqrt(_head_mean(kk * kk, bd) * HEAD_DIM + 1e-12)
    r_ref[0] = r
    v_ref[0] = v
    kk_ref[0] = kk
    g_ref[0] = g
    bonus = jnp.zeros_like(r)
    for d, (lw_ref, k_ref, b_ref) in enumerate(((lwf_ref, kf_ref, bf_ref), (lwb_ref, kb_ref, bb_ref))):
        z = w0_ref[d:d + 1, :] + _mm_x3(lw, w2_ref[d])
        lw_ref[0] = -_sigmoid(z) * math.exp(-0.5)
        a = _sigmoid(a0_ref[d:d + 1, :] + a_lr)
        kd = k * (1.0 + (a - 1.0) * ka_ref[...])
        k_ref[0] = kd
        b_ref[0] = kk * a
        bonus = bonus + _head_mean(r * kd * rk_ref[...], bd) * HEAD_DIM * v
    bonus_ref[0] = bonus


def _rwkv_prep(u3, mu, w0, w2, a0, a2, g2, kks, ka, rk):
    b, t, _ = u3.shape
    tb = min(RWKV_BLOCK, t)
    nblk = t // tb
    rpb = tb // ROW_ALIGN
    n8 = t // ROW_ALIGN
    const = lambda shape: pl.BlockSpec(shape, lambda bi, i: (0,) * len(shape))
    out = jax.ShapeDtypeStruct((b, t, W_G), F32)
    ospec = pl.BlockSpec((1, tb, W_G), lambda bi, i: (bi, i, 0))
    return pl.pallas_call(
        _rwkv_prep_kernel,
        out_shape=(out,) * 11,
        grid=(b, nblk),
        in_specs=[
            pl.BlockSpec((1, tb, 1024), lambda bi, i: (bi, i, 0)),
            pl.BlockSpec((1, ROW_ALIGN, 1024), lambda bi, i: (bi, jnp.maximum(i * rpb - 1, 0), 0)),
            pl.BlockSpec((1, ROW_ALIGN, 1024), lambda bi, i: (bi, jnp.minimum((i + 1) * rpb, n8 - 1), 0)),
            const((1, 1024)), const((2, W_G)), const((2, 64, W_G)), const((2, W_G)), const((64, W_G)),
            const((128, W_G)), const((1, W_G)), const((1, W_G)), const((1, W_G)),
        ],
        out_specs=(ospec,) * 11,
        compiler_params=_cparams(("parallel", "parallel")),
        name="rwkv_prep",
    )(u3, u3, u3, mu, w0, w2, a0, a2, g2, kks, ka, rk)


def _tri_inverse_all(lmats, n):
    r = _iota2((n, n), 0)
    c = _iota2((n, n), 1)
    eye = jnp.where(r == c, 1.0, 0.0)
    pair = (r // 2 == c // 2) & (r != c)
    invs = [eye + jnp.where(pair, lm, 0.0) for lm in lmats]
    s = 2
    while s < n:
        sel = (r // (2 * s) == c // (2 * s)) & (r // s != c // s)
        offs = [jnp.where(sel, -lm, 0.0) for lm in lmats]
        xs = [_mm(inv, off) for inv, off in zip(invs, offs)]
        invs = [inv - _mm(x, inv) for inv, x in zip(invs, xs)]
        s *= 2
    return invs


def _rwkv_tile_terms(r, k, v, kk, b, lw, reverse):
    c = r.shape[0]
    tt = _iota2((c, c), 0)
    ss = _iota2((c, c), 1)
    tri = jnp.where((tt <= ss) if reverse else (ss <= tt), 1.0, 0.0).astype(BF16)
    cum_in = _mm_r3(tri, lw)
    cum_all = jnp.sum(lw, axis=0, keepdims=True)
    e_neg = jnp.exp(-cum_in)
    e_end = jnp.exp(cum_all - cum_in)
    return dict(at=-kk * jnp.exp(cum_in - lw), rt=r * jnp.exp(cum_in), bt=b * e_neg, kt=k * e_neg,
                gb=b * e_end, gk=k * e_end, v=v, e_all=jnp.exp(cum_all))


def _rwkv_chunk_terms(tiles, reverses):
    c = RWKV_CHUNK
    tt = _iota2((c, c), 0)
    ss = _iota2((c, c), 1)
    heads = [(ti, h) for ti in range(len(tiles)) for h in range(H_G)]
    sl = lambda h: slice(h * HEAD_DIM, (h + 1) * HEAD_DIM)
    get = lambda name: [tiles[ti][name][:, sl(h)] for ti, h in heads]
    at, rt, bt, kt, gb, gk, v = (get(nm) for nm in ("at", "rt", "bt", "kt", "gb", "gk", "v"))
    strict = [(ss > tt) if reverses[ti] else (ss < tt) for ti, _ in heads]
    incl = [(ss >= tt) if reverses[ti] else (ss <= tt) for ti, _ in heads]
    ps = [_mm_nt(jnp.concatenate([a, r_], axis=0), jnp.concatenate([b_, k_], axis=0))
          for a, r_, b_, k_ in zip(at, rt, bt, kt)]
    l_ab = [jnp.where(m, p[:c, :c], 0.0) for m, p in zip(strict, ps)]
    l_ak = [jnp.where(m, p[:c, c:], 0.0) for m, p in zip(strict, ps)]
    m_r = [jnp.concatenate([jnp.where(m, p[c:, :c], 0.0), jnp.where(m, p[c:, c:], 0.0)], axis=1)
           for m, p in zip(incl, ps)]
    lakv = [_mm(l, v_) for l, v_ in zip(l_ak, v)]
    invs = _tri_inverse_all(l_ab, c)
    tw = [_mm(inv, jnp.concatenate([a, lv], axis=1)) for inv, a, lv in zip(invs, at, lakv)]
    zeros = jnp.zeros((c, HEAD_DIM), F32)
    mm2 = [_mm(m, jnp.concatenate([t_, jnp.concatenate([zeros, v_], axis=1)], axis=0))
           for m, t_, v_ in zip(m_r, tw, v)]
    r1 = [r_ + m[:, :HEAD_DIM] for r_, m in zip(rt, mm2)]
    y0 = [m[:, HEAD_DIM:] for m in mm2]
    twg = [_mm_tn(t_, g_) for t_, g_ in zip(tw, gb)]
    vgk = [_mm_tn(v_, g_) for v_, g_ in zip(v, gk)]
    mlow = [x[:HEAD_DIM] for x in twg]
    nadd = [x[HEAD_DIM:] + y for x, y in zip(twg, vgk)]
    e_all = [tiles[ti]["e_all"][:, sl(h)] for ti, h in heads]
    return r1, y0, mlow, nadd, e_all


def _rwkv_core_kernel(rf_ref, kf_ref, vf_ref, kkf_ref, bf_ref, lwf_ref, rb_ref, kb_ref, vb_ref, kkb_ref, bb_ref,
                      lwb_ref, yf_ref, yb_ref, s_scr, *, group):
    i = pl.program_id(1)
    c = RWKV_CHUNK
    nch = rf_ref.shape[1] // c
    dirs = ((rf_ref, kf_ref, vf_ref, kkf_ref, bf_ref, lwf_ref), (rb_ref, kb_ref, vb_ref, kkb_ref, bb_ref, lwb_ref))
    y_refs = (yf_ref, yb_ref)

    @pl.when(i == 0)
    def _():
        s_scr[...] = jnp.zeros_like(s_scr)

    def step(j, states):
        tiles, reverses, rows = [], [], []
        for q in range(group):
            for d in range(2):
                cj = j * group + q
                cj = cj if d == 0 else nch - 1 - cj
                rw = pl.ds(pl.multiple_of(cj * c, c), c)
                tiles.append(_rwkv_tile_terms(*(ref[0, rw, :] for ref in dirs[d]), d == 1))
                reverses.append(d == 1)
                rows.append(rw)
        r1, y0, mlow, nadd, e_all = _rwkv_chunk_terms(tiles, reverses)
        states = list(states)
        for q in range(group):
            ys = [[], []]
            for d in range(2):
                for h in range(H_G):
                    n = (q * 2 + d) * H_G + h
                    s = states[d * H_G + h]
                    ys[d].append(_mm_nt(r1[n], s) + y0[n])
                    states[d * H_G + h] = s * e_all[n] + _mm(s, mlow[n]) + nadd[n]
            for d in range(2):
                y_refs[d][0, rows[q * 2 + d], :] = jnp.concatenate(ys[d], axis=1)
        return tuple(states)

    init = tuple(s_scr[n] for n in range(2 * H_G))
    if nch == group:
        states = step(0, init)
    else:
        states = lax.fori_loop(0, nch // group, step, init)
    for n in range(2 * H_G):
        s_scr[n] = states[n]


def _rwkv_core(r, v, kk, lwf, lwb, kf, kb, bf, bb):
    b, t, _ = r.shape
    tb = min(RWKV_BLOCK, t)
    nblk = t // tb
    fwd = pl.BlockSpec((1, tb, W_G), lambda bi, i: (bi, i, 0))
    bwd = pl.BlockSpec((1, tb, W_G), lambda bi, i: (bi, nblk - 1 - i, 0))
    out = jax.ShapeDtypeStruct((b, t, W_G), F32)
    return pl.pallas_call(
        functools.partial(_rwkv_core_kernel, group=min(RWKV_GROUP, tb // RWKV_CHUNK)),
        out_shape=(out, out),
        grid=(b, nblk),
        in_specs=[fwd] * 6 + [bwd] * 6,
        out_specs=(fwd, bwd),
        scratch_shapes=[pltpu.VMEM((2 * H_G, HEAD_DIM, HEAD_DIM), F32)],
        compiler_params=_cparams(("parallel", "arbitrary")),
        name="rwkv_core",
    )(r, kf, v, kk, bf, lwf, r, kb, v, kk, bb, lwb)


def _group_norm(y, gain, eps, bd):
    mu = _head_mean(y, bd)
    d = y - mu
    var = _head_mean(d * d, bd)
    return d * lax.rsqrt(var + eps) * gain


def _outproj_kernel(*refs, tiles):
    x_refs = refs[:len(tiles)]
    (ryf_ref, ryb_ref, rbon_ref, rg_ref, at_ref, mhf_ref, mhb_ref, mo_ref, tof_ref, tob_ref,
     tg_ref, rln_ref, mln_ref, tln_ref, w_ref, o_ref) = refs[len(tiles):]
    bd = _head_mean_matrix(W_G)
    o_a = (_group_norm(ryf_ref[...] + ryb_ref[...], rln_ref[...], RWKV_GN_EPS, bd) + rbon_ref[...]) * rg_ref[...]
    o_c = _group_norm(mhf_ref[...] + mhb_ref[...], mln_ref[...], HEAD_NORM_EPS, bd) * _sigmoid(mo_ref[...])
    o_d = _group_norm(tof_ref[...] + tob_ref[...], tln_ref[...], HEAD_NORM_EPS, bd) * _silu(tg_ref[...])
    mix = jnp.concatenate([o_a, at_ref[...], o_c, o_d], axis=1).astype(BF16)
    o_ref[...] = _part_tile(pl.program_id(0), x_refs, tiles) + jnp.dot(mix, w_ref[...], preferred_element_type=F32)


def _out_proj(x_parts, rw, at, ml, rt, u2, rln, mln, tln, w_out, tm):
    tiles = _part_tiles(x_parts, tm)
    n = sum(tiles) * tm
    row = lambda w: pl.BlockSpec((tm, w), lambda i: (i, 0))
    ucol = lambda off: pl.BlockSpec((tm, W_G), lambda i: (i, off // W_G))
    const = lambda shape: pl.BlockSpec(shape, lambda i: (0, 0))
    return pl.pallas_call(
        functools.partial(_outproj_kernel, tiles=tiles),
        out_shape=jax.ShapeDtypeStruct((n, D_MODEL), F32),
        grid=(n // tm,),
        in_specs=_part_specs(x_parts, tm, D_MODEL) + [row(W_G)] * 7 + [ucol(U_MLSTM + 3 * W_G)] + [row(W_G)] * 2
                 + [ucol(U_RET + 3 * W_G)] + [const((1, W_G))] * 3
                 + [pl.BlockSpec((D_MODEL, D_MODEL), lambda i: (0, 0), pipeline_mode=pl.Buffered(1))],
        out_specs=row(D_MODEL),
        compiler_params=_cparams(("arbitrary",)),
        name="out_proj",
    )(*x_parts, *rw, at, *ml, u2, *rt, u2, rln, mln, tln, w_out)


def _ffn_kernel(x_ref, g_ref, wg_ref, wu_ref, wd_ref, o_ref):
    x = x_ref[...]
    h = _rms_norm_rows(x, g_ref[...]).astype(BF16)
    a = jnp.dot(h, wg_ref[...], preferred_element_type=F32)
    b = jnp.dot(h, wu_ref[...], preferred_element_type=F32)
    z = (_silu(a) * b).astype(BF16)
    o_ref[...] = x + jnp.dot(z, wd_ref[...], preferred_element_type=F32)


def _ffn(x2, gain, wg, wu, wd, tm):
    n = x2.shape[0]
    res = lambda shape: pl.BlockSpec(shape, lambda i: (0, 0), pipeline_mode=pl.Buffered(1))
    return pl.pallas_call(
        _ffn_kernel,
        out_shape=jax.ShapeDtypeStruct((n, D_MODEL), F32),
        grid=(n // tm,),
        in_specs=[pl.BlockSpec((tm, D_MODEL), lambda i: (i, 0)), pl.BlockSpec((1, D_MODEL), lambda i: (0, 0)),
                  res((D_MODEL, D_FF)), res((D_MODEL, D_FF)), res((D_FF, D_MODEL))],
        out_specs=pl.BlockSpec((tm, D_MODEL), lambda i: (i, 0)),
        compiler_params=_cparams(("parallel",)),
        name="ffn",
    )(x2, gain, wg, wu, wd)


def _router_kernel(x_ref, g_ref, wr_ref, h_ref, gate_ref, idx_ref):
    h = _rms_norm_rows(x_ref[...], g_ref[...])
    h_ref[...] = h.astype(BF16)
    logits = _mm_x3(h, wr_ref[...])
    lane = _iota2(logits.shape, 1)
    logits = jnp.where(lane < N_EXPERTS, logits, NEG_INF)
    e = jnp.exp(logits - jnp.max(logits, axis=-1, keepdims=True))
    p = e / jnp.sum(e, axis=-1, keepdims=True)
    p = jnp.where(lane < N_EXPERTS, p, -1.0)
    m1 = jnp.max(p, axis=-1, keepdims=True)
    i1 = jnp.min(jnp.where(p == m1, lane, LANES), axis=-1, keepdims=True)
    p2 = jnp.where(lane == i1, -1.0, p)
    m2 = jnp.max(p2, axis=-1, keepdims=True)
    i2 = jnp.min(jnp.where(p2 == m2, lane, LANES), axis=-1, keepdims=True)
    tot = m1 + m2
    gate_ref[...] = jnp.where(lane == 0, m1 / tot, jnp.where(lane == 1, m2 / tot, 0.0))
    idx_ref[...] = jnp.where(lane == 0, i1, jnp.where(lane == 1, i2, 0))


def _router(x2, gain, wr_pad, tm):
    n = x2.shape[0]
    return pl.pallas_call(
        _router_kernel,
        out_shape=(jax.ShapeDtypeStruct((n, D_MODEL), BF16), jax.ShapeDtypeStruct((n, LANES), F32),
                   jax.ShapeDtypeStruct((n, LANES), jnp.int32)),
        grid=(n // tm,),
        in_specs=[pl.BlockSpec((tm, D_MODEL), lambda i: (i, 0)), pl.BlockSpec((1, D_MODEL), lambda i: (0, 0)),
                  pl.BlockSpec((D_MODEL, LANES), lambda i: (0, 0))],
        out_specs=(pl.BlockSpec((tm, D_MODEL), lambda i: (i, 0)), pl.BlockSpec((tm, LANES), lambda i: (i, 0)),
                   pl.BlockSpec((tm, LANES), lambda i: (i, 0))),
        compiler_params=_cparams(("parallel",)),
        name="router",
    )(x2, gain, wr_pad)


def _expert_ffn_kernel(te_ref, nv_ref, xs_ref, wg_ref, wu_ref, wd_ref, o_ref):
    i = pl.program_id(0)

    @pl.when(i < nv_ref[0])
    def _():
        h = xs_ref[...]
        a = jnp.dot(h, wg_ref[0], preferred_element_type=F32)
        b = jnp.dot(h, wu_ref[0], preferred_element_type=F32)
        z = (_silu(a) * b).astype(BF16)
        o_ref[...] = jnp.dot(z, wd_ref[0], preferred_element_type=F32).astype(o_ref.dtype)

    @pl.when(i >= nv_ref[0])
    def _():
        o_ref[...] = jnp.zeros_like(o_ref)


def _expert_ffn(xs, tile_expert, n_valid, wg, wu, wd, tm):
    rows = xs.shape[0]
    wspec = lambda shape: pl.BlockSpec((1,) + shape, lambda i, te, nv: (te[i], 0, 0))
    return pl.pallas_call(
        _expert_ffn_kernel,
        out_shape=jax.ShapeDtypeStruct((rows, D_MODEL), BF16),
        grid_spec=pltpu.PrefetchScalarGridSpec(
            num_scalar_prefetch=2,
            grid=(rows // tm,),
            in_specs=[pl.BlockSpec((tm, D_MODEL), lambda i, te, nv: (i, 0)),
                      wspec((D_MODEL, D_FF)), wspec((D_MODEL, D_FF)), wspec((D_FF, D_MODEL))],
            out_specs=pl.BlockSpec((tm, D_MODEL), lambda i, te, nv: (i, 0)),
        ),
        compiler_params=_cparams(("arbitrary",)),
        name="expert_ffn",
    )(tile_expert, n_valid, xs, wg, wu, wd)


def _moe_combine_kernel(x_ref, y1_ref, y2_ref, gate_ref, nf_ref, *o_refs, tiles):
    i = pl.program_id(0)
    g = gate_ref[...]
    y = x_ref[...] + g[:, 0:1] * y1_ref[...].astype(F32) + g[:, 1:2] * y2_ref[...].astype(F32)
    out = _rms_norm_rows(y, nf_ref[...])
    start = 0
    for o_ref, nt in zip(o_refs, tiles):
        @pl.when((i >= start) & (i < start + nt))
        def _(o_ref=o_ref):
            o_ref[...] = out
        start += nt


def _moe_combine(x2, y1, y2, gates, norm_final, tm, part_rows):
    n = x2.shape[0]
    row = lambda w: pl.BlockSpec((tm, w), lambda i: (i, 0))
    outs = tuple(jax.ShapeDtypeStruct((r, D_MODEL), F32) for r in part_rows)
    return pl.pallas_call(
        functools.partial(_moe_combine_kernel, tiles=_part_tiles(outs, tm)),
        out_shape=outs,
        grid=(n // tm,),
        in_specs=[row(D_MODEL), row(D_MODEL), row(D_MODEL), row(LANES), pl.BlockSpec((1, D_MODEL), lambda i: (0, 0))],
        out_specs=tuple(_part_specs(outs, tm, D_MODEL)),
        compiler_params=_cparams(("arbitrary",)),
        name="moe_combine",
    )(x2, y1, y2, gates, norm_final)


def _moe(x2, h, gates, idx, wg, wu, wd, norm_final, tm, part_rows):
    n = x2.shape[0]
    tme = EXPERT_TILE
    n_tiles = (2 * n + N_EXPERTS * (tme - 1)) // tme + 1
    e_pair = idx[:, :2]
    onehot = (e_pair[:, :, None] == jnp.arange(N_EXPERTS)[None, None, :]).astype(jnp.int32)
    per_tok = onehot.sum(1)
    pos = jnp.cumsum(per_tok, axis=0) - per_tok
    cnt = per_tok.sum(0)
    padded = (cnt + tme - 1) // tme * tme
    group_end = jnp.cumsum(padded)
    group_off = group_end - padded
    dense_off = jnp.cumsum(cnt) - cnt
    slot = jnp.take_along_axis(group_off[None, :] + pos, e_pair, axis=1)
    order = jnp.argsort(e_pair.reshape(-1), stable=True)
    sorted_tok = (order // 2).astype(jnp.int32)
    tile_start = jnp.arange(n_tiles, dtype=jnp.int32) * tme
    tile_expert = jnp.minimum(jnp.searchsorted(group_end, tile_start, side='right'), N_EXPERTS - 1).astype(jnp.int32)
    n_valid = (group_end[-1:] // tme).astype(jnp.int32)
    row = jnp.arange(n_tiles * tme, dtype=jnp.int32)
    row_e = jnp.repeat(tile_expert, tme)
    rank = row - group_off[row_e]
    src = jnp.where(rank < cnt[row_e], sorted_tok[jnp.clip(dense_off[row_e] + rank, 0, 2 * n - 1)], 0)
    xs = jnp.take(h, src, axis=0)
    ys = _expert_ffn(xs, tile_expert, n_valid, wg, wu, wd, tme)
    y1 = jnp.take(ys, slot[:, 0], axis=0)
    y2 = jnp.take(ys, slot[:, 1], axis=0)
    return _moe_combine(x2, y1, y2, gates, norm_final, tm, part_rows)


def _rope_tables(t):
    rows = t // GRID_W
    pos = np.arange(rows * GRID_W)
    row = (pos // GRID_W).astype(np.float32)
    col = (pos % GRID_W).astype(np.float32)
    nf = HEAD_DIM // 4
    inv = jnp.asarray(ROPE_THETA, F32) ** (-jnp.arange(nf, dtype=F32) / nf)
    ar = jnp.asarray(row)[:, None] * inv
    ac = jnp.asarray(col)[:, None] * inv
    cos = jnp.concatenate([jnp.cos(ar), jnp.cos(ar), jnp.cos(ac), jnp.cos(ac)], axis=-1)
    sin = jnp.concatenate([-jnp.sin(ar), jnp.sin(ar), -jnp.sin(ac), jnp.sin(ac)], axis=-1)
    return jnp.tile(cos, (1, H_G)), jnp.tile(sin, (1, H_G))


def _pad_w_in(w):
    a, b_, c, d = 1024, 512, 1040, 1024
    w_a, w_b, w_c, w_d = w[:, :a], w[:, a:a + b_], w[:, a + b_:a + b_ + c], w[:, a + b_ + c:]
    gates = jnp.pad(w_c[:, 1024:], ((0, 0), (0, LANES - 16)))
    return jnp.concatenate([w_a, w_c[:, :1024], w_d, w_b, gates], axis=1).astype(BF16)


def _row(v):
    return v.reshape(1, -1).astype(F32)


def _trunk(xs, p):
    t = xs[0].shape[1]
    part_rows = tuple(x.shape[0] * t for x in xs)
    b = sum(x.shape[0] for x in xs)
    n = b * t
    tm = 256
    cos, sin = _rope_tables(t)
    x_parts = tuple(x.reshape(-1, D_MODEL) for x in xs)
    depth = p['w_in'].shape[0]
    for l in range(depth):
        u2 = _in_proj(x_parts, _row(p['norm_mix'][l]), _pad_w_in(p['w_in'][l]), tm)
        u3 = u2.reshape(b, t, U_COLS)
        (r, v, kk, g, bonus, lwf, lwb, kf, kb, bf, bb) = _rwkv_prep(
            u3, _row(p['rwkv_mu'][l]), p['rwkv_w0'][l], p['rwkv_w2'][l], p['rwkv_a0'][l], p['rwkv_a2'][l],
            p['rwkv_g2'][l], _row(p['rwkv_kk'][l]), _row(p['rwkv_ka'][l]), _row(p['rwkv_rk'][l]))
        yf, yb = _rwkv_core(r, v, kk, lwf, lwb, kf, kb, bf, bb)
        at = _attention(u3, cos, sin, _row(jnp.tile(p['attn_q_norm'][l], H_G)),
                        _row(jnp.tile(p['attn_k_norm'][l], KV_ATTN)))
        gate_bias = jnp.pad(jnp.concatenate([p['mlstm_i_bias'][l].reshape(-1), p['mlstm_f_bias'][l].reshape(-1)]),
                            (0, LANES - 4 * H_G))
        hf, hb = _mlstm(u3, p['mlstm_conv_w'][l], _row(p['mlstm_conv_b'][l]), _row(gate_bias))
        of, ob = _retention(u3, cos, sin)
        flat = lambda z: z.reshape(n, W_G)
        x2 = _out_proj(x_parts, tuple(map(flat, (yf, yb, bonus, g))), flat(at), tuple(map(flat, (hf, hb))),
                       tuple(map(flat, (of, ob))), u2, _row(p['rwkv_ln'][l]), _row(p['mlstm_ln'][l]),
                       _row(p['ret_ln'][l]), p['w_out'][l].astype(BF16), tm)
        j = l // 2
        if l % 2 == 0:
            x2 = _ffn(x2, _row(p['norm_ffn'][l]), p['ffn_w_gate'][j].astype(BF16), p['ffn_w_up'][j].astype(BF16),
                      p['ffn_w_down'][j].astype(BF16), tm)
            x_parts = (x2,)
            if l == depth - 1:
                raise NotImplementedError("final norm after a dense FFN layer")
        else:
            wr = jnp.pad(p['moe_router'][j], ((0, 0), (0, LANES - N_EXPERTS)))
            h, gates, idx = _router(x2, _row(p['norm_ffn'][l]), wr, tm)
            if l != depth - 1:
                raise NotImplementedError("expert layer that is not the last layer")
            outs = _moe(x2, h, gates, idx, p['moe_w_gate'][j].astype(BF16), p['moe_w_up'][j].astype(BF16),
                        p['moe_w_down'][j].astype(BF16), _row(p['norm_final']), tm, part_rows)
    return tuple(o.reshape(x.shape) for o, x in zip(outs, xs))


def kernel(x_prompt, x_sample, norm_mix, norm_ffn, norm_final, w_in, w_out, rwkv_mu, rwkv_w0, rwkv_w2,
           rwkv_a0, rwkv_a2, rwkv_g2, rwkv_kk, rwkv_ka, rwkv_rk, rwkv_ln, attn_q_norm, attn_k_norm,
           mlstm_conv_w, mlstm_conv_b, mlstm_i_bias, mlstm_f_bias, mlstm_ln, ret_ln, ffn_w_gate, ffn_w_up,
           ffn_w_down, moe_router, moe_w_gate, moe_w_up, moe_w_down):
    p = dict(norm_mix=norm_mix, norm_ffn=norm_ffn, norm_final=norm_final, w_in=w_in, w_out=w_out,
             rwkv_mu=rwkv_mu, rwkv_w0=rwkv_w0, rwkv_w2=rwkv_w2, rwkv_a0=rwkv_a0, rwkv_a2=rwkv_a2,
             rwkv_g2=rwkv_g2, rwkv_kk=rwkv_kk, rwkv_ka=rwkv_ka, rwkv_rk=rwkv_rk, rwkv_ln=rwkv_ln,
             attn_q_norm=attn_q_norm, attn_k_norm=attn_k_norm, mlstm_conv_w=mlstm_conv_w,
             mlstm_conv_b=mlstm_conv_b, mlstm_i_bias=mlstm_i_bias, mlstm_f_bias=mlstm_f_bias,
             mlstm_ln=mlstm_ln, ret_ln=ret_ln, ffn_w_gate=ffn_w_gate, ffn_w_up=ffn_w_up,
             ffn_w_down=ffn_w_down, moe_router=moe_router, moe_w_gate=moe_w_gate, moe_w_up=moe_w_up,
             moe_w_down=moe_w_down)
    return _trunk((x_prompt, x_sample), p)
```

```python
import functools
import math

import numpy as np
import jax
import jax.numpy as jnp
from jax import lax
from jax.experimental import pallas as pl
from jax.experimental.pallas import tpu as pltpu

F32 = jnp.float32
BF16 = jnp.bfloat16

D_MODEL = 1024
HEAD_DIM = 64
W_G = 256
H_G = 4
KV_ATTN = 2
D_FF = 2816
N_EXPERTS = 8
NORM_EPS = 1e-6
HEAD_NORM_EPS = 1e-5
RWKV_GN_EPS = 64e-5
NEG_INF = -1e30
ROPE_THETA = 10000.0
GRID_W = 64

LANES = 128
ROW_ALIGN = 8
VMEM_LIMIT_BYTES = 56 * 1024 * 1024

U_RWKV = 0
U_MLSTM = 1024
U_RET = 2048
U_ATTN = 3072
U_GATE = 3584
U_COLS = 3712

RWKV_CHUNK = 64
RWKV_BLOCK = 256
RWKV_GROUP = 4
MIX_CHUNK = 128
MIX_BLOCK = 256
EXPERT_TILE = 256


def _cparams(sem):
    return pltpu.CompilerParams(dimension_semantics=sem, vmem_limit_bytes=VMEM_LIMIT_BYTES)


def _bdot(a, b, dims):
    return lax.dot_general(a, b, (dims, ((), ())), preferred_element_type=F32)


def _mm(a, b):
    return _bdot(a.astype(BF16), b.astype(BF16), ((1,), (0,)))


def _mm_nt(a, b):
    return _bdot(a.astype(BF16), b.astype(BF16), ((1,), (1,)))


def _mm_tn(a, b):
    return _bdot(a.astype(BF16), b.astype(BF16), ((0,), (0,)))


def _split2(a):
    hi = a.astype(BF16)
    lo = (a - hi.astype(F32)).astype(BF16)
    return hi, lo


def _split3(a):
    hi = a.astype(BF16)
    r = a - hi.astype(F32)
    mid = r.astype(BF16)
    lo = (r - mid.astype(F32)).astype(BF16)
    return hi, mid, lo


def _mm_l2(a, b_exact):
    hi, lo = _split2(a)
    return _bdot(hi, b_exact, ((1,), (0,))) + _bdot(lo, b_exact, ((1,), (0,)))


def _mm_l3(a, b_exact):
    h, m, l = _split3(a)
    return _bdot(h, b_exact, ((1,), (0,))) + _bdot(m, b_exact, ((1,), (0,))) + _bdot(l, b_exact, ((1,), (0,)))


def _mm_r3(a_exact, b):
    h, m, l = _split3(b)
    return _bdot(a_exact, h, ((1,), (0,))) + _bdot(a_exact, m, ((1,), (0,))) + _bdot(a_exact, l, ((1,), (0,)))


def _mm_x3(a, b):
    ah, al = _split2(a)
    bh, bl = _split2(b)
    d = ((1,), (0,))
    return _bdot(ah, bh, d) + _bdot(ah, bl, d) + _bdot(al, bh, d)


def _iota2(shape, axis):
    return lax.broadcasted_iota(jnp.int32, shape, axis)


def _head_mean_matrix(width):
    r = _iota2((width, width), 0) // HEAD_DIM
    c = _iota2((width, width), 1) // HEAD_DIM
    return jnp.where(r == c, 1.0 / HEAD_DIM, 0.0).astype(BF16)


def _head_mean(z, bd):
    return _mm_l2(z, bd)


def _sigmoid(x):
    return 1.0 / (1.0 + jnp.exp(-x))


def _silu(x):
    return x * _sigmoid(x)


def _log_sigmoid(x):
    return jnp.minimum(x, 0.0) - jnp.log(1.0 + jnp.exp(-jnp.abs(x)))


def _rms_norm_rows(x, gain):
    ms = jnp.mean(x * x, axis=-1, keepdims=True)
    return x * lax.rsqrt(ms + NORM_EPS) * gain


def _rope_swap(z):
    w = z.shape[-1]
    lane = _iota2(z.shape, z.ndim - 1)
    fwd = pltpu.roll(z, w - 16, z.ndim - 1)
    bwd = pltpu.roll(z, 16, z.ndim - 1)
    return jnp.where((lane % 32) < 16, fwd, bwd)


def _rope(z, cos, sin):
    return z * cos + _rope_swap(z) * sin


def _shift_rows(x, prev_row, next_row):
    n = x.shape[0]
    row = _iota2(x.shape, 0)
    prev = jnp.where(row == 0, prev_row, pltpu.roll(x, 1, 0))
    nxt = jnp.where(row == n - 1, next_row, pltpu.roll(x, n - 1, 0))
    return prev, nxt


def _part_tiles(parts, tm):
    return tuple(p.shape[0] // tm for p in parts)


def _part_specs(parts, tm, width):
    specs, start = [], 0
    for nt in _part_tiles(parts, tm):
        specs.append(pl.BlockSpec((tm, width), lambda i, s=start, nt=nt: (jnp.clip(i - s, 0, nt - 1), 0)))
        start += nt
    return specs


def _part_tile(i, refs, tiles):
    x = refs[0][...]
    start = tiles[0]
    for ref, nt in zip(refs[1:], tiles[1:]):
        x = jnp.where(i >= start, ref[...], x)
        start += nt
    return x


def _inproj_kernel(*refs, tiles):
    x_refs, (g_ref, w_ref, o_ref) = refs[:len(tiles)], refs[len(tiles):]
    h = _rms_norm_rows(_part_tile(pl.program_id(0), x_refs, tiles), g_ref[...])
    o_ref[...] = jnp.dot(h.astype(BF16), w_ref[...], preferred_element_type=F32)


def _in_proj(x_parts, gain, w_pad, tm):
    tiles = _part_tiles(x_parts, tm)
    n = sum(tiles) * tm
    return pl.pallas_call(
        functools.partial(_inproj_kernel, tiles=tiles),
        out_shape=jax.ShapeDtypeStruct((n, U_COLS), F32),
        grid=(n // tm,),
        in_specs=_part_specs(x_parts, tm, D_MODEL) + [
            pl.BlockSpec((1, D_MODEL), lambda i: (0, 0)),
            pl.BlockSpec((D_MODEL, U_COLS), lambda i: (0, 0), pipeline_mode=pl.Buffered(1)),
        ],
        out_specs=pl.BlockSpec((tm, U_COLS), lambda i: (i, 0)),
        compiler_params=_cparams(("arbitrary",)),
        name="in_proj",
    )(*x_parts, gain, w_pad)


def _attn_kernel(u_ref, cos_ref, sin_ref, qg_ref, kg_ref, o_ref, q_scr, k_scr, v_scr, *, tq):
    t = u_ref.shape[1]
    u = u_ref[0]
    q = u[:, :W_G]
    k = u[:, W_G:W_G + KV_ATTN * HEAD_DIM]
    v = u[:, W_G + KV_ATTN * HEAD_DIM:]
    cos = cos_ref[...]
    sin = sin_ref[...]
    bd_q = _head_mean_matrix(W_G)
    bd_k = _head_mean_matrix(KV_ATTN * HEAD_DIM)
    qn = q * lax.rsqrt(_head_mean(q * q, bd_q) + NORM_EPS) * qg_ref[...]
    kn = k * lax.rsqrt(_head_mean(k * k, bd_k) + NORM_EPS) * kg_ref[...]
    kw = KV_ATTN * HEAD_DIM
    q_scr[...] = (_rope(qn, cos, sin) * (HEAD_DIM ** -0.5 * math.log2(math.e))).astype(BF16)
    k_scr[...] = _rope(kn, cos[:, :kw], sin[:, :kw]).astype(BF16)
    ones = jnp.ones((t, HEAD_DIM), BF16)
    for j in range(KV_ATTN):
        vj = v[:, j * HEAD_DIM:(j + 1) * HEAD_DIM].astype(BF16)
        v_scr[:, j * LANES:(j + 1) * LANES] = jnp.concatenate([vj, ones], axis=1)
    group = H_G // KV_ATTN

    def q_tile(i, carry):
        rows = pl.ds(pl.multiple_of(i * tq, tq), tq)
        for j in range(KV_ATTN):
            kj = k_scr[:, j * HEAD_DIM:(j + 1) * HEAD_DIM]
            vj = v_scr[:, j * LANES:(j + 1) * LANES]
            for g in range(group):
                h = j * group + g
                qh = q_scr[rows, h * HEAD_DIM:(h + 1) * HEAD_DIM]
                s = _bdot(qh, kj, ((1,), (1,)))
                m = jnp.max(s, axis=-1, keepdims=True)
                p = jnp.exp2(s - m)
                r = jnp.dot(p.astype(BF16), vj, preferred_element_type=F32)
                o_ref[0, rows, h * HEAD_DIM:(h + 1) * HEAD_DIM] = r[:, :HEAD_DIM] / r[:, HEAD_DIM:]
        return carry

    lax.fori_loop(0, t // tq, q_tile, 0)


def _attention(u3, cos, sin, q_gain, k_gain):
    b, t, _ = u3.shape
    tq = min(256, t)
    col = U_ATTN // 512
    return pl.pallas_call(
        functools.partial(_attn_kernel, tq=tq),
        out_shape=jax.ShapeDtypeStruct((b, t, W_G), F32),
        grid=(b,),
        in_specs=[
            pl.BlockSpec((1, t, 512), lambda i: (i, 0, col)),
            pl.BlockSpec((t, W_G), lambda i: (0, 0)),
            pl.BlockSpec((t, W_G), lambda i: (0, 0)),
            pl.BlockSpec((1, W_G), lambda i: (0, 0)),
            pl.BlockSpec((1, KV_ATTN * HEAD_DIM), lambda i: (0, 0)),
        ],
        out_specs=pl.BlockSpec((1, t, W_G), lambda i: (i, 0, 0)),
        scratch_shapes=[
            pltpu.VMEM((t, W_G), BF16),
            pltpu.VMEM((t, KV_ATTN * HEAD_DIM), BF16),
            pltpu.VMEM((t, KV_ATTN * LANES), BF16),
        ],
        compiler_params=_cparams(("parallel",)),
        name="attention",
    )(u3, cos, sin, q_gain, k_gain)


def _ret_log_gamma(direction):
    return [math.log1p(-2.0 ** (-5.0 - (2 * h + direction) / 2.0)) for h in range(H_G)]


def _ret_kernel(uf_ref, ub_ref, cf_ref, sf_ref, cb_ref, sb_ref, of_ref, ob_ref, rf_scr, rb_scr):
    i = pl.program_id(1)
    c = MIX_CHUNK

    @pl.when(i == 0)
    def _():
        rf_scr[...] = jnp.zeros_like(rf_scr)
        rb_scr[...] = jnp.zeros_like(rb_scr)

    tt = _iota2((c, c), 0)
    ss = _iota2((c, c), 1)
    diff = (tt - ss).astype(F32)
    jcol = tt.astype(F32)
    first = ss < HEAD_DIM
    block_diag = (tt < HEAD_DIM) == first
    lg_f = _ret_log_gamma(0)
    lg_b = _ret_log_gamma(1)
    psl = lambda p: slice(p * LANES, (p + 1) * LANES)
    lane_lg = lambda lg, p: jnp.where(first, lg[2 * p], lg[2 * p + 1])
    row_decay = lambda lg, p: jnp.where(tt < HEAD_DIM, math.exp(c * lg[2 * p]), math.exp(c * lg[2 * p + 1]))
    own = lambda h, x: jnp.where(first, x, 0.0) if h % 2 == 0 else jnp.where(first, 0.0, x)
    pairs = range(H_G // 2)

    nch = uf_ref.shape[1] // c
    uf = uf_ref[0]
    qf_all = _rope(uf[:, :W_G], cf_ref[...], sf_ref[...])
    kf_all = _rope(uf[:, W_G:2 * W_G], cf_ref[...], sf_ref[...]) * HEAD_DIM ** -0.5
    ub = ub_ref[0]
    qb_all = _rope(ub[:, :W_G], cb_ref[...], sb_ref[...])
    kb_all = _rope(ub[:, W_G:2 * W_G], cb_ref[...], sb_ref[...]) * HEAD_DIM ** -0.5
    rf_prev = [rf_scr[p] for p in pairs]
    rb_prev = [rb_scr[p] for p in pairs]
    decay = [jnp.where(tt >= ss, jnp.exp(diff * lg_f[h]), 0.0) + jnp.where(ss >= tt, jnp.exp(-diff * lg_b[h]), 0.0)
             for h in range(H_G)]
    for n in range(nch):
        rf_rows = slice(n * c, (n + 1) * c)
        rb_rows = slice((nch - 1 - n) * c, (nch - n) * c)
        qf, kf, vf = qf_all[rf_rows], kf_all[rf_rows], uf[rf_rows, 2 * W_G:3 * W_G]
        qb, kb, vb = qb_all[rb_rows], kb_all[rb_rows], ub[rb_rows, 2 * W_G:3 * W_G]
        qk = [_mm_nt(qf[:, psl(h // 2)], own(h, kf[:, psl(h // 2)])) for h in range(H_G)]
        intra = [_mm(qk[h] * decay[h], own(h, vf[:, psl(h // 2)])) for h in range(H_G)]
        inter_f = [_mm(qf[:, psl(p)] * jnp.exp((jcol + 1.0) * lane_lg(lg_f, p)), rf_prev[p]) for p in pairs]
        inter_b = [_mm(qb[:, psl(p)] * jnp.exp((c - jcol) * lane_lg(lg_b, p)), rb_prev[p]) for p in pairs]
        upd_f = [_mm_tn(kf[:, psl(p)] * jnp.exp((c - 1.0 - jcol) * lane_lg(lg_f, p)), vf[:, psl(p)]) for p in pairs]
        upd_b = [_mm_tn(kb[:, psl(p)] * jnp.exp(jcol * lane_lg(lg_b, p)), vb[:, psl(p)]) for p in pairs]
        for p in pairs:
            of_ref[0, rf_rows, psl(p)] = intra[2 * p] + intra[2 * p + 1] + inter_f[p]
            ob_ref[0, rb_rows, psl(p)] = inter_b[p]
        rf_prev = [row_decay(lg_f, p) * rf_prev[p] + jnp.where(block_diag, upd_f[p], 0.0) for p in pairs]
        rb_prev = [row_decay(lg_b, p) * rb_prev[p] + jnp.where(block_diag, upd_b[p], 0.0) for p in pairs]
    for p in pairs:
        rf_scr[p] = rf_prev[p]
        rb_scr[p] = rb_prev[p]


def _retention(u3, cos, sin):
    b, t, _ = u3.shape
    c = min(MIX_BLOCK, t)
    nblk = t // c
    col = U_RET // 1024
    tab = lambda rev: pl.BlockSpec((c, W_G), (lambda bi, i: (nblk - 1 - i, 0)) if rev else (lambda bi, i: (i, 0)))
    return pl.pallas_call(
        _ret_kernel,
        out_shape=(jax.ShapeDtypeStruct((b, t, W_G), F32), jax.ShapeDtypeStruct((b, t, W_G), F32)),
        grid=(b, nblk),
        in_specs=[
            pl.BlockSpec((1, c, 1024), lambda bi, i: (bi, i, col)),
            pl.BlockSpec((1, c, 1024), lambda bi, i: (bi, nblk - 1 - i, col)),
            tab(False), tab(False), tab(True), tab(True),
        ],
        out_specs=(
            pl.BlockSpec((1, c, W_G), lambda bi, i: (bi, i, 0)),
            pl.BlockSpec((1, c, W_G), lambda bi, i: (bi, nblk - 1 - i, 0)),
        ),
        scratch_shapes=[pltpu.VMEM((H_G // 2, LANES, LANES), F32), pltpu.VMEM((H_G // 2, LANES, LANES), F32)],
        compiler_params=_cparams(("parallel", "arbitrary")),
        name="retention",
    )(u3, u3, cos, sin, cos, sin)


def _mlstm_tile(u_ref, up_ref, un_ref, g_ref, cw_ref, cb_ref, gb_ref, blk, nblk, direction):
    c = MIX_CHUNK
    reverse = direction == 1
    u = u_ref[0]
    qk = u[:, :2 * W_G]
    prev_row = jnp.where(blk == 0, 0.0, up_ref[0][ROW_ALIGN - 1:ROW_ALIGN, :])
    next_row = jnp.where(blk == nblk - 1, 0.0, un_ref[0][0:1, :])
    prev, nxt = _shift_rows(qk, prev_row, next_row)
    cw = cw_ref[...]
    qk = _silu(cw[0:1] * prev + cw[1:2] * qk + cw[2:3] * nxt + cb_ref[...])
    qa = qk[:, :W_G]
    ka = qk[:, W_G:] * HEAD_DIM ** -0.5
    va = u[:, 2 * W_G:3 * W_G]

    x = g_ref[0] + gb_ref[...]
    xt = x.T
    lf_c = _log_sigmoid(x)
    lf_r = _log_sigmoid(xt)
    tt = _iota2((c, c), 0)
    ss = _iota2((c, c), 1)
    lower = jnp.where(ss <= tt, 1.0, 0.0).astype(BF16)
    upper = jnp.where(tt <= ss, 1.0, 0.0).astype(BF16)
    if reverse:
        b_c = _mm_r3(upper, lf_c)
        b_r = _mm_l3(lf_r, lower)
        mask = ss >= tt
    else:
        b_c = _mm_r3(lower, lf_c)
        b_r = _mm_l3(lf_r, upper)
        mask = ss <= tt
    g_all = jnp.sum(lf_c, axis=0, keepdims=True)
    return dict(q=qa, k=ka, v=va, x=x, xt=xt, b_c=b_c, b_r=b_r, g_all=g_all, mask=mask)


def _mlstm_select_matrix():
    sel = np.zeros((2, 2 * LANES, 8 * LANES), np.float32)
    for d in range(2):
        for h in range(H_G):
            ci, cf = d * H_G + h, 2 * H_G + d * H_G + h
            p, j = divmod(h, 2)
            sel[d, cf, LANES * h:LANES * (h + 1)] = 1.0
            sel[d, cf, 4 * LANES + LANES * p + HEAD_DIM * j:4 * LANES + LANES * p + HEAD_DIM * (j + 1)] = 1.0
            sel[d, LANES + ci, 6 * LANES + LANES * p + HEAD_DIM * j:6 * LANES + LANES * p + HEAD_DIM * (j + 1)] = 1.0
    return jnp.asarray(sel, BF16)


def _mlstm_kernel(uf_ref, upf_ref, unf_ref, gf_ref, ub_ref, upb_ref, unb_ref, gbk_ref,
                  cw_ref, cb_ref, gb_ref, sel_ref, of_ref, ob_ref, st_scr, m_scr):
    i = pl.program_id(1)
    nblk = pl.num_programs(1)
    c = MIX_CHUNK

    @pl.when(i == 0)
    def _():
        st_scr[...] = jnp.zeros_like(st_scr)
        m_scr[...] = jnp.zeros_like(m_scr)

    tiles = (_mlstm_tile(uf_ref, upf_ref, unf_ref, gf_ref, cw_ref, cb_ref, gb_ref, i, nblk, 0),
             _mlstm_tile(ub_ref, upb_ref, unb_ref, gbk_ref, cw_ref, cb_ref, gb_ref, nblk - 1 - i, nblk, 1))
    o_refs = (of_ref, ob_ref)
    sel = [_mm_l2(jnp.concatenate([tiles[d]["b_c"], tiles[d]["x"]], axis=1), sel_ref[d]) for d in range(2)]
    first = _iota2((c, LANES), 1) < HEAD_DIM
    row_first = _iota2((LANES, 2 * LANES), 0) < HEAD_DIM
    lane2 = _iota2((LANES, 2 * LANES), 1) % LANES < HEAD_DIM
    block_diag = row_first == lane2
    ones = jnp.ones((c, LANES), F32)
    pairs = [(d, p) for d in range(2) for p in range(H_G // 2)]
    heads = [(d, h) for d in range(2) for h in range(H_G)]
    psl = lambda p: slice(p * LANES, (p + 1) * LANES)
    q_pair = {dp: tiles[dp[0]]["q"][:, psl(dp[1])] for dp in pairs}
    k_pair = {dp: tiles[dp[0]]["k"][:, psl(dp[1])] for dp in pairs}
    v_pair = {dp: tiles[dp[0]]["v"][:, psl(dp[1])] for dp in pairs}
    state = {dp: st_scr[n] for n, dp in enumerate(pairs)}
    m_row = {dp: m_scr[n:n + 1, :] for n, dp in enumerate(pairs)}

    def own(d, h, x):
        return jnp.where(first, x, 0.0) if h % 2 == 0 else jnp.where(first, 0.0, x)

    qk = [_mm_nt(q_pair[(d, h // 2)], own(d, h, k_pair[(d, h // 2)])) for d, h in heads]
    qs = {dp: _mm(q_pair[dp], state[dp]) for dp in pairs}
    bc = [sel[d][:, LANES * h:LANES * (h + 1)] for d, h in heads]
    m_prev = [m_row[(d, h // 2)][:, HEAD_DIM * (h % 2):HEAD_DIM * (h % 2) + 1] for d, h in heads]
    dlog = []
    for (d, h), bc_ in zip(heads, bc):
        ci, cf = d * H_G + h, 2 * H_G + d * H_G + h
        rowterm = tiles[d]["xt"][ci:ci + 1, :] - tiles[d]["b_r"][cf:cf + 1, :]
        dlog.append(jnp.where(tiles[d]["mask"], bc_ + rowterm, NEG_INF))
    inter_log = [bc_ + m_ for bc_, m_ in zip(bc, m_prev)]
    m_t = [jnp.maximum(il, jnp.max(dl, axis=-1, keepdims=True)) for il, dl in zip(inter_log, dlog)]
    sc = [qk_ * jnp.exp(dl - mt) for qk_, dl, mt in zip(qk, dlog, m_t)]
    w_inter = [jnp.exp(il - mt) for il, mt in zip(inter_log, m_t)]
    e_neg = [jnp.exp(-mt) for mt in m_t]
    res = []
    for n, (d, h) in enumerate(heads):
        v_aug = jnp.concatenate([own(d, h, v_pair[(d, h // 2)]), own(d, h, ones)], axis=1)
        res.append(_mm(sc[n], v_aug))
    for n, (d, p) in enumerate(pairs):
        a, b_ = 2 * n, 2 * n + 1
        tot = res[a] + res[b_] + jnp.tile(jnp.where(first, w_inter[a], w_inter[b_]), (1, 2)) * qs[(d, p)]
        den = jnp.maximum(jnp.abs(tot[:, LANES:]), jnp.where(first, e_neg[a], e_neg[b_]))
        o_refs[d][0, :, psl(p)] = tot[:, :LANES] / den

    for n, (d, p) in enumerate(pairs):
        bcp = sel[d][:, 4 * LANES + LANES * p:4 * LANES + LANES * (p + 1)]
        lip = sel[d][:, 6 * LANES + LANES * p:6 * LANES + LANES * (p + 1)]
        g_row = bcp[0:1, :] if d == 1 else bcp[c - 1:c, :]
        a_p = g_row - bcp + lip
        m_new = jnp.maximum(g_row + m_row[(d, p)], jnp.max(a_p, axis=0, keepdims=True))
        dec = jnp.exp(g_row + m_row[(d, p)] - m_new)
        kw_t = (k_pair[(d, p)] * jnp.exp(a_p - m_new)).T
        upd = _mm(kw_t, jnp.concatenate([v_pair[(d, p)], ones], axis=1))
        dec_tile = jnp.where(row_first, dec[:, 0:1], dec[:, HEAD_DIM:HEAD_DIM + 1])
        st_scr[n] = dec_tile * state[(d, p)] + jnp.where(block_diag, upd, 0.0)
        m_scr[n:n + 1, :] = m_new


def _mlstm(u3, conv_w, conv_b, gate_bias):
    b, t, _ = u3.shape
    c = MIX_CHUNK
    nblk = t // c
    rpb = c // ROW_ALIGN
    n8 = t // ROW_ALIGN
    col = U_MLSTM // 1024
    hcol = U_MLSTM // 512
    gcol = U_GATE // LANES

    def specs(rev):
        blk = (lambda i: nblk - 1 - i) if rev else (lambda i: i)
        return [
            pl.BlockSpec((1, c, 1024), lambda bi, i: (bi, blk(i), col)),
            pl.BlockSpec((1, ROW_ALIGN, 512), lambda bi, i: (bi, jnp.maximum(blk(i) * rpb - 1, 0), hcol)),
            pl.BlockSpec((1, ROW_ALIGN, 512), lambda bi, i: (bi, jnp.minimum((blk(i) + 1) * rpb, n8 - 1), hcol)),
            pl.BlockSpec((1, c, LANES), lambda bi, i: (bi, blk(i), gcol)),
        ]

    const = lambda shape: pl.BlockSpec(shape, lambda bi, i: (0,) * len(shape))
    return pl.pallas_call(
        _mlstm_kernel,
        out_shape=(jax.ShapeDtypeStruct((b, t, W_G), F32), jax.ShapeDtypeStruct((b, t, W_G), F32)),
        grid=(b, nblk),
        in_specs=specs(False) + specs(True) + [const((3, 2 * W_G)), const((1, 2 * W_G)), const((1, LANES)),
                                               const((2, 2 * LANES, 8 * LANES))],
        out_specs=(
            pl.BlockSpec((1, c, W_G), lambda bi, i: (bi, i, 0)),
            pl.BlockSpec((1, c, W_G), lambda bi, i: (bi, nblk - 1 - i, 0)),
        ),
        scratch_shapes=[pltpu.VMEM((H_G, LANES, 2 * LANES), F32), pltpu.VMEM((ROW_ALIGN, LANES), F32)],
        compiler_params=_cparams(("parallel", "arbitrary")),
        name="mlstm",
    )(u3, u3, u3, u3, u3, u3, u3, u3, conv_w, conv_b, gate_bias, _mlstm_select_matrix())


def _rwkv_prep_kernel(u_ref, up_ref, un_ref, mu_ref, w0_ref, w2_ref, a0_ref, a2_ref, g2_ref, kks_ref, ka_ref, rk_ref,
                      r_ref, v_ref, kk_ref, g_ref, bonus_ref, lwf_ref, lwb_ref, kf_ref, kb_ref, bf_ref, bb_ref):
    i = pl.program_id(1)
    nblk = pl.num_programs(1)
    u = u_ref[0]
    prev_row = jnp.where(i == 0, 0.0, up_ref[0][ROW_ALIGN - 1:ROW_ALIGN, :])
    next_row = jnp.where(i == nblk - 1, 0.0, un_ref[0][0:1, :])
    prev, nxt = _shift_rows(u, prev_row, next_row)
    us = u + mu_ref[...] * (0.5 * (prev + nxt) - u)
    r = us[:, 0:W_G]
    k = us[:, W_G:2 * W_G]
    v = us[:, 2 * W_G:3 * W_G]
    xw = us[:, 3 * W_G:3 * W_G + 64]
    xa = us[:, 3 * W_G + 64:3 * W_G + 128]
    xg = us[:, 3 * W_G + 128:]
    bd = _head_mean_matrix(W_G)
    g = _mm(_sigmoid(xg), g2_ref[...])
    lw = jnp.tanh(xw)
    a_lr = _mm_x3(xa, a2_ref[...])
    kk = k * kks_ref[...]
    kk = kk * lax.rsqrt(_head_mean(kk * kk, bd) * HEAD_DIM + 1e-12)
    r_ref[0] = r
    v_ref[0] = v
    kk_ref[0] = kk
    g_ref[0] = g
    bonus = jnp.zeros_like(r)
    for d, (lw_ref, k_ref, b_ref) in enumerate(((lwf_ref, kf_ref, bf_ref), (lwb_ref, kb_ref, bb_ref))):
        z = w0_ref[d:d + 1, :] + _mm_x3(lw, w2_ref[d])
        lw_ref[0] = -_sigmoid(z) * math.exp(-0.5)
        a = _sigmoid(a0_ref[d:d + 1, :] + a_lr)
        kd = k * (1.0 + (a - 1.0) * ka_ref[...])
        k_ref[0] = kd
        b_ref[0] = kk * a
        bonus = bonus + _head_mean(r * kd * rk_ref[...], bd) * HEAD_DIM * v
    bonus_ref[0] = bonus


def _rwkv_prep(u3, mu, w0, w2, a0, a2, g2, kks, ka, rk):
    b, t, _ = u3.shape
    tb = min(RWKV_BLOCK, t)
    nblk = t // tb
    rpb = tb // ROW_ALIGN
    n8 = t // ROW_ALIGN
    const = lambda shape: pl.BlockSpec(shape, lambda bi, i: (0,) * len(shape))
    out = jax.ShapeDtypeStruct((b, t, W_G), F32)
    ospec = pl.BlockSpec((1, tb, W_G), lambda bi, i: (bi, i, 0))
    return pl.pallas_call(
        _rwkv_prep_kernel,
        out_shape=(out,) * 11,
        grid=(b, nblk),
        in_specs=[
            pl.BlockSpec((1, tb, 1024), lambda bi, i: (bi, i, 0)),
            pl.BlockSpec((1, ROW_ALIGN, 1024), lambda bi, i: (bi, jnp.maximum(i * rpb - 1, 0), 0)),
            pl.BlockSpec((1, ROW_ALIGN, 1024), lambda bi, i: (bi, jnp.minimum((i + 1) * rpb, n8 - 1), 0)),
            const((1, 1024)), const((2, W_G)), const((2, 64, W_G)), const((2, W_G)), const((64, W_G)),
            const((128, W_G)), const((1, W_G)), const((1, W_G)), const((1, W_G)),
        ],
        out_specs=(ospec,) * 11,
        compiler_params=_cparams(("parallel", "parallel")),
        name="rwkv_prep",
    )(u3, u3, u3, mu, w0, w2, a0, a2, g2, kks, ka, rk)


def _tri_inverse_all(lmats, n):
    r = _iota2((n, n), 0)
    c = _iota2((n, n), 1)
    eye = jnp.where(r == c, 1.0, 0.0)
    pair = (r // 2 == c // 2) & (r != c)
    invs = [eye + jnp.where(pair, lm, 0.0) for lm in lmats]
    s = 2
    while s < n:
        sel = (r // (2 * s) == c // (2 * s)) & (r // s != c // s)
        offs = [jnp.where(sel, -lm, 0.0) for lm in lmats]
        xs = [_mm(inv, off) for inv, off in zip(invs, offs)]
        invs = [inv - _mm(x, inv) for inv, x in zip(invs, xs)]
        s *= 2
    return invs


def _rwkv_tile_terms(r, k, v, kk, b, lw, reverse):
    c = r.shape[0]
    tt = _iota2((c, c), 0)
    ss = _iota2((c, c), 1)
    tri = jnp.where((tt <= ss) if reverse else (ss <= tt), 1.0, 0.0).astype(BF16)
    cum_in = _mm_r3(tri, lw)
    cum_all = jnp.sum(lw, axis=0, keepdims=True)
    e_neg = jnp.exp(-cum_in)
    e_end = jnp.exp(cum_all - cum_in)
    return dict(at=-kk * jnp.exp(cum_in - lw), rt=r * jnp.exp(cum_in), bt=b * e_neg, kt=k * e_neg,
                gb=b * e_end, gk=k * e_end, v=v, e_all=jnp.exp(cum_all))


def _rwkv_chunk_terms(tiles, reverses):
    c = RWKV_CHUNK
    tt = _iota2((c, c), 0)
    ss = _iota2((c, c), 1)
    heads = [(ti, h) for ti in range(len(tiles)) for h in range(H_G)]
    sl = lambda h: slice(h * HEAD_DIM, (h + 1) * HEAD_DIM)
    get = lambda name: [tiles[ti][name][:, sl(h)] for ti, h in heads]
    at, rt, bt, kt, gb, gk, v = (get(nm) for nm in ("at", "rt", "bt", "kt", "gb", "gk", "v"))
    strict = [(ss > tt) if reverses[ti] else (ss < tt) for ti, _ in heads]
    incl = [(ss >= tt) if reverses[ti] else (ss <= tt) for ti, _ in heads]
    ps = [_mm_nt(jnp.concatenate([a, r_], axis=0), jnp.concatenate([b_, k_], axis=0))
          for a, r_, b_, k_ in zip(at, rt, bt, kt)]
    l_ab = [jnp.where(m, p[:c, :c], 0.0) for m, p in zip(strict, ps)]
    l_ak = [jnp.where(m, p[:c, c:], 0.0) for m, p in zip(strict, ps)]
    m_r = [jnp.concatenate([jnp.where(m, p[c:, :c], 0.0), jnp.where(m, p[c:, c:], 0.0)], axis=1)
           for m, p in zip(incl, ps)]
    lakv = [_mm(l, v_) for l, v_ in zip(l_ak, v)]
    invs = _tri_inverse_all(l_ab, c)
    tw = [_mm(inv, jnp.concatenate([a, lv], axis=1)) for inv, a, lv in zip(invs, at, lakv)]
    zeros = jnp.zeros((c, HEAD_DIM), F32)
    mm2 = [_mm(m, jnp.concatenate([t_, jnp.concatenate([zeros, v_], axis=1)], axis=0))
           for m, t_, v_ in zip(m_r, tw, v)]
    r1 = [r_ + m[:, :HEAD_DIM] for r_, m in zip(rt, mm2)]
    y0 = [m[:, HEAD_DIM:] for m in mm2]
    twg = [_mm_tn(t_, g_) for t_, g_ in zip(tw, gb)]
    vgk = [_mm_tn(v_, g_) for v_, g_ in zip(v, gk)]
    mlow = [x[:HEAD_DIM] for x in twg]
    nadd = [x[HEAD_DIM:] + y for x, y in zip(twg, vgk)]
    e_all = [tiles[ti]["e_all"][:, sl(h)] for ti, h in heads]
    return r1, y0, mlow, nadd, e_all


def _rwkv_core_kernel(rf_ref, kf_ref, vf_ref, kkf_ref, bf_ref, lwf_ref, rb_ref, kb_ref, vb_ref, kkb_ref, bb_ref,
                      lwb_ref, yf_ref, yb_ref, s_scr, *, group):
    i = pl.program_id(1)
    c = RWKV_CHUNK
    nch = rf_ref.shape[1] // c
    dirs = ((rf_ref, kf_ref, vf_ref, kkf_ref, bf_ref, lwf_ref), (rb_ref, kb_ref, vb_ref, kkb_ref, bb_ref, lwb_ref))
    y_refs = (yf_ref, yb_ref)

    @pl.when(i == 0)
    def _():
        s_scr[...] = jnp.zeros_like(s_scr)

    def step(j, states):
        tiles, reverses, rows = [], [], []
        for q in range(group):
            for d in range(2):
                cj = j * group + q
                cj = cj if d == 0 else nch - 1 - cj
                rw = pl.ds(pl.multiple_of(cj * c, c), c)
                tiles.append(_rwkv_tile_terms(*(ref[0, rw, :] for ref in dirs[d]), d == 1))
                reverses.append(d == 1)
                rows.append(rw)
        r1, y0, mlow, nadd, e_all = _rwkv_chunk_terms(tiles, reverses)
        states = list(states)
        for q in range(group):
            ys = [[], []]
            for d in range(2):
                for h in range(H_G):
                    n = (q * 2 + d) * H_G + h
                    s = states[d * H_G + h]
                    ys[d].append(_mm_nt(r1[n], s) + y0[n])
                    states[d * H_G + h] = s * e_all[n] + _mm(s, mlow[n]) + nadd[n]
            for d in range(2):
                y_refs[d][0, rows[q * 2 + d], :] = jnp.concatenate(ys[d], axis=1)
        return tuple(states)

    init = tuple(s_scr[n] for n in range(2 * H_G))
    if nch == group:
        states = step(0, init)
    else:
        states = lax.fori_loop(0, nch // group, step, init)
    for n in range(2 * H_G):
        s_scr[n] = states[n]


def _rwkv_core(r, v, kk, lwf, lwb, kf, kb, bf, bb):
    b, t, _ = r.shape
    tb = min(RWKV_BLOCK, t)
    nblk = t // tb
    fwd = pl.BlockSpec((1, tb, W_G), lambda bi, i: (bi, i, 0))
    bwd = pl.BlockSpec((1, tb, W_G), lambda bi, i: (bi, nblk - 1 - i, 0))
    out = jax.ShapeDtypeStruct((b, t, W_G), F32)
    return pl.pallas_call(
        functools.partial(_rwkv_core_kernel, group=min(RWKV_GROUP, tb // RWKV_CHUNK)),
        out_shape=(out, out),
        grid=(b, nblk),
        in_specs=[fwd] * 6 + [bwd] * 6,
        out_specs=(fwd, bwd),
        scratch_shapes=[pltpu.VMEM((2 * H_G, HEAD_DIM, HEAD_DIM), F32)],
        compiler_params=_cparams(("parallel", "arbitrary")),
        name="rwkv_core",
    )(r, kf, v, kk, bf, lwf, r, kb, v, kk, bb, lwb)


def _group_norm(y, gain, eps, bd):
    mu = _head_mean(y, bd)
    d = y - mu
    var = _head_mean(d * d, bd)
    return d * lax.rsqrt(var + eps) * gain


def _outproj_kernel(*refs, tiles):
    x_refs = refs[:len(tiles)]
    (ryf_ref, ryb_ref, rbon_ref, rg_ref, at_ref, mhf_ref, mhb_ref, mo_ref, tof_ref, tob_ref,
     tg_ref, rln_ref, mln_ref, tln_ref, w_ref, o_ref) = refs[len(tiles):]
    bd = _head_mean_matrix(W_G)
    o_a = (_group_norm(ryf_ref[...] + ryb_ref[...], rln_ref[...], RWKV_GN_EPS, bd) + rbon_ref[...]) * rg_ref[...]
    o_c = _group_norm(mhf_ref[...] + mhb_ref[...], mln_ref[...], HEAD_NORM_EPS, bd) * _sigmoid(mo_ref[...])
    o_d = _group_norm(tof_ref[...] + tob_ref[...], tln_ref[...], HEAD_NORM_EPS, bd) * _silu(tg_ref[...])
    mix = jnp.concatenate([o_a, at_ref[...], o_c, o_d], axis=1).astype(BF16)
    o_ref[...] = _part_tile(pl.program_id(0), x_refs, tiles) + jnp.dot(mix, w_ref[...], preferred_element_type=F32)


def _out_proj(x_parts, rw, at, ml, rt, u2, rln, mln, tln, w_out, tm):
    tiles = _part_tiles(x_parts, tm)
    n = sum(tiles) * tm
    row = lambda w: pl.BlockSpec((tm, w), lambda i: (i, 0))
    ucol = lambda off: pl.BlockSpec((tm, W_G), lambda i: (i, off // W_G))
    const = lambda shape: pl.BlockSpec(shape, lambda i: (0, 0))
    return pl.pallas_call(
        functools.partial(_outproj_kernel, tiles=tiles),
        out_shape=jax.ShapeDtypeStruct((n, D_MODEL), F32),
        grid=(n // tm,),
        in_specs=_part_specs(x_parts, tm, D_MODEL) + [row(W_G)] * 7 + [ucol(U_MLSTM + 3 * W_G)] + [row(W_G)] * 2
                 + [ucol(U_RET + 3 * W_G)] + [const((1, W_G))] * 3
                 + [pl.BlockSpec((D_MODEL, D_MODEL), lambda i: (0, 0), pipeline_mode=pl.Buffered(1))],
        out_specs=row(D_MODEL),
        compiler_params=_cparams(("arbitrary",)),
        name="out_proj",
    )(*x_parts, *rw, at, *ml, u2, *rt, u2, rln, mln, tln, w_out)


def _ffn_kernel(x_ref, g_ref, wg_ref, wu_ref, wd_ref, o_ref):
    x = x_ref[...]
    h = _rms_norm_rows(x, g_ref[...]).astype(BF16)
    a = jnp.dot(h, wg_ref[...], preferred_element_type=F32)
    b = jnp.dot(h, wu_ref[...], preferred_element_type=F32)
    z = (_silu(a) * b).astype(BF16)
    o_ref[...] = x + jnp.dot(z, wd_ref[...], preferred_element_type=F32)


def _ffn(x2, gain, wg, wu, wd, tm):
    n = x2.shape[0]
    res = lambda shape: pl.BlockSpec(shape, lambda i: (0, 0), pipeline_mode=pl.Buffered(1))
    return pl.pallas_call(
        _ffn_kernel,
        out_shape=jax.ShapeDtypeStruct((n, D_MODEL), F32),
        grid=(n // tm,),
        in_specs=[pl.BlockSpec((tm, D_MODEL), lambda i: (i, 0)), pl.BlockSpec((1, D_MODEL), lambda i: (0, 0)),
                  res((D_MODEL, D_FF)), res((D_MODEL, D_FF)), res((D_FF, D_MODEL))],
        out_specs=pl.BlockSpec((tm, D_MODEL), lambda i: (i, 0)),
        compiler_params=_cparams(("parallel",)),
        name="ffn",
    )(x2, gain, wg, wu, wd)


def _router_kernel(x_ref, g_ref, wr_ref, h_ref, gate_ref, idx_ref, cnt_ref, cnt_scr):
    @pl.when(pl.program_id(0) == 0)
    def _():
        cnt_scr[...] = jnp.zeros_like(cnt_scr)

    h = _rms_norm_rows(x_ref[...], g_ref[...])
    h_ref[...] = h.astype(BF16)
    logits = _mm_x3(h, wr_ref[...])
    lane = _iota2(logits.shape, 1)
    logits = jnp.where(lane < N_EXPERTS, logits, NEG_INF)
    e = jnp.exp(logits - jnp.max(logits, axis=-1, keepdims=True))
    p = e / jnp.sum(e, axis=-1, keepdims=True)
    p = jnp.where(lane < N_EXPERTS, p, -1.0)
    m1 = jnp.max(p, axis=-1, keepdims=True)
    i1 = jnp.min(jnp.where(p == m1, lane, LANES), axis=-1, keepdims=True)
    p2 = jnp.where(lane == i1, -1.0, p)
    m2 = jnp.max(p2, axis=-1, keepdims=True)
    i2 = jnp.min(jnp.where(p2 == m2, lane, LANES), axis=-1, keepdims=True)
    tot = m1 + m2
    gate_ref[...] = jnp.where(lane == 0, m1 / tot, jnp.where(lane == 1, m2 / tot, 0.0))
    tm = logits.shape[0]
    chosen = jnp.where((lane == i1) | (lane == i2), 1.0, 0.0)
    tril = jnp.where(_iota2((tm, tm), 1) <= _iota2((tm, tm), 0), 1.0, 0.0).astype(BF16)
    incl = jnp.dot(tril, chosen.astype(BF16), preferred_element_type=F32)
    rank = cnt_scr[...] + incl - chosen
    r1 = jnp.sum(jnp.where(lane == i1, rank, 0.0), axis=-1, keepdims=True)
    r2 = jnp.sum(jnp.where(lane == i2, rank, 0.0), axis=-1, keepdims=True)
    idx_ref[...] = jnp.where(lane == 0, i1, jnp.where(lane == 1, i2, jnp.where(lane == 2, r1.astype(jnp.int32),
                                                                                jnp.where(lane == 3, r2.astype(jnp.int32), 0))))
    cnt_scr[...] = cnt_scr[...] + incl[tm - 1:tm, :]
    cnt_ref[...] = jnp.broadcast_to(cnt_scr[...], cnt_ref.shape).astype(jnp.int32)


def _router(x2, gain, wr_pad, tm):
    n = x2.shape[0]
    return pl.pallas_call(
        _router_kernel,
        out_shape=(jax.ShapeDtypeStruct((n, D_MODEL), BF16), jax.ShapeDtypeStruct((n, LANES), F32),
                   jax.ShapeDtypeStruct((n, LANES), jnp.int32), jax.ShapeDtypeStruct((ROW_ALIGN, LANES), jnp.int32)),
        grid=(n // tm,),
        in_specs=[pl.BlockSpec((tm, D_MODEL), lambda i: (i, 0)), pl.BlockSpec((1, D_MODEL), lambda i: (0, 0)),
                  pl.BlockSpec((D_MODEL, LANES), lambda i: (0, 0))],
        out_specs=(pl.BlockSpec((tm, D_MODEL), lambda i: (i, 0)), pl.BlockSpec((tm, LANES), lambda i: (i, 0)),
                   pl.BlockSpec((tm, LANES), lambda i: (i, 0)), pl.BlockSpec((ROW_ALIGN, LANES), lambda i: (0, 0))),
        scratch_shapes=[pltpu.VMEM((1, LANES), F32)],
        compiler_params=_cparams(("arbitrary",)),
        name="router",
    )(x2, gain, wr_pad)


def _expert_ffn_kernel(te_ref, nv_ref, xs_ref, wg_ref, wu_ref, wd_ref, o_ref):
    i = pl.program_id(0)

    @pl.when(i < nv_ref[0])
    def _():
        h = xs_ref[...]
        a = jnp.dot(h, wg_ref[0], preferred_element_type=F32)
        b = jnp.dot(h, wu_ref[0], preferred_element_type=F32)
        z = (_silu(a) * b).astype(BF16)
        o_ref[...] = jnp.dot(z, wd_ref[0], preferred_element_type=F32).astype(o_ref.dtype)

    @pl.when(i >= nv_ref[0])
    def _():
        o_ref[...] = jnp.zeros_like(o_ref)


def _expert_ffn(xs, tile_expert, n_valid, wg, wu, wd, tm):
    rows = xs.shape[0]
    wspec = lambda shape: pl.BlockSpec((1,) + shape, lambda i, te, nv: (te[i], 0, 0))
    return pl.pallas_call(
        _expert_ffn_kernel,
        out_shape=jax.ShapeDtypeStruct((rows, D_MODEL), BF16),
        grid_spec=pltpu.PrefetchScalarGridSpec(
            num_scalar_prefetch=2,
            grid=(rows // tm,),
            in_specs=[pl.BlockSpec((tm, D_MODEL), lambda i, te, nv: (i, 0)),
                      wspec((D_MODEL, D_FF)), wspec((D_MODEL, D_FF)), wspec((D_FF, D_MODEL))],
            out_specs=pl.BlockSpec((tm, D_MODEL), lambda i, te, nv: (i, 0)),
        ),
        compiler_params=_cparams(("arbitrary",)),
        name="expert_ffn",
    )(tile_expert, n_valid, xs, wg, wu, wd)


def _moe_combine_kernel(x_ref, y1_ref, y2_ref, gate_ref, nf_ref, *o_refs, tiles):
    i = pl.program_id(0)
    g = gate_ref[...]
    y = x_ref[...] + g[:, 0:1] * y1_ref[...].astype(F32) + g[:, 1:2] * y2_ref[...].astype(F32)
    out = _rms_norm_rows(y, nf_ref[...])
    start = 0
    for o_ref, nt in zip(o_refs, tiles):
        @pl.when((i >= start) & (i < start + nt))
        def _(o_ref=o_ref):
            o_ref[...] = out
        start += nt


def _moe_combine(x2, y1, y2, gates, norm_final, tm, part_rows):
    n = x2.shape[0]
    row = lambda w: pl.BlockSpec((tm, w), lambda i: (i, 0))
    outs = tuple(jax.ShapeDtypeStruct((r, D_MODEL), F32) for r in part_rows)
    return pl.pallas_call(
        functools.partial(_moe_combine_kernel, tiles=_part_tiles(outs, tm)),
        out_shape=outs,
        grid=(n // tm,),
        in_specs=[row(D_MODEL), row(D_MODEL), row(D_MODEL), row(LANES), pl.BlockSpec((1, D_MODEL), lambda i: (0, 0))],
        out_specs=tuple(_part_specs(outs, tm, D_MODEL)),
        compiler_params=_cparams(("arbitrary",)),
        name="moe_combine",
    )(x2, y1, y2, gates, norm_final)


def _moe(x2, h, gates, idx, counts, wg, wu, wd, norm_final, tm, part_rows):
    n = x2.shape[0]
    tme = EXPERT_TILE
    n_tiles = (2 * n + N_EXPERTS * (tme - 1)) // tme + 1
    e_pair = idx[:, :2]
    rank_pair = idx[:, 2:4]
    cnt = counts[0, :N_EXPERTS]
    padded = (cnt + tme - 1) // tme * tme
    group_end = jnp.cumsum(padded)
    group_off = group_end - padded
    dense_off = jnp.cumsum(cnt) - cnt
    slot = group_off[e_pair] + rank_pair
    order = jnp.argsort(e_pair.reshape(-1), stable=True)
    sorted_tok = (order // 2).astype(jnp.int32)
    tile_start = jnp.arange(n_tiles, dtype=jnp.int32) * tme
    tile_expert = jnp.minimum(jnp.searchsorted(group_end, tile_start, side='right'), N_EXPERTS - 1).astype(jnp.int32)
    n_valid = (group_end[-1:] // tme).astype(jnp.int32)
    row = jnp.arange(n_tiles * tme, dtype=jnp.int32)
    row_e = jnp.repeat(tile_expert, tme)
    rank = row - group_off[row_e]
    src = jnp.where(rank < cnt[row_e], sorted_tok[jnp.clip(dense_off[row_e] + rank, 0, 2 * n - 1)], 0)
    xs = jnp.take(h, src, axis=0)
    ys = _expert_ffn(xs, tile_expert, n_valid, wg, wu, wd, tme)
    y1 = jnp.take(ys, slot[:, 0], axis=0)
    y2 = jnp.take(ys, slot[:, 1], axis=0)
    return _moe_combine(x2, y1, y2, gates, norm_final, tm, part_rows)


def _rope_tables(t):
    rows = t // GRID_W
    pos = np.arange(rows * GRID_W)
    row = (pos // GRID_W).astype(np.float32)
    col = (pos % GRID_W).astype(np.float32)
    nf = HEAD_DIM // 4
    inv = jnp.asarray(ROPE_THETA, F32) ** (-jnp.arange(nf, dtype=F32) / nf)
    ar = jnp.asarray(row)[:, None] * inv
    ac = jnp.asarray(col)[:, None] * inv
    cos = jnp.concatenate([jnp.cos(ar), jnp.cos(ar), jnp.cos(ac), jnp.cos(ac)], axis=-1)
    sin = jnp.concatenate([-jnp.sin(ar), jnp.sin(ar), -jnp.sin(ac), jnp.sin(ac)], axis=-1)
    return jnp.tile(cos, (1, H_G)), jnp.tile(sin, (1, H_G))


def _pad_w_in(w):
    a, b_, c, d = 1024, 512, 1040, 1024
    w_a, w_b, w_c, w_d = w[:, :a], w[:, a:a + b_], w[:, a + b_:a + b_ + c], w[:, a + b_ + c:]
    gates = jnp.pad(w_c[:, 1024:], ((0, 0), (0, LANES - 16)))
    return jnp.concatenate([w_a, w_c[:, :1024], w_d, w_b, gates], axis=1).astype(BF16)


def _row(v):
    return v.reshape(1, -1).astype(F32)


def _trunk(xs, p):
    t = xs[0].shape[1]
    part_rows = tuple(x.shape[0] * t for x in xs)
    b = sum(x.shape[0] for x in xs)
    n = b * t
    tm = 256
    cos, sin = _rope_tables(t)
    x_parts = tuple(x.reshape(-1, D_MODEL) for x in xs)
    depth = p['w_in'].shape[0]
    for l in range(depth):
        u2 = _in_proj(x_parts, _row(p['norm_mix'][l]), _pad_w_in(p['w_in'][l]), tm)
        u3 = u2.reshape(b, t, U_COLS)
        (r, v, kk, g, bonus, lwf, lwb, kf, kb, bf, bb) = _rwkv_prep(
            u3, _row(p['rwkv_mu'][l]), p['rwkv_w0'][l], p['rwkv_w2'][l], p['rwkv_a0'][l], p['rwkv_a2'][l],
            p['rwkv_g2'][l], _row(p['rwkv_kk'][l]), _row(p['rwkv_ka'][l]), _row(p['rwkv_rk'][l]))
        yf, yb = _rwkv_core(r, v, kk, lwf, lwb, kf, kb, bf, bb)
        at = _attention(u3, cos, sin, _row(jnp.tile(p['attn_q_norm'][l], H_G)),
                        _row(jnp.tile(p['attn_k_norm'][l], KV_ATTN)))
        gate_bias = jnp.pad(jnp.concatenate([p['mlstm_i_bias'][l].reshape(-1), p['mlstm_f_bias'][l].reshape(-1)]),
                            (0, LANES - 4 * H_G))
        hf, hb = _mlstm(u3, p['mlstm_conv_w'][l], _row(p['mlstm_conv_b'][l]), _row(gate_bias))
        of, ob = _retention(u3, cos, sin)
        flat = lambda z: z.reshape(n, W_G)
        x2 = _out_proj(x_parts, tuple(map(flat, (yf, yb, bonus, g))), flat(at), tuple(map(flat, (hf, hb))),
                       tuple(map(flat, (of, ob))), u2, _row(p['rwkv_ln'][l]), _row(p['mlstm_ln'][l]),
                       _row(p['ret_ln'][l]), p['w_out'][l].astype(BF16), tm)
        j = l // 2
        if l % 2 == 0:
            x2 = _ffn(x2, _row(p['norm_ffn'][l]), p['ffn_w_gate'][j].astype(BF16), p['ffn_w_up'][j].astype(BF16),
                      p['ffn_w_down'][j].astype(BF16), tm)
            x_parts = (x2,)
            if l == depth - 1:
                raise NotImplementedError("final norm after a dense FFN layer")
        else:
            wr = jnp.pad(p['moe_router'][j], ((0, 0), (0, LANES - N_EXPERTS)))
            h, gates, idx, counts = _router(x2, _row(p['norm_ffn'][l]), wr, tm)
            if l != depth - 1:
                raise NotImplementedError("expert layer that is not the last layer")
            outs = _moe(x2, h, gates, idx, counts, p['moe_w_gate'][j].astype(BF16), p['moe_w_up'][j].astype(BF16),
                        p['moe_w_down'][j].astype(BF16), _row(p['norm_final']), tm, part_rows)
    return tuple(o.reshape(x.shape) for o, x in zip(outs, xs))


def kernel(x_prompt, x_sample, norm_mix, norm_ffn, norm_final, w_in, w_out, rwkv_mu, rwkv_w0, rwkv_w2,
           rwkv_a0, rwkv_a2, rwkv_g2, rwkv_kk, rwkv_ka, rwkv_rk, rwkv_ln, attn_q_norm, attn_k_norm,
           mlstm_conv_w, mlstm_conv_b, mlstm_i_bias, mlstm_f_bias, mlstm_ln, ret_ln, ffn_w_gate, ffn_w_up,
           ffn_w_down, moe_router, moe_w_gate, moe_w_up, moe_w_down):
    p = dict(norm_mix=norm_mix, norm_ffn=norm_ffn, norm_final=norm_final, w_in=w_in, w_out=w_out,
             rwkv_mu=rwkv_mu, rwkv_w0=rwkv_w0, rwkv_w2=rwkv_w2, rwkv_a0=rwkv_a0, rwkv_a2=rwkv_a2,
             rwkv_g2=rwkv_g2, rwkv_kk=rwkv_kk, rwkv_ka=rwkv_ka, rwkv_rk=rwkv_rk, rwkv_ln=rwkv_ln,
             attn_q_norm=attn_q_norm, attn_k_norm=attn_k_norm, mlstm_conv_w=mlstm_conv_w,
             mlstm_conv_b=mlstm_conv_b, mlstm_i_bias=mlstm_i_bias, mlstm_f_bias=mlstm_f_bias,
             mlstm_ln=mlstm_ln, ret_ln=ret_ln, ffn_w_gate=ffn_w_gate, ffn_w_up=ffn_w_up,
             ffn_w_down=ffn_w_down, moe_router=moe_router, moe_w_gate=moe_w_gate, moe_w_up=moe_w_up,
             moe_w_down=moe_w_down)
    return _trunk((x_prompt, x_sample), p)
```

```python
import functools
import math

import numpy as np
import jax
import jax.numpy as jnp
from jax import lax
from jax.experimental import pallas as pl
from jax.experimental.pallas import tpu as pltpu

F32 = jnp.float32
BF16 = jnp.bfloat16

D_MODEL = 1024
HEAD_DIM = 64
W_G = 256
H_G = 4
KV_ATTN = 2
D_FF = 2816
N_EXPERTS = 8
NORM_EPS = 1e-6
HEAD_NORM_EPS = 1e-5
RWKV_GN_EPS = 64e-5
NEG_INF = -1e30
ROPE_THETA = 10000.0
GRID_W = 64

LANES = 128
ROW_ALIGN = 8
VMEM_LIMIT_BYTES = 56 * 1024 * 1024

U_RWKV = 0
U_MLSTM = 1024
U_RET = 2048
U_ATTN = 3072
U_GATE = 3584
U_COLS = 3712

RWKV_CHUNK = 64
RWKV_BLOCK = 256
RWKV_GROUP = 4
MIX_CHUNK = 128
MIX_BLOCK = 256
EXPERT_TILE = 256


def _cparams(sem):
    return pltpu.CompilerParams(dimension_semantics=sem, vmem_limit_bytes=VMEM_LIMIT_BYTES)


def _bdot(a, b, dims):
    return lax.dot_general(a, b, (dims, ((), ())), preferred_element_type=F32)


def _mm(a, b):
    return _bdot(a.astype(BF16), b.astype(BF16), ((1,), (0,)))


def _mm_nt(a, b):
    return _bdot(a.astype(BF16), b.astype(BF16), ((1,), (1,)))


def _mm_tn(a, b):
    return _bdot(a.astype(BF16), b.astype(BF16), ((0,), (0,)))


def _split2(a):
    hi = a.astype(BF16)
    lo = (a - hi.astype(F32)).astype(BF16)
    return hi, lo


def _split3(a):
    hi = a.astype(BF16)
    r = a - hi.astype(F32)
    mid = r.astype(BF16)
    lo = (r - mid.astype(F32)).astype(BF16)
    return hi, mid, lo


def _mm_l2(a, b_exact):
    hi, lo = _split2(a)
    return _bdot(hi, b_exact, ((1,), (0,))) + _bdot(lo, b_exact, ((1,), (0,)))


def _mm_l3(a, b_exact):
    h, m, l = _split3(a)
    return _bdot(h, b_exact, ((1,), (0,))) + _bdot(m, b_exact, ((1,), (0,))) + _bdot(l, b_exact, ((1,), (0,)))


def _mm_r3(a_exact, b):
    h, m, l = _split3(b)
    return _bdot(a_exact, h, ((1,), (0,))) + _bdot(a_exact, m, ((1,), (0,))) + _bdot(a_exact, l, ((1,), (0,)))


def _mm_x3(a, b):
    ah, al = _split2(a)
    bh, bl = _split2(b)
    d = ((1,), (0,))
    return _bdot(ah, bh, d) + _bdot(ah, bl, d) + _bdot(al, bh, d)


def _iota2(shape, axis):
    return lax.broadcasted_iota(jnp.int32, shape, axis)


def _head_mean_matrix(width):
    r = _iota2((width, width), 0) // HEAD_DIM
    c = _iota2((width, width), 1) // HEAD_DIM
    return jnp.where(r == c, 1.0 / HEAD_DIM, 0.0).astype(BF16)


def _head_mean(z, bd):
    return _mm_l2(z, bd)


def _sigmoid(x):
    return 1.0 / (1.0 + jnp.exp(-x))


def _silu(x):
    return x * _sigmoid(x)


def _log_sigmoid(x):
    return jnp.minimum(x, 0.0) - jnp.log(1.0 + jnp.exp(-jnp.abs(x)))


def _rms_norm_rows(x, gain):
    ms = jnp.mean(x * x, axis=-1, keepdims=True)
    return x * lax.rsqrt(ms + NORM_EPS) * gain


def _rope_swap(z):
    w = z.shape[-1]
    lane = _iota2(z.shape, z.ndim - 1)
    fwd = pltpu.roll(z, w - 16, z.ndim - 1)
    bwd = pltpu.roll(z, 16, z.ndim - 1)
    return jnp.where((lane % 32) < 16, fwd, bwd)


def _rope(z, cos, sin):
    return z * cos + _rope_swap(z) * sin


def _shift_rows(x, prev_row, next_row):
    n = x.shape[0]
    row = _iota2(x.shape, 0)
    prev = jnp.where(row == 0, prev_row, pltpu.roll(x, 1, 0))
    nxt = jnp.where(row == n - 1, next_row, pltpu.roll(x, n - 1, 0))
    return prev, nxt


def _part_tiles(parts, tm):
    return tuple(p.shape[0] // tm for p in parts)


def _part_specs(parts, tm, width):
    specs, start = [], 0
    for nt in _part_tiles(parts, tm):
        specs.append(pl.BlockSpec((tm, width), lambda i, s=start, nt=nt: (jnp.clip(i - s, 0, nt - 1), 0)))
        start += nt
    return specs


def _part_tile(i, refs, tiles):
    x = refs[0][...]
    start = tiles[0]
    for ref, nt in zip(refs[1:], tiles[1:]):
        x = jnp.where(i >= start, ref[...], x)
        start += nt
    return x


def _inproj_kernel(*refs, tiles):
    x_refs, (g_ref, w_ref, o_ref) = refs[:len(tiles)], refs[len(tiles):]
    h = _rms_norm_rows(_part_tile(pl.program_id(0), x_refs, tiles), g_ref[...])
    o_ref[...] = jnp.dot(h.astype(BF16), w_ref[...], preferred_element_type=F32)


def _in_proj(x_parts, gain, w_pad, tm):
    tiles = _part_tiles(x_parts, tm)
    n = sum(tiles) * tm
    return pl.pallas_call(
        functools.partial(_inproj_kernel, tiles=tiles),
        out_shape=jax.ShapeDtypeStruct((n, U_COLS), F32),
        grid=(n // tm,),
        in_specs=_part_specs(x_parts, tm, D_MODEL) + [
            pl.BlockSpec((1, D_MODEL), lambda i: (0, 0)),
            pl.BlockSpec((D_MODEL, U_COLS), lambda i: (0, 0), pipeline_mode=pl.Buffered(1)),
        ],
        out_specs=pl.BlockSpec((tm, U_COLS), lambda i: (i, 0)),
        compiler_params=_cparams(("arbitrary",)),
        name="in_proj",
    )(*x_parts, gain, w_pad)


def _attn_kernel(u_ref, cos_ref, sin_ref, qg_ref, kg_ref, o_ref, q_scr, k_scr, v_scr, *, tq):
    t = u_ref.shape[1]
    u = u_ref[0]
    q = u[:, :W_G]
    k = u[:, W_G:W_G + KV_ATTN * HEAD_DIM]
    v = u[:, W_G + KV_ATTN * HEAD_DIM:]
    cos = cos_ref[...]
    sin = sin_ref[...]
    bd_q = _head_mean_matrix(W_G)
    bd_k = _head_mean_matrix(KV_ATTN * HEAD_DIM)
    qn = q * lax.rsqrt(_head_mean(q * q, bd_q) + NORM_EPS) * qg_ref[...]
    kn = k * lax.rsqrt(_head_mean(k * k, bd_k) + NORM_EPS) * kg_ref[...]
    kw = KV_ATTN * HEAD_DIM
    q_scr[...] = (_rope(qn, cos, sin) * (HEAD_DIM ** -0.5 * math.log2(math.e))).astype(BF16)
    k_scr[...] = _rope(kn, cos[:, :kw], sin[:, :kw]).astype(BF16)
    ones = jnp.ones((t, HEAD_DIM), BF16)
    for j in range(KV_ATTN):
        vj = v[:, j * HEAD_DIM:(j + 1) * HEAD_DIM].astype(BF16)
        v_scr[:, j * LANES:(j + 1) * LANES] = jnp.concatenate([vj, ones], axis=1)
    group = H_G // KV_ATTN

    def q_tile(i, carry):
        rows = pl.ds(pl.multiple_of(i * tq, tq), tq)
        for j in range(KV_ATTN):
            kj = k_scr[:, j * HEAD_DIM:(j + 1) * HEAD_DIM]
            vj = v_scr[:, j * LANES:(j + 1) * LANES]
            for g in range(group):
                h = j * group + g
                qh = q_scr[rows, h * HEAD_DIM:(h + 1) * HEAD_DIM]
                s = _bdot(qh, kj, ((1,), (1,)))
                m = jnp.max(s, axis=-1, keepdims=True)
                p = jnp.exp2(s - m)
                r = jnp.dot(p.astype(BF16), vj, preferred_element_type=F32)
                o_ref[0, rows, h * HEAD_DIM:(h + 1) * HEAD_DIM] = r[:, :HEAD_DIM] / r[:, HEAD_DIM:]
        return carry

    lax.fori_loop(0, t // tq, q_tile, 0)


def _attention(u3, cos, sin, q_gain, k_gain):
    b, t, _ = u3.shape
    tq = min(256, t)
    col = U_ATTN // 512
    return pl.pallas_call(
        functools.partial(_attn_kernel, tq=tq),
        out_shape=jax.ShapeDtypeStruct((b, t, W_G), F32),
        grid=(b,),
        in_specs=[
            pl.BlockSpec((1, t, 512), lambda i: (i, 0, col)),
            pl.BlockSpec((t, W_G), lambda i: (0, 0)),
            pl.BlockSpec((t, W_G), lambda i: (0, 0)),
            pl.BlockSpec((1, W_G), lambda i: (0, 0)),
            pl.BlockSpec((1, KV_ATTN * HEAD_DIM), lambda i: (0, 0)),
        ],
        out_specs=pl.BlockSpec((1, t, W_G), lambda i: (i, 0, 0)),
        scratch_shapes=[
            pltpu.VMEM((t, W_G), BF16),
            pltpu.VMEM((t, KV_ATTN * HEAD_DIM), BF16),
            pltpu.VMEM((t, KV_ATTN * LANES), BF16),
        ],
        compiler_params=_cparams(("parallel",)),
        name="attention",
    )(u3, cos, sin, q_gain, k_gain)


def _ret_log_gamma(direction):
    return [math.log1p(-2.0 ** (-5.0 - (2 * h + direction) / 2.0)) for h in range(H_G)]


def _ret_kernel(uf_ref, ub_ref, cf_ref, sf_ref, cb_ref, sb_ref, of_ref, ob_ref, rf_scr, rb_scr):
    i = pl.program_id(1)
    c = MIX_CHUNK

    @pl.when(i == 0)
    def _():
        rf_scr[...] = jnp.zeros_like(rf_scr)
        rb_scr[...] = jnp.zeros_like(rb_scr)

    tt = _iota2((c, c), 0)
    ss = _iota2((c, c), 1)
    diff = (tt - ss).astype(F32)
    jcol = tt.astype(F32)
    first = ss < HEAD_DIM
    block_diag = (tt < HEAD_DIM) == first
    lg_f = _ret_log_gamma(0)
    lg_b = _ret_log_gamma(1)
    psl = lambda p: slice(p * LANES, (p + 1) * LANES)
    lane_lg = lambda lg, p: jnp.where(first, lg[2 * p], lg[2 * p + 1])
    row_decay = lambda lg, p: jnp.where(tt < HEAD_DIM, math.exp(c * lg[2 * p]), math.exp(c * lg[2 * p + 1]))
    own = lambda h, x: jnp.where(first, x, 0.0) if h % 2 == 0 else jnp.where(first, 0.0, x)
    pairs = range(H_G // 2)

    nch = uf_ref.shape[1] // c
    uf = uf_ref[0]
    qf_all = _rope(uf[:, :W_G], cf_ref[...], sf_ref[...])
    kf_all = _rope(uf[:, W_G:2 * W_G], cf_ref[...], sf_ref[...]) * HEAD_DIM ** -0.5
    ub = ub_ref[0]
    qb_all = _rope(ub[:, :W_G], cb_ref[...], sb_ref[...])
    kb_all = _rope(ub[:, W_G:2 * W_G], cb_ref[...], sb_ref[...]) * HEAD_DIM ** -0.5
    rf_prev = [rf_scr[p] for p in pairs]
    rb_prev = [rb_scr[p] for p in pairs]
    decay = [jnp.where(tt >= ss, jnp.exp(diff * lg_f[h]), 0.0) + jnp.where(ss >= tt, jnp.exp(-diff * lg_b[h]), 0.0)
             for h in range(H_G)]
    for n in range(nch):
        rf_rows = slice(n * c, (n + 1) * c)
        rb_rows = slice((nch - 1 - n) * c, (nch - n) * c)
        qf, kf, vf = qf_all[rf_rows], kf_all[rf_rows], uf[rf_rows, 2 * W_G:3 * W_G]
        qb, kb, vb = qb_all[rb_rows], kb_all[rb_rows], ub[rb_rows, 2 * W_G:3 * W_G]
        qk = [_mm_nt(qf[:, psl(h // 2)], own(h, kf[:, psl(h // 2)])) for h in range(H_G)]
        intra = [_mm(qk[h] * decay[h], own(h, vf[:, psl(h // 2)])) for h in range(H_G)]
        inter_f = [_mm(qf[:, psl(p)] * jnp.exp((jcol + 1.0) * lane_lg(lg_f, p)), rf_prev[p]) for p in pairs]
        inter_b = [_mm(qb[:, psl(p)] * jnp.exp((c - jcol) * lane_lg(lg_b, p)), rb_prev[p]) for p in pairs]
        upd_f = [_mm_tn(kf[:, psl(p)] * jnp.exp((c - 1.0 - jcol) * lane_lg(lg_f, p)), vf[:, psl(p)]) for p in pairs]
        upd_b = [_mm_tn(kb[:, psl(p)] * jnp.exp(jcol * lane_lg(lg_b, p)), vb[:, psl(p)]) for p in pairs]
        for p in pairs:
            of_ref[0, rf_rows, psl(p)] = intra[2 * p] + intra[2 * p + 1] + inter_f[p]
            ob_ref[0, rb_rows, psl(p)] = inter_b[p]
        rf_prev = [row_decay(lg_f, p) * rf_prev[p] + jnp.where(block_diag, upd_f[p], 0.0) for p in pairs]
        rb_prev = [row_decay(lg_b, p) * rb_prev[p] + jnp.where(block_diag, upd_b[p], 0.0) for p in pairs]
    for p in pairs:
        rf_scr[p] = rf_prev[p]
        rb_scr[p] = rb_prev[p]


def _retention(u3, cos, sin):
    b, t, _ = u3.shape
    c = min(MIX_BLOCK, t)
    nblk = t // c
    col = U_RET // 1024
    tab = lambda rev: pl.BlockSpec((c, W_G), (lambda bi, i: (nblk - 1 - i, 0)) if rev else (lambda bi, i: (i, 0)))
    return pl.pallas_call(
        _ret_kernel,
        out_shape=(jax.ShapeDtypeStruct((b, t, W_G), F32), jax.ShapeDtypeStruct((b, t, W_G), F32)),
        grid=(b, nblk),
        in_specs=[
            pl.BlockSpec((1, c, 1024), lambda bi, i: (bi, i, col)),
            pl.BlockSpec((1, c, 1024), lambda bi, i: (bi, nblk - 1 - i, col)),
            tab(False), tab(False), tab(True), tab(True),
        ],
        out_specs=(
            pl.BlockSpec((1, c, W_G), lambda bi, i: (bi, i, 0)),
            pl.BlockSpec((1, c, W_G), lambda bi, i: (bi, nblk - 1 - i, 0)),
        ),
        scratch_shapes=[pltpu.VMEM((H_G // 2, LANES, LANES), F32), pltpu.VMEM((H_G // 2, LANES, LANES), F32)],
        compiler_params=_cparams(("parallel", "arbitrary")),
        name="retention",
    )(u3, u3, cos, sin, cos, sin)


def _mlstm_tile(u_ref, up_ref, un_ref, g_ref, cw_ref, cb_ref, gb_ref, blk, nblk, direction):
    c = MIX_CHUNK
    reverse = direction == 1
    u = u_ref[0]
    qk = u[:, :2 * W_G]
    prev_row = jnp.where(blk == 0, 0.0, up_ref[0][ROW_ALIGN - 1:ROW_ALIGN, :])
    next_row = jnp.where(blk == nblk - 1, 0.0, un_ref[0][0:1, :])
    prev, nxt = _shift_rows(qk, prev_row, next_row)
    cw = cw_ref[...]
    qk = _silu(cw[0:1] * prev + cw[1:2] * qk + cw[2:3] * nxt + cb_ref[...])
    qa = qk[:, :W_G]
    ka = qk[:, W_G:] * HEAD_DIM ** -0.5
    va = u[:, 2 * W_G:3 * W_G]

    x = g_ref[0] + gb_ref[...]
    xt = x.T
    lf_c = _log_sigmoid(x)
    lf_r = _log_sigmoid(xt)
    tt = _iota2((c, c), 0)
    ss = _iota2((c, c), 1)
    lower = jnp.where(ss <= tt, 1.0, 0.0).astype(BF16)
    upper = jnp.where(tt <= ss, 1.0, 0.0).astype(BF16)
    if reverse:
        b_c = _mm_r3(upper, lf_c)
        b_r = _mm_l3(lf_r, lower)
        mask = ss >= tt
    else:
        b_c = _mm_r3(lower, lf_c)
        b_r = _mm_l3(lf_r, upper)
        mask = ss <= tt
    g_all = jnp.sum(lf_c, axis=0, keepdims=True)
    return dict(q=qa, k=ka, v=va, x=x, xt=xt, b_c=b_c, b_r=b_r, g_all=g_all, mask=mask)


def _mlstm_select_matrix():
    sel = np.zeros((2, 2 * LANES, 8 * LANES), np.float32)
    for d in range(2):
        for h in range(H_G):
            ci, cf = d * H_G + h, 2 * H_G + d * H_G + h
            p, j = divmod(h, 2)
            sel[d, cf, LANES * h:LANES * (h + 1)] = 1.0
            sel[d, cf, 4 * LANES + LANES * p + HEAD_DIM * j:4 * LANES + LANES * p + HEAD_DIM * (j + 1)] = 1.0
            sel[d, LANES + ci, 6 * LANES + LANES * p + HEAD_DIM * j:6 * LANES + LANES * p + HEAD_DIM * (j + 1)] = 1.0
    return jnp.asarray(sel, BF16)


def _mlstm_kernel(uf_ref, upf_ref, unf_ref, gf_ref, ub_ref, upb_ref, unb_ref, gbk_ref,
                  cw_ref, cb_ref, gb_ref, sel_ref, of_ref, ob_ref, st_scr, m_scr):
    i = pl.program_id(1)
    nblk = pl.num_programs(1)
    c = MIX_CHUNK

    @pl.when(i == 0)
    def _():
        st_scr[...] = jnp.zeros_like(st_scr)
        m_scr[...] = jnp.zeros_like(m_scr)

    tiles = (_mlstm_tile(uf_ref, upf_ref, unf_ref, gf_ref, cw_ref, cb_ref, gb_ref, i, nblk, 0),
             _mlstm_tile(ub_ref, upb_ref, unb_ref, gbk_ref, cw_ref, cb_ref, gb_ref, nblk - 1 - i, nblk, 1))
    o_refs = (of_ref, ob_ref)
    sel = [_mm_l2(jnp.concatenate([tiles[d]["b_c"], tiles[d]["x"]], axis=1), sel_ref[d]) for d in range(2)]
    first = _iota2((c, LANES), 1) < HEAD_DIM
    row_first = _iota2((LANES, 2 * LANES), 0) < HEAD_DIM
    lane2 = _iota2((LANES, 2 * LANES), 1) % LANES < HEAD_DIM
    block_diag = row_first == lane2
    ones = jnp.ones((c, LANES), F32)
    pairs = [(d, p) for d in range(2) for p in range(H_G // 2)]
    heads = [(d, h) for d in range(2) for h in range(H_G)]
    psl = lambda p: slice(p * LANES, (p + 1) * LANES)
    q_pair = {dp: tiles[dp[0]]["q"][:, psl(dp[1])] for dp in pairs}
    k_pair = {dp: tiles[dp[0]]["k"][:, psl(dp[1])] for dp in pairs}
    v_pair = {dp: tiles[dp[0]]["v"][:, psl(dp[1])] for dp in pairs}
    state = {dp: st_scr[n] for n, dp in enumerate(pairs)}
    m_row = {dp: m_scr[n:n + 1, :] for n, dp in enumerate(pairs)}

    def own(d, h, x):
        return jnp.where(first, x, 0.0) if h % 2 == 0 else jnp.where(first, 0.0, x)

    qk = [_mm_nt(q_pair[(d, h // 2)], own(d, h, k_pair[(d, h // 2)])) for d, h in heads]
    qs = {dp: _mm(q_pair[dp], state[dp]) for dp in pairs}
    bc = [sel[d][:, LANES * h:LANES * (h + 1)] for d, h in heads]
    m_prev = [m_row[(d, h // 2)][:, HEAD_DIM * (h % 2):HEAD_DIM * (h % 2) + 1] for d, h in heads]
    dlog = []
    for (d, h), bc_ in zip(heads, bc):
        ci, cf = d * H_G + h, 2 * H_G + d * H_G + h
        rowterm = tiles[d]["xt"][ci:ci + 1, :] - tiles[d]["b_r"][cf:cf + 1, :]
        dlog.append(jnp.where(tiles[d]["mask"], bc_ + rowterm, NEG_INF))
    inter_log = [bc_ + m_ for bc_, m_ in zip(bc, m_prev)]
    m_t = [jnp.maximum(il, jnp.max(dl, axis=-1, keepdims=True)) for il, dl in zip(inter_log, dlog)]
    sc = [qk_ * jnp.exp(dl - mt) for qk_, dl, mt in zip(qk, dlog, m_t)]
    w_inter = [jnp.exp(il - mt) for il, mt in zip(inter_log, m_t)]
    e_neg = [jnp.exp(-mt) for mt in m_t]
    res = []
    for n, (d, h) in enumerate(heads):
        v_aug = jnp.concatenate([own(d, h, v_pair[(d, h // 2)]), own(d, h, ones)], axis=1)
        res.append(_mm(sc[n], v_aug))
    for n, (d, p) in enumerate(pairs):
        a, b_ = 2 * n, 2 * n + 1
        tot = res[a] + res[b_] + jnp.tile(jnp.where(first, w_inter[a], w_inter[b_]), (1, 2)) * qs[(d, p)]
        den = jnp.maximum(jnp.abs(tot[:, LANES:]), jnp.where(first, e_neg[a], e_neg[b_]))
        o_refs[d][0, :, psl(p)] = tot[:, :LANES] / den

    for n, (d, p) in enumerate(pairs):
        bcp = sel[d][:, 4 * LANES + LANES * p:4 * LANES + LANES * (p + 1)]
        lip = sel[d][:, 6 * LANES + LANES * p:6 * LANES + LANES * (p + 1)]
        g_row = bcp[0:1, :] if d == 1 else bcp[c - 1:c, :]
        a_p = g_row - bcp + lip
        m_new = jnp.maximum(g_row + m_row[(d, p)], jnp.max(a_p, axis=0, keepdims=True))
        dec = jnp.exp(g_row + m_row[(d, p)] - m_new)
        kw_t = (k_pair[(d, p)] * jnp.exp(a_p - m_new)).T
        upd = _mm(kw_t, jnp.concatenate([v_pair[(d, p)], ones], axis=1))
        dec_tile = jnp.where(row_first, dec[:, 0:1], dec[:, HEAD_DIM:HEAD_DIM + 1])
        st_scr[n] = dec_tile * state[(d, p)] + jnp.where(block_diag, upd, 0.0)
        m_scr[n:n + 1, :] = m_new


def _mlstm(u3, conv_w, conv_b, gate_bias):
    b, t, _ = u3.shape
    c = MIX_CHUNK
    nblk = t // c
    rpb = c // ROW_ALIGN
    n8 = t // ROW_ALIGN
    col = U_MLSTM // 1024
    hcol = U_MLSTM // 512
    gcol = U_GATE // LANES

    def specs(rev):
        blk = (lambda i: nblk - 1 - i) if rev else (lambda i: i)
        return [
            pl.BlockSpec((1, c, 1024), lambda bi, i: (bi, blk(i), col)),
            pl.BlockSpec((1, ROW_ALIGN, 512), lambda bi, i: (bi, jnp.maximum(blk(i) * rpb - 1, 0), hcol)),
            pl.BlockSpec((1, ROW_ALIGN, 512), lambda bi, i: (bi, jnp.minimum((blk(i) + 1) * rpb, n8 - 1), hcol)),
            pl.BlockSpec((1, c, LANES), lambda bi, i: (bi, blk(i), gcol)),
        ]

    const = lambda shape: pl.BlockSpec(shape, lambda bi, i: (0,) * len(shape))
    return pl.pallas_call(
        _mlstm_kernel,
        out_shape=(jax.ShapeDtypeStruct((b, t, W_G), F32), jax.ShapeDtypeStruct((b, t, W_G), F32)),
        grid=(b, nblk),
        in_specs=specs(False) + specs(True) + [const((3, 2 * W_G)), const((1, 2 * W_G)), const((1, LANES)),
                                               const((2, 2 * LANES, 8 * LANES))],
        out_specs=(
            pl.BlockSpec((1, c, W_G), lambda bi, i: (bi, i, 0)),
            pl.BlockSpec((1, c, W_G), lambda bi, i: (bi, nblk - 1 - i, 0)),
        ),
        scratch_shapes=[pltpu.VMEM((H_G, LANES, 2 * LANES), F32), pltpu.VMEM((ROW_ALIGN, LANES), F32)],
        compiler_params=_cparams(("parallel", "arbitrary")),
        name="mlstm",
    )(u3, u3, u3, u3, u3, u3, u3, u3, conv_w, conv_b, gate_bias, _mlstm_select_matrix())


def _rwkv_prep_kernel(u_ref, up_ref, un_ref, mu_ref, w0_ref, w2_ref, a0_ref, a2_ref, g2_ref, kks_ref, ka_ref, rk_ref,
                      r_ref, v_ref, kk_ref, g_ref, bonus_ref, lwf_ref, lwb_ref, kf_ref, kb_ref, bf_ref, bb_ref):
    i = pl.program_id(1)
    nblk = pl.num_programs(1)
    u = u_ref[0]
    prev_row = jnp.where(i == 0, 0.0, up_ref[0][ROW_ALIGN - 1:ROW_ALIGN, :])
    next_row = jnp.where(i == nblk - 1, 0.0, un_ref[0][0:1, :])
    prev, nxt = _shift_rows(u, prev_row, next_row)
    us = u + mu_ref[...] * (0.5 * (prev + nxt) - u)
    r = us[:, 0:W_G]
    k = us[:, W_G:2 * W_G]
    v = us[:, 2 * W_G:3 * W_G]
    xw = us[:, 3 * W_G:3 * W_G + 64]
    xa = us[:, 3 * W_G + 64:3 * W_G + 128]
    xg = us[:, 3 * W_G + 128:]
    bd = _head_mean_matrix(W_G)
    g = _mm(_sigmoid(xg), g2_ref[...])
    lw = jnp.tanh(xw)
    a_lr = _mm_x3(xa, a2_ref[...])
    kk = k * kks_ref[...]
    kk = kk * lax.rsqrt(_head_mean(kk * kk, bd) * HEAD_DIM + 1e-12)
    r_ref[0] = r
    v_ref[0] = v
    kk_ref[0] = kk
    g_ref[0] = g
    bonus = jnp.zeros_like(r)
    for d, (lw_ref, k_ref, b_ref) in enumerate(((lwf_ref, kf_ref, bf_ref), (lwb_ref, kb_ref, bb_ref))):
        z = w0_ref[d:d + 1, :] + _mm_x3(lw, w2_ref[d])
        lw_ref[0] = -_sigmoid(z) * math.exp(-0.5)
        a = _sigmoid(a0_ref[d:d + 1, :] + a_lr)
        kd = k * (1.0 + (a - 1.0) * ka_ref[...])
        k_ref[0] = kd
        b_ref[0] = kk * a
        bonus = bonus + _head_mean(r * kd * rk_ref[...], bd) * HEAD_DIM * v
    bonus_ref[0] = bonus


def _rwkv_prep(u3, mu, w0, w2, a0, a2, g2, kks, ka, rk):
    b, t, _ = u3.shape
    tb = min(RWKV_BLOCK, t)
    nblk = t // tb
    rpb = tb // ROW_ALIGN
    n8 = t // ROW_ALIGN
    const = lambda shape: pl.BlockSpec(shape, lambda bi, i: (0,) * len(shape))
    out = jax.ShapeDtypeStruct((b, t, W_G), F32)
    ospec = pl.BlockSpec((1, tb, W_G), lambda bi, i: (bi, i, 0))
    return pl.pallas_call(
        _rwkv_prep_kernel,
        out_shape=(out,) * 11,
        grid=(b, nblk),
        in_specs=[
            pl.BlockSpec((1, tb, 1024), lambda bi, i: (bi, i, 0)),
            pl.BlockSpec((1, ROW_ALIGN, 1024), lambda bi, i: (bi, jnp.maximum(i * rpb - 1, 0), 0)),
            pl.BlockSpec((1, ROW_ALIGN, 1024), lambda bi, i: (bi, jnp.minimum((i + 1) * rpb, n8 - 1), 0)),
            const((1, 1024)), const((2, W_G)), const((2, 64, W_G)), const((2, W_G)), const((64, W_G)),
            const((128, W_G)), const((1, W_G)), const((1, W_G)), const((1, W_G)),
        ],
        out_specs=(ospec,) * 11,
        compiler_params=_cparams(("parallel", "parallel")),
        name="rwkv_prep",
    )(u3, u3, u3, mu, w0, w2, a0, a2, g2, kks, ka, rk)


def _tri_inverse_all(lmats, n):
    r = _iota2((n, n), 0)
    c = _iota2((n, n), 1)
    eye = jnp.where(r == c, 1.0, 0.0)
    pair = (r // 2 == c // 2) & (r != c)
    invs = [eye + jnp.where(pair, lm, 0.0) for lm in lmats]
    s = 2
    while s < n:
        sel = (r // (2 * s) == c // (2 * s)) & (r // s != c // s)
        offs = [jnp.where(sel, -lm, 0.0) for lm in lmats]
        xs = [_mm(inv, off) for inv, off in zip(invs, offs)]
        invs = [inv - _mm(x, inv) for inv, x in zip(invs, xs)]
        s *= 2
    return invs


def _rwkv_tile_terms(r, k, v, kk, b, lw, reverse):
    c = r.shape[0]
    tt = _iota2((c, c), 0)
    ss = _iota2((c, c), 1)
    tri = jnp.where((tt <= ss) if reverse else (ss <= tt), 1.0, 0.0).astype(BF16)
    cum_in = _mm_r3(tri, lw)
    cum_all = jnp.sum(lw, axis=0, keepdims=True)
    e_neg = jnp.exp(-cum_in)
    e_end = jnp.exp(cum_all - cum_in)
    return dict(at=-kk * jnp.exp(cum_in - lw), rt=r * jnp.exp(cum_in), bt=b * e_neg, kt=k * e_neg,
                gb=b * e_end, gk=k * e_end, v=v, e_all=jnp.exp(cum_all))


def _rwkv_chunk_terms(tiles, reverses):
    c = RWKV_CHUNK
    tt = _iota2((c, c), 0)
    ss = _iota2((c, c), 1)
    heads = [(ti, h) for ti in range(len(tiles)) for h in range(H_G)]
    sl = lambda h: slice(h * HEAD_DIM, (h + 1) * HEAD_DIM)
    get = lambda name: [tiles[ti][name][:, sl(h)] for ti, h in heads]
    at, rt, bt, kt, gb, gk, v = (get(nm) for nm in ("at", "rt", "bt", "kt", "gb", "gk", "v"))
    strict = [(ss > tt) if reverses[ti] else (ss < tt) for ti, _ in heads]
    incl = [(ss >= tt) if reverses[ti] else (ss <= tt) for ti, _ in heads]
    ps = [_mm_nt(jnp.concatenate([a, r_], axis=0), jnp.concatenate([b_, k_], axis=0))
          for a, r_, b_, k_ in zip(at, rt, bt, kt)]
    l_ab = [jnp.where(m, p[:c, :c], 0.0) for m, p in zip(strict, ps)]
    l_ak = [jnp.where(m, p[:c, c:], 0.0) for m, p in zip(strict, ps)]
    m_r = [jnp.concatenate([jnp.where(m, p[c:, :c], 0.0), jnp.where(m, p[c:, c:], 0.0)], axis=1)
           for m, p in zip(incl, ps)]
    lakv = [_mm(l, v_) for l, v_ in zip(l_ak, v)]
    invs = _tri_inverse_all(l_ab, c)
    tw = [_mm(inv, jnp.concatenate([a, lv], axis=1)) for inv, a, lv in zip(invs, at, lakv)]
    zeros = jnp.zeros((c, HEAD_DIM), F32)
    mm2 = [_mm(m, jnp.concatenate([t_, jnp.concatenate([zeros, v_], axis=1)], axis=0))
           for m, t_, v_ in zip(m_r, tw, v)]
    r1 = [r_ + m[:, :HEAD_DIM] for r_, m in zip(rt, mm2)]
    y0 = [m[:, HEAD_DIM:] for m in mm2]
    twg = [_mm_tn(t_, g_) for t_, g_ in zip(tw, gb)]
    vgk = [_mm_tn(v_, g_) for v_, g_ in zip(v, gk)]
    mlow = [x[:HEAD_DIM] for x in twg]
    nadd = [x[HEAD_DIM:] + y for x, y in zip(twg, vgk)]
    e_all = [tiles[ti]["e_all"][:, sl(h)] for ti, h in heads]
    return r1, y0, mlow, nadd, e_all


def _rwkv_core_kernel(rf_ref, kf_ref, vf_ref, kkf_ref, bf_ref, lwf_ref, rb_ref, kb_ref, vb_ref, kkb_ref, bb_ref,
                      lwb_ref, yf_ref, yb_ref, s_scr, *, group):
    i = pl.program_id(1)
    c = RWKV_CHUNK
    nch = rf_ref.shape[1] // c
    dirs = ((rf_ref, kf_ref, vf_ref, kkf_ref, bf_ref, lwf_ref), (rb_ref, kb_ref, vb_ref, kkb_ref, bb_ref, lwb_ref))
    y_refs = (yf_ref, yb_ref)

    @pl.when(i == 0)
    def _():
        s_scr[...] = jnp.zeros_like(s_scr)

    def step(j, states):
        tiles, reverses, rows = [], [], []
        for q in range(group):
            for d in range(2):
                cj = j * group + q
                cj = cj if d == 0 else nch - 1 - cj
                rw = pl.ds(pl.multiple_of(cj * c, c), c)
                tiles.append(_rwkv_tile_terms(*(ref[0, rw, :] for ref in dirs[d]), d == 1))
                reverses.append(d == 1)
                rows.append(rw)
        r1, y0, mlow, nadd, e_all = _rwkv_chunk_terms(tiles, reverses)
        states = list(states)
        for q in range(group):
            ys = [[], []]
            for d in range(2):
                for h in range(H_G):
                    n = (q * 2 + d) * H_G + h
                    s = states[d * H_G + h]
                    ys[d].append(_mm_nt(r1[n], s) + y0[n])
                    states[d * H_G + h] = s * e_all[n] + _mm(s, mlow[n]) + nadd[n]
            for d in range(2):
                y_refs[d][0, rows[q * 2 + d], :] = jnp.concatenate(ys[d], axis=1)
        return tuple(states)

    init = tuple(s_scr[n] for n in range(2 * H_G))
    if nch == group:
        states = step(0, init)
    else:
        states = lax.fori_loop(0, nch // group, step, init)
    for n in range(2 * H_G):
        s_scr[n] = states[n]


def _rwkv_core(r, v, kk, lwf, lwb, kf, kb, bf, bb):
    b, t, _ = r.shape
    tb = min(RWKV_BLOCK, t)
    nblk = t // tb
    fwd = pl.BlockSpec((1, tb, W_G), lambda bi, i: (bi, i, 0))
    bwd = pl.BlockSpec((1, tb, W_G), lambda bi, i: (bi, nblk - 1 - i, 0))
    out = jax.ShapeDtypeStruct((b, t, W_G), F32)
    return pl.pallas_call(
        functools.partial(_rwkv_core_kernel, group=min(RWKV_GROUP, tb // RWKV_CHUNK)),
        out_shape=(out, out),
        grid=(b, nblk),
        in_specs=[fwd] * 6 + [bwd] * 6,
        out_specs=(fwd, bwd),
        scratch_shapes=[pltpu.VMEM((2 * H_G, HEAD_DIM, HEAD_DIM), F32)],
        compiler_params=_cparams(("parallel", "arbitrary")),
        name="rwkv_core",
    )(r, kf, v, kk, bf, lwf, r, kb, v, kk, bb, lwb)


def _group_norm(y, gain, eps, bd):
    mu = _head_mean(y, bd)
    d = y - mu
    var = _head_mean(d * d, bd)
    return d * lax.rsqrt(var + eps) * gain


N_MIX_REFS = 15


def _mix_residual(refs, tiles):
    x_refs = refs[:len(tiles)]
    (ryf_ref, ryb_ref, rbon_ref, rg_ref, at_ref, mhf_ref, mhb_ref, mo_ref, tof_ref, tob_ref,
     tg_ref, rln_ref, mln_ref, tln_ref, w_ref) = refs[len(tiles):]
    bd = _head_mean_matrix(W_G)
    o_a = (_group_norm(ryf_ref[...] + ryb_ref[...], rln_ref[...], RWKV_GN_EPS, bd) + rbon_ref[...]) * rg_ref[...]
    o_c = _group_norm(mhf_ref[...] + mhb_ref[...], mln_ref[...], HEAD_NORM_EPS, bd) * _sigmoid(mo_ref[...])
    o_d = _group_norm(tof_ref[...] + tob_ref[...], tln_ref[...], HEAD_NORM_EPS, bd) * _silu(tg_ref[...])
    mix = jnp.concatenate([o_a, at_ref[...], o_c, o_d], axis=1).astype(BF16)
    return _part_tile(pl.program_id(0), x_refs, tiles) + jnp.dot(mix, w_ref[...], preferred_element_type=F32)


def _mix_call(kernel, x_parts, rw, at, ml, rt, u2, rln, mln, tln, w_out, tm, extra, extra_specs, out_shape, out_specs,
              scratch_shapes, name):
    tiles = _part_tiles(x_parts, tm)
    n = sum(tiles) * tm
    row = lambda w: pl.BlockSpec((tm, w), lambda i: (i, 0))
    ucol = lambda off: pl.BlockSpec((tm, W_G), lambda i: (i, off // W_G))
    const = lambda shape: pl.BlockSpec(shape, lambda i: (0, 0))
    return pl.pallas_call(
        functools.partial(kernel, tiles=tiles),
        out_shape=out_shape,
        grid=(n // tm,),
        in_specs=_part_specs(x_parts, tm, D_MODEL) + [row(W_G)] * 7 + [ucol(U_MLSTM + 3 * W_G)] + [row(W_G)] * 2
                 + [ucol(U_RET + 3 * W_G)] + [const((1, W_G))] * 3
                 + [pl.BlockSpec((D_MODEL, D_MODEL), lambda i: (0, 0), pipeline_mode=pl.Buffered(1))] + extra_specs,
        out_specs=out_specs,
        scratch_shapes=scratch_shapes,
        compiler_params=_cparams(("arbitrary",)),
        name=name,
    )(*x_parts, *rw, at, *ml, u2, *rt, u2, rln, mln, tln, w_out, *extra)


def _mix_ffn_kernel(*refs, tiles):
    n_in = len(tiles) + N_MIX_REFS
    g_ref, wg_ref, wu_ref, wd_ref, o_ref = refs[n_in:]
    x = _mix_residual(refs[:n_in], tiles)
    h = _rms_norm_rows(x, g_ref[...]).astype(BF16)
    a = jnp.dot(h, wg_ref[...], preferred_element_type=F32)
    b = jnp.dot(h, wu_ref[...], preferred_element_type=F32)
    z = (_silu(a) * b).astype(BF16)
    o_ref[...] = x + jnp.dot(z, wd_ref[...], preferred_element_type=F32)


def _mix_ffn(mix_args, gain, wg, wu, wd, tm):
    n = sum(_part_tiles(mix_args[0], tm)) * tm
    res = lambda shape: pl.BlockSpec(shape, lambda i: (0, 0), pipeline_mode=pl.Buffered(1))
    return _mix_call(_mix_ffn_kernel, *mix_args, tm, (gain, wg, wu, wd),
                     [pl.BlockSpec((1, D_MODEL), lambda i: (0, 0)), res((D_MODEL, D_FF)), res((D_MODEL, D_FF)),
                      res((D_FF, D_MODEL))],
                     jax.ShapeDtypeStruct((n, D_MODEL), F32), pl.BlockSpec((tm, D_MODEL), lambda i: (i, 0)), [],
                     "mix_ffn")


def _mix_router_kernel(*refs, tiles):
    n_in = len(tiles) + N_MIX_REFS
    g_ref, wr_ref, x_ref, h_ref, gate_ref, idx_ref, cnt_ref, cnt_scr = refs[n_in:]

    @pl.when(pl.program_id(0) == 0)
    def _():
        cnt_scr[...] = jnp.zeros_like(cnt_scr)

    x = _mix_residual(refs[:n_in], tiles)
    x_ref[...] = x
    h = _rms_norm_rows(x, g_ref[...])
    h_ref[...] = h.astype(BF16)
    logits = _mm_x3(h, wr_ref[...])
    lane = _iota2(logits.shape, 1)
    logits = jnp.where(lane < N_EXPERTS, logits, NEG_INF)
    e = jnp.exp(logits - jnp.max(logits, axis=-1, keepdims=True))
    p = e / jnp.sum(e, axis=-1, keepdims=True)
    p = jnp.where(lane < N_EXPERTS, p, -1.0)
    m1 = jnp.max(p, axis=-1, keepdims=True)
    i1 = jnp.min(jnp.where(p == m1, lane, LANES), axis=-1, keepdims=True)
    p2 = jnp.where(lane == i1, -1.0, p)
    m2 = jnp.max(p2, axis=-1, keepdims=True)
    i2 = jnp.min(jnp.where(p2 == m2, lane, LANES), axis=-1, keepdims=True)
    tot = m1 + m2
    gate_ref[...] = jnp.where(lane == 0, m1 / tot, jnp.where(lane == 1, m2 / tot, 0.0))
    tm = logits.shape[0]
    chosen = jnp.where((lane == i1) | (lane == i2), 1.0, 0.0)
    tril = jnp.where(_iota2((tm, tm), 1) <= _iota2((tm, tm), 0), 1.0, 0.0).astype(BF16)
    incl = jnp.dot(tril, chosen.astype(BF16), preferred_element_type=F32)
    rank = cnt_scr[...] + incl - chosen
    r1 = jnp.sum(jnp.where(lane == i1, rank, 0.0), axis=-1, keepdims=True)
    r2 = jnp.sum(jnp.where(lane == i2, rank, 0.0), axis=-1, keepdims=True)
    info = jnp.where(lane == 0, i1.astype(F32), jnp.where(lane == 1, i2.astype(F32),
                                                          jnp.where(lane == 2, r1, jnp.where(lane == 3, r2, 0.0))))
    idx_ref[...] = info.T[:ROW_ALIGN, :].astype(jnp.int32)
    cnt_scr[...] = cnt_scr[...] + incl[tm - 1:tm, :]
    cnt_ref[...] = jnp.broadcast_to(cnt_scr[...], cnt_ref.shape).astype(jnp.int32)


def _mix_router(mix_args, gain, wr_pad, tm):
    n = sum(_part_tiles(mix_args[0], tm)) * tm
    return _mix_call(
        _mix_router_kernel, *mix_args, tm, (gain, wr_pad),
        [pl.BlockSpec((1, D_MODEL), lambda i: (0, 0)), pl.BlockSpec((D_MODEL, LANES), lambda i: (0, 0))],
        (jax.ShapeDtypeStruct((n, D_MODEL), F32), jax.ShapeDtypeStruct((n, D_MODEL), BF16),
         jax.ShapeDtypeStruct((n, LANES), F32), jax.ShapeDtypeStruct((ROW_ALIGN, n), jnp.int32),
         jax.ShapeDtypeStruct((ROW_ALIGN, LANES), jnp.int32)),
        (pl.BlockSpec((tm, D_MODEL), lambda i: (i, 0)), pl.BlockSpec((tm, D_MODEL), lambda i: (i, 0)),
         pl.BlockSpec((tm, LANES), lambda i: (i, 0)), pl.BlockSpec((ROW_ALIGN, tm), lambda i: (0, i)),
         pl.BlockSpec((ROW_ALIGN, LANES), lambda i: (0, 0))),
        [pltpu.VMEM((1, LANES), F32)], "mix_router")


def _expert_ffn_kernel(te_ref, nv_ref, xs_ref, wg_ref, wu_ref, wd_ref, o_ref):
    i = pl.program_id(0)

    @pl.when(i < nv_ref[0])
    def _():
        h = xs_ref[...]
        a = jnp.dot(h, wg_ref[0], preferred_element_type=F32)
        b = jnp.dot(h, wu_ref[0], preferred_element_type=F32)
        z = (_silu(a) * b).astype(BF16)
        o_ref[...] = jnp.dot(z, wd_ref[0], preferred_element_type=F32).astype(o_ref.dtype)

    @pl.when(i >= nv_ref[0])
    def _():
        o_ref[...] = jnp.zeros_like(o_ref)


def _expert_ffn(xs, tile_expert, n_valid, wg, wu, wd, tm):
    rows = xs.shape[0]
    wspec = lambda shape: pl.BlockSpec((1,) + shape, lambda i, te, nv: (te[i], 0, 0))
    return pl.pallas_call(
        _expert_ffn_kernel,
        out_shape=jax.ShapeDtypeStruct((rows, D_MODEL), BF16),
        grid_spec=pltpu.PrefetchScalarGridSpec(
            num_scalar_prefetch=2,
            grid=(rows // tm,),
            in_specs=[pl.BlockSpec((tm, D_MODEL), lambda i, te, nv: (i, 0)),
                      wspec((D_MODEL, D_FF)), wspec((D_MODEL, D_FF)), wspec((D_FF, D_MODEL))],
            out_specs=pl.BlockSpec((tm, D_MODEL), lambda i, te, nv: (i, 0)),
        ),
        compiler_params=_cparams(("arbitrary",)),
        name="expert_ffn",
    )(tile_expert, n_valid, xs, wg, wu, wd)


def _moe_combine_kernel(x_ref, y1_ref, y2_ref, gate_ref, nf_ref, *o_refs, tiles):
    i = pl.program_id(0)
    g = gate_ref[...]
    y = x_ref[...] + g[:, 0:1] * y1_ref[...].astype(F32) + g[:, 1:2] * y2_ref[...].astype(F32)
    out = _rms_norm_rows(y, nf_ref[...])
    start = 0
    for o_ref, nt in zip(o_refs, tiles):
        @pl.when((i >= start) & (i < start + nt))
        def _(o_ref=o_ref):
            o_ref[...] = out
        start += nt


def _moe_combine(x2, y1, y2, gates, norm_final, tm, part_rows):
    n = x2.shape[0]
    row = lambda w: pl.BlockSpec((tm, w), lambda i: (i, 0))
    outs = tuple(jax.ShapeDtypeStruct((r, D_MODEL), F32) for r in part_rows)
    return pl.pallas_call(
        functools.partial(_moe_combine_kernel, tiles=_part_tiles(outs, tm)),
        out_shape=outs,
        grid=(n // tm,),
        in_specs=[row(D_MODEL), row(D_MODEL), row(D_MODEL), row(LANES), pl.BlockSpec((1, D_MODEL), lambda i: (0, 0))],
        out_specs=tuple(_part_specs(outs, tm, D_MODEL)),
        compiler_params=_cparams(("arbitrary",)),
        name="moe_combine",
    )(x2, y1, y2, gates, norm_final)


def _moe(x2, h, gates, idx, counts, wg, wu, wd, norm_final, tm, part_rows):
    n = x2.shape[0]
    tme = EXPERT_TILE
    n_tiles = (2 * n + N_EXPERTS * (tme - 1)) // tme + 1
    e1, e2, r1, r2 = idx[0], idx[1], idx[2], idx[3]
    cnt = counts[0, :N_EXPERTS]
    padded = (cnt + tme - 1) // tme * tme
    group_end = jnp.cumsum(padded)
    group_off = group_end - padded
    dense_off = jnp.cumsum(cnt) - cnt
    slot1 = group_off[e1] + r1
    slot2 = group_off[e2] + r2
    tok = jnp.arange(n, dtype=jnp.int32)
    sorted_tok = jnp.sort(jnp.concatenate([e1 * n + tok, e2 * n + tok])) % n
    tile_start = jnp.arange(n_tiles, dtype=jnp.int32) * tme
    tile_expert = jnp.minimum(jnp.searchsorted(group_end, tile_start, side='right'), N_EXPERTS - 1).astype(jnp.int32)
    n_valid = (group_end[-1:] // tme).astype(jnp.int32)
    row = jnp.arange(n_tiles * tme, dtype=jnp.int32)
    row_e = jnp.repeat(tile_expert, tme)
    rank = row - group_off[row_e]
    src = jnp.where(rank < cnt[row_e], sorted_tok[jnp.clip(dense_off[row_e] + rank, 0, 2 * n - 1)], 0)
    xs = jnp.take(h, src, axis=0)
    ys = _expert_ffn(xs, tile_expert, n_valid, wg, wu, wd, tme)
    y1 = jnp.take(ys, slot1, axis=0)
    y2 = jnp.take(ys, slot2, axis=0)
    return _moe_combine(x2, y1, y2, gates, norm_final, tm, part_rows)


def _rope_tables(t):
    rows = t // GRID_W
    pos = np.arange(rows * GRID_W)
    row = (pos // GRID_W).astype(np.float32)
    col = (pos % GRID_W).astype(np.float32)
    nf = HEAD_DIM // 4
    inv = jnp.asarray(ROPE_THETA, F32) ** (-jnp.arange(nf, dtype=F32) / nf)
    ar = jnp.asarray(row)[:, None] * inv
    ac = jnp.asarray(col)[:, None] * inv
    cos = jnp.concatenate([jnp.cos(ar), jnp.cos(ar), jnp.cos(ac), jnp.cos(ac)], axis=-1)
    sin = jnp.concatenate([-jnp.sin(ar), jnp.sin(ar), -jnp.sin(ac), jnp.sin(ac)], axis=-1)
    return jnp.tile(cos, (1, H_G)), jnp.tile(sin, (1, H_G))


def _pad_w_in(w):
    a, b_, c, d = 1024, 512, 1040, 1024
    w_a, w_b, w_c, w_d = w[:, :a], w[:, a:a + b_], w[:, a + b_:a + b_ + c], w[:, a + b_ + c:]
    gates = jnp.pad(w_c[:, 1024:], ((0, 0), (0, LANES - 16)))
    return jnp.concatenate([w_a, w_c[:, :1024], w_d, w_b, gates], axis=1).astype(BF16)


def _row(v):
    return v.reshape(1, -1).astype(F32)


def _trunk(xs, p):
    t = xs[0].shape[1]
    part_rows = tuple(x.shape[0] * t for x in xs)
    b = sum(x.shape[0] for x in xs)
    n = b * t
    tm = 256
    cos, sin = _rope_tables(t)
    x_parts = tuple(x.reshape(-1, D_MODEL) for x in xs)
    depth = p['w_in'].shape[0]
    for l in range(depth):
        u2 = _in_proj(x_parts, _row(p['norm_mix'][l]), _pad_w_in(p['w_in'][l]), tm)
        u3 = u2.reshape(b, t, U_COLS)
        (r, v, kk, g, bonus, lwf, lwb, kf, kb, bf, bb) = _rwkv_prep(
            u3, _row(p['rwkv_mu'][l]), p['rwkv_w0'][l], p['rwkv_w2'][l], p['rwkv_a0'][l], p['rwkv_a2'][l],
            p['rwkv_g2'][l], _row(p['rwkv_kk'][l]), _row(p['rwkv_ka'][l]), _row(p['rwkv_rk'][l]))
        yf, yb = _rwkv_core(r, v, kk, lwf, lwb, kf, kb, bf, bb)
        at = _attention(u3, cos, sin, _row(jnp.tile(p['attn_q_norm'][l], H_G)),
                        _row(jnp.tile(p['attn_k_norm'][l], KV_ATTN)))
        gate_bias = jnp.pad(jnp.concatenate([p['mlstm_i_bias'][l].reshape(-1), p['mlstm_f_bias'][l].reshape(-1)]),
                            (0, LANES - 4 * H_G))
        hf, hb = _mlstm(u3, p['mlstm_conv_w'][l], _row(p['mlstm_conv_b'][l]), _row(gate_bias))
        of, ob = _retention(u3, cos, sin)
        flat = lambda z: z.reshape(n, W_G)
        mix_args = (x_parts, tuple(map(flat, (yf, yb, bonus, g))), flat(at), tuple(map(flat, (hf, hb))),
                    tuple(map(flat, (of, ob))), u2, _row(p['rwkv_ln'][l]), _row(p['mlstm_ln'][l]),
                    _row(p['ret_ln'][l]), p['w_out'][l].astype(BF16))
        j = l // 2
        if l % 2 == 0:
            x2 = _mix_ffn(mix_args, _row(p['norm_ffn'][l]), p['ffn_w_gate'][j].astype(BF16),
                          p['ffn_w_up'][j].astype(BF16), p['ffn_w_down'][j].astype(BF16), tm)
            x_parts = (x2,)
            if l == depth - 1:
                raise NotImplementedError("final norm after a dense FFN layer")
        else:
            wr = jnp.pad(p['moe_router'][j], ((0, 0), (0, LANES - N_EXPERTS)))
            x2, h, gates, idx, counts = _mix_router(mix_args, _row(p['norm_ffn'][l]), wr, tm)
            if l != depth - 1:
                raise NotImplementedError("expert layer that is not the last layer")
            outs = _moe(x2, h, gates, idx, counts, p['moe_w_gate'][j].astype(BF16), p['moe_w_up'][j].astype(BF16),
                        p['moe_w_down'][j].astype(BF16), _row(p['norm_final']), tm, part_rows)
    return tuple(o.reshape(x.shape) for o, x in zip(outs, xs))


def kernel(x_prompt, x_sample, norm_mix, norm_ffn, norm_final, w_in, w_out, rwkv_mu, rwkv_w0, rwkv_w2,
           rwkv_a0, rwkv_a2, rwkv_g2, rwkv_kk, rwkv_ka, rwkv_rk, rwkv_ln, attn_q_norm, attn_k_norm,
           mlstm_conv_w, mlstm_conv_b, mlstm_i_bias, mlstm_f_bias, mlstm_ln, ret_ln, ffn_w_gate, ffn_w_up,
           ffn_w_down, moe_router, moe_w_gate, moe_w_up, moe_w_down):
    p = dict(norm_mix=norm_mix, norm_ffn=norm_ffn, norm_final=norm_final, w_in=w_in, w_out=w_out,
             rwkv_mu=rwkv_mu, rwkv_w0=rwkv_w0, rwkv_w2=rwkv_w2, rwkv_a0=rwkv_a0, rwkv_a2=rwkv_a2,
             rwkv_g2=rwkv_g2, rwkv_kk=rwkv_kk, rwkv_ka=rwkv_ka, rwkv_rk=rwkv_rk, rwkv_ln=rwkv_ln,
             attn_q_norm=attn_q_norm, attn_k_norm=attn_k_norm, mlstm_conv_w=mlstm_conv_w,
             mlstm_conv_b=mlstm_conv_b, mlstm_i_bias=mlstm_i_bias, mlstm_f_bias=mlstm_f_bias,
             mlstm_ln=mlstm_ln, ret_ln=ret_ln, ffn_w_gate=ffn_w_gate, ffn_w_up=ffn_w_up,
             ffn_w_down=ffn_w_down, moe_router=moe_router, moe_w_gate=moe_w_gate, moe_w_up=moe_w_up,
             moe_w_down=moe_w_down)
    return _trunk((x_prompt, x_sample), p)
```

```python
import functools
import math

import numpy as np
import jax
import jax.numpy as jnp
from jax import lax
from jax.experimental import pallas as pl
from jax.experimental.pallas import tpu as pltpu

F32 = jnp.float32
BF16 = jnp.bfloat16

D_MODEL = 1024
HEAD_DIM = 64
W_G = 256
H_G = 4
KV_ATTN = 2
D_FF = 2816
N_EXPERTS = 8
NORM_EPS = 1e-6
HEAD_NORM_EPS = 1e-5
RWKV_GN_EPS = 64e-5
NEG_INF = -1e30
ROPE_THETA = 10000.0
GRID_W = 64

LANES = 128
ROW_ALIGN = 8
VMEM_LIMIT_BYTES = 56 * 1024 * 1024

U_RWKV = 0
U_MLSTM = 1024
U_RET = 2048
U_ATTN = 3072
U_GATE = 3584
U_COLS = 3712

RWKV_CHUNK = 64
RWKV_BLOCK = 256
RWKV_GROUP = 4
MIX_CHUNK = 128
MIX_BLOCK = 256
EXPERT_TILE = 256


def _cparams(sem):
    return pltpu.CompilerParams(dimension_semantics=sem, vmem_limit_bytes=VMEM_LIMIT_BYTES)


def _bdot(a, b, dims):
    return lax.dot_general(a, b, (dims, ((), ())), preferred_element_type=F32)


def _mm(a, b):
    return _bdot(a.astype(BF16), b.astype(BF16), ((1,), (0,)))


def _mm_nt(a, b):
    return _bdot(a.astype(BF16), b.astype(BF16), ((1,), (1,)))


def _mm_tn(a, b):
    return _bdot(a.astype(BF16), b.astype(BF16), ((0,), (0,)))


def _split2(a):
    hi = a.astype(BF16)
    lo = (a - hi.astype(F32)).astype(BF16)
    return hi, lo


def _split3(a):
    hi = a.astype(BF16)
    r = a - hi.astype(F32)
    mid = r.astype(BF16)
    lo = (r - mid.astype(F32)).astype(BF16)
    return hi, mid, lo


def _mm_l2(a, b_exact):
    hi, lo = _split2(a)
    return _bdot(hi, b_exact, ((1,), (0,))) + _bdot(lo, b_exact, ((1,), (0,)))


def _mm_l3(a, b_exact):
    h, m, l = _split3(a)
    return _bdot(h, b_exact, ((1,), (0,))) + _bdot(m, b_exact, ((1,), (0,))) + _bdot(l, b_exact, ((1,), (0,)))


def _mm_r3(a_exact, b):
    h, m, l = _split3(b)
    return _bdot(a_exact, h, ((1,), (0,))) + _bdot(a_exact, m, ((1,), (0,))) + _bdot(a_exact, l, ((1,), (0,)))


def _mm_x3(a, b):
    ah, al = _split2(a)
    bh, bl = _split2(b)
    d = ((1,), (0,))
    return _bdot(ah, bh, d) + _bdot(ah, bl, d) + _bdot(al, bh, d)


def _iota2(shape, axis):
    return lax.broadcasted_iota(jnp.int32, shape, axis)


def _head_mean_matrix(width):
    r = _iota2((width, width), 0) // HEAD_DIM
    c = _iota2((width, width), 1) // HEAD_DIM
    return jnp.where(r == c, 1.0 / HEAD_DIM, 0.0).astype(BF16)


def _head_mean(z, bd):
    return _mm_l2(z, bd)


def _sigmoid(x):
    return 1.0 / (1.0 + jnp.exp(-x))


def _silu(x):
    return x * _sigmoid(x)


def _log_sigmoid(x):
    return jnp.minimum(x, 0.0) - jnp.log(1.0 + jnp.exp(-jnp.abs(x)))


def _rms_norm_rows(x, gain):
    ms = jnp.mean(x * x, axis=-1, keepdims=True)
    return x * lax.rsqrt(ms + NORM_EPS) * gain


def _rope_swap(z):
    w = z.shape[-1]
    lane = _iota2(z.shape, z.ndim - 1)
    fwd = pltpu.roll(z, w - 16, z.ndim - 1)
    bwd = pltpu.roll(z, 16, z.ndim - 1)
    return jnp.where((lane % 32) < 16, fwd, bwd)


def _rope(z, cos, sin):
    return z * cos + _rope_swap(z) * sin


def _shift_rows(x, prev_row, next_row):
    n = x.shape[0]
    row = _iota2(x.shape, 0)
    prev = jnp.where(row == 0, prev_row, pltpu.roll(x, 1, 0))
    nxt = jnp.where(row == n - 1, next_row, pltpu.roll(x, n - 1, 0))
    return prev, nxt


def _part_tiles(parts, tm):
    return tuple(p.shape[0] // tm for p in parts)


def _part_specs(parts, tm, width):
    specs, start = [], 0
    for nt in _part_tiles(parts, tm):
        specs.append(pl.BlockSpec((tm, width), lambda i, s=start, nt=nt: (jnp.clip(i - s, 0, nt - 1), 0)))
        start += nt
    return specs


def _part_tile(i, refs, tiles):
    x = refs[0][...]
    start = tiles[0]
    for ref, nt in zip(refs[1:], tiles[1:]):
        x = jnp.where(i >= start, ref[...], x)
        start += nt
    return x


def _inproj_kernel(*refs, tiles):
    x_refs, (g_ref, w_ref, o_ref) = refs[:len(tiles)], refs[len(tiles):]
    h = _rms_norm_rows(_part_tile(pl.program_id(0), x_refs, tiles), g_ref[...])
    o_ref[...] = jnp.dot(h.astype(BF16), w_ref[...], preferred_element_type=F32)


def _in_proj(x_parts, gain, w_pad, tm):
    tiles = _part_tiles(x_parts, tm)
    n = sum(tiles) * tm
    return pl.pallas_call(
        functools.partial(_inproj_kernel, tiles=tiles),
        out_shape=jax.ShapeDtypeStruct((n, U_COLS), F32),
        grid=(n // tm,),
        in_specs=_part_specs(x_parts, tm, D_MODEL) + [
            pl.BlockSpec((1, D_MODEL), lambda i: (0, 0)),
            pl.BlockSpec((D_MODEL, U_COLS), lambda i: (0, 0), pipeline_mode=pl.Buffered(1)),
        ],
        out_specs=pl.BlockSpec((tm, U_COLS), lambda i: (i, 0)),
        compiler_params=_cparams(("arbitrary",)),
        name="in_proj",
    )(*x_parts, gain, w_pad)


def _attn_kernel(u_ref, cos_ref, sin_ref, qg_ref, kg_ref, o_ref, q_scr, k_scr, v_scr, *, tq):
    t = u_ref.shape[1]
    u = u_ref[0]
    q = u[:, :W_G]
    k = u[:, W_G:W_G + KV_ATTN * HEAD_DIM]
    v = u[:, W_G + KV_ATTN * HEAD_DIM:]
    cos = cos_ref[...]
    sin = sin_ref[...]
    bd_q = _head_mean_matrix(W_G)
    bd_k = _head_mean_matrix(KV_ATTN * HEAD_DIM)
    qn = q * lax.rsqrt(_head_mean(q * q, bd_q) + NORM_EPS) * qg_ref[...]
    kn = k * lax.rsqrt(_head_mean(k * k, bd_k) + NORM_EPS) * kg_ref[...]
    kw = KV_ATTN * HEAD_DIM
    q_scr[...] = (_rope(qn, cos, sin) * (HEAD_DIM ** -0.5 * math.log2(math.e))).astype(BF16)
    k_scr[...] = _rope(kn, cos[:, :kw], sin[:, :kw]).astype(BF16)
    ones = jnp.ones((t, HEAD_DIM), BF16)
    for j in range(KV_ATTN):
        vj = v[:, j * HEAD_DIM:(j + 1) * HEAD_DIM].astype(BF16)
        v_scr[:, j * LANES:(j + 1) * LANES] = jnp.concatenate([vj, ones], axis=1)
    group = H_G // KV_ATTN

    def q_tile(i, carry):
        rows = pl.ds(pl.multiple_of(i * tq, tq), tq)

        def scores(h):
            j = h // group
            qh = q_scr[rows, h * HEAD_DIM:(h + 1) * HEAD_DIM]
            return _bdot(qh, k_scr[:, j * HEAD_DIM:(j + 1) * HEAD_DIM], ((1,), (1,)))

        s_next = scores(0)
        for h in range(H_G):
            s = s_next
            if h + 1 < H_G:
                s_next = scores(h + 1)
            m = jnp.max(s, axis=-1, keepdims=True)
            p = jnp.exp2(s - m)
            vj = v_scr[:, (h // group) * LANES:(h // group + 1) * LANES]
            r = jnp.dot(p.astype(BF16), vj, preferred_element_type=F32)
            o_ref[0, rows, h * HEAD_DIM:(h + 1) * HEAD_DIM] = r[:, :HEAD_DIM] / r[:, HEAD_DIM:]
        return carry

    lax.fori_loop(0, t // tq, q_tile, 0)


def _attention(u3, cos, sin, q_gain, k_gain):
    b, t, _ = u3.shape
    tq = min(256, t)
    col = U_ATTN // 512
    return pl.pallas_call(
        functools.partial(_attn_kernel, tq=tq),
        out_shape=jax.ShapeDtypeStruct((b, t, W_G), F32),
        grid=(b,),
        in_specs=[
            pl.BlockSpec((1, t, 512), lambda i: (i, 0, col)),
            pl.BlockSpec((t, W_G), lambda i: (0, 0)),
            pl.BlockSpec((t, W_G), lambda i: (0, 0)),
            pl.BlockSpec((1, W_G), lambda i: (0, 0)),
            pl.BlockSpec((1, KV_ATTN * HEAD_DIM), lambda i: (0, 0)),
        ],
        out_specs=pl.BlockSpec((1, t, W_G), lambda i: (i, 0, 0)),
        scratch_shapes=[
            pltpu.VMEM((t, W_G), BF16),
            pltpu.VMEM((t, KV_ATTN * HEAD_DIM), BF16),
            pltpu.VMEM((t, KV_ATTN * LANES), BF16),
        ],
        compiler_params=_cparams(("parallel",)),
        name="attention",
    )(u3, cos, sin, q_gain, k_gain)


def _ret_log_gamma(direction):
    return [math.log1p(-2.0 ** (-5.0 - (2 * h + direction) / 2.0)) for h in range(H_G)]


def _ret_kernel(uf_ref, ub_ref, cf_ref, sf_ref, cb_ref, sb_ref, of_ref, ob_ref, rf_scr, rb_scr):
    i = pl.program_id(1)
    c = MIX_CHUNK

    @pl.when(i == 0)
    def _():
        rf_scr[...] = jnp.zeros_like(rf_scr)
        rb_scr[...] = jnp.zeros_like(rb_scr)

    tt = _iota2((c, c), 0)
    ss = _iota2((c, c), 1)
    diff = (tt - ss).astype(F32)
    jcol = tt.astype(F32)
    first = ss < HEAD_DIM
    block_diag = (tt < HEAD_DIM) == first
    lg_f = _ret_log_gamma(0)
    lg_b = _ret_log_gamma(1)
    psl = lambda p: slice(p * LANES, (p + 1) * LANES)
    lane_lg = lambda lg, p: jnp.where(first, lg[2 * p], lg[2 * p + 1])
    row_decay = lambda lg, p: jnp.where(tt < HEAD_DIM, math.exp(c * lg[2 * p]), math.exp(c * lg[2 * p + 1]))
    own = lambda h, x: jnp.where(first, x, 0.0) if h % 2 == 0 else jnp.where(first, 0.0, x)
    pairs = range(H_G // 2)

    nch = uf_ref.shape[1] // c
    uf = uf_ref[0]
    qf_all = _rope(uf[:, :W_G], cf_ref[...], sf_ref[...])
    kf_all = _rope(uf[:, W_G:2 * W_G], cf_ref[...], sf_ref[...]) * HEAD_DIM ** -0.5
    ub = ub_ref[0]
    qb_all = _rope(ub[:, :W_G], cb_ref[...], sb_ref[...])
    kb_all = _rope(ub[:, W_G:2 * W_G], cb_ref[...], sb_ref[...]) * HEAD_DIM ** -0.5
    rf_prev = [rf_scr[p] for p in pairs]
    rb_prev = [rb_scr[p] for p in pairs]
    decay = [jnp.where(tt >= ss, jnp.exp(diff * lg_f[h]), 0.0) + jnp.where(ss >= tt, jnp.exp(-diff * lg_b[h]), 0.0)
             for h in range(H_G)]
    for n in range(nch):
        rf_rows = slice(n * c, (n + 1) * c)
        rb_rows = slice((nch - 1 - n) * c, (nch - n) * c)
        qf, kf, vf = qf_all[rf_rows], kf_all[rf_rows], uf[rf_rows, 2 * W_G:3 * W_G]
        qb, kb, vb = qb_all[rb_rows], kb_all[rb_rows], ub[rb_rows, 2 * W_G:3 * W_G]
        qk = [_mm_nt(qf[:, psl(h // 2)], own(h, kf[:, psl(h // 2)])) for h in range(H_G)]
        intra = [_mm(qk[h] * decay[h], own(h, vf[:, psl(h // 2)])) for h in range(H_G)]
        inter_f = [_mm(qf[:, psl(p)] * jnp.exp((jcol + 1.0) * lane_lg(lg_f, p)), rf_prev[p]) for p in pairs]
        inter_b = [_mm(qb[:, psl(p)] * jnp.exp((c - jcol) * lane_lg(lg_b, p)), rb_prev[p]) for p in pairs]
        upd_f = [_mm_tn(kf[:, psl(p)] * jnp.exp((c - 1.0 - jcol) * lane_lg(lg_f, p)), vf[:, psl(p)]) for p in pairs]
        upd_b = [_mm_tn(kb[:, psl(p)] * jnp.exp(jcol * lane_lg(lg_b, p)), vb[:, psl(p)]) for p in pairs]
        for p in pairs:
            of_ref[0, rf_rows, psl(p)] = intra[2 * p] + intra[2 * p + 1] + inter_f[p]
            ob_ref[0, rb_rows, psl(p)] = inter_b[p]
        rf_prev = [row_decay(lg_f, p) * rf_prev[p] + jnp.where(block_diag, upd_f[p], 0.0) for p in pairs]
        rb_prev = [row_decay(lg_b, p) * rb_prev[p] + jnp.where(block_diag, upd_b[p], 0.0) for p in pairs]
    for p in pairs:
        rf_scr[p] = rf_prev[p]
        rb_scr[p] = rb_prev[p]


def _retention(u3, cos, sin):
    b, t, _ = u3.shape
    c = min(MIX_BLOCK, t)
    nblk = t // c
    col = U_RET // 1024
    tab = lambda rev: pl.BlockSpec((c, W_G), (lambda bi, i: (nblk - 1 - i, 0)) if rev else (lambda bi, i: (i, 0)))
    return pl.pallas_call(
        _ret_kernel,
        out_shape=(jax.ShapeDtypeStruct((b, t, W_G), F32), jax.ShapeDtypeStruct((b, t, W_G), F32)),
        grid=(b, nblk),
        in_specs=[
            pl.BlockSpec((1, c, 1024), lambda bi, i: (bi, i, col)),
            pl.BlockSpec((1, c, 1024), lambda bi, i: (bi, nblk - 1 - i, col)),
            tab(False), tab(False), tab(True), tab(True),
        ],
        out_specs=(
            pl.BlockSpec((1, c, W_G), lambda bi, i: (bi, i, 0)),
            pl.BlockSpec((1, c, W_G), lambda bi, i: (bi, nblk - 1 - i, 0)),
        ),
        scratch_shapes=[pltpu.VMEM((H_G // 2, LANES, LANES), F32), pltpu.VMEM((H_G // 2, LANES, LANES), F32)],
        compiler_params=_cparams(("parallel", "arbitrary")),
        name="retention",
    )(u3, u3, cos, sin, cos, sin)


def _mlstm_tile(u_ref, up_ref, un_ref, g_ref, cw_ref, cb_ref, gb_ref, blk, nblk, direction):
    c = MIX_CHUNK
    reverse = direction == 1
    u = u_ref[0]
    qk = u[:, :2 * W_G]
    prev_row = jnp.where(blk == 0, 0.0, up_ref[0][ROW_ALIGN - 1:ROW_ALIGN, :])
    next_row = jnp.where(blk == nblk - 1, 0.0, un_ref[0][0:1, :])
    prev, nxt = _shift_rows(qk, prev_row, next_row)
    cw = cw_ref[...]
    qk = _silu(cw[0:1] * prev + cw[1:2] * qk + cw[2:3] * nxt + cb_ref[...])
    qa = qk[:, :W_G]
    ka = qk[:, W_G:] * HEAD_DIM ** -0.5
    va = u[:, 2 * W_G:3 * W_G]

    x = g_ref[0] + gb_ref[...]
    xt = x.T
    lf_c = _log_sigmoid(x)
    lf_r = _log_sigmoid(xt)
    tt = _iota2((c, c), 0)
    ss = _iota2((c, c), 1)
    lower = jnp.where(ss <= tt, 1.0, 0.0).astype(BF16)
    upper = jnp.where(tt <= ss, 1.0, 0.0).astype(BF16)
    if reverse:
        b_c = _mm_r3(upper, lf_c)
        b_r = _mm_l3(lf_r, lower)
        mask = ss >= tt
    else:
        b_c = _mm_r3(lower, lf_c)
        b_r = _mm_l3(lf_r, upper)
        mask = ss <= tt
    g_all = jnp.sum(lf_c, axis=0, keepdims=True)
    return dict(q=qa, k=ka, v=va, x=x, xt=xt, b_c=b_c, b_r=b_r, g_all=g_all, mask=mask)


def _mlstm_select_matrix():
    sel = np.zeros((2, 2 * LANES, 8 * LANES), np.float32)
    for d in range(2):
        for h in range(H_G):
            ci, cf = d * H_G + h, 2 * H_G + d * H_G + h
            p, j = divmod(h, 2)
            sel[d, cf, LANES * h:LANES * (h + 1)] = 1.0
            sel[d, cf, 4 * LANES + LANES * p + HEAD_DIM * j:4 * LANES + LANES * p + HEAD_DIM * (j + 1)] = 1.0
            sel[d, LANES + ci, 6 * LANES + LANES * p + HEAD_DIM * j:6 * LANES + LANES * p + HEAD_DIM * (j + 1)] = 1.0
    return jnp.asarray(sel, BF16)


def _mlstm_kernel(uf_ref, upf_ref, unf_ref, gf_ref, ub_ref, upb_ref, unb_ref, gbk_ref,
                  cw_ref, cb_ref, gb_ref, sel_ref, of_ref, ob_ref, st_scr, m_scr):
    i = pl.program_id(1)
    nblk = pl.num_programs(1)
    c = MIX_CHUNK

    @pl.when(i == 0)
    def _():
        st_scr[...] = jnp.zeros_like(st_scr)
        m_scr[...] = jnp.zeros_like(m_scr)

    tiles = (_mlstm_tile(uf_ref, upf_ref, unf_ref, gf_ref, cw_ref, cb_ref, gb_ref, i, nblk, 0),
             _mlstm_tile(ub_ref, upb_ref, unb_ref, gbk_ref, cw_ref, cb_ref, gb_ref, nblk - 1 - i, nblk, 1))
    o_refs = (of_ref, ob_ref)
    sel = [_mm_l2(jnp.concatenate([tiles[d]["b_c"], tiles[d]["x"]], axis=1), sel_ref[d]) for d in range(2)]
    first = _iota2((c, LANES), 1) < HEAD_DIM
    row_first = _iota2((LANES, 2 * LANES), 0) < HEAD_DIM
    lane2 = _iota2((LANES, 2 * LANES), 1) % LANES < HEAD_DIM
    block_diag = row_first == lane2
    ones = jnp.ones((c, LANES), F32)
    pairs = [(d, p) for d in range(2) for p in range(H_G // 2)]
    heads = [(d, h) for d in range(2) for h in range(H_G)]
    psl = lambda p: slice(p * LANES, (p + 1) * LANES)
    q_pair = {dp: tiles[dp[0]]["q"][:, psl(dp[1])] for dp in pairs}
    k_pair = {dp: tiles[dp[0]]["k"][:, psl(dp[1])] for dp in pairs}
    v_pair = {dp: tiles[dp[0]]["v"][:, psl(dp[1])] for dp in pairs}
    state = {dp: st_scr[n] for n, dp in enumerate(pairs)}
    m_row = {dp: m_scr[n:n + 1, :] for n, dp in enumerate(pairs)}

    def own(d, h, x):
        return jnp.where(first, x, 0.0) if h % 2 == 0 else jnp.where(first, 0.0, x)

    qk = [_mm_nt(q_pair[(d, h // 2)], own(d, h, k_pair[(d, h // 2)])) for d, h in heads]
    qs = {dp: _mm(q_pair[dp], state[dp]) for dp in pairs}
    bc = [sel[d][:, LANES * h:LANES * (h + 1)] for d, h in heads]
    m_prev = [m_row[(d, h // 2)][:, HEAD_DIM * (h % 2):HEAD_DIM * (h % 2) + 1] for d, h in heads]
    dlog = []
    for (d, h), bc_ in zip(heads, bc):
        ci, cf = d * H_G + h, 2 * H_G + d * H_G + h
        rowterm = tiles[d]["xt"][ci:ci + 1, :] - tiles[d]["b_r"][cf:cf + 1, :]
        dlog.append(jnp.where(tiles[d]["mask"], bc_ + rowterm, NEG_INF))
    inter_log = [bc_ + m_ for bc_, m_ in zip(bc, m_prev)]
    m_t = [jnp.maximum(il, jnp.max(dl, axis=-1, keepdims=True)) for il, dl in zip(inter_log, dlog)]
    sc = [qk_ * jnp.exp(dl - mt) for qk_, dl, mt in zip(qk, dlog, m_t)]
    w_inter = [jnp.exp(il - mt) for il, mt in zip(inter_log, m_t)]
    e_neg = [jnp.exp(-mt) for mt in m_t]
    res = []
    for n, (d, h) in enumerate(heads):
        v_aug = jnp.concatenate([own(d, h, v_pair[(d, h // 2)]), own(d, h, ones)], axis=1)
        res.append(_mm(sc[n], v_aug))
    for n, (d, p) in enumerate(pairs):
        a, b_ = 2 * n, 2 * n + 1
        tot = res[a] + res[b_] + jnp.tile(jnp.where(first, w_inter[a], w_inter[b_]), (1, 2)) * qs[(d, p)]
        den = jnp.maximum(jnp.abs(tot[:, LANES:]), jnp.where(first, e_neg[a], e_neg[b_]))
        o_refs[d][0, :, psl(p)] = tot[:, :LANES] / den

    for n, (d, p) in enumerate(pairs):
        bcp = sel[d][:, 4 * LANES + LANES * p:4 * LANES + LANES * (p + 1)]
        lip = sel[d][:, 6 * LANES + LANES * p:6 * LANES + LANES * (p + 1)]
        g_row = bcp[0:1, :] if d == 1 else bcp[c - 1:c, :]
        a_p = g_row - bcp + lip
        m_new = jnp.maximum(g_row + m_row[(d, p)], jnp.max(a_p, axis=0, keepdims=True))
        dec = jnp.exp(g_row + m_row[(d, p)] - m_new)
        kw_t = (k_pair[(d, p)] * jnp.exp(a_p - m_new)).T
        upd = _mm(kw_t, jnp.concatenate([v_pair[(d, p)], ones], axis=1))
        dec_tile = jnp.where(row_first, dec[:, 0:1], dec[:, HEAD_DIM:HEAD_DIM + 1])
        st_scr[n] = dec_tile * state[(d, p)] + jnp.where(block_diag, upd, 0.0)
        m_scr[n:n + 1, :] = m_new


def _mlstm(u3, conv_w, conv_b, gate_bias):
    b, t, _ = u3.shape
    c = MIX_CHUNK
    nblk = t // c
    rpb = c // ROW_ALIGN
    n8 = t // ROW_ALIGN
    col = U_MLSTM // 1024
    hcol = U_MLSTM // 512
    gcol = U_GATE // LANES

    def specs(rev):
        blk = (lambda i: nblk - 1 - i) if rev else (lambda i: i)
        return [
            pl.BlockSpec((1, c, 1024), lambda bi, i: (bi, blk(i), col)),
            pl.BlockSpec((1, ROW_ALIGN, 512), lambda bi, i: (bi, jnp.maximum(blk(i) * rpb - 1, 0), hcol)),
            pl.BlockSpec((1, ROW_ALIGN, 512), lambda bi, i: (bi, jnp.minimum((blk(i) + 1) * rpb, n8 - 1), hcol)),
            pl.BlockSpec((1, c, LANES), lambda bi, i: (bi, blk(i), gcol)),
        ]

    const = lambda shape: pl.BlockSpec(shape, lambda bi, i: (0,) * len(shape))
    return pl.pallas_call(
        _mlstm_kernel,
        out_shape=(jax.ShapeDtypeStruct((b, t, W_G), F32), jax.ShapeDtypeStruct((b, t, W_G), F32)),
        grid=(b, nblk),
        in_specs=specs(False) + specs(True) + [const((3, 2 * W_G)), const((1, 2 * W_G)), const((1, LANES)),
                                               const((2, 2 * LANES, 8 * LANES))],
        out_specs=(
            pl.BlockSpec((1, c, W_G), lambda bi, i: (bi, i, 0)),
            pl.BlockSpec((1, c, W_G), lambda bi, i: (bi, nblk - 1 - i, 0)),
        ),
        scratch_shapes=[pltpu.VMEM((H_G, LANES, 2 * LANES), F32), pltpu.VMEM((ROW_ALIGN, LANES), F32)],
        compiler_params=_cparams(("parallel", "arbitrary")),
        name="mlstm",
    )(u3, u3, u3, u3, u3, u3, u3, u3, conv_w, conv_b, gate_bias, _mlstm_select_matrix())


def _rwkv_prep_kernel(u_ref, up_ref, un_ref, mu_ref, w0_ref, w2_ref, a0_ref, a2_ref, g2_ref, kks_ref, ka_ref, rk_ref,
                      r_ref, v_ref, kk_ref, g_ref, bonus_ref, lwf_ref, lwb_ref, kf_ref, kb_ref, bf_ref, bb_ref):
    i = pl.program_id(1)
    nblk = pl.num_programs(1)
    u = u_ref[0]
    prev_row = jnp.where(i == 0, 0.0, up_ref[0][ROW_ALIGN - 1:ROW_ALIGN, :])
    next_row = jnp.where(i == nblk - 1, 0.0, un_ref[0][0:1, :])
    prev, nxt = _shift_rows(u, prev_row, next_row)
    us = u + mu_ref[...] * (0.5 * (prev + nxt) - u)
    r = us[:, 0:W_G]
    k = us[:, W_G:2 * W_G]
    v = us[:, 2 * W_G:3 * W_G]
    xw = us[:, 3 * W_G:3 * W_G + 64]
    xa = us[:, 3 * W_G + 64:3 * W_G + 128]
    xg = us[:, 3 * W_G + 128:]
    bd = _head_mean_matrix(W_G)
    g = _mm(_sigmoid(xg), g2_ref[...])
    lw = jnp.tanh(xw)
    a_lr = _mm_x3(xa, a2_ref[...])
    kk = k * kks_ref[...]
    kk = kk * lax.rsqrt(_head_mean(kk * kk, bd) * HEAD_DIM + 1e-12)
    r_ref[0] = r
    v_ref[0] = v
    kk_ref[0] = kk
    g_ref[0] = g
    bonus = jnp.zeros_like(r)
    for d, (lw_ref, k_ref, b_ref) in enumerate(((lwf_ref, kf_ref, bf_ref), (lwb_ref, kb_ref, bb_ref))):
        z = w0_ref[d:d + 1, :] + _mm_x3(lw, w2_ref[d])
        lw_ref[0] = -_sigmoid(z) * math.exp(-0.5)
        a = _sigmoid(a0_ref[d:d + 1, :] + a_lr)
        kd = k * (1.0 + (a - 1.0) * ka_ref[...])
        k_ref[0] = kd
        b_ref[0] = kk * a
        bonus = bonus + _head_mean(r * kd * rk_ref[...], bd) * HEAD_DIM * v
    bonus_ref[0] = bonus


def _rwkv_prep(u3, mu, w0, w2, a0, a2, g2, kks, ka, rk):
    b, t, _ = u3.shape
    tb = min(RWKV_BLOCK, t)
    nblk = t // tb
    rpb = tb // ROW_ALIGN
    n8 = t // ROW_ALIGN
    const = lambda shape: pl.BlockSpec(shape, lambda bi, i: (0,) * len(shape))
    out = jax.ShapeDtypeStruct((b, t, W_G), F32)
    ospec = pl.BlockSpec((1, tb, W_G), lambda bi, i: (bi, i, 0))
    return pl.pallas_call(
        _rwkv_prep_kernel,
        out_shape=(out,) * 11,
        grid=(b, nblk),
        in_specs=[
            pl.BlockSpec((1, tb, 1024), lambda bi, i: (bi, i, 0)),
            pl.BlockSpec((1, ROW_ALIGN, 1024), lambda bi, i: (bi, jnp.maximum(i * rpb - 1, 0), 0)),
            pl.BlockSpec((1, ROW_ALIGN, 1024), lambda bi, i: (bi, jnp.minimum((i + 1) * rpb, n8 - 1), 0)),
            const((1, 1024)), const((2, W_G)), const((2, 64, W_G)), const((2, W_G)), const((64, W_G)),
            const((128, W_G)), const((1, W_G)), const((1, W_G)), const((1, W_G)),
        ],
        out_specs=(ospec,) * 11,
        compiler_params=_cparams(("parallel", "parallel")),
        name="rwkv_prep",
    )(u3, u3, u3, mu, w0, w2, a0, a2, g2, kks, ka, rk)


def _tri_inverse_all(lmats, n):
    r = _iota2((n, n), 0)
    c = _iota2((n, n), 1)
    eye = jnp.where(r == c, 1.0, 0.0)
    pair = (r // 2 == c // 2) & (r != c)
    invs = [eye + jnp.where(pair, lm, 0.0) for lm in lmats]
    s = 2
    while s < n:
        sel = (r // (2 * s) == c // (2 * s)) & (r // s != c // s)
        offs = [jnp.where(sel, -lm, 0.0) for lm in lmats]
        xs = [_mm(inv, off) for inv, off in zip(invs, offs)]
        invs = [inv - _mm(x, inv) for inv, x in zip(invs, xs)]
        s *= 2
    return invs


def _rwkv_tile_terms(r, k, v, kk, b, lw, reverse):
    c = r.shape[0]
    tt = _iota2((c, c), 0)
    ss = _iota2((c, c), 1)
    tri = jnp.where((tt <= ss) if reverse else (ss <= tt), 1.0, 0.0).astype(BF16)
    cum_in = _mm_r3(tri, lw)
    cum_all = jnp.sum(lw, axis=0, keepdims=True)
    e_neg = jnp.exp(-cum_in)
    e_end = jnp.exp(cum_all - cum_in)
    return dict(at=-kk * jnp.exp(cum_in - lw), rt=r * jnp.exp(cum_in), bt=b * e_neg, kt=k * e_neg,
                gb=b * e_end, gk=k * e_end, v=v, e_all=jnp.exp(cum_all))


def _rwkv_chunk_terms(tiles, reverses):
    c = RWKV_CHUNK
    tt = _iota2((c, c), 0)
    ss = _iota2((c, c), 1)
    heads = [(ti, h) for ti in range(len(tiles)) for h in range(H_G)]
    sl = lambda h: slice(h * HEAD_DIM, (h + 1) * HEAD_DIM)
    get = lambda name: [tiles[ti][name][:, sl(h)] for ti, h in heads]
    at, rt, bt, kt, gb, gk, v = (get(nm) for nm in ("at", "rt", "bt", "kt", "gb", "gk", "v"))
    strict = [(ss > tt) if reverses[ti] else (ss < tt) for ti, _ in heads]
    incl = [(ss >= tt) if reverses[ti] else (ss <= tt) for ti, _ in heads]
    ps = [_mm_nt(jnp.concatenate([a, r_], axis=0), jnp.concatenate([b_, k_], axis=0))
          for a, r_, b_, k_ in zip(at, rt, bt, kt)]
    l_ab = [jnp.where(m, p[:c, :c], 0.0) for m, p in zip(strict, ps)]
    l_ak = [jnp.where(m, p[:c, c:], 0.0) for m, p in zip(strict, ps)]
    m_r = [jnp.concatenate([jnp.where(m, p[c:, :c], 0.0), jnp.where(m, p[c:, c:], 0.0)], axis=1)
           for m, p in zip(incl, ps)]
    lakv = [_mm(l, v_) for l, v_ in zip(l_ak, v)]
    invs = _tri_inverse_all(l_ab, c)
    tw = [_mm(inv, jnp.concatenate([a, lv], axis=1)) for inv, a, lv in zip(invs, at, lakv)]
    zeros = jnp.zeros((c, HEAD_DIM), F32)
    mm2 = [_mm(m, jnp.concatenate([t_, jnp.concatenate([zeros, v_], axis=1)], axis=0))
           for m, t_, v_ in zip(m_r, tw, v)]
    r1 = [r_ + m[:, :HEAD_DIM] for r_, m in zip(rt, mm2)]
    y0 = [m[:, HEAD_DIM:] for m in mm2]
    twg = [_mm_tn(t_, g_) for t_, g_ in zip(tw, gb)]
    vgk = [_mm_tn(v_, g_) for v_, g_ in zip(v, gk)]
    mlow = [x[:HEAD_DIM] for x in twg]
    nadd = [x[HEAD_DIM:] + y for x, y in zip(twg, vgk)]
    e_all = [tiles[ti]["e_all"][:, sl(h)] for ti, h in heads]
    return r1, y0, mlow, nadd, e_all


def _rwkv_core_kernel(rf_ref, kf_ref, vf_ref, kkf_ref, bf_ref, lwf_ref, rb_ref, kb_ref, vb_ref, kkb_ref, bb_ref,
                      lwb_ref, yf_ref, yb_ref, s_scr, *, group):
    i = pl.program_id(1)
    c = RWKV_CHUNK
    nch = rf_ref.shape[1] // c
    dirs = ((rf_ref, kf_ref, vf_ref, kkf_ref, bf_ref, lwf_ref), (rb_ref, kb_ref, vb_ref, kkb_ref, bb_ref, lwb_ref))
    y_refs = (yf_ref, yb_ref)

    @pl.when(i == 0)
    def _():
        s_scr[...] = jnp.zeros_like(s_scr)

    def step(j, states):
        tiles, reverses, rows = [], [], []
        for q in range(group):
            for d in range(2):
                cj = j * group + q
                cj = cj if d == 0 else nch - 1 - cj
                rw = pl.ds(pl.multiple_of(cj * c, c), c)
                tiles.append(_rwkv_tile_terms(*(ref[0, rw, :] for ref in dirs[d]), d == 1))
                reverses.append(d == 1)
                rows.append(rw)
        r1, y0, mlow, nadd, e_all = _rwkv_chunk_terms(tiles, reverses)
        states = list(states)
        for q in range(group):
            ys = [[], []]
            for d in range(2):
                for h in range(H_G):
                    n = (q * 2 + d) * H_G + h
                    s = states[d * H_G + h]
                    ys[d].append(_mm_nt(r1[n], s) + y0[n])
                    states[d * H_G + h] = s * e_all[n] + _mm(s, mlow[n]) + nadd[n]
            for d in range(2):
                y_refs[d][0, rows[q * 2 + d], :] = jnp.concatenate(ys[d], axis=1)
        return tuple(states)

    init = tuple(s_scr[n] for n in range(2 * H_G))
    if nch == group:
        states = step(0, init)
    else:
        states = lax.fori_loop(0, nch // group, step, init)
    for n in range(2 * H_G):
        s_scr[n] = states[n]


def _rwkv_core(r, v, kk, lwf, lwb, kf, kb, bf, bb):
    b, t, _ = r.shape
    tb = min(RWKV_BLOCK, t)
    nblk = t // tb
    fwd = pl.BlockSpec((1, tb, W_G), lambda bi, i: (bi, i, 0))
    bwd = pl.BlockSpec((1, tb, W_G), lambda bi, i: (bi, nblk - 1 - i, 0))
    out = jax.ShapeDtypeStruct((b, t, W_G), F32)
    return pl.pallas_call(
        functools.partial(_rwkv_core_kernel, group=min(RWKV_GROUP, tb // RWKV_CHUNK)),
        out_shape=(out, out),
        grid=(b, nblk),
        in_specs=[fwd] * 6 + [bwd] * 6,
        out_specs=(fwd, bwd),
        scratch_shapes=[pltpu.VMEM((2 * H_G, HEAD_DIM, HEAD_DIM), F32)],
        compiler_params=_cparams(("parallel", "arbitrary")),
        name="rwkv_core",
    )(r, kf, v, kk, bf, lwf, r, kb, v, kk, bb, lwb)


def _group_norm(y, gain, eps, bd):
    mu = _head_mean(y, bd)
    d = y - mu
    var = _head_mean(d * d, bd)
    return d * lax.rsqrt(var + eps) * gain


N_MIX_REFS = 15


def _mix_residual(refs, tiles):
    x_refs = refs[:len(tiles)]
    (ryf_ref, ryb_ref, rbon_ref, rg_ref, at_ref, mhf_ref, mhb_ref, mo_ref, tof_ref, tob_ref,
     tg_ref, rln_ref, mln_ref, tln_ref, w_ref) = refs[len(tiles):]
    bd = _head_mean_matrix(W_G)
    o_a = (_group_norm(ryf_ref[...] + ryb_ref[...], rln_ref[...], RWKV_GN_EPS, bd) + rbon_ref[...]) * rg_ref[...]
    o_c = _group_norm(mhf_ref[...] + mhb_ref[...], mln_ref[...], HEAD_NORM_EPS, bd) * _sigmoid(mo_ref[...])
    o_d = _group_norm(tof_ref[...] + tob_ref[...], tln_ref[...], HEAD_NORM_EPS, bd) * _silu(tg_ref[...])
    mix = jnp.concatenate([o_a, at_ref[...], o_c, o_d], axis=1).astype(BF16)
    return _part_tile(pl.program_id(0), x_refs, tiles) + jnp.dot(mix, w_ref[...], preferred_element_type=F32)


def _mix_call(kernel, x_parts, rw, at, ml, rt, u2, rln, mln, tln, w_out, tm, extra, extra_specs, out_shape, out_specs,
              scratch_shapes, name):
    tiles = _part_tiles(x_parts, tm)
    n = sum(tiles) * tm
    row = lambda w: pl.BlockSpec((tm, w), lambda i: (i, 0))
    ucol = lambda off: pl.BlockSpec((tm, W_G), lambda i: (i, off // W_G))
    const = lambda shape: pl.BlockSpec(shape, lambda i: (0, 0))
    return pl.pallas_call(
        functools.partial(kernel, tiles=tiles),
        out_shape=out_shape,
        grid=(n // tm,),
        in_specs=_part_specs(x_parts, tm, D_MODEL) + [row(W_G)] * 7 + [ucol(U_MLSTM + 3 * W_G)] + [row(W_G)] * 2
                 + [ucol(U_RET + 3 * W_G)] + [const((1, W_G))] * 3
                 + [pl.BlockSpec((D_MODEL, D_MODEL), lambda i: (0, 0), pipeline_mode=pl.Buffered(1))] + extra_specs,
        out_specs=out_specs,
        scratch_shapes=scratch_shapes,
        compiler_params=_cparams(("arbitrary",)),
        name=name,
    )(*x_parts, *rw, at, *ml, u2, *rt, u2, rln, mln, tln, w_out, *extra)


def _mix_ffn_kernel(*refs, tiles):
    n_in = len(tiles) + N_MIX_REFS
    g_ref, wg_ref, wu_ref, wd_ref, o_ref = refs[n_in:]
    x = _mix_residual(refs[:n_in], tiles)
    h = _rms_norm_rows(x, g_ref[...]).astype(BF16)
    a = jnp.dot(h, wg_ref[...], preferred_element_type=F32)
    b = jnp.dot(h, wu_ref[...], preferred_element_type=F32)
    z = (_silu(a) * b).astype(BF16)
    o_ref[...] = x + jnp.dot(z, wd_ref[...], preferred_element_type=F32)


def _mix_ffn(mix_args, gain, wg, wu, wd, tm):
    n = sum(_part_tiles(mix_args[0], tm)) * tm
    res = lambda shape: pl.BlockSpec(shape, lambda i: (0, 0), pipeline_mode=pl.Buffered(1))
    return _mix_call(_mix_ffn_kernel, *mix_args, tm, (gain, wg, wu, wd),
                     [pl.BlockSpec((1, D_MODEL), lambda i: (0, 0)), res((D_MODEL, D_FF)), res((D_MODEL, D_FF)),
                      res((D_FF, D_MODEL))],
                     jax.ShapeDtypeStruct((n, D_MODEL), F32), pl.BlockSpec((tm, D_MODEL), lambda i: (i, 0)), [],
                     "mix_ffn")


def _mix_router_kernel(*refs, tiles):
    n_in = len(tiles) + N_MIX_REFS
    g_ref, wr_ref, x_ref, h_ref, gate_ref, idx_ref, cnt_ref, cnt_scr = refs[n_in:]

    @pl.when(pl.program_id(0) == 0)
    def _():
        cnt_scr[...] = jnp.zeros_like(cnt_scr)

    x = _mix_residual(refs[:n_in], tiles)
    x_ref[...] = x
    h = _rms_norm_rows(x, g_ref[...])
    h_ref[...] = h.astype(BF16)
    logits = _mm_x3(h, wr_ref[...])
    lane = _iota2(logits.shape, 1)
    logits = jnp.where(lane < N_EXPERTS, logits, NEG_INF)
    e = jnp.exp(logits - jnp.max(logits, axis=-1, keepdims=True))
    p = e / jnp.sum(e, axis=-1, keepdims=True)
    p = jnp.where(lane < N_EXPERTS, p, -1.0)
    m1 = jnp.max(p, axis=-1, keepdims=True)
    i1 = jnp.min(jnp.where(p == m1, lane, LANES), axis=-1, keepdims=True)
    p2 = jnp.where(lane == i1, -1.0, p)
    m2 = jnp.max(p2, axis=-1, keepdims=True)
    i2 = jnp.min(jnp.where(p2 == m2, lane, LANES), axis=-1, keepdims=True)
    tot = m1 + m2
    gate_ref[...] = jnp.where(lane == 0, m1 / tot, jnp.where(lane == 1, m2 / tot, 0.0))
    tm = logits.shape[0]
    chosen = jnp.where((lane == i1) | (lane == i2), 1.0, 0.0)
    tril = jnp.where(_iota2((tm, tm), 1) <= _iota2((tm, tm), 0), 1.0, 0.0).astype(BF16)
    incl = jnp.dot(tril, chosen.astype(BF16), preferred_element_type=F32)
    rank = cnt_scr[...] + incl - chosen
    r1 = jnp.sum(jnp.where(lane == i1, rank, 0.0), axis=-1, keepdims=True)
    r2 = jnp.sum(jnp.where(lane == i2, rank, 0.0), axis=-1, keepdims=True)
    info = jnp.where(lane == 0, i1.astype(F32), jnp.where(lane == 1, i2.astype(F32),
                                                          jnp.where(lane == 2, r1, jnp.where(lane == 3, r2, 0.0))))
    idx_ref[...] = info.T[:ROW_ALIGN, :].astype(jnp.int32)
    cnt_scr[...] = cnt_scr[...] + incl[tm - 1:tm, :]
    cnt_ref[...] = jnp.broadcast_to(cnt_scr[...], cnt_ref.shape).astype(jnp.int32)


def _mix_router(mix_args, gain, wr_pad, tm):
    n = sum(_part_tiles(mix_args[0], tm)) * tm
    return _mix_call(
        _mix_router_kernel, *mix_args, tm, (gain, wr_pad),
        [pl.BlockSpec((1, D_MODEL), lambda i: (0, 0)), pl.BlockSpec((D_MODEL, LANES), lambda i: (0, 0))],
        (jax.ShapeDtypeStruct((n, D_MODEL), F32), jax.ShapeDtypeStruct((n, D_MODEL), BF16),
         jax.ShapeDtypeStruct((n, LANES), F32), jax.ShapeDtypeStruct((ROW_ALIGN, n), jnp.int32),
         jax.ShapeDtypeStruct((ROW_ALIGN, LANES), jnp.int32)),
        (pl.BlockSpec((tm, D_MODEL), lambda i: (i, 0)), pl.BlockSpec((tm, D_MODEL), lambda i: (i, 0)),
         pl.BlockSpec((tm, LANES), lambda i: (i, 0)), pl.BlockSpec((ROW_ALIGN, tm), lambda i: (0, i)),
         pl.BlockSpec((ROW_ALIGN, LANES), lambda i: (0, 0))),
        [pltpu.VMEM((1, LANES), F32)], "mix_router")


def _expert_ffn_kernel(te_ref, nv_ref, xs_ref, wg_ref, wu_ref, wd_ref, o_ref):
    i = pl.program_id(0)

    @pl.when(i < nv_ref[0])
    def _():
        h = xs_ref[...]
        a = jnp.dot(h, wg_ref[0], preferred_element_type=F32)
        b = jnp.dot(h, wu_ref[0], preferred_element_type=F32)
        z = (_silu(a) * b).astype(BF16)
        o_ref[...] = jnp.dot(z, wd_ref[0], preferred_element_type=F32).astype(o_ref.dtype)

    @pl.when(i >= nv_ref[0])
    def _():
        o_ref[...] = jnp.zeros_like(o_ref)


def _expert_ffn(xs, tile_expert, n_valid, wg, wu, wd, tm):
    rows = xs.shape[0]
    wspec = lambda shape: pl.BlockSpec((1,) + shape, lambda i, te, nv: (te[i], 0, 0))
    return pl.pallas_call(
        _expert_ffn_kernel,
        out_shape=jax.ShapeDtypeStruct((rows, D_MODEL), BF16),
        grid_spec=pltpu.PrefetchScalarGridSpec(
            num_scalar_prefetch=2,
            grid=(rows // tm,),
            in_specs=[pl.BlockSpec((tm, D_MODEL), lambda i, te, nv: (i, 0)),
                      wspec((D_MODEL, D_FF)), wspec((D_MODEL, D_FF)), wspec((D_FF, D_MODEL))],
            out_specs=pl.BlockSpec((tm, D_MODEL), lambda i, te, nv: (i, 0)),
        ),
        compiler_params=_cparams(("arbitrary",)),
        name="expert_ffn",
    )(tile_expert, n_valid, xs, wg, wu, wd)


def _moe_combine_kernel(x_ref, y1_ref, y2_ref, gate_ref, nf_ref, *o_refs, tiles):
    i = pl.program_id(0)
    g = gate_ref[...]
    y = x_ref[...] + g[:, 0:1] * y1_ref[...].astype(F32) + g[:, 1:2] * y2_ref[...].astype(F32)
    out = _rms_norm_rows(y, nf_ref[...])
    start = 0
    for o_ref, nt in zip(o_refs, tiles):
        @pl.when((i >= start) & (i < start + nt))
        def _(o_ref=o_ref):
            o_ref[...] = out
        start += nt


def _moe_combine(x2, y1, y2, gates, norm_final, tm, part_rows):
    n = x2.shape[0]
    row = lambda w: pl.BlockSpec((tm, w), lambda i: (i, 0))
    outs = tuple(jax.ShapeDtypeStruct((r, D_MODEL), F32) for r in part_rows)
    return pl.pallas_call(
        functools.partial(_moe_combine_kernel, tiles=_part_tiles(outs, tm)),
        out_shape=outs,
        grid=(n // tm,),
        in_specs=[row(D_MODEL), row(D_MODEL), row(D_MODEL), row(LANES), pl.BlockSpec((1, D_MODEL), lambda i: (0, 0))],
        out_specs=tuple(_part_specs(outs, tm, D_MODEL)),
        compiler_params=_cparams(("arbitrary",)),
        name="moe_combine",
    )(x2, y1, y2, gates, norm_final)


def _moe(x2, h, gates, idx, counts, wg, wu, wd, norm_final, tm, part_rows):
    n = x2.shape[0]
    tme = EXPERT_TILE
    n_tiles = (2 * n + N_EXPERTS * (tme - 1)) // tme + 1
    e1, e2, r1, r2 = idx[0], idx[1], idx[2], idx[3]
    cnt = counts[0, :N_EXPERTS]
    padded = (cnt + tme - 1) // tme * tme
    group_end = jnp.cumsum(padded)
    group_off = group_end - padded
    dense_off = jnp.cumsum(cnt) - cnt
    lookup = lambda table, e: sum(jnp.where(e == k, table[k], 0) for k in range(N_EXPERTS))
    slot1 = lookup(group_off, e1) + r1
    slot2 = lookup(group_off, e2) + r2
    tok = jnp.arange(n, dtype=jnp.int32)
    sorted_tok = jnp.sort(jnp.concatenate([e1 * n + tok, e2 * n + tok])) % n
    tile_start = jnp.arange(n_tiles, dtype=jnp.int32) * tme
    tile_expert = jnp.minimum(jnp.searchsorted(group_end, tile_start, side='right'), N_EXPERTS - 1).astype(jnp.int32)
    n_valid = (group_end[-1:] // tme).astype(jnp.int32)
    rank = (tile_start - group_off[tile_expert])[:, None] + jnp.arange(tme, dtype=jnp.int32)[None, :]
    dense = jnp.clip(dense_off[tile_expert][:, None] + rank, 0, 2 * n - 1)
    src = jnp.where(rank < cnt[tile_expert][:, None], jnp.take(sorted_tok, dense.reshape(-1)).reshape(dense.shape), 0)
    xs = jnp.take(h, src.reshape(-1), axis=0)
    ys = _expert_ffn(xs, tile_expert, n_valid, wg, wu, wd, tme)
    y1 = jnp.take(ys, slot1, axis=0)
    y2 = jnp.take(ys, slot2, axis=0)
    return _moe_combine(x2, y1, y2, gates, norm_final, tm, part_rows)


def _rope_tables(t):
    rows = t // GRID_W
    pos = np.arange(rows * GRID_W)
    row = (pos // GRID_W).astype(np.float32)
    col = (pos % GRID_W).astype(np.float32)
    nf = HEAD_DIM // 4
    inv = jnp.asarray(ROPE_THETA, F32) ** (-jnp.arange(nf, dtype=F32) / nf)
    ar = jnp.asarray(row)[:, None] * inv
    ac = jnp.asarray(col)[:, None] * inv
    cos = jnp.concatenate([jnp.cos(ar), jnp.cos(ar), jnp.cos(ac), jnp.cos(ac)], axis=-1)
    sin = jnp.concatenate([-jnp.sin(ar), jnp.sin(ar), -jnp.sin(ac), jnp.sin(ac)], axis=-1)
    return jnp.tile(cos, (1, H_G)), jnp.tile(sin, (1, H_G))


def _pad_w_in(w):
    a, b_, c, d = 1024, 512, 1040, 1024
    w_a, w_b, w_c, w_d = w[:, :a], w[:, a:a + b_], w[:, a + b_:a + b_ + c], w[:, a + b_ + c:]
    gates = jnp.pad(w_c[:, 1024:], ((0, 0), (0, LANES - 16)))
    return jnp.concatenate([w_a, w_c[:, :1024], w_d, w_b, gates], axis=1).astype(BF16)


def _row(v):
    return v.reshape(1, -1).astype(F32)


def _trunk(xs, p):
    t = xs[0].shape[1]
    part_rows = tuple(x.shape[0] * t for x in xs)
    b = sum(x.shape[0] for x in xs)
    n = b * t
    tm = 256
    cos, sin = _rope_tables(t)
    x_parts = tuple(x.reshape(-1, D_MODEL) for x in xs)
    depth = p['w_in'].shape[0]
    for l in range(depth):
        u2 = _in_proj(x_parts, _row(p['norm_mix'][l]), _pad_w_in(p['w_in'][l]), tm)
        u3 = u2.reshape(b, t, U_COLS)
        (r, v, kk, g, bonus, lwf, lwb, kf, kb, bf, bb) = _rwkv_prep(
            u3, _row(p['rwkv_mu'][l]), p['rwkv_w0'][l], p['rwkv_w2'][l], p['rwkv_a0'][l], p['rwkv_a2'][l],
            p['rwkv_g2'][l], _row(p['rwkv_kk'][l]), _row(p['rwkv_ka'][l]), _row(p['rwkv_rk'][l]))
        yf, yb = _rwkv_core(r, v, kk, lwf, lwb, kf, kb, bf, bb)
        at = _attention(u3, cos, sin, _row(jnp.tile(p['attn_q_norm'][l], H_G)),
                        _row(jnp.tile(p['attn_k_norm'][l], KV_ATTN)))
        gate_bias = jnp.pad(jnp.concatenate([p['mlstm_i_bias'][l].reshape(-1), p['mlstm_f_bias'][l].reshape(-1)]),
                            (0, LANES - 4 * H_G))
        hf, hb = _mlstm(u3, p['mlstm_conv_w'][l], _row(p['mlstm_conv_b'][l]), _row(gate_bias))
        of, ob = _retention(u3, cos, sin)
        flat = lambda z: z.reshape(n, W_G)
        mix_args = (x_parts, tuple(map(flat, (yf, yb, bonus, g))), flat(at), tuple(map(flat, (hf, hb))),
                    tuple(map(flat, (of, ob))), u2, _row(p['rwkv_ln'][l]), _row(p['mlstm_ln'][l]),
                    _row(p['ret_ln'][l]), p['w_out'][l].astype(BF16))
        j = l // 2
        if l % 2 == 0:
            x2 = _mix_ffn(mix_args, _row(p['norm_ffn'][l]), p['ffn_w_gate'][j].astype(BF16),
                          p['ffn_w_up'][j].astype(BF16), p['ffn_w_down'][j].astype(BF16), tm)
            x_parts = (x2,)
            if l == depth - 1:
                raise NotImplementedError("final norm after a dense FFN layer")
        else:
            wr = jnp.pad(p['moe_router'][j], ((0, 0), (0, LANES - N_EXPERTS)))
            x2, h, gates, idx, counts = _mix_router(mix_args, _row(p['norm_ffn'][l]), wr, tm)
            if l != depth - 1:
                raise NotImplementedError("expert layer that is not the last layer")
            outs = _moe(x2, h, gates, idx, counts, p['moe_w_gate'][j].astype(BF16), p['moe_w_up'][j].astype(BF16),
                        p['moe_w_down'][j].astype(BF16), _row(p['norm_final']), tm, part_rows)
    return tuple(o.reshape(x.shape) for o, x in zip(outs, xs))


def kernel(x_prompt, x_sample, norm_mix, norm_ffn, norm_final, w_in, w_out, rwkv_mu, rwkv_w0, rwkv_w2,
           rwkv_a0, rwkv_a2, rwkv_g2, rwkv_kk, rwkv_ka, rwkv_rk, rwkv_ln, attn_q_norm, attn_k_norm,
           mlstm_conv_w, mlstm_conv_b, mlstm_i_bias, mlstm_f_bias, mlstm_ln, ret_ln, ffn_w_gate, ffn_w_up,
           ffn_w_down, moe_router, moe_w_gate, moe_w_up, moe_w_down):
    p = dict(norm_mix=norm_mix, norm_ffn=norm_ffn, norm_final=norm_final, w_in=w_in, w_out=w_out,
             rwkv_mu=rwkv_mu, rwkv_w0=rwkv_w0, rwkv_w2=rwkv_w2, rwkv_a0=rwkv_a0, rwkv_a2=rwkv_a2,
             rwkv_g2=rwkv_g2, rwkv_kk=rwkv_kk, rwkv_ka=rwkv_ka, rwkv_rk=rwkv_rk, rwkv_ln=rwkv_ln,
             attn_q_norm=attn_q_norm, attn_k_norm=attn_k_norm, mlstm_conv_w=mlstm_conv_w,
             mlstm_conv_b=mlstm_conv_b, mlstm_i_bias=mlstm_i_bias, mlstm_f_bias=mlstm_f_bias,
             mlstm_ln=mlstm_ln, ret_ln=ret_ln, ffn_w_gate=ffn_w_gate, ffn_w_up=ffn_w_up,
             ffn_w_down=ffn_w_down, moe_router=moe_router, moe_w_gate=moe_w_gate, moe_w_up=moe_w_up,
             moe_w_down=moe_w_down)
    return _trunk((x_prompt, x_sample), p)
```

```python
import functools
import math

import numpy as np
import jax
import jax.numpy as jnp
from jax import lax
from jax.experimental import pallas as pl
from jax.experimental.pallas import tpu as pltpu

F32 = jnp.float32
BF16 = jnp.bfloat16

D_MODEL = 1024
HEAD_DIM = 64
W_G = 256
H_G = 4
KV_ATTN = 2
D_FF = 2816
N_EXPERTS = 8
NORM_EPS = 1e-6
HEAD_NORM_EPS = 1e-5
RWKV_GN_EPS = 64e-5
NEG_INF = -1e30
ROPE_THETA = 10000.0
GRID_W = 64

LANES = 128
ROW_ALIGN = 8
VMEM_LIMIT_BYTES = 56 * 1024 * 1024

U_RWKV = 0
U_MLSTM = 1024
U_RET = 2048
U_ATTN = 3072
U_GATE = 3584
U_COLS = 3712

RWKV_CHUNK = 64
RWKV_BLOCK = 256
RWKV_GROUP = 4
MIX_CHUNK = 128
MIX_BLOCK = 256
EXPERT_TILE = 512
EXPERT_FF_SPLIT = 2


def _cparams(sem):
    return pltpu.CompilerParams(dimension_semantics=sem, vmem_limit_bytes=VMEM_LIMIT_BYTES)


def _bdot(a, b, dims):
    return lax.dot_general(a, b, (dims, ((), ())), preferred_element_type=F32)


def _mm(a, b):
    return _bdot(a.astype(BF16), b.astype(BF16), ((1,), (0,)))


def _mm_nt(a, b):
    return _bdot(a.astype(BF16), b.astype(BF16), ((1,), (1,)))


def _mm_tn(a, b):
    return _bdot(a.astype(BF16), b.astype(BF16), ((0,), (0,)))


def _split2(a):
    hi = a.astype(BF16)
    lo = (a - hi.astype(F32)).astype(BF16)
    return hi, lo


def _split3(a):
    hi = a.astype(BF16)
    r = a - hi.astype(F32)
    mid = r.astype(BF16)
    lo = (r - mid.astype(F32)).astype(BF16)
    return hi, mid, lo


def _mm_l2(a, b_exact):
    hi, lo = _split2(a)
    return _bdot(hi, b_exact, ((1,), (0,))) + _bdot(lo, b_exact, ((1,), (0,)))


def _mm_l3(a, b_exact):
    h, m, l = _split3(a)
    return _bdot(h, b_exact, ((1,), (0,))) + _bdot(m, b_exact, ((1,), (0,))) + _bdot(l, b_exact, ((1,), (0,)))


def _mm_r3(a_exact, b):
    h, m, l = _split3(b)
    return _bdot(a_exact, h, ((1,), (0,))) + _bdot(a_exact, m, ((1,), (0,))) + _bdot(a_exact, l, ((1,), (0,)))


def _mm_x3(a, b):
    ah, al = _split2(a)
    bh, bl = _split2(b)
    d = ((1,), (0,))
    return _bdot(ah, bh, d) + _bdot(ah, bl, d) + _bdot(al, bh, d)


def _iota2(shape, axis):
    return lax.broadcasted_iota(jnp.int32, shape, axis)


def _head_mean_matrix(width):
    r = _iota2((width, width), 0) // HEAD_DIM
    c = _iota2((width, width), 1) // HEAD_DIM
    return jnp.where(r == c, 1.0 / HEAD_DIM, 0.0).astype(BF16)


def _head_mean(z, bd):
    return _mm_l2(z, bd)


def _sigmoid(x):
    return 1.0 / (1.0 + jnp.exp(-x))


def _silu(x):
    return x * _sigmoid(x)


def _log_sigmoid(x):
    return jnp.minimum(x, 0.0) - jnp.log(1.0 + jnp.exp(-jnp.abs(x)))


def _rms_norm_rows(x, gain):
    ms = jnp.mean(x * x, axis=-1, keepdims=True)
    return x * lax.rsqrt(ms + NORM_EPS) * gain


def _rope_swap(z):
    w = z.shape[-1]
    lane = _iota2(z.shape, z.ndim - 1)
    fwd = pltpu.roll(z, w - 16, z.ndim - 1)
    bwd = pltpu.roll(z, 16, z.ndim - 1)
    return jnp.where((lane % 32) < 16, fwd, bwd)


def _rope(z, cos, sin):
    return z * cos + _rope_swap(z) * sin


def _shift_rows(x, prev_row, next_row):
    n = x.shape[0]
    row = _iota2(x.shape, 0)
    prev = jnp.where(row == 0, prev_row, pltpu.roll(x, 1, 0))
    nxt = jnp.where(row == n - 1, next_row, pltpu.roll(x, n - 1, 0))
    return prev, nxt


def _part_tiles(parts, tm):
    return tuple(p.shape[0] // tm for p in parts)


def _part_specs(parts, tm, width):
    specs, start = [], 0
    for nt in _part_tiles(parts, tm):
        specs.append(pl.BlockSpec((tm, width), lambda i, s=start, nt=nt: (jnp.clip(i - s, 0, nt - 1), 0)))
        start += nt
    return specs


def _part_tile(i, refs, tiles):
    x = refs[0][...]
    start = tiles[0]
    for ref, nt in zip(refs[1:], tiles[1:]):
        x = jnp.where(i >= start, ref[...], x)
        start += nt
    return x


def _inproj_kernel(*refs, tiles):
    x_refs, (g_ref, w_ref, o_ref) = refs[:len(tiles)], refs[len(tiles):]
    h = _rms_norm_rows(_part_tile(pl.program_id(0), x_refs, tiles), g_ref[...])
    o_ref[...] = jnp.dot(h.astype(BF16), w_ref[...], preferred_element_type=F32)


def _in_proj(x_parts, gain, w_pad, tm):
    tiles = _part_tiles(x_parts, tm)
    n = sum(tiles) * tm
    return pl.pallas_call(
        functools.partial(_inproj_kernel, tiles=tiles),
        out_shape=jax.ShapeDtypeStruct((n, U_COLS), F32),
        grid=(n // tm,),
        in_specs=_part_specs(x_parts, tm, D_MODEL) + [
            pl.BlockSpec((1, D_MODEL), lambda i: (0, 0)),
            pl.BlockSpec((D_MODEL, U_COLS), lambda i: (0, 0), pipeline_mode=pl.Buffered(1)),
        ],
        out_specs=pl.BlockSpec((tm, U_COLS), lambda i: (i, 0)),
        compiler_params=_cparams(("arbitrary",)),
        name="in_proj",
    )(*x_parts, gain, w_pad)


def _attn_kernel(u_ref, cos_ref, sin_ref, qg_ref, kg_ref, o_ref, q_scr, k_scr, v_scr, *, tq):
    t = u_ref.shape[1]
    u = u_ref[0]
    q = u[:, :W_G]
    k = u[:, W_G:W_G + KV_ATTN * HEAD_DIM]
    v = u[:, W_G + KV_ATTN * HEAD_DIM:]
    cos = cos_ref[...]
    sin = sin_ref[...]
    bd_q = _head_mean_matrix(W_G)
    bd_k = _head_mean_matrix(KV_ATTN * HEAD_DIM)
    qn = q * lax.rsqrt(_head_mean(q * q, bd_q) + NORM_EPS) * qg_ref[...]
    kn = k * lax.rsqrt(_head_mean(k * k, bd_k) + NORM_EPS) * kg_ref[...]
    kw = KV_ATTN * HEAD_DIM
    q_scr[...] = (_rope(qn, cos, sin) * (HEAD_DIM ** -0.5 * math.log2(math.e))).astype(BF16)
    k_scr[...] = _rope(kn, cos[:, :kw], sin[:, :kw]).astype(BF16)
    ones = jnp.ones((t, HEAD_DIM), BF16)
    for j in range(KV_ATTN):
        vj = v[:, j * HEAD_DIM:(j + 1) * HEAD_DIM].astype(BF16)
        v_scr[:, j * LANES:(j + 1) * LANES] = jnp.concatenate([vj, ones], axis=1)
    group = H_G // KV_ATTN

    def q_tile(i, carry):
        rows = pl.ds(pl.multiple_of(i * tq, tq), tq)

        def scores(h):
            j = h // group
            qh = q_scr[rows, h * HEAD_DIM:(h + 1) * HEAD_DIM]
            return _bdot(qh, k_scr[:, j * HEAD_DIM:(j + 1) * HEAD_DIM], ((1,), (1,)))

        s_next = scores(0)
        for h in range(H_G):
            s = s_next
            if h + 1 < H_G:
                s_next = scores(h + 1)
            m = jnp.max(s, axis=-1, keepdims=True)
            p = jnp.exp2(s - m)
            vj = v_scr[:, (h // group) * LANES:(h // group + 1) * LANES]
            r = jnp.dot(p.astype(BF16), vj, preferred_element_type=F32)
            o_ref[0, rows, h * HEAD_DIM:(h + 1) * HEAD_DIM] = r[:, :HEAD_DIM] / r[:, HEAD_DIM:]
        return carry

    lax.fori_loop(0, t // tq, q_tile, 0)


def _attention(u3, cos, sin, q_gain, k_gain):
    b, t, _ = u3.shape
    tq = min(256, t)
    col = U_ATTN // 512
    return pl.pallas_call(
        functools.partial(_attn_kernel, tq=tq),
        out_shape=jax.ShapeDtypeStruct((b, t, W_G), F32),
        grid=(b,),
        in_specs=[
            pl.BlockSpec((1, t, 512), lambda i: (i, 0, col)),
            pl.BlockSpec((t, W_G), lambda i: (0, 0)),
            pl.BlockSpec((t, W_G), lambda i: (0, 0)),
            pl.BlockSpec((1, W_G), lambda i: (0, 0)),
            pl.BlockSpec((1, KV_ATTN * HEAD_DIM), lambda i: (0, 0)),
        ],
        out_specs=pl.BlockSpec((1, t, W_G), lambda i: (i, 0, 0)),
        scratch_shapes=[
            pltpu.VMEM((t, W_G), BF16),
            pltpu.VMEM((t, KV_ATTN * HEAD_DIM), BF16),
            pltpu.VMEM((t, KV_ATTN * LANES), BF16),
        ],
        compiler_params=_cparams(("parallel",)),
        name="attention",
    )(u3, cos, sin, q_gain, k_gain)


def _ret_log_gamma(direction):
    return [math.log1p(-2.0 ** (-5.0 - (2 * h + direction) / 2.0)) for h in range(H_G)]


def _ret_kernel(uf_ref, ub_ref, cf_ref, sf_ref, cb_ref, sb_ref, of_ref, ob_ref, rf_scr, rb_scr):
    i = pl.program_id(1)
    c = MIX_CHUNK

    @pl.when(i == 0)
    def _():
        rf_scr[...] = jnp.zeros_like(rf_scr)
        rb_scr[...] = jnp.zeros_like(rb_scr)

    tt = _iota2((c, c), 0)
    ss = _iota2((c, c), 1)
    diff = (tt - ss).astype(F32)
    jcol = tt.astype(F32)
    first = ss < HEAD_DIM
    block_diag = (tt < HEAD_DIM) == first
    lg_f = _ret_log_gamma(0)
    lg_b = _ret_log_gamma(1)
    psl = lambda p: slice(p * LANES, (p + 1) * LANES)
    lane_lg = lambda lg, p: jnp.where(first, lg[2 * p], lg[2 * p + 1])
    row_decay = lambda lg, p: jnp.where(tt < HEAD_DIM, math.exp(c * lg[2 * p]), math.exp(c * lg[2 * p + 1]))
    own = lambda h, x: jnp.where(first, x, 0.0) if h % 2 == 0 else jnp.where(first, 0.0, x)
    pairs = range(H_G // 2)

    nch = uf_ref.shape[1] // c
    uf = uf_ref[0]
    qf_all = _rope(uf[:, :W_G], cf_ref[...], sf_ref[...])
    kf_all = _rope(uf[:, W_G:2 * W_G], cf_ref[...], sf_ref[...]) * HEAD_DIM ** -0.5
    ub = ub_ref[0]
    qb_all = _rope(ub[:, :W_G], cb_ref[...], sb_ref[...])
    kb_all = _rope(ub[:, W_G:2 * W_G], cb_ref[...], sb_ref[...]) * HEAD_DIM ** -0.5
    rf_prev = [rf_scr[p] for p in pairs]
    rb_prev = [rb_scr[p] for p in pairs]
    decay = [jnp.where(tt >= ss, jnp.exp(diff * lg_f[h]), 0.0) + jnp.where(ss >= tt, jnp.exp(-diff * lg_b[h]), 0.0)
             for h in range(H_G)]
    for n in range(nch):
        rf_rows = slice(n * c, (n + 1) * c)
        rb_rows = slice((nch - 1 - n) * c, (nch - n) * c)
        qf, kf, vf = qf_all[rf_rows], kf_all[rf_rows], uf[rf_rows, 2 * W_G:3 * W_G]
        qb, kb, vb = qb_all[rb_rows], kb_all[rb_rows], ub[rb_rows, 2 * W_G:3 * W_G]
        qk = [_mm_nt(qf[:, psl(h // 2)], own(h, kf[:, psl(h // 2)])) for h in range(H_G)]
        intra = [_mm(qk[h] * decay[h], own(h, vf[:, psl(h // 2)])) for h in range(H_G)]
        inter_f = [_mm(qf[:, psl(p)] * jnp.exp((jcol + 1.0) * lane_lg(lg_f, p)), rf_prev[p]) for p in pairs]
        inter_b = [_mm(qb[:, psl(p)] * jnp.exp((c - jcol) * lane_lg(lg_b, p)), rb_prev[p]) for p in pairs]
        upd_f = [_mm_tn(kf[:, psl(p)] * jnp.exp((c - 1.0 - jcol) * lane_lg(lg_f, p)), vf[:, psl(p)]) for p in pairs]
        upd_b = [_mm_tn(kb[:, psl(p)] * jnp.exp(jcol * lane_lg(lg_b, p)), vb[:, psl(p)]) for p in pairs]
        for p in pairs:
            of_ref[0, rf_rows, psl(p)] = intra[2 * p] + intra[2 * p + 1] + inter_f[p]
            ob_ref[0, rb_rows, psl(p)] = inter_b[p]
        rf_prev = [row_decay(lg_f, p) * rf_prev[p] + jnp.where(block_diag, upd_f[p], 0.0) for p in pairs]
        rb_prev = [row_decay(lg_b, p) * rb_prev[p] + jnp.where(block_diag, upd_b[p], 0.0) for p in pairs]
    for p in pairs:
        rf_scr[p] = rf_prev[p]
        rb_scr[p] = rb_prev[p]


def _retention(u3, cos, sin):
    b, t, _ = u3.shape
    c = min(MIX_BLOCK, t)
    nblk = t // c
    col = U_RET // 1024
    tab = lambda rev: pl.BlockSpec((c, W_G), (lambda bi, i: (nblk - 1 - i, 0)) if rev else (lambda bi, i: (i, 0)))
    return pl.pallas_call(
        _ret_kernel,
        out_shape=(jax.ShapeDtypeStruct((b, t, W_G), F32), jax.ShapeDtypeStruct((b, t, W_G), F32)),
        grid=(b, nblk),
        in_specs=[
            pl.BlockSpec((1, c, 1024), lambda bi, i: (bi, i, col)),
            pl.BlockSpec((1, c, 1024), lambda bi, i: (bi, nblk - 1 - i, col)),
            tab(False), tab(False), tab(True), tab(True),
        ],
        out_specs=(
            pl.BlockSpec((1, c, W_G), lambda bi, i: (bi, i, 0)),
            pl.BlockSpec((1, c, W_G), lambda bi, i: (bi, nblk - 1 - i, 0)),
        ),
        scratch_shapes=[pltpu.VMEM((H_G // 2, LANES, LANES), F32), pltpu.VMEM((H_G // 2, LANES, LANES), F32)],
        compiler_params=_cparams(("parallel", "arbitrary")),
        name="retention",
    )(u3, u3, cos, sin, cos, sin)


def _mlstm_tile(u_ref, up_ref, un_ref, g_ref, cw_ref, cb_ref, gb_ref, blk, nblk, direction):
    c = MIX_CHUNK
    reverse = direction == 1
    u = u_ref[0]
    qk = u[:, :2 * W_G]
    prev_row = jnp.where(blk == 0, 0.0, up_ref[0][ROW_ALIGN - 1:ROW_ALIGN, :])
    next_row = jnp.where(blk == nblk - 1, 0.0, un_ref[0][0:1, :])
    prev, nxt = _shift_rows(qk, prev_row, next_row)
    cw = cw_ref[...]
    qk = _silu(cw[0:1] * prev + cw[1:2] * qk + cw[2:3] * nxt + cb_ref[...])
    qa = qk[:, :W_G]
    ka = qk[:, W_G:] * HEAD_DIM ** -0.5
    va = u[:, 2 * W_G:3 * W_G]

    x = g_ref[0] + gb_ref[...]
    xt = x.T
    lf_c = _log_sigmoid(x)
    lf_r = _log_sigmoid(xt)
    tt = _iota2((c, c), 0)
    ss = _iota2((c, c), 1)
    lower = jnp.where(ss <= tt, 1.0, 0.0).astype(BF16)
    upper = jnp.where(tt <= ss, 1.0, 0.0).astype(BF16)
    if reverse:
        b_c = _mm_r3(upper, lf_c)
        b_r = _mm_l3(lf_r, lower)
        mask = ss >= tt
    else:
        b_c = _mm_r3(lower, lf_c)
        b_r = _mm_l3(lf_r, upper)
        mask = ss <= tt
    g_all = jnp.sum(lf_c, axis=0, keepdims=True)
    return dict(q=qa, k=ka, v=va, x=x, xt=xt, b_c=b_c, b_r=b_r, g_all=g_all, mask=mask)


def _mlstm_select_matrix():
    sel = np.zeros((2, 2 * LANES, 8 * LANES), np.float32)
    for d in range(2):
        for h in range(H_G):
            ci, cf = d * H_G + h, 2 * H_G + d * H_G + h
            p, j = divmod(h, 2)
            sel[d, cf, LANES * h:LANES * (h + 1)] = 1.0
            sel[d, cf, 4 * LANES + LANES * p + HEAD_DIM * j:4 * LANES + LANES * p + HEAD_DIM * (j + 1)] = 1.0
            sel[d, LANES + ci, 6 * LANES + LANES * p + HEAD_DIM * j:6 * LANES + LANES * p + HEAD_DIM * (j + 1)] = 1.0
    return jnp.asarray(sel, BF16)


def _mlstm_kernel(uf_ref, upf_ref, unf_ref, gf_ref, ub_ref, upb_ref, unb_ref, gbk_ref,
                  cw_ref, cb_ref, gb_ref, sel_ref, of_ref, ob_ref, st_scr, m_scr):
    i = pl.program_id(1)
    nblk = pl.num_programs(1)
    c = MIX_CHUNK

    @pl.when(i == 0)
    def _():
        st_scr[...] = jnp.zeros_like(st_scr)
        m_scr[...] = jnp.zeros_like(m_scr)

    tiles = (_mlstm_tile(uf_ref, upf_ref, unf_ref, gf_ref, cw_ref, cb_ref, gb_ref, i, nblk, 0),
             _mlstm_tile(ub_ref, upb_ref, unb_ref, gbk_ref, cw_ref, cb_ref, gb_ref, nblk - 1 - i, nblk, 1))
    o_refs = (of_ref, ob_ref)
    sel = [_mm_l2(jnp.concatenate([tiles[d]["b_c"], tiles[d]["x"]], axis=1), sel_ref[d]) for d in range(2)]
    first = _iota2((c, LANES), 1) < HEAD_DIM
    row_first = _iota2((LANES, 2 * LANES), 0) < HEAD_DIM
    lane2 = _iota2((LANES, 2 * LANES), 1) % LANES < HEAD_DIM
    block_diag = row_first == lane2
    ones = jnp.ones((c, LANES), F32)
    pairs = [(d, p) for d in range(2) for p in range(H_G // 2)]
    heads = [(d, h) for d in range(2) for h in range(H_G)]
    psl = lambda p: slice(p * LANES, (p + 1) * LANES)
    q_pair = {dp: tiles[dp[0]]["q"][:, psl(dp[1])] for dp in pairs}
    k_pair = {dp: tiles[dp[0]]["k"][:, psl(dp[1])] for dp in pairs}
    v_pair = {dp: tiles[dp[0]]["v"][:, psl(dp[1])] for dp in pairs}
    state = {dp: st_scr[n] for n, dp in enumerate(pairs)}
    m_row = {dp: m_scr[n:n + 1, :] for n, dp in enumerate(pairs)}

    def own(d, h, x):
        return jnp.where(first, x, 0.0) if h % 2 == 0 else jnp.where(first, 0.0, x)

    qk = [_mm_nt(q_pair[(d, h // 2)], own(d, h, k_pair[(d, h // 2)])) for d, h in heads]
    qs = {dp: _mm(q_pair[dp], state[dp]) for dp in pairs}
    bc = [sel[d][:, LANES * h:LANES * (h + 1)] for d, h in heads]
    m_prev = [m_row[(d, h // 2)][:, HEAD_DIM * (h % 2):HEAD_DIM * (h % 2) + 1] for d, h in heads]
    dlog = []
    for (d, h), bc_ in zip(heads, bc):
        ci, cf = d * H_G + h, 2 * H_G + d * H_G + h
        rowterm = tiles[d]["xt"][ci:ci + 1, :] - tiles[d]["b_r"][cf:cf + 1, :]
        dlog.append(jnp.where(tiles[d]["mask"], bc_ + rowterm, NEG_INF))
    inter_log = [bc_ + m_ for bc_, m_ in zip(bc, m_prev)]
    m_t = [jnp.maximum(il, jnp.max(dl, axis=-1, keepdims=True)) for il, dl in zip(inter_log, dlog)]
    sc = [qk_ * jnp.exp(dl - mt) for qk_, dl, mt in zip(qk, dlog, m_t)]
    w_inter = [jnp.exp(il - mt) for il, mt in zip(inter_log, m_t)]
    e_neg = [jnp.exp(-mt) for mt in m_t]
    res = []
    for n, (d, h) in enumerate(heads):
        v_aug = jnp.concatenate([own(d, h, v_pair[(d, h // 2)]), own(d, h, ones)], axis=1)
        res.append(_mm(sc[n], v_aug))
    for n, (d, p) in enumerate(pairs):
        a, b_ = 2 * n, 2 * n + 1
        tot = res[a] + res[b_] + jnp.tile(jnp.where(first, w_inter[a], w_inter[b_]), (1, 2)) * qs[(d, p)]
        den = jnp.maximum(jnp.abs(tot[:, LANES:]), jnp.where(first, e_neg[a], e_neg[b_]))
        o_refs[d][0, :, psl(p)] = tot[:, :LANES] / den

    for n, (d, p) in enumerate(pairs):
        bcp = sel[d][:, 4 * LANES + LANES * p:4 * LANES + LANES * (p + 1)]
        lip = sel[d][:, 6 * LANES + LANES * p:6 * LANES + LANES * (p + 1)]
        g_row = bcp[0:1, :] if d == 1 else bcp[c - 1:c, :]
        a_p = g_row - bcp + lip
        m_new = jnp.maximum(g_row + m_row[(d, p)], jnp.max(a_p, axis=0, keepdims=True))
        dec = jnp.exp(g_row + m_row[(d, p)] - m_new)
        kw_t = (k_pair[(d, p)] * jnp.exp(a_p - m_new)).T
        upd = _mm(kw_t, jnp.concatenate([v_pair[(d, p)], ones], axis=1))
        dec_tile = jnp.where(row_first, dec[:, 0:1], dec[:, HEAD_DIM:HEAD_DIM + 1])
        st_scr[n] = dec_tile * state[(d, p)] + jnp.where(block_diag, upd, 0.0)
        m_scr[n:n + 1, :] = m_new


def _mlstm(u3, conv_w, conv_b, gate_bias):
    b, t, _ = u3.shape
    c = MIX_CHUNK
    nblk = t // c
    rpb = c // ROW_ALIGN
    n8 = t // ROW_ALIGN
    col = U_MLSTM // 1024
    hcol = U_MLSTM // 512
    gcol = U_GATE // LANES

    def specs(rev):
        blk = (lambda i: nblk - 1 - i) if rev else (lambda i: i)
        return [
            pl.BlockSpec((1, c, 1024), lambda bi, i: (bi, blk(i), col)),
            pl.BlockSpec((1, ROW_ALIGN, 512), lambda bi, i: (bi, jnp.maximum(blk(i) * rpb - 1, 0), hcol)),
            pl.BlockSpec((1, ROW_ALIGN, 512), lambda bi, i: (bi, jnp.minimum((blk(i) + 1) * rpb, n8 - 1), hcol)),
            pl.BlockSpec((1, c, LANES), lambda bi, i: (bi, blk(i), gcol)),
        ]

    const = lambda shape: pl.BlockSpec(shape, lambda bi, i: (0,) * len(shape))
    return pl.pallas_call(
        _mlstm_kernel,
        out_shape=(jax.ShapeDtypeStruct((b, t, W_G), F32), jax.ShapeDtypeStruct((b, t, W_G), F32)),
        grid=(b, nblk),
        in_specs=specs(False) + specs(True) + [const((3, 2 * W_G)), const((1, 2 * W_G)), const((1, LANES)),
                                               const((2, 2 * LANES, 8 * LANES))],
        out_specs=(
            pl.BlockSpec((1, c, W_G), lambda bi, i: (bi, i, 0)),
            pl.BlockSpec((1, c, W_G), lambda bi, i: (bi, nblk - 1 - i, 0)),
        ),
        scratch_shapes=[pltpu.VMEM((H_G, LANES, 2 * LANES), F32), pltpu.VMEM((ROW_ALIGN, LANES), F32)],
        compiler_params=_cparams(("parallel", "arbitrary")),
        name="mlstm",
    )(u3, u3, u3, u3, u3, u3, u3, u3, conv_w, conv_b, gate_bias, _mlstm_select_matrix())


def _rwkv_prep_kernel(u_ref, up_ref, un_ref, mu_ref, w0_ref, w2_ref, a0_ref, a2_ref, g2_ref, kks_ref, ka_ref, rk_ref,
                      r_ref, v_ref, kk_ref, g_ref, bonus_ref, lwf_ref, lwb_ref, kf_ref, kb_ref, bf_ref, bb_ref):
    i = pl.program_id(1)
    nblk = pl.num_programs(1)
    u = u_ref[0]
    prev_row = jnp.where(i == 0, 0.0, up_ref[0][ROW_ALIGN - 1:ROW_ALIGN, :])
    next_row = jnp.where(i == nblk - 1, 0.0, un_ref[0][0:1, :])
    prev, nxt = _shift_rows(u, prev_row, next_row)
    us = u + mu_ref[...] * (0.5 * (prev + nxt) - u)
    r = us[:, 0:W_G]
    k = us[:, W_G:2 * W_G]
    v = us[:, 2 * W_G:3 * W_G]
    xw = us[:, 3 * W_G:3 * W_G + 64]
    xa = us[:, 3 * W_G + 64:3 * W_G + 128]
    xg = us[:, 3 * W_G + 128:]
    bd = _head_mean_matrix(W_G)
    g = _mm(_sigmoid(xg), g2_ref[...])
    lw = jnp.tanh(xw)
    a_lr = _mm_x3(xa, a2_ref[...])
    kk = k * kks_ref[...]
    kk = kk * lax.rsqrt(_head_mean(kk * kk, bd) * HEAD_DIM + 1e-12)
    r_ref[0] = r
    v_ref[0] = v
    kk_ref[0] = kk
    g_ref[0] = g
    bonus = jnp.zeros_like(r)
    for d, (lw_ref, k_ref, b_ref) in enumerate(((lwf_ref, kf_ref, bf_ref), (lwb_ref, kb_ref, bb_ref))):
        z = w0_ref[d:d + 1, :] + _mm_x3(lw, w2_ref[d])
        lw_ref[0] = -_sigmoid(z) * math.exp(-0.5)
        a = _sigmoid(a0_ref[d:d + 1, :] + a_lr)
        kd = k * (1.0 + (a - 1.0) * ka_ref[...])
        k_ref[0] = kd
        b_ref[0] = kk * a
        bonus = bonus + _head_mean(r * kd * rk_ref[...], bd) * HEAD_DIM * v
    bonus_ref[0] = bonus


def _rwkv_prep(u3, mu, w0, w2, a0, a2, g2, kks, ka, rk):
    b, t, _ = u3.shape
    tb = min(RWKV_BLOCK, t)
    nblk = t // tb
    rpb = tb // ROW_ALIGN
    n8 = t // ROW_ALIGN
    const = lambda shape: pl.BlockSpec(shape, lambda bi, i: (0,) * len(shape))
    out = jax.ShapeDtypeStruct((b, t, W_G), F32)
    ospec = pl.BlockSpec((1, tb, W_G), lambda bi, i: (bi, i, 0))
    return pl.pallas_call(
        _rwkv_prep_kernel,
        out_shape=(out,) * 11,
        grid=(b, nblk),
        in_specs=[
            pl.BlockSpec((1, tb, 1024), lambda bi, i: (bi, i, 0)),
            pl.BlockSpec((1, ROW_ALIGN, 1024), lambda bi, i: (bi, jnp.maximum(i * rpb - 1, 0), 0)),
            pl.BlockSpec((1, ROW_ALIGN, 1024), lambda bi, i: (bi, jnp.minimum((i + 1) * rpb, n8 - 1), 0)),
            const((1, 1024)), const((2, W_G)), const((2, 64, W_G)), const((2, W_G)), const((64, W_G)),
            const((128, W_G)), const((1, W_G)), const((1, W_G)), const((1, W_G)),
        ],
        out_specs=(ospec,) * 11,
        compiler_params=_cparams(("parallel", "parallel")),
        name="rwkv_prep",
    )(u3, u3, u3, mu, w0, w2, a0, a2, g2, kks, ka, rk)


def _tri_inverse_all(lmats, n):
    r = _iota2((n, n), 0)
    c = _iota2((n, n), 1)
    eye = jnp.where(r == c, 1.0, 0.0)
    pair = (r // 2 == c // 2) & (r != c)
    invs = [eye + jnp.where(pair, lm, 0.0) for lm in lmats]
    s = 2
    while s < n:
        sel = (r // (2 * s) == c // (2 * s)) & (r // s != c // s)
        offs = [jnp.where(sel, -lm, 0.0) for lm in lmats]
        xs = [_mm(inv, off) for inv, off in zip(invs, offs)]
        invs = [inv - _mm(x, inv) for inv, x in zip(invs, xs)]
        s *= 2
    return invs


def _rwkv_tile_terms(r, k, v, kk, b, lw, reverse):
    c = r.shape[0]
    tt = _iota2((c, c), 0)
    ss = _iota2((c, c), 1)
    tri = jnp.where((tt <= ss) if reverse else (ss <= tt), 1.0, 0.0).astype(BF16)
    cum_in = _mm_r3(tri, lw)
    cum_all = jnp.sum(lw, axis=0, keepdims=True)
    e_neg = jnp.exp(-cum_in)
    e_end = jnp.exp(cum_all - cum_in)
    return dict(at=-kk * jnp.exp(cum_in - lw), rt=r * jnp.exp(cum_in), bt=b * e_neg, kt=k * e_neg,
                gb=b * e_end, gk=k * e_end, v=v, e_all=jnp.exp(cum_all))


def _rwkv_chunk_terms(tiles, reverses):
    c = RWKV_CHUNK
    tt = _iota2((c, c), 0)
    ss = _iota2((c, c), 1)
    heads = [(ti, h) for ti in range(len(tiles)) for h in range(H_G)]
    sl = lambda h: slice(h * HEAD_DIM, (h + 1) * HEAD_DIM)
    get = lambda name: [tiles[ti][name][:, sl(h)] for ti, h in heads]
    at, rt, bt, kt, gb, gk, v = (get(nm) for nm in ("at", "rt", "bt", "kt", "gb", "gk", "v"))
    strict = [(ss > tt) if reverses[ti] else (ss < tt) for ti, _ in heads]
    incl = [(ss >= tt) if reverses[ti] else (ss <= tt) for ti, _ in heads]
    ps = [_mm_nt(jnp.concatenate([a, r_], axis=0), jnp.concatenate([b_, k_], axis=0))
          for a, r_, b_, k_ in zip(at, rt, bt, kt)]
    l_ab = [jnp.where(m, p[:c, :c], 0.0) for m, p in zip(strict, ps)]
    l_ak = [jnp.where(m, p[:c, c:], 0.0) for m, p in zip(strict, ps)]
    m_r = [jnp.concatenate([jnp.where(m, p[c:, :c], 0.0), jnp.where(m, p[c:, c:], 0.0)], axis=1)
           for m, p in zip(incl, ps)]
    lakv = [_mm(l, v_) for l, v_ in zip(l_ak, v)]
    invs = _tri_inverse_all(l_ab, c)
    tw = [_mm(inv, jnp.concatenate([a, lv], axis=1)) for inv, a, lv in zip(invs, at, lakv)]
    zeros = jnp.zeros((c, HEAD_DIM), F32)
    mm2 = [_mm(m, jnp.concatenate([t_, jnp.concatenate([zeros, v_], axis=1)], axis=0))
           for m, t_, v_ in zip(m_r, tw, v)]
    r1 = [r_ + m[:, :HEAD_DIM] for r_, m in zip(rt, mm2)]
    y0 = [m[:, HEAD_DIM:] for m in mm2]
    twg = [_mm_tn(t_, g_) for t_, g_ in zip(tw, gb)]
    vgk = [_mm_tn(v_, g_) for v_, g_ in zip(v, gk)]
    mlow = [x[:HEAD_DIM] for x in twg]
    nadd = [x[HEAD_DIM:] + y for x, y in zip(twg, vgk)]
    e_all = [tiles[ti]["e_all"][:, sl(h)] for ti, h in heads]
    return r1, y0, mlow, nadd, e_all


def _rwkv_core_kernel(rf_ref, kf_ref, vf_ref, kkf_ref, bf_ref, lwf_ref, rb_ref, kb_ref, vb_ref, kkb_ref, bb_ref,
                      lwb_ref, yf_ref, yb_ref, s_scr, *, group):
    i = pl.program_id(1)
    c = RWKV_CHUNK
    nch = rf_ref.shape[1] // c
    dirs = ((rf_ref, kf_ref, vf_ref, kkf_ref, bf_ref, lwf_ref), (rb_ref, kb_ref, vb_ref, kkb_ref, bb_ref, lwb_ref))
    y_refs = (yf_ref, yb_ref)

    @pl.when(i == 0)
    def _():
        s_scr[...] = jnp.zeros_like(s_scr)

    def step(j, states):
        tiles, reverses, rows = [], [], []
        for q in range(group):
            for d in range(2):
                cj = j * group + q
                cj = cj if d == 0 else nch - 1 - cj
                rw = pl.ds(pl.multiple_of(cj * c, c), c)
                tiles.append(_rwkv_tile_terms(*(ref[0, rw, :] for ref in dirs[d]), d == 1))
                reverses.append(d == 1)
                rows.append(rw)
        r1, y0, mlow, nadd, e_all = _rwkv_chunk_terms(tiles, reverses)
        states = list(states)
        for q in range(group):
            ys = [[], []]
            for d in range(2):
                for h in range(H_G):
                    n = (q * 2 + d) * H_G + h
                    s = states[d * H_G + h]
                    ys[d].append(_mm_nt(r1[n], s) + y0[n])
                    states[d * H_G + h] = s * e_all[n] + _mm(s, mlow[n]) + nadd[n]
            for d in range(2):
                y_refs[d][0, rows[q * 2 + d], :] = jnp.concatenate(ys[d], axis=1)
        return tuple(states)

    init = tuple(s_scr[n] for n in range(2 * H_G))
    if nch == group:
        states = step(0, init)
    else:
        states = lax.fori_loop(0, nch // group, step, init)
    for n in range(2 * H_G):
        s_scr[n] = states[n]


def _rwkv_core(r, v, kk, lwf, lwb, kf, kb, bf, bb):
    b, t, _ = r.shape
    tb = min(RWKV_BLOCK, t)
    nblk = t // tb
    fwd = pl.BlockSpec((1, tb, W_G), lambda bi, i: (bi, i, 0))
    bwd = pl.BlockSpec((1, tb, W_G), lambda bi, i: (bi, nblk - 1 - i, 0))
    out = jax.ShapeDtypeStruct((b, t, W_G), F32)
    return pl.pallas_call(
        functools.partial(_rwkv_core_kernel, group=min(RWKV_GROUP, tb // RWKV_CHUNK)),
        out_shape=(out, out),
        grid=(b, nblk),
        in_specs=[fwd] * 6 + [bwd] * 6,
        out_specs=(fwd, bwd),
        scratch_shapes=[pltpu.VMEM((2 * H_G, HEAD_DIM, HEAD_DIM), F32)],
        compiler_params=_cparams(("parallel", "arbitrary")),
        name="rwkv_core",
    )(r, kf, v, kk, bf, lwf, r, kb, v, kk, bb, lwb)


def _group_norm(y, gain, eps, bd):
    mu = _head_mean(y, bd)
    d = y - mu
    var = _head_mean(d * d, bd)
    return d * lax.rsqrt(var + eps) * gain


N_MIX_REFS = 15


def _mix_residual(refs, tiles):
    x_refs = refs[:len(tiles)]
    (ryf_ref, ryb_ref, rbon_ref, rg_ref, at_ref, mhf_ref, mhb_ref, mo_ref, tof_ref, tob_ref,
     tg_ref, rln_ref, mln_ref, tln_ref, w_ref) = refs[len(tiles):]
    bd = _head_mean_matrix(W_G)
    o_a = (_group_norm(ryf_ref[...] + ryb_ref[...], rln_ref[...], RWKV_GN_EPS, bd) + rbon_ref[...]) * rg_ref[...]
    o_c = _group_norm(mhf_ref[...] + mhb_ref[...], mln_ref[...], HEAD_NORM_EPS, bd) * _sigmoid(mo_ref[...])
    o_d = _group_norm(tof_ref[...] + tob_ref[...], tln_ref[...], HEAD_NORM_EPS, bd) * _silu(tg_ref[...])
    mix = jnp.concatenate([o_a, at_ref[...], o_c, o_d], axis=1).astype(BF16)
    return _part_tile(pl.program_id(0), x_refs, tiles) + jnp.dot(mix, w_ref[...], preferred_element_type=F32)


def _mix_call(kernel, x_parts, rw, at, ml, rt, u2, rln, mln, tln, w_out, tm, extra, extra_specs, out_shape, out_specs,
              scratch_shapes, name):
    tiles = _part_tiles(x_parts, tm)
    n = sum(tiles) * tm
    row = lambda w: pl.BlockSpec((tm, w), lambda i: (i, 0))
    ucol = lambda off: pl.BlockSpec((tm, W_G), lambda i: (i, off // W_G))
    const = lambda shape: pl.BlockSpec(shape, lambda i: (0, 0))
    return pl.pallas_call(
        functools.partial(kernel, tiles=tiles),
        out_shape=out_shape,
        grid=(n // tm,),
        in_specs=_part_specs(x_parts, tm, D_MODEL) + [row(W_G)] * 7 + [ucol(U_MLSTM + 3 * W_G)] + [row(W_G)] * 2
                 + [ucol(U_RET + 3 * W_G)] + [const((1, W_G))] * 3
                 + [pl.BlockSpec((D_MODEL, D_MODEL), lambda i: (0, 0), pipeline_mode=pl.Buffered(1))] + extra_specs,
        out_specs=out_specs,
        scratch_shapes=scratch_shapes,
        compiler_params=_cparams(("arbitrary",)),
        name=name,
    )(*x_parts, *rw, at, *ml, u2, *rt, u2, rln, mln, tln, w_out, *extra)


def _mix_ffn_kernel(*refs, tiles):
    n_in = len(tiles) + N_MIX_REFS
    g_ref, wg_ref, wu_ref, wd_ref, o_ref = refs[n_in:]
    x = _mix_residual(refs[:n_in], tiles)
    h = _rms_norm_rows(x, g_ref[...]).astype(BF16)
    a = jnp.dot(h, wg_ref[...], preferred_element_type=F32)
    b = jnp.dot(h, wu_ref[...], preferred_element_type=F32)
    z = (_silu(a) * b).astype(BF16)
    o_ref[...] = x + jnp.dot(z, wd_ref[...], preferred_element_type=F32)


def _mix_ffn(mix_args, gain, wg, wu, wd, tm):
    n = sum(_part_tiles(mix_args[0], tm)) * tm
    res = lambda shape: pl.BlockSpec(shape, lambda i: (0, 0), pipeline_mode=pl.Buffered(1))
    return _mix_call(_mix_ffn_kernel, *mix_args, tm, (gain, wg, wu, wd),
                     [pl.BlockSpec((1, D_MODEL), lambda i: (0, 0)), res((D_MODEL, D_FF)), res((D_MODEL, D_FF)),
                      res((D_FF, D_MODEL))],
                     jax.ShapeDtypeStruct((n, D_MODEL), F32), pl.BlockSpec((tm, D_MODEL), lambda i: (i, 0)), [],
                     "mix_ffn")


def _mix_router_kernel(*refs, tiles):
    n_in = len(tiles) + N_MIX_REFS
    g_ref, wr_ref, x_ref, h_ref, gate_ref, idx_ref, cnt_ref, cnt_scr = refs[n_in:]

    @pl.when(pl.program_id(0) == 0)
    def _():
        cnt_scr[...] = jnp.zeros_like(cnt_scr)

    x = _mix_residual(refs[:n_in], tiles)
    x_ref[...] = x
    h = _rms_norm_rows(x, g_ref[...])
    h_ref[...] = h.astype(BF16)
    logits = _mm_x3(h, wr_ref[...])
    lane = _iota2(logits.shape, 1)
    logits = jnp.where(lane < N_EXPERTS, logits, NEG_INF)
    e = jnp.exp(logits - jnp.max(logits, axis=-1, keepdims=True))
    p = e / jnp.sum(e, axis=-1, keepdims=True)
    p = jnp.where(lane < N_EXPERTS, p, -1.0)
    m1 = jnp.max(p, axis=-1, keepdims=True)
    i1 = jnp.min(jnp.where(p == m1, lane, LANES), axis=-1, keepdims=True)
    p2 = jnp.where(lane == i1, -1.0, p)
    m2 = jnp.max(p2, axis=-1, keepdims=True)
    i2 = jnp.min(jnp.where(p2 == m2, lane, LANES), axis=-1, keepdims=True)
    tot = m1 + m2
    gate_ref[...] = jnp.where(lane == 0, m1 / tot, jnp.where(lane == 1, m2 / tot, 0.0))
    tm = logits.shape[0]
    chosen = jnp.where((lane == i1) | (lane == i2), 1.0, 0.0)
    tril = jnp.where(_iota2((tm, tm), 1) <= _iota2((tm, tm), 0), 1.0, 0.0).astype(BF16)
    incl = jnp.dot(tril, chosen.astype(BF16), preferred_element_type=F32)
    rank = cnt_scr[...] + incl - chosen
    r1 = jnp.sum(jnp.where(lane == i1, rank, 0.0), axis=-1, keepdims=True)
    r2 = jnp.sum(jnp.where(lane == i2, rank, 0.0), axis=-1, keepdims=True)
    info = jnp.where(lane == 0, i1.astype(F32), jnp.where(lane == 1, i2.astype(F32),
                                                          jnp.where(lane == 2, r1, jnp.where(lane == 3, r2, 0.0))))
    idx_ref[...] = info.T[:ROW_ALIGN, :].astype(jnp.int32)
    cnt_scr[...] = cnt_scr[...] + incl[tm - 1:tm, :]
    cnt_ref[...] = jnp.broadcast_to(cnt_scr[...], cnt_ref.shape).astype(jnp.int32)


def _mix_router(mix_args, gain, wr_pad, tm):
    n = sum(_part_tiles(mix_args[0], tm)) * tm
    return _mix_call(
        _mix_router_kernel, *mix_args, tm, (gain, wr_pad),
        [pl.BlockSpec((1, D_MODEL), lambda i: (0, 0)), pl.BlockSpec((D_MODEL, LANES), lambda i: (0, 0))],
        (jax.ShapeDtypeStruct((n, D_MODEL), F32), jax.ShapeDtypeStruct((n, D_MODEL), BF16),
         jax.ShapeDtypeStruct((n, LANES), F32), jax.ShapeDtypeStruct((ROW_ALIGN, n), jnp.int32),
         jax.ShapeDtypeStruct((ROW_ALIGN, LANES), jnp.int32)),
        (pl.BlockSpec((tm, D_MODEL), lambda i: (i, 0)), pl.BlockSpec((tm, D_MODEL), lambda i: (i, 0)),
         pl.BlockSpec((tm, LANES), lambda i: (i, 0)), pl.BlockSpec((ROW_ALIGN, tm), lambda i: (0, i)),
         pl.BlockSpec((ROW_ALIGN, LANES), lambda i: (0, 0))),
        [pltpu.VMEM((1, LANES), F32)], "mix_router")


def _expert_ffn_kernel(te_ref, nv_ref, xs_ref, wg_ref, wu_ref, wd_ref, o_ref):
    i = pl.program_id(0)

    @pl.when(i < nv_ref[0])
    def _():
        h = xs_ref[...]
        fw = D_FF // EXPERT_FF_SPLIT
        acc = None
        for f in range(EXPERT_FF_SPLIT):
            cols = slice(f * fw, (f + 1) * fw)
            a = jnp.dot(h, wg_ref[0, :, cols], preferred_element_type=F32)
            b = jnp.dot(h, wu_ref[0, :, cols], preferred_element_type=F32)
            z = (_silu(a) * b).astype(BF16)
            y = jnp.dot(z, wd_ref[0, cols, :], preferred_element_type=F32)
            acc = y if acc is None else acc + y
        o_ref[...] = acc.astype(o_ref.dtype)

    @pl.when(i >= nv_ref[0])
    def _():
        o_ref[...] = jnp.zeros_like(o_ref)


def _expert_ffn(xs, tile_expert, n_valid, wg, wu, wd, tm):
    rows = xs.shape[0]
    wspec = lambda shape: pl.BlockSpec((1,) + shape, lambda i, te, nv: (te[i], 0, 0))
    return pl.pallas_call(
        _expert_ffn_kernel,
        out_shape=jax.ShapeDtypeStruct((rows, D_MODEL), BF16),
        grid_spec=pltpu.PrefetchScalarGridSpec(
            num_scalar_prefetch=2,
            grid=(rows // tm,),
            in_specs=[pl.BlockSpec((tm, D_MODEL), lambda i, te, nv: (i, 0)),
                      wspec((D_MODEL, D_FF)), wspec((D_MODEL, D_FF)), wspec((D_FF, D_MODEL))],
            out_specs=pl.BlockSpec((tm, D_MODEL), lambda i, te, nv: (i, 0)),
        ),
        compiler_params=_cparams(("arbitrary",)),
        name="expert_ffn",
    )(tile_expert, n_valid, xs, wg, wu, wd)


def _moe_combine_kernel(x_ref, y1_ref, y2_ref, gate_ref, nf_ref, *o_refs, tiles):
    i = pl.program_id(0)
    g = gate_ref[...]
    y = x_ref[...] + g[:, 0:1] * y1_ref[...].astype(F32) + g[:, 1:2] * y2_ref[...].astype(F32)
    out = _rms_norm_rows(y, nf_ref[...])
    start = 0
    for o_ref, nt in zip(o_refs, tiles):
        @pl.when((i >= start) & (i < start + nt))
        def _(o_ref=o_ref):
            o_ref[...] = out
        start += nt


def _moe_combine(x2, y1, y2, gates, norm_final, tm, part_rows):
    n = x2.shape[0]
    row = lambda w: pl.BlockSpec((tm, w), lambda i: (i, 0))
    outs = tuple(jax.ShapeDtypeStruct((r, D_MODEL), F32) for r in part_rows)
    return pl.pallas_call(
        functools.partial(_moe_combine_kernel, tiles=_part_tiles(outs, tm)),
        out_shape=outs,
        grid=(n // tm,),
        in_specs=[row(D_MODEL), row(D_MODEL), row(D_MODEL), row(LANES), pl.BlockSpec((1, D_MODEL), lambda i: (0, 0))],
        out_specs=tuple(_part_specs(outs, tm, D_MODEL)),
        compiler_params=_cparams(("arbitrary",)),
        name="moe_combine",
    )(x2, y1, y2, gates, norm_final)


def _moe(x2, h, gates, idx, counts, wg, wu, wd, norm_final, tm, part_rows):
    n = x2.shape[0]
    tme = EXPERT_TILE
    n_tiles = (2 * n + N_EXPERTS * (tme - 1)) // tme + 1
    e1, e2, r1, r2 = idx[0], idx[1], idx[2], idx[3]
    cnt = counts[0, :N_EXPERTS]
    padded = (cnt + tme - 1) // tme * tme
    group_end = jnp.cumsum(padded)
    group_off = group_end - padded
    dense_off = jnp.cumsum(cnt) - cnt
    lookup = lambda table, e: sum(jnp.where(e == k, table[k], 0) for k in range(N_EXPERTS))
    slot1 = lookup(group_off, e1) + r1
    slot2 = lookup(group_off, e2) + r2
    tok = jnp.arange(n, dtype=jnp.int32)
    sorted_tok = jnp.sort(jnp.concatenate([e1 * n + tok, e2 * n + tok])) % n
    tile_start = jnp.arange(n_tiles, dtype=jnp.int32) * tme
    tile_expert = jnp.minimum(jnp.searchsorted(group_end, tile_start, side='right'), N_EXPERTS - 1).astype(jnp.int32)
    n_valid = (group_end[-1:] // tme).astype(jnp.int32)
    rank = (tile_start - group_off[tile_expert])[:, None] + jnp.arange(tme, dtype=jnp.int32)[None, :]
    dense = jnp.clip(dense_off[tile_expert][:, None] + rank, 0, 2 * n - 1)
    take = lambda table, rows: table.at[rows].get(mode='promise_in_bounds')
    src = jnp.where(rank < cnt[tile_expert][:, None], take(sorted_tok, dense.reshape(-1)).reshape(dense.shape), 0)
    xs = take(h, src.reshape(-1))
    ys = _expert_ffn(xs, tile_expert, n_valid, wg, wu, wd, tme)
    y1 = take(ys, slot1)
    y2 = take(ys, slot2)
    return _moe_combine(x2, y1, y2, gates, norm_final, tm, part_rows)


def _rope_tables(t):
    rows = t // GRID_W
    pos = np.arange(rows * GRID_W)
    row = (pos // GRID_W).astype(np.float32)
    col = (pos % GRID_W).astype(np.float32)
    nf = HEAD_DIM // 4
    inv = jnp.asarray(ROPE_THETA, F32) ** (-jnp.arange(nf, dtype=F32) / nf)
    ar = jnp.asarray(row)[:, None] * inv
    ac = jnp.asarray(col)[:, None] * inv
    cos = jnp.concatenate([jnp.cos(ar), jnp.cos(ar), jnp.cos(ac), jnp.cos(ac)], axis=-1)
    sin = jnp.concatenate([-jnp.sin(ar), jnp.sin(ar), -jnp.sin(ac), jnp.sin(ac)], axis=-1)
    return jnp.tile(cos, (1, H_G)), jnp.tile(sin, (1, H_G))


def _pad_w_in(w):
    a, b_, c, d = 1024, 512, 1040, 1024
    w_a, w_b, w_c, w_d = w[:, :a], w[:, a:a + b_], w[:, a + b_:a + b_ + c], w[:, a + b_ + c:]
    gates = jnp.pad(w_c[:, 1024:], ((0, 0), (0, LANES - 16)))
    return jnp.concatenate([w_a, w_c[:, :1024], w_d, w_b, gates], axis=1).astype(BF16)


def _row(v):
    return v.reshape(1, -1).astype(F32)


def _trunk(xs, p):
    t = xs[0].shape[1]
    part_rows = tuple(x.shape[0] * t for x in xs)
    b = sum(x.shape[0] for x in xs)
    n = b * t
    tm = 256
    cos, sin = _rope_tables(t)
    x_parts = tuple(x.reshape(-1, D_MODEL) for x in xs)
    depth = p['w_in'].shape[0]
    for l in range(depth):
        tm_in = 2 * tm if all(xp.shape[0] % (2 * tm) == 0 for xp in x_parts) else tm
        u2 = _in_proj(x_parts, _row(p['norm_mix'][l]), _pad_w_in(p['w_in'][l]), tm_in)
        u3 = u2.reshape(b, t, U_COLS)
        (r, v, kk, g, bonus, lwf, lwb, kf, kb, bf, bb) = _rwkv_prep(
            u3, _row(p['rwkv_mu'][l]), p['rwkv_w0'][l], p['rwkv_w2'][l], p['rwkv_a0'][l], p['rwkv_a2'][l],
            p['rwkv_g2'][l], _row(p['rwkv_kk'][l]), _row(p['rwkv_ka'][l]), _row(p['rwkv_rk'][l]))
        yf, yb = _rwkv_core(r, v, kk, lwf, lwb, kf, kb, bf, bb)
        at = _attention(u3, cos, sin, _row(jnp.tile(p['attn_q_norm'][l], H_G)),
                        _row(jnp.tile(p['attn_k_norm'][l], KV_ATTN)))
        gate_bias = jnp.pad(jnp.concatenate([p['mlstm_i_bias'][l].reshape(-1), p['mlstm_f_bias'][l].reshape(-1)]),
                            (0, LANES - 4 * H_G))
        hf, hb = _mlstm(u3, p['mlstm_conv_w'][l], _row(p['mlstm_conv_b'][l]), _row(gate_bias))
        of, ob = _retention(u3, cos, sin)
        flat = lambda z: z.reshape(n, W_G)
        mix_args = (x_parts, tuple(map(flat, (yf, yb, bonus, g))), flat(at), tuple(map(flat, (hf, hb))),
                    tuple(map(flat, (of, ob))), u2, _row(p['rwkv_ln'][l]), _row(p['mlstm_ln'][l]),
                    _row(p['ret_ln'][l]), p['w_out'][l].astype(BF16))
        j = l // 2
        if l % 2 == 0:
            x2 = _mix_ffn(mix_args, _row(p['norm_ffn'][l]), p['ffn_w_gate'][j].astype(BF16),
                          p['ffn_w_up'][j].astype(BF16), p['ffn_w_down'][j].astype(BF16), tm)
            x_parts = (x2,)
            if l == depth - 1:
                raise NotImplementedError("final norm after a dense FFN layer")
        else:
            wr = jnp.pad(p['moe_router'][j], ((0, 0), (0, LANES - N_EXPERTS)))
            x2, h, gates, idx, counts = _mix_router(mix_args, _row(p['norm_ffn'][l]), wr, tm)
            if l != depth - 1:
                raise NotImplementedError("expert layer that is not the last layer")
            outs = _moe(x2, h, gates, idx, counts, p['moe_w_gate'][j].astype(BF16), p['moe_w_up'][j].astype(BF16),
                        p['moe_w_down'][j].astype(BF16), _row(p['norm_final']), tm, part_rows)
    return tuple(o.reshape(x.shape) for o, x in zip(outs, xs))


def kernel(x_prompt, x_sample, norm_mix, norm_ffn, norm_final, w_in, w_out, rwkv_mu, rwkv_w0, rwkv_w2,
           rwkv_a0, rwkv_a2, rwkv_g2, rwkv_kk, rwkv_ka, rwkv_rk, rwkv_ln, attn_q_norm, attn_k_norm,
           mlstm_conv_w, mlstm_conv_b, mlstm_i_bias, mlstm_f_bias, mlstm_ln, ret_ln, ffn_w_gate, ffn_w_up,
           ffn_w_down, moe_router, moe_w_gate, moe_w_up, moe_w_down):
    p = dict(norm_mix=norm_mix, norm_ffn=norm_ffn, norm_final=norm_final, w_in=w_in, w_out=w_out,
             rwkv_mu=rwkv_mu, rwkv_w0=rwkv_w0, rwkv_w2=rwkv_w2, rwkv_a0=rwkv_a0, rwkv_a2=rwkv_a2,
             rwkv_g2=rwkv_g2, rwkv_kk=rwkv_kk, rwkv_ka=rwkv_ka, rwkv_rk=rwkv_rk, rwkv_ln=rwkv_ln,
             attn_q_norm=attn_q_norm, attn_k_norm=attn_k_norm, mlstm_conv_w=mlstm_conv_w,
             mlstm_conv_b=mlstm_conv_b, mlstm_i_bias=mlstm_i_bias, mlstm_f_bias=mlstm_f_bias,
             mlstm_ln=mlstm_ln, ret_ln=ret_ln, ffn_w_gate=ffn_w_gate, ffn_w_up=ffn_w_up,
             ffn_w_down=ffn_w_down, moe_router=moe_router, moe_w_gate=moe_w_gate, moe_w_up=moe_w_up,
             moe_w_down=moe_w_down)
    return _trunk((x_prompt, x_sample), p)
```

```python
import functools
import math

import numpy as np
import jax
import jax.numpy as jnp
from jax import lax
from jax.experimental import pallas as pl
from jax.experimental.pallas import tpu as pltpu

F32 = jnp.float32
BF16 = jnp.bfloat16
MIXER_DTYPE = BF16

D_MODEL = 1024
HEAD_DIM = 64
W_G = 256
H_G = 4
KV_ATTN = 2
D_FF = 2816
N_EXPERTS = 8
NORM_EPS = 1e-6
HEAD_NORM_EPS = 1e-5
RWKV_GN_EPS = 64e-5
NEG_INF = -1e30
ROPE_THETA = 10000.0
GRID_W = 64

LANES = 128
ROW_ALIGN = 8
VMEM_LIMIT_BYTES = 56 * 1024 * 1024

U_RWKV = 0
U_MLSTM = 1024
U_RET = 2048
U_ATTN = 3072
U_GATE = 3584
U_COLS = 3712

RWKV_CHUNK = 64
RWKV_BLOCK = 256
RWKV_GROUP = 4
MIX_CHUNK = 128
MIX_BLOCK = 256
EXPERT_TILE = 256
EXPERT_FF_SPLIT = 1


def _cparams(sem):
    return pltpu.CompilerParams(dimension_semantics=sem, vmem_limit_bytes=VMEM_LIMIT_BYTES)


def _bdot(a, b, dims):
    return lax.dot_general(a, b, (dims, ((), ())), preferred_element_type=F32)


def _mm(a, b):
    return _bdot(a.astype(BF16), b.astype(BF16), ((1,), (0,)))


def _mm_nt(a, b):
    return _bdot(a.astype(BF16), b.astype(BF16), ((1,), (1,)))


def _mm_tn(a, b):
    return _bdot(a.astype(BF16), b.astype(BF16), ((0,), (0,)))


def _split2(a):
    hi = a.astype(BF16)
    lo = (a - hi.astype(F32)).astype(BF16)
    return hi, lo


def _split3(a):
    hi = a.astype(BF16)
    r = a - hi.astype(F32)
    mid = r.astype(BF16)
    lo = (r - mid.astype(F32)).astype(BF16)
    return hi, mid, lo


def _mm_l2(a, b_exact):
    hi, lo = _split2(a)
    return _bdot(hi, b_exact, ((1,), (0,))) + _bdot(lo, b_exact, ((1,), (0,)))


def _mm_l3(a, b_exact):
    h, m, l = _split3(a)
    return _bdot(h, b_exact, ((1,), (0,))) + _bdot(m, b_exact, ((1,), (0,))) + _bdot(l, b_exact, ((1,), (0,)))


def _mm_r3(a_exact, b):
    h, m, l = _split3(b)
    return _bdot(a_exact, h, ((1,), (0,))) + _bdot(a_exact, m, ((1,), (0,))) + _bdot(a_exact, l, ((1,), (0,)))


def _mm_x3(a, b):
    ah, al = _split2(a)
    bh, bl = _split2(b)
    d = ((1,), (0,))
    return _bdot(ah, bh, d) + _bdot(ah, bl, d) + _bdot(al, bh, d)


def _iota2(shape, axis):
    return lax.broadcasted_iota(jnp.int32, shape, axis)


def _head_mean_matrix(width):
    r = _iota2((width, width), 0) // HEAD_DIM
    c = _iota2((width, width), 1) // HEAD_DIM
    return jnp.where(r == c, 1.0 / HEAD_DIM, 0.0).astype(BF16)


def _head_mean(z, bd):
    return _mm_l2(z, bd)


def _sigmoid(x):
    return 1.0 / (1.0 + jnp.exp(-x))


def _silu(x):
    return x * _sigmoid(x)


def _log_sigmoid(x):
    return jnp.minimum(x, 0.0) - jnp.log(1.0 + jnp.exp(-jnp.abs(x)))


def _rms_norm_rows(x, gain):
    ms = jnp.mean(x * x, axis=-1, keepdims=True)
    return x * lax.rsqrt(ms + NORM_EPS) * gain


def _rope_swap(z):
    w = z.shape[-1]
    lane = _iota2(z.shape, z.ndim - 1)
    fwd = pltpu.roll(z, w - 16, z.ndim - 1)
    bwd = pltpu.roll(z, 16, z.ndim - 1)
    return jnp.where((lane % 32) < 16, fwd, bwd)


def _rope(z, cos, sin):
    return z * cos + _rope_swap(z) * sin


def _shift_rows(x, prev_row, next_row):
    n = x.shape[0]
    row = _iota2(x.shape, 0)
    prev = jnp.where(row == 0, prev_row, pltpu.roll(x, 1, 0))
    nxt = jnp.where(row == n - 1, next_row, pltpu.roll(x, n - 1, 0))
    return prev, nxt


def _part_tiles(parts, tm):
    return tuple(p.shape[0] // tm for p in parts)


def _part_specs(parts, tm, width):
    specs, start = [], 0
    for nt in _part_tiles(parts, tm):
        specs.append(pl.BlockSpec((tm, width), lambda i, s=start, nt=nt: (jnp.clip(i - s, 0, nt - 1), 0)))
        start += nt
    return specs


def _part_tile(i, refs, tiles):
    x = refs[0][...]
    start = tiles[0]
    for ref, nt in zip(refs[1:], tiles[1:]):
        x = jnp.where(i >= start, ref[...], x)
        start += nt
    return x


def _inproj_kernel(*refs, tiles):
    x_refs, (g_ref, w_ref, o_ref) = refs[:len(tiles)], refs[len(tiles):]
    h = _rms_norm_rows(_part_tile(pl.program_id(0), x_refs, tiles), g_ref[...])
    o_ref[...] = jnp.dot(h.astype(BF16), w_ref[...], preferred_element_type=F32)


def _in_proj(x_parts, gain, w_pad, tm):
    tiles = _part_tiles(x_parts, tm)
    n = sum(tiles) * tm
    return pl.pallas_call(
        functools.partial(_inproj_kernel, tiles=tiles),
        out_shape=jax.ShapeDtypeStruct((n, U_COLS), F32),
        grid=(n // tm,),
        in_specs=_part_specs(x_parts, tm, D_MODEL) + [
            pl.BlockSpec((1, D_MODEL), lambda i: (0, 0)),
            pl.BlockSpec((D_MODEL, U_COLS), lambda i: (0, 0), pipeline_mode=pl.Buffered(1)),
        ],
        out_specs=pl.BlockSpec((tm, U_COLS), lambda i: (i, 0)),
        compiler_params=_cparams(("arbitrary",)),
        name="in_proj",
    )(*x_parts, gain, w_pad)


def _attn_kernel(u_ref, cos_ref, sin_ref, qg_ref, kg_ref, o_ref, q_scr, k_scr, v_scr, *, tq):
    t = u_ref.shape[1]
    u = u_ref[0]
    q = u[:, :W_G]
    k = u[:, W_G:W_G + KV_ATTN * HEAD_DIM]
    v = u[:, W_G + KV_ATTN * HEAD_DIM:]
    cos = cos_ref[...]
    sin = sin_ref[...]
    bd_q = _head_mean_matrix(W_G)
    bd_k = _head_mean_matrix(KV_ATTN * HEAD_DIM)
    qn = q * lax.rsqrt(_head_mean(q * q, bd_q) + NORM_EPS) * qg_ref[...]
    kn = k * lax.rsqrt(_head_mean(k * k, bd_k) + NORM_EPS) * kg_ref[...]
    kw = KV_ATTN * HEAD_DIM
    q_scr[...] = (_rope(qn, cos, sin) * (HEAD_DIM ** -0.5 * math.log2(math.e))).astype(BF16)
    k_scr[...] = _rope(kn, cos[:, :kw], sin[:, :kw]).astype(BF16)
    ones = jnp.ones((t, HEAD_DIM), BF16)
    for j in range(KV_ATTN):
        vj = v[:, j * HEAD_DIM:(j + 1) * HEAD_DIM].astype(BF16)
        v_scr[:, j * LANES:(j + 1) * LANES] = jnp.concatenate([vj, ones], axis=1)
    group = H_G // KV_ATTN

    def q_tile(i, carry):
        rows = pl.ds(pl.multiple_of(i * tq, tq), tq)

        def scores(h):
            j = h // group
            qh = q_scr[rows, h * HEAD_DIM:(h + 1) * HEAD_DIM]
            return _bdot(qh, k_scr[:, j * HEAD_DIM:(j + 1) * HEAD_DIM], ((1,), (1,)))

        s_next = scores(0)
        for h in range(H_G):
            s = s_next
            if h + 1 < H_G:
                s_next = scores(h + 1)
            m = jnp.max(s, axis=-1, keepdims=True)
            p = jnp.exp2(s - m)
            vj = v_scr[:, (h // group) * LANES:(h // group + 1) * LANES]
            r = jnp.dot(p.astype(BF16), vj, preferred_element_type=F32)
            o_ref[0, rows, h * HEAD_DIM:(h + 1) * HEAD_DIM] = r[:, :HEAD_DIM] / r[:, HEAD_DIM:]
        return carry

    lax.fori_loop(0, t // tq, q_tile, 0)


def _attention(u3, cos, sin, q_gain, k_gain):
    b, t, _ = u3.shape
    tq = min(256, t)
    col = U_ATTN // 512
    return pl.pallas_call(
        functools.partial(_attn_kernel, tq=tq),
        out_shape=jax.ShapeDtypeStruct((b, t, W_G), F32),
        grid=(b,),
        in_specs=[
            pl.BlockSpec((1, t, 512), lambda i: (i, 0, col)),
            pl.BlockSpec((t, W_G), lambda i: (0, 0)),
            pl.BlockSpec((t, W_G), lambda i: (0, 0)),
            pl.BlockSpec((1, W_G), lambda i: (0, 0)),
            pl.BlockSpec((1, KV_ATTN * HEAD_DIM), lambda i: (0, 0)),
        ],
        out_specs=pl.BlockSpec((1, t, W_G), lambda i: (i, 0, 0)),
        scratch_shapes=[
            pltpu.VMEM((t, W_G), BF16),
            pltpu.VMEM((t, KV_ATTN * HEAD_DIM), BF16),
            pltpu.VMEM((t, KV_ATTN * LANES), BF16),
        ],
        compiler_params=_cparams(("parallel",)),
        name="attention",
    )(u3, cos, sin, q_gain, k_gain)


def _ret_log_gamma(direction):
    return [math.log1p(-2.0 ** (-5.0 - (2 * h + direction) / 2.0)) for h in range(H_G)]


def _ret_kernel(uf_ref, ub_ref, cf_ref, sf_ref, cb_ref, sb_ref, of_ref, ob_ref, rf_scr, rb_scr):
    i = pl.program_id(1)
    c = MIX_CHUNK

    @pl.when(i == 0)
    def _():
        rf_scr[...] = jnp.zeros_like(rf_scr)
        rb_scr[...] = jnp.zeros_like(rb_scr)

    tt = _iota2((c, c), 0)
    ss = _iota2((c, c), 1)
    diff = (tt - ss).astype(F32)
    jcol = tt.astype(F32)
    first = ss < HEAD_DIM
    block_diag = (tt < HEAD_DIM) == first
    lg_f = _ret_log_gamma(0)
    lg_b = _ret_log_gamma(1)
    psl = lambda p: slice(p * LANES, (p + 1) * LANES)
    lane_lg = lambda lg, p: jnp.where(first, lg[2 * p], lg[2 * p + 1])
    row_decay = lambda lg, p: jnp.where(tt < HEAD_DIM, math.exp(c * lg[2 * p]), math.exp(c * lg[2 * p + 1]))
    own = lambda h, x: jnp.where(first, x, 0.0) if h % 2 == 0 else jnp.where(first, 0.0, x)
    pairs = range(H_G // 2)

    nch = uf_ref.shape[1] // c
    uf = uf_ref[0]
    qf_all = _rope(uf[:, :W_G], cf_ref[...], sf_ref[...])
    kf_all = _rope(uf[:, W_G:2 * W_G], cf_ref[...], sf_ref[...]) * HEAD_DIM ** -0.5
    ub = ub_ref[0]
    qb_all = _rope(ub[:, :W_G], cb_ref[...], sb_ref[...])
    kb_all = _rope(ub[:, W_G:2 * W_G], cb_ref[...], sb_ref[...]) * HEAD_DIM ** -0.5
    rf_prev = [rf_scr[p] for p in pairs]
    rb_prev = [rb_scr[p] for p in pairs]
    decay = [jnp.where(tt >= ss, jnp.exp(diff * lg_f[h]), 0.0) + jnp.where(ss >= tt, jnp.exp(-diff * lg_b[h]), 0.0)
             for h in range(H_G)]
    for n in range(nch):
        rf_rows = slice(n * c, (n + 1) * c)
        rb_rows = slice((nch - 1 - n) * c, (nch - n) * c)
        qf, kf, vf = qf_all[rf_rows], kf_all[rf_rows], uf[rf_rows, 2 * W_G:3 * W_G]
        qb, kb, vb = qb_all[rb_rows], kb_all[rb_rows], ub[rb_rows, 2 * W_G:3 * W_G]
        qk = [_mm_nt(qf[:, psl(h // 2)], own(h, kf[:, psl(h // 2)])) for h in range(H_G)]
        intra = [_mm(qk[h] * decay[h], own(h, vf[:, psl(h // 2)])) for h in range(H_G)]
        inter_f = [_mm(qf[:, psl(p)] * jnp.exp((jcol + 1.0) * lane_lg(lg_f, p)), rf_prev[p]) for p in pairs]
        inter_b = [_mm(qb[:, psl(p)] * jnp.exp((c - jcol) * lane_lg(lg_b, p)), rb_prev[p]) for p in pairs]
        upd_f = [_mm_tn(kf[:, psl(p)] * jnp.exp((c - 1.0 - jcol) * lane_lg(lg_f, p)), vf[:, psl(p)]) for p in pairs]
        upd_b = [_mm_tn(kb[:, psl(p)] * jnp.exp(jcol * lane_lg(lg_b, p)), vb[:, psl(p)]) for p in pairs]
        for p in pairs:
            of_ref[0, rf_rows, psl(p)] = (intra[2 * p] + intra[2 * p + 1] + inter_f[p]).astype(of_ref.dtype)
            ob_ref[0, rb_rows, psl(p)] = inter_b[p].astype(ob_ref.dtype)
        rf_prev = [row_decay(lg_f, p) * rf_prev[p] + jnp.where(block_diag, upd_f[p], 0.0) for p in pairs]
        rb_prev = [row_decay(lg_b, p) * rb_prev[p] + jnp.where(block_diag, upd_b[p], 0.0) for p in pairs]
    for p in pairs:
        rf_scr[p] = rf_prev[p]
        rb_scr[p] = rb_prev[p]


def _retention(u3, cos, sin):
    b, t, _ = u3.shape
    c = min(MIX_BLOCK, t)
    nblk = t // c
    col = U_RET // 1024
    tab = lambda rev: pl.BlockSpec((c, W_G), (lambda bi, i: (nblk - 1 - i, 0)) if rev else (lambda bi, i: (i, 0)))
    return pl.pallas_call(
        _ret_kernel,
        out_shape=(jax.ShapeDtypeStruct((b, t, W_G), MIXER_DTYPE), jax.ShapeDtypeStruct((b, t, W_G), MIXER_DTYPE)),
        grid=(b, nblk),
        in_specs=[
            pl.BlockSpec((1, c, 1024), lambda bi, i: (bi, i, col)),
            pl.BlockSpec((1, c, 1024), lambda bi, i: (bi, nblk - 1 - i, col)),
            tab(False), tab(False), tab(True), tab(True),
        ],
        out_specs=(
            pl.BlockSpec((1, c, W_G), lambda bi, i: (bi, i, 0)),
            pl.BlockSpec((1, c, W_G), lambda bi, i: (bi, nblk - 1 - i, 0)),
        ),
        scratch_shapes=[pltpu.VMEM((H_G // 2, LANES, LANES), F32), pltpu.VMEM((H_G // 2, LANES, LANES), F32)],
        compiler_params=_cparams(("parallel", "arbitrary")),
        name="retention",
    )(u3, u3, cos, sin, cos, sin)


def _mlstm_tile(u_ref, up_ref, un_ref, g_ref, cw_ref, cb_ref, gb_ref, blk, nblk, direction):
    c = MIX_CHUNK
    reverse = direction == 1
    u = u_ref[0]
    qk = u[:, :2 * W_G]
    prev_row = jnp.where(blk == 0, 0.0, up_ref[0][ROW_ALIGN - 1:ROW_ALIGN, :])
    next_row = jnp.where(blk == nblk - 1, 0.0, un_ref[0][0:1, :])
    prev, nxt = _shift_rows(qk, prev_row, next_row)
    cw = cw_ref[...]
    qk = _silu(cw[0:1] * prev + cw[1:2] * qk + cw[2:3] * nxt + cb_ref[...])
    qa = qk[:, :W_G]
    ka = qk[:, W_G:] * HEAD_DIM ** -0.5
    va = u[:, 2 * W_G:3 * W_G]

    x = g_ref[0] + gb_ref[...]
    xt = x.T
    lf_c = _log_sigmoid(x)
    lf_r = _log_sigmoid(xt)
    tt = _iota2((c, c), 0)
    ss = _iota2((c, c), 1)
    lower = jnp.where(ss <= tt, 1.0, 0.0).astype(BF16)
    upper = jnp.where(tt <= ss, 1.0, 0.0).astype(BF16)
    if reverse:
        b_c = _mm_r3(upper, lf_c)
        b_r = _mm_l3(lf_r, lower)
        mask = ss >= tt
    else:
        b_c = _mm_r3(lower, lf_c)
        b_r = _mm_l3(lf_r, upper)
        mask = ss <= tt
    g_all = jnp.sum(lf_c, axis=0, keepdims=True)
    return dict(q=qa, k=ka, v=va, x=x, xt=xt, b_c=b_c, b_r=b_r, g_all=g_all, mask=mask)


def _mlstm_select_matrix():
    sel = np.zeros((2, 2 * LANES, 8 * LANES), np.float32)
    for d in range(2):
        for h in range(H_G):
            ci, cf = d * H_G + h, 2 * H_G + d * H_G + h
            p, j = divmod(h, 2)
            sel[d, cf, LANES * h:LANES * (h + 1)] = 1.0
            sel[d, cf, 4 * LANES + LANES * p + HEAD_DIM * j:4 * LANES + LANES * p + HEAD_DIM * (j + 1)] = 1.0
            sel[d, LANES + ci, 6 * LANES + LANES * p + HEAD_DIM * j:6 * LANES + LANES * p + HEAD_DIM * (j + 1)] = 1.0
    return jnp.asarray(sel, BF16)


def _mlstm_kernel(uf_ref, upf_ref, unf_ref, gf_ref, ub_ref, upb_ref, unb_ref, gbk_ref,
                  cw_ref, cb_ref, gb_ref, sel_ref, of_ref, ob_ref, st_scr, m_scr):
    i = pl.program_id(1)
    nblk = pl.num_programs(1)
    c = MIX_CHUNK

    @pl.when(i == 0)
    def _():
        st_scr[...] = jnp.zeros_like(st_scr)
        m_scr[...] = jnp.zeros_like(m_scr)

    tiles = (_mlstm_tile(uf_ref, upf_ref, unf_ref, gf_ref, cw_ref, cb_ref, gb_ref, i, nblk, 0),
             _mlstm_tile(ub_ref, upb_ref, unb_ref, gbk_ref, cw_ref, cb_ref, gb_ref, nblk - 1 - i, nblk, 1))
    o_refs = (of_ref, ob_ref)
    sel = [_mm_l2(jnp.concatenate([tiles[d]["b_c"], tiles[d]["x"]], axis=1), sel_ref[d]) for d in range(2)]
    first = _iota2((c, LANES), 1) < HEAD_DIM
    row_first = _iota2((LANES, 2 * LANES), 0) < HEAD_DIM
    lane2 = _iota2((LANES, 2 * LANES), 1) % LANES < HEAD_DIM
    block_diag = row_first == lane2
    ones = jnp.ones((c, LANES), F32)
    pairs = [(d, p) for d in range(2) for p in range(H_G // 2)]
    heads = [(d, h) for d in range(2) for h in range(H_G)]
    psl = lambda p: slice(p * LANES, (p + 1) * LANES)
    q_pair = {dp: tiles[dp[0]]["q"][:, psl(dp[1])] for dp in pairs}
    k_pair = {dp: tiles[dp[0]]["k"][:, psl(dp[1])] for dp in pairs}
    v_pair = {dp: tiles[dp[0]]["v"][:, psl(dp[1])] for dp in pairs}
    state = {dp: st_scr[n] for n, dp in enumerate(pairs)}
    m_row = {dp: m_scr[n:n + 1, :] for n, dp in enumerate(pairs)}

    def own(d, h, x):
        return jnp.where(first, x, 0.0) if h % 2 == 0 else jnp.where(first, 0.0, x)

    qk = [_mm_nt(q_pair[(d, h // 2)], own(d, h, k_pair[(d, h // 2)])) for d, h in heads]
    qs = {dp: _mm(q_pair[dp], state[dp]) for dp in pairs}
    bc = [sel[d][:, LANES * h:LANES * (h + 1)] for d, h in heads]
    m_prev = [m_row[(d, h // 2)][:, HEAD_DIM * (h % 2):HEAD_DIM * (h % 2) + 1] for d, h in heads]
    dlog = []
    for (d, h), bc_ in zip(heads, bc):
        ci, cf = d * H_G + h, 2 * H_G + d * H_G + h
        rowterm = tiles[d]["xt"][ci:ci + 1, :] - tiles[d]["b_r"][cf:cf + 1, :]
        dlog.append(jnp.where(tiles[d]["mask"], bc_ + rowterm, NEG_INF))
    inter_log = [bc_ + m_ for bc_, m_ in zip(bc, m_prev)]
    m_t = [jnp.maximum(il, jnp.max(dl, axis=-1, keepdims=True)) for il, dl in zip(inter_log, dlog)]
    sc = [qk_ * jnp.exp(dl - mt) for qk_, dl, mt in zip(qk, dlog, m_t)]
    w_inter = [jnp.exp(il - mt) for il, mt in zip(inter_log, m_t)]
    e_neg = [jnp.exp(-mt) for mt in m_t]
    res = []
    for n, (d, h) in enumerate(heads):
        v_aug = jnp.concatenate([own(d, h, v_pair[(d, h // 2)]), own(d, h, ones)], axis=1)
        res.append(_mm(sc[n], v_aug))
    for n, (d, p) in enumerate(pairs):
        a, b_ = 2 * n, 2 * n + 1
        tot = res[a] + res[b_] + jnp.tile(jnp.where(first, w_inter[a], w_inter[b_]), (1, 2)) * qs[(d, p)]
        den = jnp.maximum(jnp.abs(tot[:, LANES:]), jnp.where(first, e_neg[a], e_neg[b_]))
        o_refs[d][0, :, psl(p)] = (tot[:, :LANES] / den).astype(o_refs[d].dtype)

    for n, (d, p) in enumerate(pairs):
        bcp = sel[d][:, 4 * LANES + LANES * p:4 * LANES + LANES * (p + 1)]
        lip = sel[d][:, 6 * LANES + LANES * p:6 * LANES + LANES * (p + 1)]
        g_row = bcp[0:1, :] if d == 1 else bcp[c - 1:c, :]
        a_p = g_row - bcp + lip
        m_new = jnp.maximum(g_row + m_row[(d, p)], jnp.max(a_p, axis=0, keepdims=True))
        dec = jnp.exp(g_row + m_row[(d, p)] - m_new)
        kw_t = (k_pair[(d, p)] * jnp.exp(a_p - m_new)).T
        upd = _mm(kw_t, jnp.concatenate([v_pair[(d, p)], ones], axis=1))
        dec_tile = jnp.where(row_first, dec[:, 0:1], dec[:, HEAD_DIM:HEAD_DIM + 1])
        st_scr[n] = dec_tile * state[(d, p)] + jnp.where(block_diag, upd, 0.0)
        m_scr[n:n + 1, :] = m_new


def _mlstm(u3, conv_w, conv_b, gate_bias):
    b, t, _ = u3.shape
    c = MIX_CHUNK
    nblk = t // c
    rpb = c // ROW_ALIGN
    n8 = t // ROW_ALIGN
    col = U_MLSTM // 1024
    hcol = U_MLSTM // 512
    gcol = U_GATE // LANES

    def specs(rev):
        blk = (lambda i: nblk - 1 - i) if rev else (lambda i: i)
        return [
            pl.BlockSpec((1, c, 1024), lambda bi, i: (bi, blk(i), col)),
            pl.BlockSpec((1, ROW_ALIGN, 512), lambda bi, i: (bi, jnp.maximum(blk(i) * rpb - 1, 0), hcol)),
            pl.BlockSpec((1, ROW_ALIGN, 512), lambda bi, i: (bi, jnp.minimum((blk(i) + 1) * rpb, n8 - 1), hcol)),
            pl.BlockSpec((1, c, LANES), lambda bi, i: (bi, blk(i), gcol)),
        ]

    const = lambda shape: pl.BlockSpec(shape, lambda bi, i: (0,) * len(shape))
    return pl.pallas_call(
        _mlstm_kernel,
        out_shape=(jax.ShapeDtypeStruct((b, t, W_G), MIXER_DTYPE), jax.ShapeDtypeStruct((b, t, W_G), MIXER_DTYPE)),
        grid=(b, nblk),
        in_specs=specs(False) + specs(True) + [const((3, 2 * W_G)), const((1, 2 * W_G)), const((1, LANES)),
                                               const((2, 2 * LANES, 8 * LANES))],
        out_specs=(
            pl.BlockSpec((1, c, W_G), lambda bi, i: (bi, i, 0)),
            pl.BlockSpec((1, c, W_G), lambda bi, i: (bi, nblk - 1 - i, 0)),
        ),
        scratch_shapes=[pltpu.VMEM((H_G, LANES, 2 * LANES), F32), pltpu.VMEM((ROW_ALIGN, LANES), F32)],
        compiler_params=_cparams(("parallel", "arbitrary")),
        name="mlstm",
    )(u3, u3, u3, u3, u3, u3, u3, u3, conv_w, conv_b, gate_bias, _mlstm_select_matrix())


def _rwkv_prep_kernel(u_ref, up_ref, un_ref, mu_ref, w0_ref, w2_ref, a0_ref, a2_ref, g2_ref, kks_ref, ka_ref, rk_ref,
                      r_ref, v_ref, kk_ref, g_ref, bonus_ref, lwf_ref, lwb_ref, kf_ref, kb_ref, bf_ref, bb_ref):
    i = pl.program_id(1)
    nblk = pl.num_programs(1)
    u = u_ref[0]
    prev_row = jnp.where(i == 0, 0.0, up_ref[0][ROW_ALIGN - 1:ROW_ALIGN, :])
    next_row = jnp.where(i == nblk - 1, 0.0, un_ref[0][0:1, :])
    prev, nxt = _shift_rows(u, prev_row, next_row)
    us = u + mu_ref[...] * (0.5 * (prev + nxt) - u)
    r = us[:, 0:W_G]
    k = us[:, W_G:2 * W_G]
    v = us[:, 2 * W_G:3 * W_G]
    xw = us[:, 3 * W_G:3 * W_G + 64]
    xa = us[:, 3 * W_G + 64:3 * W_G + 128]
    xg = us[:, 3 * W_G + 128:]
    bd = _head_mean_matrix(W_G)
    g = _mm(_sigmoid(xg), g2_ref[...])
    lw = jnp.tanh(xw)
    a_lr = _mm_x3(xa, a2_ref[...])
    kk = k * kks_ref[...]
    kk = kk * lax.rsqrt(_head_mean(kk * kk, bd) * HEAD_DIM + 1e-12)
    r_ref[0] = r.astype(r_ref.dtype)
    v_ref[0] = v.astype(v_ref.dtype)
    kk_ref[0] = kk.astype(kk_ref.dtype)
    g_ref[0] = g.astype(g_ref.dtype)
    bonus = jnp.zeros_like(r)
    for d, (lw_ref, k_ref, b_ref) in enumerate(((lwf_ref, kf_ref, bf_ref), (lwb_ref, kb_ref, bb_ref))):
        z = w0_ref[d:d + 1, :] + _mm_x3(lw, w2_ref[d])
        lw_ref[0] = -_sigmoid(z) * math.exp(-0.5)
        a = _sigmoid(a0_ref[d:d + 1, :] + a_lr)
        kd = k * (1.0 + (a - 1.0) * ka_ref[...])
        k_ref[0] = kd.astype(k_ref.dtype)
        b_ref[0] = (kk * a).astype(b_ref.dtype)
        bonus = bonus + _head_mean(r * kd * rk_ref[...], bd) * HEAD_DIM * v
    bonus_ref[0] = bonus.astype(bonus_ref.dtype)


def _rwkv_prep(u3, mu, w0, w2, a0, a2, g2, kks, ka, rk):
    b, t, _ = u3.shape
    tb = min(RWKV_BLOCK, t)
    nblk = t // tb
    rpb = tb // ROW_ALIGN
    n8 = t // ROW_ALIGN
    const = lambda shape: pl.BlockSpec(shape, lambda bi, i: (0,) * len(shape))
    out = lambda dtype: jax.ShapeDtypeStruct((b, t, W_G), dtype)
    ospec = pl.BlockSpec((1, tb, W_G), lambda bi, i: (bi, i, 0))
    return pl.pallas_call(
        _rwkv_prep_kernel,
        out_shape=(out(MIXER_DTYPE),) * 3 + (out(MIXER_DTYPE),) * 2 + (out(F32),) * 2 + (out(MIXER_DTYPE),) * 4,
        grid=(b, nblk),
        in_specs=[
            pl.BlockSpec((1, tb, 1024), lambda bi, i: (bi, i, 0)),
            pl.BlockSpec((1, ROW_ALIGN, 1024), lambda bi, i: (bi, jnp.maximum(i * rpb - 1, 0), 0)),
            pl.BlockSpec((1, ROW_ALIGN, 1024), lambda bi, i: (bi, jnp.minimum((i + 1) * rpb, n8 - 1), 0)),
            const((1, 1024)), const((2, W_G)), const((2, 64, W_G)), const((2, W_G)), const((64, W_G)),
            const((128, W_G)), const((1, W_G)), const((1, W_G)), const((1, W_G)),
        ],
        out_specs=(ospec,) * 11,
        compiler_params=_cparams(("parallel", "parallel")),
        name="rwkv_prep",
    )(u3, u3, u3, mu, w0, w2, a0, a2, g2, kks, ka, rk)


def _tri_inverse_all(lmats, n):
    r = _iota2((n, n), 0)
    c = _iota2((n, n), 1)
    eye = jnp.where(r == c, 1.0, 0.0)
    pair = (r // 2 == c // 2) & (r != c)
    invs = [eye + jnp.where(pair, lm, 0.0) for lm in lmats]
    s = 2
    while s < n:
        sel = (r // (2 * s) == c // (2 * s)) & (r // s != c // s)
        offs = [jnp.where(sel, -lm, 0.0) for lm in lmats]
        xs = [_mm(inv, off) for inv, off in zip(invs, offs)]
        invs = [inv - _mm(x, inv) for inv, x in zip(invs, xs)]
        s *= 2
    return invs


def _rwkv_tile_terms(r, k, v, kk, b, lw, reverse):
    c = r.shape[0]
    tt = _iota2((c, c), 0)
    ss = _iota2((c, c), 1)
    tri = jnp.where((tt <= ss) if reverse else (ss <= tt), 1.0, 0.0).astype(BF16)
    cum_in = _mm_r3(tri, lw)
    cum_all = jnp.sum(lw, axis=0, keepdims=True)
    e_neg = jnp.exp(-cum_in)
    e_end = jnp.exp(cum_all - cum_in)
    return dict(at=-kk * jnp.exp(cum_in - lw), rt=r * jnp.exp(cum_in), bt=b * e_neg, kt=k * e_neg,
                gb=b * e_end, gk=k * e_end, v=v, e_all=jnp.exp(cum_all))


def _rwkv_chunk_terms(tiles, reverses):
    c = RWKV_CHUNK
    tt = _iota2((c, c), 0)
    ss = _iota2((c, c), 1)
    heads = [(ti, h) for ti in range(len(tiles)) for h in range(H_G)]
    sl = lambda h: slice(h * HEAD_DIM, (h + 1) * HEAD_DIM)
    get = lambda name: [tiles[ti][name][:, sl(h)] for ti, h in heads]
    at, rt, bt, kt, gb, gk, v = (get(nm) for nm in ("at", "rt", "bt", "kt", "gb", "gk", "v"))
    strict = [(ss > tt) if reverses[ti] else (ss < tt) for ti, _ in heads]
    incl = [(ss >= tt) if reverses[ti] else (ss <= tt) for ti, _ in heads]
    ps = [_mm_nt(jnp.concatenate([a, r_], axis=0), jnp.concatenate([b_, k_], axis=0))
          for a, r_, b_, k_ in zip(at, rt, bt, kt)]
    l_ab = [jnp.where(m, p[:c, :c], 0.0) for m, p in zip(strict, ps)]
    l_ak = [jnp.where(m, p[:c, c:], 0.0) for m, p in zip(strict, ps)]
    m_r = [jnp.concatenate([jnp.where(m, p[c:, :c], 0.0), jnp.where(m, p[c:, c:], 0.0)], axis=1)
           for m, p in zip(incl, ps)]
    lakv = [_mm(l, v_) for l, v_ in zip(l_ak, v)]
    invs = _tri_inverse_all(l_ab, c)
    tw = [_mm(inv, jnp.concatenate([a, lv], axis=1)) for inv, a, lv in zip(invs, at, lakv)]
    zeros = jnp.zeros((c, HEAD_DIM), F32)
    mm2 = [_mm(m, jnp.concatenate([t_, jnp.concatenate([zeros, v_], axis=1)], axis=0))
           for m, t_, v_ in zip(m_r, tw, v)]
    r1 = [r_ + m[:, :HEAD_DIM] for r_, m in zip(rt, mm2)]
    y0 = [m[:, HEAD_DIM:] for m in mm2]
    twg = [_mm_tn(t_, g_) for t_, g_ in zip(tw, gb)]
    vgk = [_mm_tn(v_, g_) for v_, g_ in zip(v, gk)]
    mlow = [x[:HEAD_DIM] for x in twg]
    nadd = [x[HEAD_DIM:] + y for x, y in zip(twg, vgk)]
    e_all = [tiles[ti]["e_all"][:, sl(h)] for ti, h in heads]
    return r1, y0, mlow, nadd, e_all


def _rwkv_core_kernel(rf_ref, kf_ref, vf_ref, kkf_ref, bf_ref, lwf_ref, rb_ref, kb_ref, vb_ref, kkb_ref, bb_ref,
                      lwb_ref, yf_ref, yb_ref, s_scr, *, group):
    i = pl.program_id(1)
    c = RWKV_CHUNK
    nch = rf_ref.shape[1] // c
    dirs = ((rf_ref, kf_ref, vf_ref, kkf_ref, bf_ref, lwf_ref), (rb_ref, kb_ref, vb_ref, kkb_ref, bb_ref, lwb_ref))
    y_refs = (yf_ref, yb_ref)

    @pl.when(i == 0)
    def _():
        s_scr[...] = jnp.zeros_like(s_scr)

    def step(j, states):
        tiles, reverses, rows = [], [], []
        for q in range(group):
            for d in range(2):
                cj = j * group + q
                cj = cj if d == 0 else nch - 1 - cj
                rw = pl.ds(pl.multiple_of(cj * c, c), c)
                tiles.append(_rwkv_tile_terms(*(ref[0, rw, :].astype(F32) for ref in dirs[d]), d == 1))
                reverses.append(d == 1)
                rows.append(rw)
        r1, y0, mlow, nadd, e_all = _rwkv_chunk_terms(tiles, reverses)
        states = list(states)
        for q in range(group):
            ys = [[], []]
            for d in range(2):
                for h in range(H_G):
                    n = (q * 2 + d) * H_G + h
                    s = states[d * H_G + h]
                    ys[d].append(_mm_nt(r1[n], s) + y0[n])
                    states[d * H_G + h] = s * e_all[n] + _mm(s, mlow[n]) + nadd[n]
            for d in range(2):
                y_refs[d][0, rows[q * 2 + d], :] = jnp.concatenate(ys[d], axis=1).astype(y_refs[d].dtype)
        return tuple(states)

    init = tuple(s_scr[n] for n in range(2 * H_G))
    if nch == group:
        states = step(0, init)
    else:
        states = lax.fori_loop(0, nch // group, step, init)
    for n in range(2 * H_G):
        s_scr[n] = states[n]


def _rwkv_core(r, v, kk, lwf, lwb, kf, kb, bf, bb):
    b, t, _ = r.shape
    tb = min(RWKV_BLOCK, t)
    nblk = t // tb
    fwd = pl.BlockSpec((1, tb, W_G), lambda bi, i: (bi, i, 0))
    bwd = pl.BlockSpec((1, tb, W_G), lambda bi, i: (bi, nblk - 1 - i, 0))
    out = jax.ShapeDtypeStruct((b, t, W_G), MIXER_DTYPE)
    return pl.pallas_call(
        functools.partial(_rwkv_core_kernel, group=min(RWKV_GROUP, tb // RWKV_CHUNK)),
        out_shape=(out, out),
        grid=(b, nblk),
        in_specs=[fwd] * 6 + [bwd] * 6,
        out_specs=(fwd, bwd),
        scratch_shapes=[pltpu.VMEM((2 * H_G, HEAD_DIM, HEAD_DIM), F32)],
        compiler_params=_cparams(("parallel", "arbitrary")),
        name="rwkv_core",
    )(r, kf, v, kk, bf, lwf, r, kb, v, kk, bb, lwb)


def _group_norm(y, gain, eps, bd):
    mu = _head_mean(y, bd)
    d = y - mu
    var = _head_mean(d * d, bd)
    return d * lax.rsqrt(var + eps) * gain


N_MIX_REFS = 15


def _mix_residual(refs, tiles):
    x_refs = refs[:len(tiles)]
    (ryf_ref, ryb_ref, rbon_ref, rg_ref, at_ref, mhf_ref, mhb_ref, mo_ref, tof_ref, tob_ref,
     tg_ref, rln_ref, mln_ref, tln_ref, w_ref) = refs[len(tiles):]
    bd = _head_mean_matrix(W_G)
    f = lambda ref: ref[...].astype(F32)
    o_a = (_group_norm(f(ryf_ref) + f(ryb_ref), rln_ref[...], RWKV_GN_EPS, bd) + f(rbon_ref)) * f(rg_ref)
    o_c = _group_norm(f(mhf_ref) + f(mhb_ref), mln_ref[...], HEAD_NORM_EPS, bd) * _sigmoid(mo_ref[...])
    o_d = _group_norm(f(tof_ref) + f(tob_ref), tln_ref[...], HEAD_NORM_EPS, bd) * _silu(tg_ref[...])
    mix = jnp.concatenate([o_a, at_ref[...], o_c, o_d], axis=1).astype(BF16)
    return _part_tile(pl.program_id(0), x_refs, tiles) + jnp.dot(mix, w_ref[...], preferred_element_type=F32)


def _mix_call(kernel, x_parts, rw, at, ml, rt, u2, rln, mln, tln, w_out, tm, extra, extra_specs, out_shape, out_specs,
              scratch_shapes, name):
    tiles = _part_tiles(x_parts, tm)
    n = sum(tiles) * tm
    row = lambda w: pl.BlockSpec((tm, w), lambda i: (i, 0))
    ucol = lambda off: pl.BlockSpec((tm, W_G), lambda i: (i, off // W_G))
    const = lambda shape: pl.BlockSpec(shape, lambda i: (0, 0))
    return pl.pallas_call(
        functools.partial(kernel, tiles=tiles),
        out_shape=out_shape,
        grid=(n // tm,),
        in_specs=_part_specs(x_parts, tm, D_MODEL) + [row(W_G)] * 7 + [ucol(U_MLSTM + 3 * W_G)] + [row(W_G)] * 2
                 + [ucol(U_RET + 3 * W_G)] + [const((1, W_G))] * 3
                 + [pl.BlockSpec((D_MODEL, D_MODEL), lambda i: (0, 0), pipeline_mode=pl.Buffered(1))] + extra_specs,
        out_specs=out_specs,
        scratch_shapes=scratch_shapes,
        compiler_params=_cparams(("arbitrary",)),
        name=name,
    )(*x_parts, *rw, at, *ml, u2, *rt, u2, rln, mln, tln, w_out, *extra)


def _mix_ffn_kernel(*refs, tiles):
    n_in = len(tiles) + N_MIX_REFS
    g_ref, wg_ref, wu_ref, wd_ref, o_ref = refs[n_in:]
    x = _mix_residual(refs[:n_in], tiles)
    h = _rms_norm_rows(x, g_ref[...]).astype(BF16)
    a = jnp.dot(h, wg_ref[...], preferred_element_type=F32)
    b = jnp.dot(h, wu_ref[...], preferred_element_type=F32)
    z = (_silu(a) * b).astype(BF16)
    o_ref[...] = x + jnp.dot(z, wd_ref[...], preferred_element_type=F32)


def _mix_ffn(mix_args, gain, wg, wu, wd, tm):
    n = sum(_part_tiles(mix_args[0], tm)) * tm
    res = lambda shape: pl.BlockSpec(shape, lambda i: (0, 0), pipeline_mode=pl.Buffered(1))
    return _mix_call(_mix_ffn_kernel, *mix_args, tm, (gain, wg, wu, wd),
                     [pl.BlockSpec((1, D_MODEL), lambda i: (0, 0)), res((D_MODEL, D_FF)), res((D_MODEL, D_FF)),
                      res((D_FF, D_MODEL))],
                     jax.ShapeDtypeStruct((n, D_MODEL), F32), pl.BlockSpec((tm, D_MODEL), lambda i: (i, 0)), [],
                     "mix_ffn")


def _mix_router_kernel(*refs, tiles):
    n_in = len(tiles) + N_MIX_REFS
    g_ref, wr_ref, x_ref, h_ref, gate_ref, idx_ref, cnt_ref, cnt_scr = refs[n_in:]

    @pl.when(pl.program_id(0) == 0)
    def _():
        cnt_scr[...] = jnp.zeros_like(cnt_scr)

    x = _mix_residual(refs[:n_in], tiles)
    x_ref[...] = x
    h = _rms_norm_rows(x, g_ref[...])
    h_ref[...] = h.astype(BF16)
    logits = _mm_x3(h, wr_ref[...])
    lane = _iota2(logits.shape, 1)
    logits = jnp.where(lane < N_EXPERTS, logits, NEG_INF)
    e = jnp.exp(logits - jnp.max(logits, axis=-1, keepdims=True))
    p = e / jnp.sum(e, axis=-1, keepdims=True)
    p = jnp.where(lane < N_EXPERTS, p, -1.0)
    m1 = jnp.max(p, axis=-1, keepdims=True)
    i1 = jnp.min(jnp.where(p == m1, lane, LANES), axis=-1, keepdims=True)
    p2 = jnp.where(lane == i1, -1.0, p)
    m2 = jnp.max(p2, axis=-1, keepdims=True)
    i2 = jnp.min(jnp.where(p2 == m2, lane, LANES), axis=-1, keepdims=True)
    tot = m1 + m2
    gate_ref[...] = jnp.where(lane == 0, m1 / tot, jnp.where(lane == 1, m2 / tot, 0.0))
    tm = logits.shape[0]
    chosen = jnp.where((lane == i1) | (lane == i2), 1.0, 0.0)
    tril = jnp.where(_iota2((tm, tm), 1) <= _iota2((tm, tm), 0), 1.0, 0.0).astype(BF16)
    incl = jnp.dot(tril, chosen.astype(BF16), preferred_element_type=F32)
    rank = cnt_scr[...] + incl - chosen
    r1 = jnp.sum(jnp.where(lane == i1, rank, 0.0), axis=-1, keepdims=True)
    r2 = jnp.sum(jnp.where(lane == i2, rank, 0.0), axis=-1, keepdims=True)
    info = jnp.where(lane == 0, i1.astype(F32), jnp.where(lane == 1, i2.astype(F32),
                                                          jnp.where(lane == 2, r1, jnp.where(lane == 3, r2, 0.0))))
    idx_ref[...] = info.T[:ROW_ALIGN, :].astype(jnp.int32)
    cnt_scr[...] = cnt_scr[...] + incl[tm - 1:tm, :]
    cnt_ref[...] = jnp.broadcast_to(cnt_scr[...], cnt_ref.shape).astype(jnp.int32)


def _mix_router(mix_args, gain, wr_pad, tm):
    n = sum(_part_tiles(mix_args[0], tm)) * tm
    return _mix_call(
        _mix_router_kernel, *mix_args, tm, (gain, wr_pad),
        [pl.BlockSpec((1, D_MODEL), lambda i: (0, 0)), pl.BlockSpec((D_MODEL, LANES), lambda i: (0, 0))],
        (jax.ShapeDtypeStruct((n, D_MODEL), F32), jax.ShapeDtypeStruct((n, D_MODEL), BF16),
         jax.ShapeDtypeStruct((n, LANES), F32), jax.ShapeDtypeStruct((ROW_ALIGN, n), jnp.int32),
         jax.ShapeDtypeStruct((ROW_ALIGN, LANES), jnp.int32)),
        (pl.BlockSpec((tm, D_MODEL), lambda i: (i, 0)), pl.BlockSpec((tm, D_MODEL), lambda i: (i, 0)),
         pl.BlockSpec((tm, LANES), lambda i: (i, 0)), pl.BlockSpec((ROW_ALIGN, tm), lambda i: (0, i)),
         pl.BlockSpec((ROW_ALIGN, LANES), lambda i: (0, 0))),
        [pltpu.VMEM((1, LANES), F32)], "mix_router")


def _expert_ffn_kernel(te_ref, nv_ref, xs_ref, wg_ref, wu_ref, wd_ref, o_ref):
    i = pl.program_id(0)

    @pl.when(i < nv_ref[0])
    def _():
        h = xs_ref[...]
        fw = D_FF // EXPERT_FF_SPLIT
        acc = None
        for f in range(EXPERT_FF_SPLIT):
            cols = slice(f * fw, (f + 1) * fw)
            a = jnp.dot(h, wg_ref[0, :, cols], preferred_element_type=F32)
            b = jnp.dot(h, wu_ref[0, :, cols], preferred_element_type=F32)
            z = (_silu(a) * b).astype(BF16)
            y = jnp.dot(z, wd_ref[0, cols, :], preferred_element_type=F32)
            acc = y if acc is None else acc + y
        o_ref[...] = acc.astype(o_ref.dtype)

    @pl.when(i >= nv_ref[0])
    def _():
        o_ref[...] = jnp.zeros_like(o_ref)


def _expert_ffn(xs, tile_expert, n_valid, wg, wu, wd, tm):
    rows = xs.shape[0]
    wspec = lambda shape: pl.BlockSpec((1,) + shape, lambda i, te, nv: (te[i], 0, 0))
    return pl.pallas_call(
        _expert_ffn_kernel,
        out_shape=jax.ShapeDtypeStruct((rows, D_MODEL), BF16),
        grid_spec=pltpu.PrefetchScalarGridSpec(
            num_scalar_prefetch=2,
            grid=(rows // tm,),
            in_specs=[pl.BlockSpec((tm, D_MODEL), lambda i, te, nv: (i, 0)),
                      wspec((D_MODEL, D_FF)), wspec((D_MODEL, D_FF)), wspec((D_FF, D_MODEL))],
            out_specs=pl.BlockSpec((tm, D_MODEL), lambda i, te, nv: (i, 0)),
        ),
        compiler_params=_cparams(("arbitrary",)),
        name="expert_ffn",
    )(tile_expert, n_valid, xs, wg, wu, wd)


def _moe_combine_kernel(x_ref, y1_ref, y2_ref, gate_ref, nf_ref, *o_refs, tiles):
    i = pl.program_id(0)
    g = gate_ref[...]
    y = x_ref[...] + g[:, 0:1] * y1_ref[...].astype(F32) + g[:, 1:2] * y2_ref[...].astype(F32)
    out = _rms_norm_rows(y, nf_ref[...])
    start = 0
    for o_ref, nt in zip(o_refs, tiles):
        @pl.when((i >= start) & (i < start + nt))
        def _(o_ref=o_ref):
            o_ref[...] = out
        start += nt


def _moe_combine(x2, y1, y2, gates, norm_final, tm, part_rows):
    n = x2.shape[0]
    row = lambda w: pl.BlockSpec((tm, w), lambda i: (i, 0))
    outs = tuple(jax.ShapeDtypeStruct((r, D_MODEL), F32) for r in part_rows)
    return pl.pallas_call(
        functools.partial(_moe_combine_kernel, tiles=_part_tiles(outs, tm)),
        out_shape=outs,
        grid=(n // tm,),
        in_specs=[row(D_MODEL), row(D_MODEL), row(D_MODEL), row(LANES), pl.BlockSpec((1, D_MODEL), lambda i: (0, 0))],
        out_specs=tuple(_part_specs(outs, tm, D_MODEL)),
        compiler_params=_cparams(("arbitrary",)),
        name="moe_combine",
    )(x2, y1, y2, gates, norm_final)


def _moe(x2, h, gates, idx, counts, wg, wu, wd, norm_final, tm, part_rows):
    n = x2.shape[0]
    tme = EXPERT_TILE
    n_tiles = (2 * n + N_EXPERTS * (tme - 1)) // tme + 1
    e1, e2, r1, r2 = idx[0], idx[1], idx[2], idx[3]
    cnt = counts[0, :N_EXPERTS]
    padded = (cnt + tme - 1) // tme * tme
    group_end = jnp.cumsum(padded)
    group_off = group_end - padded
    dense_off = jnp.cumsum(cnt) - cnt
    lookup = lambda table, e: sum(jnp.where(e == k, table[k], 0) for k in range(N_EXPERTS))
    slot1 = lookup(group_off, e1) + r1
    slot2 = lookup(group_off, e2) + r2
    tok = jnp.arange(n, dtype=jnp.int32)
    sorted_tok = jnp.sort(jnp.concatenate([e1 * n + tok, e2 * n + tok])) % n
    tile_start = jnp.arange(n_tiles, dtype=jnp.int32) * tme
    tile_expert = jnp.minimum(jnp.searchsorted(group_end, tile_start, side='right'), N_EXPERTS - 1).astype(jnp.int32)
    n_valid = (group_end[-1:] // tme).astype(jnp.int32)
    rank = (tile_start - group_off[tile_expert])[:, None] + jnp.arange(tme, dtype=jnp.int32)[None, :]
    dense = jnp.clip(dense_off[tile_expert][:, None] + rank, 0, 2 * n - 1)
    take = functools.partial(jnp.take, axis=0, mode='clip')
    spread = (tile_start[:, None] + jnp.arange(tme, dtype=jnp.int32)[None, :]) % n
    src = jnp.where(rank < cnt[tile_expert][:, None], take(sorted_tok, dense.reshape(-1)).reshape(dense.shape), spread)
    xs = take(h, src.reshape(-1))
    ys = _expert_ffn(xs, tile_expert, n_valid, wg, wu, wd, tme)
    y1 = take(ys, slot1)
    y2 = take(ys, slot2)
    return _moe_combine(x2, y1, y2, gates, norm_final, tm, part_rows)


def _rope_tables(t):
    rows = t // GRID_W
    pos = np.arange(rows * GRID_W)
    row = (pos // GRID_W).astype(np.float32)
    col = (pos % GRID_W).astype(np.float32)
    nf = HEAD_DIM // 4
    inv = jnp.asarray(ROPE_THETA, F32) ** (-jnp.arange(nf, dtype=F32) / nf)
    ar = jnp.asarray(row)[:, None] * inv
    ac = jnp.asarray(col)[:, None] * inv
    cos = jnp.concatenate([jnp.cos(ar), jnp.cos(ar), jnp.cos(ac), jnp.cos(ac)], axis=-1)
    sin = jnp.concatenate([-jnp.sin(ar), jnp.sin(ar), -jnp.sin(ac), jnp.sin(ac)], axis=-1)
    return jnp.tile(cos, (1, H_G)), jnp.tile(sin, (1, H_G))


def _pad_w_in(w):
    a, b_, c, d = 1024, 512, 1040, 1024
    w_a, w_b, w_c, w_d = w[:, :a], w[:, a:a + b_], w[:, a + b_:a + b_ + c], w[:, a + b_ + c:]
    gates = jnp.pad(w_c[:, 1024:], ((0, 0), (0, LANES - 16)))
    return jnp.concatenate([w_a, w_c[:, :1024], w_d, w_b, gates], axis=1).astype(BF16)


def _row(v):
    return v.reshape(1, -1).astype(F32)


def _trunk(xs, p):
    t = xs[0].shape[1]
    part_rows = tuple(x.shape[0] * t for x in xs)
    b = sum(x.shape[0] for x in xs)
    n = b * t
    tm = 256
    cos, sin = _rope_tables(t)
    x_parts = tuple(x.reshape(-1, D_MODEL) for x in xs)
    depth = p['w_in'].shape[0]
    for l in range(depth):
        tm_in = 2 * tm if all(xp.shape[0] % (2 * tm) == 0 for xp in x_parts) else tm
        u2 = _in_proj(x_parts, _row(p['norm_mix'][l]), _pad_w_in(p['w_in'][l]), tm_in)
        u3 = u2.reshape(b, t, U_COLS)
        (r, v, kk, g, bonus, lwf, lwb, kf, kb, bf, bb) = _rwkv_prep(
            u3, _row(p['rwkv_mu'][l]), p['rwkv_w0'][l], p['rwkv_w2'][l], p['rwkv_a0'][l], p['rwkv_a2'][l],
            p['rwkv_g2'][l], _row(p['rwkv_kk'][l]), _row(p['rwkv_ka'][l]), _row(p['rwkv_rk'][l]))
        yf, yb = _rwkv_core(r, v, kk, lwf, lwb, kf, kb, bf, bb)
        at = _attention(u3, cos, sin, _row(jnp.tile(p['attn_q_norm'][l], H_G)),
                        _row(jnp.tile(p['attn_k_norm'][l], KV_ATTN)))
        gate_bias = jnp.pad(jnp.concatenate([p['mlstm_i_bias'][l].reshape(-1), p['mlstm_f_bias'][l].reshape(-1)]),
                            (0, LANES - 4 * H_G))
        hf, hb = _mlstm(u3, p['mlstm_conv_w'][l], _row(p['mlstm_conv_b'][l]), _row(gate_bias))
        of, ob = _retention(u3, cos, sin)
        flat = lambda z: z.reshape(n, W_G)
        mix_args = (x_parts, tuple(map(flat, (yf, yb, bonus, g))), flat(at), tuple(map(flat, (hf, hb))),
                    tuple(map(flat, (of, ob))), u2, _row(p['rwkv_ln'][l]), _row(p['mlstm_ln'][l]),
                    _row(p['ret_ln'][l]), p['w_out'][l].astype(BF16))
        j = l // 2
        if l % 2 == 0:
            x2 = _mix_ffn(mix_args, _row(p['norm_ffn'][l]), p['ffn_w_gate'][j].astype(BF16),
                          p['ffn_w_up'][j].astype(BF16), p['ffn_w_down'][j].astype(BF16), tm)
            x_parts = (x2,)
            if l == depth - 1:
                raise NotImplementedError("final norm after a dense FFN layer")
        else:
            wr = jnp.pad(p['moe_router'][j], ((0, 0), (0, LANES - N_EXPERTS)))
            x2, h, gates, idx, counts = _mix_router(mix_args, _row(p['norm_ffn'][l]), wr, tm)
            if l != depth - 1:
                raise NotImplementedError("expert layer that is not the last layer")
            outs = _moe(x2, h, gates, idx, counts, p['moe_w_gate'][j].astype(BF16), p['moe_w_up'][j].astype(BF16),
                        p['moe_w_down'][j].astype(BF16), _row(p['norm_final']), tm, part_rows)
    return tuple(o.reshape(x.shape) for o, x in zip(outs, xs))


def kernel(x_prompt, x_sample, norm_mix, norm_ffn, norm_final, w_in, w_out, rwkv_mu, rwkv_w0, rwkv_w2,
           rwkv_a0, rwkv_a2, rwkv_g2, rwkv_kk, rwkv_ka, rwkv_rk, rwkv_ln, attn_q_norm, attn_k_norm,
           mlstm_conv_w, mlstm_conv_b, mlstm_i_bias, mlstm_f_bias, mlstm_ln, ret_ln, ffn_w_gate, ffn_w_up,
           ffn_w_down, moe_router, moe_w_gate, moe_w_up, moe_w_down):
    p = dict(norm_mix=norm_mix, norm_ffn=norm_ffn, norm_final=norm_final, w_in=w_in, w_out=w_out,
             rwkv_mu=rwkv_mu, rwkv_w0=rwkv_w0, rwkv_w2=rwkv_w2, rwkv_a0=rwkv_a0, rwkv_a2=rwkv_a2,
             rwkv_g2=rwkv_g2, rwkv_kk=rwkv_kk, rwkv_ka=rwkv_ka, rwkv_rk=rwkv_rk, rwkv_ln=rwkv_ln,
             attn_q_norm=attn_q_norm, attn_k_norm=attn_k_norm, mlstm_conv_w=mlstm_conv_w,
             mlstm_conv_b=mlstm_conv_b, mlstm_i_bias=mlstm_i_bias, mlstm_f_bias=mlstm_f_bias,
             mlstm_ln=mlstm_ln, ret_ln=ret_ln, ffn_w_gate=ffn_w_gate, ffn_w_up=ffn_w_up,
             ffn_w_down=ffn_w_down, moe_router=moe_router, moe_w_gate=moe_w_gate, moe_w_up=moe_w_up,
             moe_w_down=moe_w_down)
    return _trunk((x_prompt, x_sample), p)
```

```python
import functools
import math

import numpy as np
import jax
import jax.numpy as jnp
from jax import lax
from jax.experimental import pallas as pl
from jax.experimental.pallas import tpu as pltpu

F32 = jnp.float32
BF16 = jnp.bfloat16
MIXER_DTYPE = BF16

D_MODEL = 1024
HEAD_DIM = 64
W_G = 256
H_G = 4
KV_ATTN = 2
D_FF = 2816
N_EXPERTS = 8
NORM_EPS = 1e-6
HEAD_NORM_EPS = 1e-5
RWKV_GN_EPS = 64e-5
NEG_INF = -1e30
ROPE_THETA = 10000.0
GRID_W = 64

LANES = 128
ROW_ALIGN = 8
VMEM_LIMIT_BYTES = 56 * 1024 * 1024

U_RWKV = 0
U_MLSTM = 1024
U_RET = 2048
U_ATTN = 3072
U_GATE = 3584
U_COLS = 3712

RWKV_CHUNK = 64
RWKV_BLOCK = 256
RWKV_GROUP = 4
MIX_CHUNK = 128
MIX_BLOCK = 256
EXPERT_TILE = 256
EXPERT_FF_SPLIT = 1


def _cparams(sem):
    return pltpu.CompilerParams(dimension_semantics=sem, vmem_limit_bytes=VMEM_LIMIT_BYTES)


def _bdot(a, b, dims):
    return lax.dot_general(a, b, (dims, ((), ())), preferred_element_type=F32)


def _mm(a, b):
    return _bdot(a.astype(BF16), b.astype(BF16), ((1,), (0,)))


def _mm_nt(a, b):
    return _bdot(a.astype(BF16), b.astype(BF16), ((1,), (1,)))


def _mm_tn(a, b):
    return _bdot(a.astype(BF16), b.astype(BF16), ((0,), (0,)))


def _split2(a):
    hi = a.astype(BF16)
    lo = (a - hi.astype(F32)).astype(BF16)
    return hi, lo


def _split3(a):
    hi = a.astype(BF16)
    r = a - hi.astype(F32)
    mid = r.astype(BF16)
    lo = (r - mid.astype(F32)).astype(BF16)
    return hi, mid, lo


def _mm_l2(a, b_exact):
    hi, lo = _split2(a)
    return _bdot(hi, b_exact, ((1,), (0,))) + _bdot(lo, b_exact, ((1,), (0,)))


def _mm_l3(a, b_exact):
    h, m, l = _split3(a)
    return _bdot(h, b_exact, ((1,), (0,))) + _bdot(m, b_exact, ((1,), (0,))) + _bdot(l, b_exact, ((1,), (0,)))


def _mm_r3(a_exact, b):
    h, m, l = _split3(b)
    return _bdot(a_exact, h, ((1,), (0,))) + _bdot(a_exact, m, ((1,), (0,))) + _bdot(a_exact, l, ((1,), (0,)))


def _mm_x3(a, b):
    ah, al = _split2(a)
    bh, bl = _split2(b)
    d = ((1,), (0,))
    return _bdot(ah, bh, d) + _bdot(ah, bl, d) + _bdot(al, bh, d)


def _iota2(shape, axis):
    return lax.broadcasted_iota(jnp.int32, shape, axis)


def _head_mean_matrix(width):
    r = _iota2((width, width), 0) // HEAD_DIM
    c = _iota2((width, width), 1) // HEAD_DIM
    return jnp.where(r == c, 1.0 / HEAD_DIM, 0.0).astype(BF16)


def _head_mean(z, bd):
    return _bdot(z.astype(BF16), bd, ((1,), (0,)))


def _sigmoid(x):
    return 1.0 / (1.0 + jnp.exp(-x))


def _silu(x):
    return x * _sigmoid(x)


def _log_sigmoid(x):
    return jnp.minimum(x, 0.0) - jnp.log(1.0 + jnp.exp(-jnp.abs(x)))


def _rms_norm_rows(x, gain):
    ms = jnp.mean(x * x, axis=-1, keepdims=True)
    return x * lax.rsqrt(ms + NORM_EPS) * gain


def _rope_swap(z):
    w = z.shape[-1]
    lane = _iota2(z.shape, z.ndim - 1)
    fwd = pltpu.roll(z, w - 16, z.ndim - 1)
    bwd = pltpu.roll(z, 16, z.ndim - 1)
    return jnp.where((lane % 32) < 16, fwd, bwd)


def _rope(z, cos, sin):
    return z * cos + _rope_swap(z) * sin


def _shift_rows(x, prev_row, next_row):
    n = x.shape[0]
    row = _iota2(x.shape, 0)
    prev = jnp.where(row == 0, prev_row, pltpu.roll(x, 1, 0))
    nxt = jnp.where(row == n - 1, next_row, pltpu.roll(x, n - 1, 0))
    return prev, nxt


def _part_tiles(parts, tm):
    return tuple(p.shape[0] // tm for p in parts)


def _part_specs(parts, tm, width):
    specs, start = [], 0
    for nt in _part_tiles(parts, tm):
        specs.append(pl.BlockSpec((tm, width), lambda i, s=start, nt=nt: (jnp.clip(i - s, 0, nt - 1), 0)))
        start += nt
    return specs


def _part_tile(i, refs, tiles):
    x = refs[0][...]
    start = tiles[0]
    for ref, nt in zip(refs[1:], tiles[1:]):
        x = jnp.where(i >= start, ref[...], x)
        start += nt
    return x


def _inproj_kernel(*refs, tiles):
    x_refs, (g_ref, w_ref, o_ref) = refs[:len(tiles)], refs[len(tiles):]
    h = _rms_norm_rows(_part_tile(pl.program_id(0), x_refs, tiles), g_ref[...])
    o_ref[...] = jnp.dot(h.astype(BF16), w_ref[...], preferred_element_type=F32)


def _in_proj(x_parts, gain, w_pad, tm):
    tiles = _part_tiles(x_parts, tm)
    n = sum(tiles) * tm
    return pl.pallas_call(
        functools.partial(_inproj_kernel, tiles=tiles),
        out_shape=jax.ShapeDtypeStruct((n, U_COLS), F32),
        grid=(n // tm,),
        in_specs=_part_specs(x_parts, tm, D_MODEL) + [
            pl.BlockSpec((1, D_MODEL), lambda i: (0, 0)),
            pl.BlockSpec((D_MODEL, U_COLS), lambda i: (0, 0), pipeline_mode=pl.Buffered(1)),
        ],
        out_specs=pl.BlockSpec((tm, U_COLS), lambda i: (i, 0)),
        compiler_params=_cparams(("arbitrary",)),
        name="in_proj",
    )(*x_parts, gain, w_pad)


def _attn_kernel(u_ref, cos_ref, sin_ref, qg_ref, kg_ref, o_ref, q_scr, k_scr, v_scr, *, tq):
    t = u_ref.shape[1]
    u = u_ref[0]
    q = u[:, :W_G]
    k = u[:, W_G:W_G + KV_ATTN * HEAD_DIM]
    v = u[:, W_G + KV_ATTN * HEAD_DIM:]
    cos = cos_ref[...]
    sin = sin_ref[...]
    bd_q = _head_mean_matrix(W_G)
    bd_k = _head_mean_matrix(KV_ATTN * HEAD_DIM)
    qn = q * lax.rsqrt(_head_mean(q * q, bd_q) + NORM_EPS) * qg_ref[...]
    kn = k * lax.rsqrt(_head_mean(k * k, bd_k) + NORM_EPS) * kg_ref[...]
    kw = KV_ATTN * HEAD_DIM
    q_scr[...] = (_rope(qn, cos, sin) * (HEAD_DIM ** -0.5 * math.log2(math.e))).astype(BF16)
    k_scr[...] = _rope(kn, cos[:, :kw], sin[:, :kw]).astype(BF16)
    ones = jnp.ones((t, HEAD_DIM), BF16)
    for j in range(KV_ATTN):
        vj = v[:, j * HEAD_DIM:(j + 1) * HEAD_DIM].astype(BF16)
        v_scr[:, j * LANES:(j + 1) * LANES] = jnp.concatenate([vj, ones], axis=1)
    group = H_G // KV_ATTN

    def q_tile(i, carry):
        rows = pl.ds(pl.multiple_of(i * tq, tq), tq)

        def scores(h):
            j = h // group
            qh = q_scr[rows, h * HEAD_DIM:(h + 1) * HEAD_DIM]
            return _bdot(qh, k_scr[:, j * HEAD_DIM:(j + 1) * HEAD_DIM], ((1,), (1,)))

        s_next = scores(0)
        for h in range(H_G):
            s = s_next
            if h + 1 < H_G:
                s_next = scores(h + 1)
            m = jnp.max(s, axis=-1, keepdims=True)
            p = jnp.exp2(s - m)
            vj = v_scr[:, (h // group) * LANES:(h // group + 1) * LANES]
            r = jnp.dot(p.astype(BF16), vj, preferred_element_type=F32)
            o_ref[0, rows, h * HEAD_DIM:(h + 1) * HEAD_DIM] = r[:, :HEAD_DIM] / r[:, HEAD_DIM:]
        return carry

    lax.fori_loop(0, t // tq, q_tile, 0)


def _attention(u3, cos, sin, q_gain, k_gain):
    b, t, _ = u3.shape
    tq = min(256, t)
    col = U_ATTN // 512
    return pl.pallas_call(
        functools.partial(_attn_kernel, tq=tq),
        out_shape=jax.ShapeDtypeStruct((b, t, W_G), F32),
        grid=(b,),
        in_specs=[
            pl.BlockSpec((1, t, 512), lambda i: (i, 0, col)),
            pl.BlockSpec((t, W_G), lambda i: (0, 0)),
            pl.BlockSpec((t, W_G), lambda i: (0, 0)),
            pl.BlockSpec((1, W_G), lambda i: (0, 0)),
            pl.BlockSpec((1, KV_ATTN * HEAD_DIM), lambda i: (0, 0)),
        ],
        out_specs=pl.BlockSpec((1, t, W_G), lambda i: (i, 0, 0)),
        scratch_shapes=[
            pltpu.VMEM((t, W_G), BF16),
            pltpu.VMEM((t, KV_ATTN * HEAD_DIM), BF16),
            pltpu.VMEM((t, KV_ATTN * LANES), BF16),
        ],
        compiler_params=_cparams(("parallel",)),
        name="attention",
    )(u3, cos, sin, q_gain, k_gain)


def _ret_log_gamma(direction):
    return [math.log1p(-2.0 ** (-5.0 - (2 * h + direction) / 2.0)) for h in range(H_G)]


def _ret_kernel(uf_ref, ub_ref, cf_ref, sf_ref, cb_ref, sb_ref, of_ref, ob_ref, rf_scr, rb_scr):
    i = pl.program_id(1)
    c = MIX_CHUNK

    @pl.when(i == 0)
    def _():
        rf_scr[...] = jnp.zeros_like(rf_scr)
        rb_scr[...] = jnp.zeros_like(rb_scr)

    tt = _iota2((c, c), 0)
    ss = _iota2((c, c), 1)
    diff = (tt - ss).astype(F32)
    jcol = tt.astype(F32)
    first = ss < HEAD_DIM
    block_diag = (tt < HEAD_DIM) == first
    lg_f = _ret_log_gamma(0)
    lg_b = _ret_log_gamma(1)
    psl = lambda p: slice(p * LANES, (p + 1) * LANES)
    lane_lg = lambda lg, p: jnp.where(first, lg[2 * p], lg[2 * p + 1])
    row_decay = lambda lg, p: jnp.where(tt < HEAD_DIM, math.exp(c * lg[2 * p]), math.exp(c * lg[2 * p + 1]))
    own = lambda h, x: jnp.where(first, x, 0.0) if h % 2 == 0 else jnp.where(first, 0.0, x)
    pairs = range(H_G // 2)

    nch = uf_ref.shape[1] // c
    uf = uf_ref[0]
    qf_all = _rope(uf[:, :W_G], cf_ref[...], sf_ref[...])
    kf_all = _rope(uf[:, W_G:2 * W_G], cf_ref[...], sf_ref[...]) * HEAD_DIM ** -0.5
    ub = ub_ref[0]
    qb_all = _rope(ub[:, :W_G], cb_ref[...], sb_ref[...])
    kb_all = _rope(ub[:, W_G:2 * W_G], cb_ref[...], sb_ref[...]) * HEAD_DIM ** -0.5
    rf_prev = [rf_scr[p] for p in pairs]
    rb_prev = [rb_scr[p] for p in pairs]
    decay = [jnp.where(tt >= ss, jnp.exp(diff * lg_f[h]), 0.0) + jnp.where(ss >= tt, jnp.exp(-diff * lg_b[h]), 0.0)
             for h in range(H_G)]
    for n in range(nch):
        rf_rows = slice(n * c, (n + 1) * c)
        rb_rows = slice((nch - 1 - n) * c, (nch - n) * c)
        qf, kf, vf = qf_all[rf_rows], kf_all[rf_rows], uf[rf_rows, 2 * W_G:3 * W_G]
        qb, kb, vb = qb_all[rb_rows], kb_all[rb_rows], ub[rb_rows, 2 * W_G:3 * W_G]
        qk = [_mm_nt(qf[:, psl(h // 2)], own(h, kf[:, psl(h // 2)])) for h in range(H_G)]
        intra = [_mm(qk[h] * decay[h], own(h, vf[:, psl(h // 2)])) for h in range(H_G)]
        inter_f = [_mm(qf[:, psl(p)] * jnp.exp((jcol + 1.0) * lane_lg(lg_f, p)), rf_prev[p]) for p in pairs]
        inter_b = [_mm(qb[:, psl(p)] * jnp.exp((c - jcol) * lane_lg(lg_b, p)), rb_prev[p]) for p in pairs]
        upd_f = [_mm_tn(kf[:, psl(p)] * jnp.exp((c - 1.0 - jcol) * lane_lg(lg_f, p)), vf[:, psl(p)]) for p in pairs]
        upd_b = [_mm_tn(kb[:, psl(p)] * jnp.exp(jcol * lane_lg(lg_b, p)), vb[:, psl(p)]) for p in pairs]
        for p in pairs:
            of_ref[0, rf_rows, psl(p)] = (intra[2 * p] + intra[2 * p + 1] + inter_f[p]).astype(of_ref.dtype)
            ob_ref[0, rb_rows, psl(p)] = inter_b[p].astype(ob_ref.dtype)
        rf_prev = [row_decay(lg_f, p) * rf_prev[p] + jnp.where(block_diag, upd_f[p], 0.0) for p in pairs]
        rb_prev = [row_decay(lg_b, p) * rb_prev[p] + jnp.where(block_diag, upd_b[p], 0.0) for p in pairs]
    for p in pairs:
        rf_scr[p] = rf_prev[p]
        rb_scr[p] = rb_prev[p]


def _retention(u3, cos, sin):
    b, t, _ = u3.shape
    c = min(MIX_BLOCK, t)
    nblk = t // c
    col = U_RET // 1024
    tab = lambda rev: pl.BlockSpec((c, W_G), (lambda bi, i: (nblk - 1 - i, 0)) if rev else (lambda bi, i: (i, 0)))
    return pl.pallas_call(
        _ret_kernel,
        out_shape=(jax.ShapeDtypeStruct((b, t, W_G), MIXER_DTYPE), jax.ShapeDtypeStruct((b, t, W_G), MIXER_DTYPE)),
        grid=(b, nblk),
        in_specs=[
            pl.BlockSpec((1, c, 1024), lambda bi, i: (bi, i, col)),
            pl.BlockSpec((1, c, 1024), lambda bi, i: (bi, nblk - 1 - i, col)),
            tab(False), tab(False), tab(True), tab(True),
        ],
        out_specs=(
            pl.BlockSpec((1, c, W_G), lambda bi, i: (bi, i, 0)),
            pl.BlockSpec((1, c, W_G), lambda bi, i: (bi, nblk - 1 - i, 0)),
        ),
        scratch_shapes=[pltpu.VMEM((H_G // 2, LANES, LANES), F32), pltpu.VMEM((H_G // 2, LANES, LANES), F32)],
        compiler_params=_cparams(("parallel", "arbitrary")),
        name="retention",
    )(u3, u3, cos, sin, cos, sin)


def _mlstm_tile(u_ref, up_ref, un_ref, g_ref, cw_ref, cb_ref, gb_ref, blk, nblk, direction):
    c = MIX_CHUNK
    reverse = direction == 1
    u = u_ref[0]
    qk = u[:, :2 * W_G]
    prev_row = jnp.where(blk == 0, 0.0, up_ref[0][ROW_ALIGN - 1:ROW_ALIGN, :])
    next_row = jnp.where(blk == nblk - 1, 0.0, un_ref[0][0:1, :])
    prev, nxt = _shift_rows(qk, prev_row, next_row)
    cw = cw_ref[...]
    qk = _silu(cw[0:1] * prev + cw[1:2] * qk + cw[2:3] * nxt + cb_ref[...])
    qa = qk[:, :W_G]
    ka = qk[:, W_G:] * HEAD_DIM ** -0.5
    va = u[:, 2 * W_G:3 * W_G]

    x = g_ref[0] + gb_ref[...]
    xt = x.T
    lf_c = _log_sigmoid(x)
    lf_r = _log_sigmoid(xt)
    tt = _iota2((c, c), 0)
    ss = _iota2((c, c), 1)
    lower = jnp.where(ss <= tt, 1.0, 0.0).astype(BF16)
    upper = jnp.where(tt <= ss, 1.0, 0.0).astype(BF16)
    if reverse:
        b_c = _mm_r3(upper, lf_c)
        b_r = _mm_l3(lf_r, lower)
        mask = ss >= tt
    else:
        b_c = _mm_r3(lower, lf_c)
        b_r = _mm_l3(lf_r, upper)
        mask = ss <= tt
    g_all = jnp.sum(lf_c, axis=0, keepdims=True)
    return dict(q=qa, k=ka, v=va, x=x, xt=xt, b_c=b_c, b_r=b_r, g_all=g_all, mask=mask)


def _mlstm_select_matrix():
    sel = np.zeros((2, 2 * LANES, 8 * LANES), np.float32)
    for d in range(2):
        for h in range(H_G):
            ci, cf = d * H_G + h, 2 * H_G + d * H_G + h
            p, j = divmod(h, 2)
            sel[d, cf, LANES * h:LANES * (h + 1)] = 1.0
            sel[d, cf, 4 * LANES + LANES * p + HEAD_DIM * j:4 * LANES + LANES * p + HEAD_DIM * (j + 1)] = 1.0
            sel[d, LANES + ci, 6 * LANES + LANES * p + HEAD_DIM * j:6 * LANES + LANES * p + HEAD_DIM * (j + 1)] = 1.0
    return jnp.asarray(sel, BF16)


def _mlstm_kernel(uf_ref, upf_ref, unf_ref, gf_ref, ub_ref, upb_ref, unb_ref, gbk_ref,
                  cw_ref, cb_ref, gb_ref, sel_ref, of_ref, ob_ref, st_scr, m_scr):
    i = pl.program_id(1)
    nblk = pl.num_programs(1)
    c = MIX_CHUNK

    @pl.when(i == 0)
    def _():
        st_scr[...] = jnp.zeros_like(st_scr)
        m_scr[...] = jnp.zeros_like(m_scr)

    tiles = (_mlstm_tile(uf_ref, upf_ref, unf_ref, gf_ref, cw_ref, cb_ref, gb_ref, i, nblk, 0),
             _mlstm_tile(ub_ref, upb_ref, unb_ref, gbk_ref, cw_ref, cb_ref, gb_ref, nblk - 1 - i, nblk, 1))
    o_refs = (of_ref, ob_ref)
    sel = [_mm_l2(jnp.concatenate([tiles[d]["b_c"], tiles[d]["x"]], axis=1), sel_ref[d]) for d in range(2)]
    first = _iota2((c, LANES), 1) < HEAD_DIM
    row_first = _iota2((LANES, 2 * LANES), 0) < HEAD_DIM
    lane2 = _iota2((LANES, 2 * LANES), 1) % LANES < HEAD_DIM
    block_diag = row_first == lane2
    ones = jnp.ones((c, LANES), F32)
    pairs = [(d, p) for d in range(2) for p in range(H_G // 2)]
    heads = [(d, h) for d in range(2) for h in range(H_G)]
    psl = lambda p: slice(p * LANES, (p + 1) * LANES)
    q_pair = {dp: tiles[dp[0]]["q"][:, psl(dp[1])] for dp in pairs}
    k_pair = {dp: tiles[dp[0]]["k"][:, psl(dp[1])] for dp in pairs}
    v_pair = {dp: tiles[dp[0]]["v"][:, psl(dp[1])] for dp in pairs}
    state = {dp: st_scr[n] for n, dp in enumerate(pairs)}
    m_row = {dp: m_scr[n:n + 1, :] for n, dp in enumerate(pairs)}

    def own(d, h, x):
        return jnp.where(first, x, 0.0) if h % 2 == 0 else jnp.where(first, 0.0, x)

    qk = [_mm_nt(q_pair[(d, h // 2)], own(d, h, k_pair[(d, h // 2)])) for d, h in heads]
    qs = {dp: _mm(q_pair[dp], state[dp]) for dp in pairs}
    bc = [sel[d][:, LANES * h:LANES * (h + 1)] for d, h in heads]
    m_prev = [m_row[(d, h // 2)][:, HEAD_DIM * (h % 2):HEAD_DIM * (h % 2) + 1] for d, h in heads]
    dlog = []
    for (d, h), bc_ in zip(heads, bc):
        ci, cf = d * H_G + h, 2 * H_G + d * H_G + h
        rowterm = tiles[d]["xt"][ci:ci + 1, :] - tiles[d]["b_r"][cf:cf + 1, :]
        dlog.append(jnp.where(tiles[d]["mask"], bc_ + rowterm, NEG_INF))
    inter_log = [bc_ + m_ for bc_, m_ in zip(bc, m_prev)]
    m_t = [jnp.maximum(il, jnp.max(dl, axis=-1, keepdims=True)) for il, dl in zip(inter_log, dlog)]
    sc = [qk_ * jnp.exp(dl - mt) for qk_, dl, mt in zip(qk, dlog, m_t)]
    w_inter = [jnp.exp(il - mt) for il, mt in zip(inter_log, m_t)]
    e_neg = [jnp.exp(-mt) for mt in m_t]
    res = []
    for n, (d, h) in enumerate(heads):
        v_aug = jnp.concatenate([own(d, h, v_pair[(d, h // 2)]), own(d, h, ones)], axis=1)
        res.append(_mm(sc[n], v_aug))
    for n, (d, p) in enumerate(pairs):
        a, b_ = 2 * n, 2 * n + 1
        tot = res[a] + res[b_] + jnp.tile(jnp.where(first, w_inter[a], w_inter[b_]), (1, 2)) * qs[(d, p)]
        den = jnp.maximum(jnp.abs(tot[:, LANES:]), jnp.where(first, e_neg[a], e_neg[b_]))
        o_refs[d][0, :, psl(p)] = (tot[:, :LANES] / den).astype(o_refs[d].dtype)

    for n, (d, p) in enumerate(pairs):
        bcp = sel[d][:, 4 * LANES + LANES * p:4 * LANES + LANES * (p + 1)]
        lip = sel[d][:, 6 * LANES + LANES * p:6 * LANES + LANES * (p + 1)]
        g_row = bcp[0:1, :] if d == 1 else bcp[c - 1:c, :]
        a_p = g_row - bcp + lip
        m_new = jnp.maximum(g_row + m_row[(d, p)], jnp.max(a_p, axis=0, keepdims=True))
        dec = jnp.exp(g_row + m_row[(d, p)] - m_new)
        kw_t = (k_pair[(d, p)] * jnp.exp(a_p - m_new)).T
        upd = _mm(kw_t, jnp.concatenate([v_pair[(d, p)], ones], axis=1))
        dec_tile = jnp.where(row_first, dec[:, 0:1], dec[:, HEAD_DIM:HEAD_DIM + 1])
        st_scr[n] = dec_tile * state[(d, p)] + jnp.where(block_diag, upd, 0.0)
        m_scr[n:n + 1, :] = m_new


def _mlstm(u3, conv_w, conv_b, gate_bias):
    b, t, _ = u3.shape
    c = MIX_CHUNK
    nblk = t // c
    rpb = c // ROW_ALIGN
    n8 = t // ROW_ALIGN
    col = U_MLSTM // 1024
    hcol = U_MLSTM // 512
    gcol = U_GATE // LANES

    def specs(rev):
        blk = (lambda i: nblk - 1 - i) if rev else (lambda i: i)
        return [
            pl.BlockSpec((1, c, 1024), lambda bi, i: (bi, blk(i), col)),
            pl.BlockSpec((1, ROW_ALIGN, 512), lambda bi, i: (bi, jnp.maximum(blk(i) * rpb - 1, 0), hcol)),
            pl.BlockSpec((1, ROW_ALIGN, 512), lambda bi, i: (bi, jnp.minimum((blk(i) + 1) * rpb, n8 - 1), hcol)),
            pl.BlockSpec((1, c, LANES), lambda bi, i: (bi, blk(i), gcol)),
        ]

    const = lambda shape: pl.BlockSpec(shape, lambda bi, i: (0,) * len(shape))
    return pl.pallas_call(
        _mlstm_kernel,
        out_shape=(jax.ShapeDtypeStruct((b, t, W_G), MIXER_DTYPE), jax.ShapeDtypeStruct((b, t, W_G), MIXER_DTYPE)),
        grid=(b, nblk),
        in_specs=specs(False) + specs(True) + [const((3, 2 * W_G)), const((1, 2 * W_G)), const((1, LANES)),
                                               const((2, 2 * LANES, 8 * LANES))],
        out_specs=(
            pl.BlockSpec((1, c, W_G), lambda bi, i: (bi, i, 0)),
            pl.BlockSpec((1, c, W_G), lambda bi, i: (bi, nblk - 1 - i, 0)),
        ),
        scratch_shapes=[pltpu.VMEM((H_G, LANES, 2 * LANES), F32), pltpu.VMEM((ROW_ALIGN, LANES), F32)],
        compiler_params=_cparams(("parallel", "arbitrary")),
        name="mlstm",
    )(u3, u3, u3, u3, u3, u3, u3, u3, conv_w, conv_b, gate_bias, _mlstm_select_matrix())


def _rwkv_prep_kernel(u_ref, up_ref, un_ref, mu_ref, w0_ref, w2_ref, a0_ref, a2_ref, g2_ref, kks_ref, ka_ref, rk_ref,
                      r_ref, v_ref, kk_ref, g_ref, bonus_ref, lwf_ref, lwb_ref, kf_ref, kb_ref, bf_ref, bb_ref):
    i = pl.program_id(1)
    nblk = pl.num_programs(1)
    u = u_ref[0]
    prev_row = jnp.where(i == 0, 0.0, up_ref[0][ROW_ALIGN - 1:ROW_ALIGN, :])
    next_row = jnp.where(i == nblk - 1, 0.0, un_ref[0][0:1, :])
    prev, nxt = _shift_rows(u, prev_row, next_row)
    us = u + mu_ref[...] * (0.5 * (prev + nxt) - u)
    r = us[:, 0:W_G]
    k = us[:, W_G:2 * W_G]
    v = us[:, 2 * W_G:3 * W_G]
    xw = us[:, 3 * W_G:3 * W_G + 64]
    xa = us[:, 3 * W_G + 64:3 * W_G + 128]
    xg = us[:, 3 * W_G + 128:]
    bd = _head_mean_matrix(W_G)
    g = _mm(_sigmoid(xg), g2_ref[...])
    lw = jnp.tanh(xw)
    a_lr = _mm_x3(xa, a2_ref[...])
    kk = k * kks_ref[...]
    kk = kk * lax.rsqrt(_head_mean(kk * kk, bd) * HEAD_DIM + 1e-12)
    r_ref[0] = r.astype(r_ref.dtype)
    v_ref[0] = v.astype(v_ref.dtype)
    kk_ref[0] = kk.astype(kk_ref.dtype)
    g_ref[0] = g.astype(g_ref.dtype)
    bonus = jnp.zeros_like(r)
    for d, (lw_ref, k_ref, b_ref) in enumerate(((lwf_ref, kf_ref, bf_ref), (lwb_ref, kb_ref, bb_ref))):
        z = w0_ref[d:d + 1, :] + _mm_x3(lw, w2_ref[d])
        lw_ref[0] = -_sigmoid(z) * math.exp(-0.5)
        a = _sigmoid(a0_ref[d:d + 1, :] + a_lr)
        kd = k * (1.0 + (a - 1.0) * ka_ref[...])
        k_ref[0] = kd.astype(k_ref.dtype)
        b_ref[0] = (kk * a).astype(b_ref.dtype)
        bonus = bonus + _head_mean(r * kd * rk_ref[...], bd) * HEAD_DIM * v
    bonus_ref[0] = bonus.astype(bonus_ref.dtype)


def _rwkv_prep(u3, mu, w0, w2, a0, a2, g2, kks, ka, rk):
    b, t, _ = u3.shape
    tb = min(RWKV_BLOCK, t)
    nblk = t // tb
    rpb = tb // ROW_ALIGN
    n8 = t // ROW_ALIGN
    const = lambda shape: pl.BlockSpec(shape, lambda bi, i: (0,) * len(shape))
    out = lambda dtype: jax.ShapeDtypeStruct((b, t, W_G), dtype)
    ospec = pl.BlockSpec((1, tb, W_G), lambda bi, i: (bi, i, 0))
    return pl.pallas_call(
        _rwkv_prep_kernel,
        out_shape=(out(MIXER_DTYPE),) * 3 + (out(MIXER_DTYPE),) * 2 + (out(F32),) * 2 + (out(MIXER_DTYPE),) * 4,
        grid=(b, nblk),
        in_specs=[
            pl.BlockSpec((1, tb, 1024), lambda bi, i: (bi, i, 0)),
            pl.BlockSpec((1, ROW_ALIGN, 1024), lambda bi, i: (bi, jnp.maximum(i * rpb - 1, 0), 0)),
            pl.BlockSpec((1, ROW_ALIGN, 1024), lambda bi, i: (bi, jnp.minimum((i + 1) * rpb, n8 - 1), 0)),
            const((1, 1024)), const((2, W_G)), const((2, 64, W_G)), const((2, W_G)), const((64, W_G)),
            const((128, W_G)), const((1, W_G)), const((1, W_G)), const((1, W_G)),
        ],
        out_specs=(ospec,) * 11,
        compiler_params=_cparams(("parallel", "parallel")),
        name="rwkv_prep",
    )(u3, u3, u3, mu, w0, w2, a0, a2, g2, kks, ka, rk)


def _tri_inverse_all(lmats, n):
    r = _iota2((n, n), 0)
    c = _iota2((n, n), 1)
    eye = jnp.where(r == c, 1.0, 0.0)
    pair = (r // 2 == c // 2) & (r != c)
    invs = [eye + jnp.where(pair, lm, 0.0) for lm in lmats]
    s = 2
    while s < n:
        sel = (r // (2 * s) == c // (2 * s)) & (r // s != c // s)
        offs = [jnp.where(sel, -lm, 0.0) for lm in lmats]
        xs = [_mm(inv, off) for inv, off in zip(invs, offs)]
        invs = [inv - _mm(x, inv) for inv, x in zip(invs, xs)]
        s *= 2
    return invs


def _rwkv_tile_terms(r, k, v, kk, b, lw, reverse):
    c = r.shape[0]
    tt = _iota2((c, c), 0)
    ss = _iota2((c, c), 1)
    tri = jnp.where((tt <= ss) if reverse else (ss <= tt), 1.0, 0.0).astype(BF16)
    cum_in = _mm_r3(tri, lw)
    cum_all = jnp.sum(lw, axis=0, keepdims=True)
    e_neg = jnp.exp(-cum_in)
    e_end = jnp.exp(cum_all - cum_in)
    return dict(at=-kk * jnp.exp(cum_in - lw), rt=r * jnp.exp(cum_in), bt=b * e_neg, kt=k * e_neg,
                gb=b * e_end, gk=k * e_end, v=v, e_all=jnp.exp(cum_all))


def _rwkv_chunk_terms(tiles, reverses):
    c = RWKV_CHUNK
    tt = _iota2((c, c), 0)
    ss = _iota2((c, c), 1)
    heads = [(ti, h) for ti in range(len(tiles)) for h in range(H_G)]
    sl = lambda h: slice(h * HEAD_DIM, (h + 1) * HEAD_DIM)
    get = lambda name: [tiles[ti][name][:, sl(h)] for ti, h in heads]
    at, rt, bt, kt, gb, gk, v = (get(nm) for nm in ("at", "rt", "bt", "kt", "gb", "gk", "v"))
    strict = [(ss > tt) if reverses[ti] else (ss < tt) for ti, _ in heads]
    incl = [(ss >= tt) if reverses[ti] else (ss <= tt) for ti, _ in heads]
    ps = [_mm_nt(jnp.concatenate([a, r_], axis=0), jnp.concatenate([b_, k_], axis=0))
          for a, r_, b_, k_ in zip(at, rt, bt, kt)]
    l_ab = [jnp.where(m, p[:c, :c], 0.0) for m, p in zip(strict, ps)]
    l_ak = [jnp.where(m, p[:c, c:], 0.0) for m, p in zip(strict, ps)]
    m_r = [jnp.concatenate([jnp.where(m, p[c:, :c], 0.0), jnp.where(m, p[c:, c:], 0.0)], axis=1)
           for m, p in zip(incl, ps)]
    lakv = [_mm(l, v_) for l, v_ in zip(l_ak, v)]
    invs = _tri_inverse_all(l_ab, c)
    tw = [_mm(inv, jnp.concatenate([a, lv], axis=1)) for inv, a, lv in zip(invs, at, lakv)]
    zeros = jnp.zeros((c, HEAD_DIM), F32)
    mm2 = [_mm(m, jnp.concatenate([t_, jnp.concatenate([zeros, v_], axis=1)], axis=0))
           for m, t_, v_ in zip(m_r, tw, v)]
    r1 = [r_ + m[:, :HEAD_DIM] for r_, m in zip(rt, mm2)]
    y0 = [m[:, HEAD_DIM:] for m in mm2]
    twg = [_mm_tn(t_, g_) for t_, g_ in zip(tw, gb)]
    vgk = [_mm_tn(v_, g_) for v_, g_ in zip(v, gk)]
    mlow = [x[:HEAD_DIM] for x in twg]
    nadd = [x[HEAD_DIM:] + y for x, y in zip(twg, vgk)]
    e_all = [tiles[ti]["e_all"][:, sl(h)] for ti, h in heads]
    return r1, y0, mlow, nadd, e_all


def _rwkv_core_kernel(rf_ref, kf_ref, vf_ref, kkf_ref, bf_ref, lwf_ref, rb_ref, kb_ref, vb_ref, kkb_ref, bb_ref,
                      lwb_ref, yf_ref, yb_ref, s_scr, *, group):
    i = pl.program_id(1)
    c = RWKV_CHUNK
    nch = rf_ref.shape[1] // c
    dirs = ((rf_ref, kf_ref, vf_ref, kkf_ref, bf_ref, lwf_ref), (rb_ref, kb_ref, vb_ref, kkb_ref, bb_ref, lwb_ref))
    y_refs = (yf_ref, yb_ref)

    @pl.when(i == 0)
    def _():
        s_scr[...] = jnp.zeros_like(s_scr)

    def step(j, states):
        tiles, reverses, rows = [], [], []
        for q in range(group):
            for d in range(2):
                cj = j * group + q
                cj = cj if d == 0 else nch - 1 - cj
                rw = pl.ds(pl.multiple_of(cj * c, c), c)
                tiles.append(_rwkv_tile_terms(*(ref[0, rw, :].astype(F32) for ref in dirs[d]), d == 1))
                reverses.append(d == 1)
                rows.append(rw)
        r1, y0, mlow, nadd, e_all = _rwkv_chunk_terms(tiles, reverses)
        states = list(states)
        for q in range(group):
            ys = [[], []]
            for d in range(2):
                for h in range(H_G):
                    n = (q * 2 + d) * H_G + h
                    s = states[d * H_G + h]
                    ys[d].append(_mm_nt(r1[n], s) + y0[n])
                    states[d * H_G + h] = s * e_all[n] + _mm(s, mlow[n]) + nadd[n]
            for d in range(2):
                y_refs[d][0, rows[q * 2 + d], :] = jnp.concatenate(ys[d], axis=1).astype(y_refs[d].dtype)
        return tuple(states)

    init = tuple(s_scr[n] for n in range(2 * H_G))
    if nch == group:
        states = step(0, init)
    else:
        states = lax.fori_loop(0, nch // group, step, init)
    for n in range(2 * H_G):
        s_scr[n] = states[n]


def _rwkv_core(r, v, kk, lwf, lwb, kf, kb, bf, bb):
    b, t, _ = r.shape
    tb = min(RWKV_BLOCK, t)
    nblk = t // tb
    fwd = pl.BlockSpec((1, tb, W_G), lambda bi, i: (bi, i, 0))
    bwd = pl.BlockSpec((1, tb, W_G), lambda bi, i: (bi, nblk - 1 - i, 0))
    out = jax.ShapeDtypeStruct((b, t, W_G), MIXER_DTYPE)
    return pl.pallas_call(
        functools.partial(_rwkv_core_kernel, group=min(RWKV_GROUP, tb // RWKV_CHUNK)),
        out_shape=(out, out),
        grid=(b, nblk),
        in_specs=[fwd] * 6 + [bwd] * 6,
        out_specs=(fwd, bwd),
        scratch_shapes=[pltpu.VMEM((2 * H_G, HEAD_DIM, HEAD_DIM), F32)],
        compiler_params=_cparams(("parallel", "arbitrary")),
        name="rwkv_core",
    )(r, kf, v, kk, bf, lwf, r, kb, v, kk, bb, lwb)


def _group_norm(y, gain, eps, bd):
    mu = _head_mean(y, bd)
    d = y - mu
    var = _head_mean(d * d, bd)
    return d * lax.rsqrt(var + eps) * gain


N_MIX_REFS = 15


def _mix_residual(refs, tiles):
    x_refs = refs[:len(tiles)]
    (ryf_ref, ryb_ref, rbon_ref, rg_ref, at_ref, mhf_ref, mhb_ref, mo_ref, tof_ref, tob_ref,
     tg_ref, rln_ref, mln_ref, tln_ref, w_ref) = refs[len(tiles):]
    bd = _head_mean_matrix(W_G)
    f = lambda ref: ref[...].astype(F32)
    o_a = (_group_norm(f(ryf_ref) + f(ryb_ref), rln_ref[...], RWKV_GN_EPS, bd) + f(rbon_ref)) * f(rg_ref)
    o_c = _group_norm(f(mhf_ref) + f(mhb_ref), mln_ref[...], HEAD_NORM_EPS, bd) * _sigmoid(mo_ref[...])
    o_d = _group_norm(f(tof_ref) + f(tob_ref), tln_ref[...], HEAD_NORM_EPS, bd) * _silu(tg_ref[...])
    mix = jnp.concatenate([o_a, at_ref[...], o_c, o_d], axis=1).astype(BF16)
    return _part_tile(pl.program_id(0), x_refs, tiles) + jnp.dot(mix, w_ref[...], preferred_element_type=F32)


def _mix_call(kernel, x_parts, rw, at, ml, rt, u2, rln, mln, tln, w_out, tm, extra, extra_specs, out_shape, out_specs,
              scratch_shapes, name, lag=0):
    tiles = _part_tiles(x_parts, tm)
    nt = sum(tiles)
    row = lambda w: pl.BlockSpec((tm, w), lambda i: (jnp.minimum(i, nt - 1), 0))
    ucol = lambda off: pl.BlockSpec((tm, W_G), lambda i: (jnp.minimum(i, nt - 1), off // W_G))
    const = lambda shape: pl.BlockSpec(shape, lambda i: (0, 0))
    return pl.pallas_call(
        functools.partial(kernel, tiles=tiles),
        out_shape=out_shape,
        grid=(nt + lag,),
        in_specs=_part_specs(x_parts, tm, D_MODEL) + [row(W_G)] * 7 + [ucol(U_MLSTM + 3 * W_G)] + [row(W_G)] * 2
                 + [ucol(U_RET + 3 * W_G)] + [const((1, W_G))] * 3
                 + [pl.BlockSpec((D_MODEL, D_MODEL), lambda i: (0, 0), pipeline_mode=pl.Buffered(1))] + extra_specs,
        out_specs=out_specs,
        scratch_shapes=scratch_shapes,
        compiler_params=_cparams(("arbitrary",)),
        name=name,
    )(*x_parts, *rw, at, *ml, u2, *rt, u2, rln, mln, tln, w_out, *extra)


def _mix_ffn_kernel(*refs, tiles):
    n_in = len(tiles) + N_MIX_REFS
    g_ref, wg_ref, wu_ref, wd_ref, o_ref, x_scr = refs[n_in:]

    @pl.when(pl.program_id(0) == 0)
    def _():
        x_scr[...] = jnp.zeros_like(x_scr)

    x = x_scr[...]
    h = _rms_norm_rows(x, g_ref[...]).astype(BF16)
    a = jnp.dot(h, wg_ref[...], preferred_element_type=F32)
    b = jnp.dot(h, wu_ref[...], preferred_element_type=F32)
    z = (_silu(a) * b).astype(BF16)
    o_ref[...] = x + jnp.dot(z, wd_ref[...], preferred_element_type=F32)
    x_scr[...] = _mix_residual(refs[:n_in], tiles)


def _mix_ffn(mix_args, gain, wg, wu, wd, tm):
    n = sum(_part_tiles(mix_args[0], tm)) * tm
    res = lambda shape: pl.BlockSpec(shape, lambda i: (0, 0), pipeline_mode=pl.Buffered(1))
    return _mix_call(_mix_ffn_kernel, *mix_args, tm, (gain, wg, wu, wd),
                     [pl.BlockSpec((1, D_MODEL), lambda i: (0, 0)), res((D_MODEL, D_FF)), res((D_MODEL, D_FF)),
                      res((D_FF, D_MODEL))],
                     jax.ShapeDtypeStruct((n, D_MODEL), F32),
                     pl.BlockSpec((tm, D_MODEL), lambda i: (jnp.maximum(i - 1, 0), 0)),
                     [pltpu.VMEM((tm, D_MODEL), F32)], "mix_ffn", lag=1)


def _mix_router_kernel(*refs, tiles):
    n_in = len(tiles) + N_MIX_REFS
    g_ref, wr_ref, x_ref, h_ref, gate_ref, idx_ref, cnt_ref, cnt_scr = refs[n_in:]

    @pl.when(pl.program_id(0) == 0)
    def _():
        cnt_scr[...] = jnp.zeros_like(cnt_scr)

    x = _mix_residual(refs[:n_in], tiles)
    x_ref[...] = x
    h = _rms_norm_rows(x, g_ref[...])
    h_ref[...] = h.astype(BF16)
    logits = _mm_x3(h, wr_ref[...])
    lane = _iota2(logits.shape, 1)
    logits = jnp.where(lane < N_EXPERTS, logits, NEG_INF)
    e = jnp.exp(logits - jnp.max(logits, axis=-1, keepdims=True))
    p = e / jnp.sum(e, axis=-1, keepdims=True)
    p = jnp.where(lane < N_EXPERTS, p, -1.0)
    m1 = jnp.max(p, axis=-1, keepdims=True)
    i1 = jnp.min(jnp.where(p == m1, lane, LANES), axis=-1, keepdims=True)
    p2 = jnp.where(lane == i1, -1.0, p)
    m2 = jnp.max(p2, axis=-1, keepdims=True)
    i2 = jnp.min(jnp.where(p2 == m2, lane, LANES), axis=-1, keepdims=True)
    tot = m1 + m2
    gate_ref[...] = jnp.where(lane == 0, m1 / tot, jnp.where(lane == 1, m2 / tot, 0.0))
    tm = logits.shape[0]
    chosen = jnp.where((lane == i1) | (lane == i2), 1.0, 0.0)
    tril = jnp.where(_iota2((tm, tm), 1) <= _iota2((tm, tm), 0), 1.0, 0.0).astype(BF16)
    incl = jnp.dot(tril, chosen.astype(BF16), preferred_element_type=F32)
    rank = cnt_scr[...] + incl - chosen
    r1 = jnp.sum(jnp.where(lane == i1, rank, 0.0), axis=-1, keepdims=True)
    r2 = jnp.sum(jnp.where(lane == i2, rank, 0.0), axis=-1, keepdims=True)
    info = jnp.where(lane == 0, i1.astype(F32), jnp.where(lane == 1, i2.astype(F32),
                                                          jnp.where(lane == 2, r1, jnp.where(lane == 3, r2, 0.0))))
    idx_ref[...] = info.T[:ROW_ALIGN, :].astype(jnp.int32)
    cnt_scr[...] = cnt_scr[...] + incl[tm - 1:tm, :]
    cnt_ref[...] = jnp.broadcast_to(cnt_scr[...], cnt_ref.shape).astype(jnp.int32)


def _mix_router(mix_args, gain, wr_pad, tm):
    n = sum(_part_tiles(mix_args[0], tm)) * tm
    return _mix_call(
        _mix_router_kernel, *mix_args, tm, (gain, wr_pad),
        [pl.BlockSpec((1, D_MODEL), lambda i: (0, 0)), pl.BlockSpec((D_MODEL, LANES), lambda i: (0, 0))],
        (jax.ShapeDtypeStruct((n, D_MODEL), F32), jax.ShapeDtypeStruct((n, D_MODEL), BF16),
         jax.ShapeDtypeStruct((n, LANES), F32), jax.ShapeDtypeStruct((ROW_ALIGN, n), jnp.int32),
         jax.ShapeDtypeStruct((ROW_ALIGN, LANES), jnp.int32)),
        (pl.BlockSpec((tm, D_MODEL), lambda i: (i, 0)), pl.BlockSpec((tm, D_MODEL), lambda i: (i, 0)),
         pl.BlockSpec((tm, LANES), lambda i: (i, 0)), pl.BlockSpec((ROW_ALIGN, tm), lambda i: (0, i)),
         pl.BlockSpec((ROW_ALIGN, LANES), lambda i: (0, 0))),
        [pltpu.VMEM((1, LANES), F32)], "mix_router")


def _expert_ffn_kernel(te_ref, nv_ref, xs_ref, wg_ref, wu_ref, wd_ref, o_ref):
    i = pl.program_id(0)

    @pl.when(i < nv_ref[0])
    def _():
        h = xs_ref[...]
        fw = D_FF // EXPERT_FF_SPLIT
        acc = None
        for f in range(EXPERT_FF_SPLIT):
            cols = slice(f * fw, (f + 1) * fw)
            a = jnp.dot(h, wg_ref[0, :, cols], preferred_element_type=F32)
            b = jnp.dot(h, wu_ref[0, :, cols], preferred_element_type=F32)
            z = (_silu(a) * b).astype(BF16)
            y = jnp.dot(z, wd_ref[0, cols, :], preferred_element_type=F32)
            acc = y if acc is None else acc + y
        o_ref[...] = acc.astype(o_ref.dtype)

    @pl.when(i >= nv_ref[0])
    def _():
        o_ref[...] = jnp.zeros_like(o_ref)


def _expert_ffn(xs, tile_expert, n_valid, wg, wu, wd, tm):
    rows = xs.shape[0]
    wspec = lambda shape: pl.BlockSpec((1,) + shape, lambda i, te, nv: (te[i], 0, 0))
    return pl.pallas_call(
        _expert_ffn_kernel,
        out_shape=jax.ShapeDtypeStruct((rows, D_MODEL), BF16),
        grid_spec=pltpu.PrefetchScalarGridSpec(
            num_scalar_prefetch=2,
            grid=(rows // tm,),
            in_specs=[pl.BlockSpec((tm, D_MODEL), lambda i, te, nv: (i, 0)),
                      wspec((D_MODEL, D_FF)), wspec((D_MODEL, D_FF)), wspec((D_FF, D_MODEL))],
            out_specs=pl.BlockSpec((tm, D_MODEL), lambda i, te, nv: (i, 0)),
        ),
        compiler_params=_cparams(("arbitrary",)),
        name="expert_ffn",
    )(tile_expert, n_valid, xs, wg, wu, wd)


def _moe_combine_kernel(x_ref, y1_ref, y2_ref, gate_ref, nf_ref, *o_refs, tiles):
    i = pl.program_id(0)
    g = gate_ref[...]
    y = x_ref[...] + g[:, 0:1] * y1_ref[...].astype(F32) + g[:, 1:2] * y2_ref[...].astype(F32)
    out = _rms_norm_rows(y, nf_ref[...])
    start = 0
    for o_ref, nt in zip(o_refs, tiles):
        @pl.when((i >= start) & (i < start + nt))
        def _(o_ref=o_ref):
            o_ref[...] = out
        start += nt


def _moe_combine(x2, y1, y2, gates, norm_final, tm, part_rows):
    n = x2.shape[0]
    row = lambda w: pl.BlockSpec((tm, w), lambda i: (i, 0))
    outs = tuple(jax.ShapeDtypeStruct((r, D_MODEL), F32) for r in part_rows)
    return pl.pallas_call(
        functools.partial(_moe_combine_kernel, tiles=_part_tiles(outs, tm)),
        out_shape=outs,
        grid=(n // tm,),
        in_specs=[row(D_MODEL), row(D_MODEL), row(D_MODEL), row(LANES), pl.BlockSpec((1, D_MODEL), lambda i: (0, 0))],
        out_specs=tuple(_part_specs(outs, tm, D_MODEL)),
        compiler_params=_cparams(("arbitrary",)),
        name="moe_combine",
    )(x2, y1, y2, gates, norm_final)


def _moe(x2, h, gates, idx, counts, wg, wu, wd, norm_final, tm, part_rows):
    n = x2.shape[0]
    tme = EXPERT_TILE
    n_tiles = (2 * n + N_EXPERTS * (tme - 1)) // tme + 1
    e1, e2, r1, r2 = idx[0], idx[1], idx[2], idx[3]
    cnt = counts[0, :N_EXPERTS]
    padded = (cnt + tme - 1) // tme * tme
    group_end = jnp.cumsum(padded)
    group_off = group_end - padded
    dense_off = jnp.cumsum(cnt) - cnt
    lookup = lambda table, e: sum(jnp.where(e == k, table[k], 0) for k in range(N_EXPERTS))
    slot1 = lookup(group_off, e1) + r1
    slot2 = lookup(group_off, e2) + r2
    tok = jnp.arange(n, dtype=jnp.int32)
    sorted_tok = jnp.sort(jnp.concatenate([e1 * n + tok, e2 * n + tok])) % n
    tile_start = jnp.arange(n_tiles, dtype=jnp.int32) * tme
    tile_expert = jnp.minimum(jnp.searchsorted(group_end, tile_start, side='right'), N_EXPERTS - 1).astype(jnp.int32)
    n_valid = (group_end[-1:] // tme).astype(jnp.int32)
    rank = (tile_start - group_off[tile_expert])[:, None] + jnp.arange(tme, dtype=jnp.int32)[None, :]
    dense = jnp.clip(dense_off[tile_expert][:, None] + rank, 0, 2 * n - 1)
    take = functools.partial(jnp.take, axis=0, mode='clip')
    spread = (tile_start[:, None] + jnp.arange(tme, dtype=jnp.int32)[None, :]) % n
    src = jnp.where(rank < cnt[tile_expert][:, None], take(sorted_tok, dense.reshape(-1)).reshape(dense.shape), spread)
    xs = take(h, src.reshape(-1))
    ys = _expert_ffn(xs, tile_expert, n_valid, wg, wu, wd, tme)
    y1 = take(ys, slot1)
    y2 = take(ys, slot2)
    return _moe_combine(x2, y1, y2, gates, norm_final, tm, part_rows)


def _rope_tables(t):
    rows = t // GRID_W
    pos = np.arange(rows * GRID_W)
    row = (pos // GRID_W).astype(np.float32)
    col = (pos % GRID_W).astype(np.float32)
    nf = HEAD_DIM // 4
    inv = jnp.asarray(ROPE_THETA, F32) ** (-jnp.arange(nf, dtype=F32) / nf)
    ar = jnp.asarray(row)[:, None] * inv
    ac = jnp.asarray(col)[:, None] * inv
    cos = jnp.concatenate([jnp.cos(ar), jnp.cos(ar), jnp.cos(ac), jnp.cos(ac)], axis=-1)
    sin = jnp.concatenate([-jnp.sin(ar), jnp.sin(ar), -jnp.sin(ac), jnp.sin(ac)], axis=-1)
    return jnp.tile(cos, (1, H_G)), jnp.tile(sin, (1, H_G))


def _pad_w_in(w):
    a, b_, c, d = 1024, 512, 1040, 1024
    w_a, w_b, w_c, w_d = w[:, :a], w[:, a:a + b_], w[:, a + b_:a + b_ + c], w[:, a + b_ + c:]
    gates = jnp.pad(w_c[:, 1024:], ((0, 0), (0, LANES - 16)))
    return jnp.concatenate([w_a, w_c[:, :1024], w_d, w_b, gates], axis=1).astype(BF16)


def _row(v):
    return v.reshape(1, -1).astype(F32)


def _trunk(xs, p):
    t = xs[0].shape[1]
    part_rows = tuple(x.shape[0] * t for x in xs)
    b = sum(x.shape[0] for x in xs)
    n = b * t
    tm = 256
    cos, sin = _rope_tables(t)
    x_parts = tuple(x.reshape(-1, D_MODEL) for x in xs)
    depth = p['w_in'].shape[0]
    for l in range(depth):
        tm_in = 2 * tm if all(xp.shape[0] % (2 * tm) == 0 for xp in x_parts) else tm
        u2 = _in_proj(x_parts, _row(p['norm_mix'][l]), _pad_w_in(p['w_in'][l]), tm_in)
        u3 = u2.reshape(b, t, U_COLS)
        (r, v, kk, g, bonus, lwf, lwb, kf, kb, bf, bb) = _rwkv_prep(
            u3, _row(p['rwkv_mu'][l]), p['rwkv_w0'][l], p['rwkv_w2'][l], p['rwkv_a0'][l], p['rwkv_a2'][l],
            p['rwkv_g2'][l], _row(p['rwkv_kk'][l]), _row(p['rwkv_ka'][l]), _row(p['rwkv_rk'][l]))
        yf, yb = _rwkv_core(r, v, kk, lwf, lwb, kf, kb, bf, bb)
        at = _attention(u3, cos, sin, _row(jnp.tile(p['attn_q_norm'][l], H_G)),
                        _row(jnp.tile(p['attn_k_norm'][l], KV_ATTN)))
        gate_bias = jnp.pad(jnp.concatenate([p['mlstm_i_bias'][l].reshape(-1), p['mlstm_f_bias'][l].reshape(-1)]),
                            (0, LANES - 4 * H_G))
        hf, hb = _mlstm(u3, p['mlstm_conv_w'][l], _row(p['mlstm_conv_b'][l]), _row(gate_bias))
        of, ob = _retention(u3, cos, sin)
        flat = lambda z: z.reshape(n, W_G)
        mix_args = (x_parts, tuple(map(flat, (yf, yb, bonus, g))), flat(at), tuple(map(flat, (hf, hb))),
                    tuple(map(flat, (of, ob))), u2, _row(p['rwkv_ln'][l]), _row(p['mlstm_ln'][l]),
                    _row(p['ret_ln'][l]), p['w_out'][l].astype(BF16))
        j = l // 2
        if l % 2 == 0:
            x2 = _mix_ffn(mix_args, _row(p['norm_ffn'][l]), p['ffn_w_gate'][j].astype(BF16),
                          p['ffn_w_up'][j].astype(BF16), p['ffn_w_down'][j].astype(BF16), tm)
            x_parts = (x2,)
            if l == depth - 1:
                raise NotImplementedError("final norm after a dense FFN layer")
        else:
            wr = jnp.pad(p['moe_router'][j], ((0, 0), (0, LANES - N_EXPERTS)))
            x2, h, gates, idx, counts = _mix_router(mix_args, _row(p['norm_ffn'][l]), wr, tm)
            if l != depth - 1:
                raise NotImplementedError("expert layer that is not the last layer")
            outs = _moe(x2, h, gates, idx, counts, p['moe_w_gate'][j].astype(BF16), p['moe_w_up'][j].astype(BF16),
                        p['moe_w_down'][j].astype(BF16), _row(p['norm_final']), tm, part_rows)
    return tuple(o.reshape(x.shape) for o, x in zip(outs, xs))


def kernel(x_prompt, x_sample, norm_mix, norm_ffn, norm_final, w_in, w_out, rwkv_mu, rwkv_w0, rwkv_w2,
           rwkv_a0, rwkv_a2, rwkv_g2, rwkv_kk, rwkv_ka, rwkv_rk, rwkv_ln, attn_q_norm, attn_k_norm,
           mlstm_conv_w, mlstm_conv_b, mlstm_i_bias, mlstm_f_bias, mlstm_ln, ret_ln, ffn_w_gate, ffn_w_up,
           ffn_w_down, moe_router, moe_w_gate, moe_w_up, moe_w_down):
    p = dict(norm_mix=norm_mix, norm_ffn=norm_ffn, norm_final=norm_final, w_in=w_in, w_out=w_out,
             rwkv_mu=rwkv_mu, rwkv_w0=rwkv_w0, rwkv_w2=rwkv_w2, rwkv_a0=rwkv_a0, rwkv_a2=rwkv_a2,
             rwkv_g2=rwkv_g2, rwkv_kk=rwkv_kk, rwkv_ka=rwkv_ka, rwkv_rk=rwkv_rk, rwkv_ln=rwkv_ln,
             attn_q_norm=attn_q_norm, attn_k_norm=attn_k_norm, mlstm_conv_w=mlstm_conv_w,
             mlstm_conv_b=mlstm_conv_b, mlstm_i_bias=mlstm_i_bias, mlstm_f_bias=mlstm_f_bias,
             mlstm_ln=mlstm_ln, ret_ln=ret_ln, ffn_w_gate=ffn_w_gate, ffn_w_up=ffn_w_up,
             ffn_w_down=ffn_w_down, moe_router=moe_router, moe_w_gate=moe_w_gate, moe_w_up=moe_w_up,
             moe_w_down=moe_w_down)
    return _trunk((x_prompt, x_sample), p)
```

```python
import functools
import math

import numpy as np
import jax
import jax.numpy as jnp
from jax import lax
from jax.experimental import pallas as pl
from jax.experimental.pallas import tpu as pltpu

F32 = jnp.float32
BF16 = jnp.bfloat16
MIXER_DTYPE = BF16

D_MODEL = 1024
HEAD_DIM = 64
W_G = 256
H_G = 4
KV_ATTN = 2
D_FF = 2816
N_EXPERTS = 8
NORM_EPS = 1e-6
HEAD_NORM_EPS = 1e-5
RWKV_GN_EPS = 64e-5
NEG_INF = -1e30
ROPE_THETA = 10000.0
GRID_W = 64

LANES = 128
ROW_ALIGN = 8
VMEM_LIMIT_BYTES = 56 * 1024 * 1024

U_RWKV = 0
U_MLSTM = 1024
U_RET = 2048
U_ATTN = 3072
U_GATE = 3584
U_COLS = 3712

RWKV_CHUNK = 64
RWKV_BLOCK = 256
RWKV_GROUP = 4
MIX_CHUNK = 128
MIX_BLOCK = 256
EXPERT_TILE = 256
EXPERT_FF_SPLIT = 1


def _cparams(sem):
    return pltpu.CompilerParams(dimension_semantics=sem, vmem_limit_bytes=VMEM_LIMIT_BYTES)


def _bdot(a, b, dims):
    return lax.dot_general(a, b, (dims, ((), ())), preferred_element_type=F32)


def _mm(a, b):
    return _bdot(a.astype(BF16), b.astype(BF16), ((1,), (0,)))


def _mm_nt(a, b):
    return _bdot(a.astype(BF16), b.astype(BF16), ((1,), (1,)))


def _mm_tn(a, b):
    return _bdot(a.astype(BF16), b.astype(BF16), ((0,), (0,)))


def _split2(a):
    hi = a.astype(BF16)
    lo = (a - hi.astype(F32)).astype(BF16)
    return hi, lo


def _split3(a):
    hi = a.astype(BF16)
    r = a - hi.astype(F32)
    mid = r.astype(BF16)
    lo = (r - mid.astype(F32)).astype(BF16)
    return hi, mid, lo


def _mm_l2(a, b_exact):
    hi, lo = _split2(a)
    return _bdot(hi, b_exact, ((1,), (0,))) + _bdot(lo, b_exact, ((1,), (0,)))


def _mm_l3(a, b_exact):
    h, m, l = _split3(a)
    return _bdot(h, b_exact, ((1,), (0,))) + _bdot(m, b_exact, ((1,), (0,))) + _bdot(l, b_exact, ((1,), (0,)))


def _mm_r3(a_exact, b):
    h, m, l = _split3(b)
    return _bdot(a_exact, h, ((1,), (0,))) + _bdot(a_exact, m, ((1,), (0,))) + _bdot(a_exact, l, ((1,), (0,)))


def _mm_x3(a, b):
    ah, al = _split2(a)
    bh, bl = _split2(b)
    d = ((1,), (0,))
    return _bdot(ah, bh, d) + _bdot(ah, bl, d) + _bdot(al, bh, d)


def _iota2(shape, axis):
    return lax.broadcasted_iota(jnp.int32, shape, axis)


def _head_mean_matrix(width):
    r = _iota2((width, width), 0) // HEAD_DIM
    c = _iota2((width, width), 1) // HEAD_DIM
    return jnp.where(r == c, 1.0 / HEAD_DIM, 0.0).astype(BF16)


def _head_mean(z, bd):
    return _bdot(z.astype(BF16), bd, ((1,), (0,)))


def _sigmoid(x):
    return 1.0 / (1.0 + jnp.exp(-x))


def _silu(x):
    return x * _sigmoid(x)


def _log_sigmoid(x):
    return jnp.minimum(x, 0.0) - jnp.log(1.0 + jnp.exp(-jnp.abs(x)))


def _rms_norm_rows(x, gain):
    ms = jnp.mean(x * x, axis=-1, keepdims=True)
    return x * lax.rsqrt(ms + NORM_EPS) * gain


def _rope_swap(z):
    w = z.shape[-1]
    lane = _iota2(z.shape, z.ndim - 1)
    fwd = pltpu.roll(z, w - 16, z.ndim - 1)
    bwd = pltpu.roll(z, 16, z.ndim - 1)
    return jnp.where((lane % 32) < 16, fwd, bwd)


def _rope(z, cos, sin):
    return z * cos + _rope_swap(z) * sin


def _shift_rows(x, prev_row, next_row):
    n = x.shape[0]
    row = _iota2(x.shape, 0)
    prev = jnp.where(row == 0, prev_row, pltpu.roll(x, 1, 0))
    nxt = jnp.where(row == n - 1, next_row, pltpu.roll(x, n - 1, 0))
    return prev, nxt


def _part_tiles(parts, tm):
    return tuple(p.shape[0] // tm for p in parts)


def _part_specs(parts, tm, width):
    specs, start = [], 0
    for nt in _part_tiles(parts, tm):
        specs.append(pl.BlockSpec((tm, width), lambda i, s=start, nt=nt: (jnp.clip(i - s, 0, nt - 1), 0)))
        start += nt
    return specs


def _part_tile(i, refs, tiles):
    x = refs[0][...]
    start = tiles[0]
    for ref, nt in zip(refs[1:], tiles[1:]):
        x = jnp.where(i >= start, ref[...], x)
        start += nt
    return x


def _inproj_kernel(*refs, tiles):
    x_refs, (g_ref, w_ref, o_ref) = refs[:len(tiles)], refs[len(tiles):]
    h = _rms_norm_rows(_part_tile(pl.program_id(0), x_refs, tiles), g_ref[...])
    o_ref[...] = jnp.dot(h.astype(BF16), w_ref[...], preferred_element_type=F32)


def _in_proj(x_parts, gain, w_pad, tm):
    tiles = _part_tiles(x_parts, tm)
    n = sum(tiles) * tm
    return pl.pallas_call(
        functools.partial(_inproj_kernel, tiles=tiles),
        out_shape=jax.ShapeDtypeStruct((n, U_COLS), F32),
        grid=(n // tm,),
        in_specs=_part_specs(x_parts, tm, D_MODEL) + [
            pl.BlockSpec((1, D_MODEL), lambda i: (0, 0)),
            pl.BlockSpec((D_MODEL, U_COLS), lambda i: (0, 0), pipeline_mode=pl.Buffered(1)),
        ],
        out_specs=pl.BlockSpec((tm, U_COLS), lambda i: (i, 0)),
        compiler_params=_cparams(("arbitrary",)),
        name="in_proj",
    )(*x_parts, gain, w_pad)


def _attn_kernel(u_ref, cos_ref, sin_ref, qg_ref, kg_ref, o_ref, q_scr, k_scr, v_scr, *, tq):
    t = u_ref.shape[1]
    u = u_ref[0]
    q = u[:, :W_G]
    k = u[:, W_G:W_G + KV_ATTN * HEAD_DIM]
    v = u[:, W_G + KV_ATTN * HEAD_DIM:]
    cos = cos_ref[...]
    sin = sin_ref[...]
    bd_q = _head_mean_matrix(W_G)
    bd_k = _head_mean_matrix(KV_ATTN * HEAD_DIM)
    qn = q * lax.rsqrt(_head_mean(q * q, bd_q) + NORM_EPS) * qg_ref[...]
    kn = k * lax.rsqrt(_head_mean(k * k, bd_k) + NORM_EPS) * kg_ref[...]
    kw = KV_ATTN * HEAD_DIM
    q_scr[...] = (_rope(qn, cos, sin) * (HEAD_DIM ** -0.5 * math.log2(math.e))).astype(BF16)
    k_scr[...] = _rope(kn, cos[:, :kw], sin[:, :kw]).astype(BF16)
    ones = jnp.ones((t, HEAD_DIM), BF16)
    for j in range(KV_ATTN):
        vj = v[:, j * HEAD_DIM:(j + 1) * HEAD_DIM].astype(BF16)
        v_scr[:, j * LANES:(j + 1) * LANES] = jnp.concatenate([vj, ones], axis=1)
    group = H_G // KV_ATTN

    def q_tile(i, carry):
        rows = pl.ds(pl.multiple_of(i * tq, tq), tq)

        def scores(j):
            q = jnp.concatenate([q_scr[rows, (j * group + g) * HEAD_DIM:(j * group + g + 1) * HEAD_DIM]
                                 for g in range(group)], axis=0)
            return _bdot(q, k_scr[:, j * HEAD_DIM:(j + 1) * HEAD_DIM], ((1,), (1,)))

        s_next = scores(0)
        for j in range(KV_ATTN):
            s = s_next
            if j + 1 < KV_ATTN:
                s_next = scores(j + 1)
            m = jnp.max(s, axis=-1, keepdims=True)
            p = jnp.exp2(s - m)
            r = jnp.dot(p.astype(BF16), v_scr[:, j * LANES:(j + 1) * LANES], preferred_element_type=F32)
            o = r[:, :HEAD_DIM] / r[:, HEAD_DIM:]
            for g in range(group):
                h = j * group + g
                o_ref[0, rows, h * HEAD_DIM:(h + 1) * HEAD_DIM] = o[g * tq:(g + 1) * tq]
        return carry

    lax.fori_loop(0, t // tq, q_tile, 0)


def _attention(u3, cos, sin, q_gain, k_gain):
    b, t, _ = u3.shape
    tq = min(256, t)
    col = U_ATTN // 512
    return pl.pallas_call(
        functools.partial(_attn_kernel, tq=tq),
        out_shape=jax.ShapeDtypeStruct((b, t, W_G), F32),
        grid=(b,),
        in_specs=[
            pl.BlockSpec((1, t, 512), lambda i: (i, 0, col)),
            pl.BlockSpec((t, W_G), lambda i: (0, 0)),
            pl.BlockSpec((t, W_G), lambda i: (0, 0)),
            pl.BlockSpec((1, W_G), lambda i: (0, 0)),
            pl.BlockSpec((1, KV_ATTN * HEAD_DIM), lambda i: (0, 0)),
        ],
        out_specs=pl.BlockSpec((1, t, W_G), lambda i: (i, 0, 0)),
        scratch_shapes=[
            pltpu.VMEM((t, W_G), BF16),
            pltpu.VMEM((t, KV_ATTN * HEAD_DIM), BF16),
            pltpu.VMEM((t, KV_ATTN * LANES), BF16),
        ],
        compiler_params=_cparams(("parallel",)),
        name="attention",
    )(u3, cos, sin, q_gain, k_gain)


def _ret_log_gamma(direction):
    return [math.log1p(-2.0 ** (-5.0 - (2 * h + direction) / 2.0)) for h in range(H_G)]


def _ret_kernel(uf_ref, ub_ref, cf_ref, sf_ref, cb_ref, sb_ref, of_ref, ob_ref, rf_scr, rb_scr):
    i = pl.program_id(1)
    c = MIX_CHUNK

    @pl.when(i == 0)
    def _():
        rf_scr[...] = jnp.zeros_like(rf_scr)
        rb_scr[...] = jnp.zeros_like(rb_scr)

    tt = _iota2((c, c), 0)
    ss = _iota2((c, c), 1)
    diff = (tt - ss).astype(F32)
    jcol = tt.astype(F32)
    first = ss < HEAD_DIM
    block_diag = (tt < HEAD_DIM) == first
    lg_f = _ret_log_gamma(0)
    lg_b = _ret_log_gamma(1)
    psl = lambda p: slice(p * LANES, (p + 1) * LANES)
    lane_lg = lambda lg, p: jnp.where(first, lg[2 * p], lg[2 * p + 1])
    row_decay = lambda lg, p: jnp.where(tt < HEAD_DIM, math.exp(c * lg[2 * p]), math.exp(c * lg[2 * p + 1]))
    own = lambda h, x: jnp.where(first, x, 0.0) if h % 2 == 0 else jnp.where(first, 0.0, x)
    pairs = range(H_G // 2)

    nch = uf_ref.shape[1] // c
    uf = uf_ref[0]
    qf_all = _rope(uf[:, :W_G], cf_ref[...], sf_ref[...])
    kf_all = _rope(uf[:, W_G:2 * W_G], cf_ref[...], sf_ref[...]) * HEAD_DIM ** -0.5
    ub = ub_ref[0]
    qb_all = _rope(ub[:, :W_G], cb_ref[...], sb_ref[...])
    kb_all = _rope(ub[:, W_G:2 * W_G], cb_ref[...], sb_ref[...]) * HEAD_DIM ** -0.5
    rf_prev = [rf_scr[p] for p in pairs]
    rb_prev = [rb_scr[p] for p in pairs]
    decay = [jnp.where(tt >= ss, jnp.exp(diff * lg_f[h]), 0.0) + jnp.where(ss >= tt, jnp.exp(-diff * lg_b[h]), 0.0)
             for h in range(H_G)]
    for n in range(nch):
        rf_rows = slice(n * c, (n + 1) * c)
        rb_rows = slice((nch - 1 - n) * c, (nch - n) * c)
        qf, kf, vf = qf_all[rf_rows], kf_all[rf_rows], uf[rf_rows, 2 * W_G:3 * W_G]
        qb, kb, vb = qb_all[rb_rows], kb_all[rb_rows], ub[rb_rows, 2 * W_G:3 * W_G]
        qk = [_mm_nt(qf[:, psl(h // 2)], own(h, kf[:, psl(h // 2)])) for h in range(H_G)]
        intra = [_mm(qk[h] * decay[h], own(h, vf[:, psl(h // 2)])) for h in range(H_G)]
        inter_f = [_mm(qf[:, psl(p)] * jnp.exp((jcol + 1.0) * lane_lg(lg_f, p)), rf_prev[p]) for p in pairs]
        inter_b = [_mm(qb[:, psl(p)] * jnp.exp((c - jcol) * lane_lg(lg_b, p)), rb_prev[p]) for p in pairs]
        upd_f = [_mm_tn(kf[:, psl(p)] * jnp.exp((c - 1.0 - jcol) * lane_lg(lg_f, p)), vf[:, psl(p)]) for p in pairs]
        upd_b = [_mm_tn(kb[:, psl(p)] * jnp.exp(jcol * lane_lg(lg_b, p)), vb[:, psl(p)]) for p in pairs]
        for p in pairs:
            of_ref[0, rf_rows, psl(p)] = (intra[2 * p] + intra[2 * p + 1] + inter_f[p]).astype(of_ref.dtype)
            ob_ref[0, rb_rows, psl(p)] = inter_b[p].astype(ob_ref.dtype)
        rf_prev = [row_decay(lg_f, p) * rf_prev[p] + jnp.where(block_diag, upd_f[p], 0.0) for p in pairs]
        rb_prev = [row_decay(lg_b, p) * rb_prev[p] + jnp.where(block_diag, upd_b[p], 0.0) for p in pairs]
    for p in pairs:
        rf_scr[p] = rf_prev[p]
        rb_scr[p] = rb_prev[p]


def _retention(u3, cos, sin):
    b, t, _ = u3.shape
    c = min(MIX_BLOCK, t)
    nblk = t // c
    col = U_RET // 1024
    tab = lambda rev: pl.BlockSpec((c, W_G), (lambda bi, i: (nblk - 1 - i, 0)) if rev else (lambda bi, i: (i, 0)))
    return pl.pallas_call(
        _ret_kernel,
        out_shape=(jax.ShapeDtypeStruct((b, t, W_G), MIXER_DTYPE), jax.ShapeDtypeStruct((b, t, W_G), MIXER_DTYPE)),
        grid=(b, nblk),
        in_specs=[
            pl.BlockSpec((1, c, 1024), lambda bi, i: (bi, i, col)),
            pl.BlockSpec((1, c, 1024), lambda bi, i: (bi, nblk - 1 - i, col)),
            tab(False), tab(False), tab(True), tab(True),
        ],
        out_specs=(
            pl.BlockSpec((1, c, W_G), lambda bi, i: (bi, i, 0)),
            pl.BlockSpec((1, c, W_G), lambda bi, i: (bi, nblk - 1 - i, 0)),
        ),
        scratch_shapes=[pltpu.VMEM((H_G // 2, LANES, LANES), F32), pltpu.VMEM((H_G // 2, LANES, LANES), F32)],
        compiler_params=_cparams(("parallel", "arbitrary")),
        name="retention",
    )(u3, u3, cos, sin, cos, sin)


def _mlstm_tile(u_ref, up_ref, un_ref, g_ref, cw_ref, cb_ref, gb_ref, blk, nblk, direction):
    c = MIX_CHUNK
    reverse = direction == 1
    u = u_ref[0]
    qk = u[:, :2 * W_G]
    prev_row = jnp.where(blk == 0, 0.0, up_ref[0][ROW_ALIGN - 1:ROW_ALIGN, :])
    next_row = jnp.where(blk == nblk - 1, 0.0, un_ref[0][0:1, :])
    prev, nxt = _shift_rows(qk, prev_row, next_row)
    cw = cw_ref[...]
    qk = _silu(cw[0:1] * prev + cw[1:2] * qk + cw[2:3] * nxt + cb_ref[...])
    qa = qk[:, :W_G]
    ka = qk[:, W_G:] * HEAD_DIM ** -0.5
    va = u[:, 2 * W_G:3 * W_G]

    x = g_ref[0] + gb_ref[...]
    xt = x.T
    lf_c = _log_sigmoid(x)
    lf_r = _log_sigmoid(xt)
    tt = _iota2((c, c), 0)
    ss = _iota2((c, c), 1)
    lower = jnp.where(ss <= tt, 1.0, 0.0).astype(BF16)
    upper = jnp.where(tt <= ss, 1.0, 0.0).astype(BF16)
    if reverse:
        b_c = _mm_r3(upper, lf_c)
        b_r = _mm_l3(lf_r, lower)
        mask = ss >= tt
    else:
        b_c = _mm_r3(lower, lf_c)
        b_r = _mm_l3(lf_r, upper)
        mask = ss <= tt
    g_all = jnp.sum(lf_c, axis=0, keepdims=True)
    return dict(q=qa, k=ka, v=va, x=x, xt=xt, b_c=b_c, b_r=b_r, g_all=g_all, mask=mask)


def _mlstm_select_matrix():
    sel = np.zeros((2, 2 * LANES, 8 * LANES), np.float32)
    for d in range(2):
        for h in range(H_G):
            ci, cf = d * H_G + h, 2 * H_G + d * H_G + h
            p, j = divmod(h, 2)
            sel[d, cf, LANES * h:LANES * (h + 1)] = 1.0
            sel[d, cf, 4 * LANES + LANES * p + HEAD_DIM * j:4 * LANES + LANES * p + HEAD_DIM * (j + 1)] = 1.0
            sel[d, LANES + ci, 6 * LANES + LANES * p + HEAD_DIM * j:6 * LANES + LANES * p + HEAD_DIM * (j + 1)] = 1.0
    return jnp.asarray(sel, BF16)


def _mlstm_kernel(uf_ref, upf_ref, unf_ref, gf_ref, ub_ref, upb_ref, unb_ref, gbk_ref,
                  cw_ref, cb_ref, gb_ref, sel_ref, of_ref, ob_ref, st_scr, m_scr):
    i = pl.program_id(1)
    nblk = pl.num_programs(1)
    c = MIX_CHUNK

    @pl.when(i == 0)
    def _():
        st_scr[...] = jnp.zeros_like(st_scr)
        m_scr[...] = jnp.zeros_like(m_scr)

    tiles = (_mlstm_tile(uf_ref, upf_ref, unf_ref, gf_ref, cw_ref, cb_ref, gb_ref, i, nblk, 0),
             _mlstm_tile(ub_ref, upb_ref, unb_ref, gbk_ref, cw_ref, cb_ref, gb_ref, nblk - 1 - i, nblk, 1))
    o_refs = (of_ref, ob_ref)
    sel = [_mm_l2(jnp.concatenate([tiles[d]["b_c"], tiles[d]["x"]], axis=1), sel_ref[d]) for d in range(2)]
    first = _iota2((c, LANES), 1) < HEAD_DIM
    row_first = _iota2((LANES, 2 * LANES), 0) < HEAD_DIM
    lane2 = _iota2((LANES, 2 * LANES), 1) % LANES < HEAD_DIM
    block_diag = row_first == lane2
    ones = jnp.ones((c, LANES), F32)
    pairs = [(d, p) for d in range(2) for p in range(H_G // 2)]
    heads = [(d, h) for d in range(2) for h in range(H_G)]
    psl = lambda p: slice(p * LANES, (p + 1) * LANES)
    q_pair = {dp: tiles[dp[0]]["q"][:, psl(dp[1])] for dp in pairs}
    k_pair = {dp: tiles[dp[0]]["k"][:, psl(dp[1])] for dp in pairs}
    v_pair = {dp: tiles[dp[0]]["v"][:, psl(dp[1])] for dp in pairs}
    state = {dp: st_scr[n] for n, dp in enumerate(pairs)}
    m_row = {dp: m_scr[n:n + 1, :] for n, dp in enumerate(pairs)}

    def own(d, h, x):
        return jnp.where(first, x, 0.0) if h % 2 == 0 else jnp.where(first, 0.0, x)

    qk = [_mm_nt(q_pair[(d, h // 2)], own(d, h, k_pair[(d, h // 2)])) for d, h in heads]
    qs = {dp: _mm(q_pair[dp], state[dp]) for dp in pairs}
    bc = [sel[d][:, LANES * h:LANES * (h + 1)] for d, h in heads]
    m_prev = [m_row[(d, h // 2)][:, HEAD_DIM * (h % 2):HEAD_DIM * (h % 2) + 1] for d, h in heads]
    dlog = []
    for (d, h), bc_ in zip(heads, bc):
        ci, cf = d * H_G + h, 2 * H_G + d * H_G + h
        rowterm = tiles[d]["xt"][ci:ci + 1, :] - tiles[d]["b_r"][cf:cf + 1, :]
        dlog.append(jnp.where(tiles[d]["mask"], bc_ + rowterm, NEG_INF))
    inter_log = [bc_ + m_ for bc_, m_ in zip(bc, m_prev)]
    m_t = [jnp.maximum(il, jnp.max(dl, axis=-1, keepdims=True)) for il, dl in zip(inter_log, dlog)]
    sc = [qk_ * jnp.exp(dl - mt) for qk_, dl, mt in zip(qk, dlog, m_t)]
    w_inter = [jnp.exp(il - mt) for il, mt in zip(inter_log, m_t)]
    e_neg = [jnp.exp(-mt) for mt in m_t]
    res = []
    for n, (d, h) in enumerate(heads):
        v_aug = jnp.concatenate([own(d, h, v_pair[(d, h // 2)]), own(d, h, ones)], axis=1)
        res.append(_mm(sc[n], v_aug))
    for n, (d, p) in enumerate(pairs):
        a, b_ = 2 * n, 2 * n + 1
        tot = res[a] + res[b_] + jnp.tile(jnp.where(first, w_inter[a], w_inter[b_]), (1, 2)) * qs[(d, p)]
        den = jnp.maximum(jnp.abs(tot[:, LANES:]), jnp.where(first, e_neg[a], e_neg[b_]))
        o_refs[d][0, :, psl(p)] = (tot[:, :LANES] / den).astype(o_refs[d].dtype)

    for n, (d, p) in enumerate(pairs):
        bcp = sel[d][:, 4 * LANES + LANES * p:4 * LANES + LANES * (p + 1)]
        lip = sel[d][:, 6 * LANES + LANES * p:6 * LANES + LANES * (p + 1)]
        g_row = bcp[0:1, :] if d == 1 else bcp[c - 1:c, :]
        a_p = g_row - bcp + lip
        m_new = jnp.maximum(g_row + m_row[(d, p)], jnp.max(a_p, axis=0, keepdims=True))
        dec = jnp.exp(g_row + m_row[(d, p)] - m_new)
        kw_t = (k_pair[(d, p)] * jnp.exp(a_p - m_new)).T
        upd = _mm(kw_t, jnp.concatenate([v_pair[(d, p)], ones], axis=1))
        dec_tile = jnp.where(row_first, dec[:, 0:1], dec[:, HEAD_DIM:HEAD_DIM + 1])
        st_scr[n] = dec_tile * state[(d, p)] + jnp.where(block_diag, upd, 0.0)
        m_scr[n:n + 1, :] = m_new


def _mlstm(u3, conv_w, conv_b, gate_bias):
    b, t, _ = u3.shape
    c = MIX_CHUNK
    nblk = t // c
    rpb = c // ROW_ALIGN
    n8 = t // ROW_ALIGN
    col = U_MLSTM // 1024
    hcol = U_MLSTM // 512
    gcol = U_GATE // LANES

    def specs(rev):
        blk = (lambda i: nblk - 1 - i) if rev else (lambda i: i)
        return [
            pl.BlockSpec((1, c, 1024), lambda bi, i: (bi, blk(i), col)),
            pl.BlockSpec((1, ROW_ALIGN, 512), lambda bi, i: (bi, jnp.maximum(blk(i) * rpb - 1, 0), hcol)),
            pl.BlockSpec((1, ROW_ALIGN, 512), lambda bi, i: (bi, jnp.minimum((blk(i) + 1) * rpb, n8 - 1), hcol)),
            pl.BlockSpec((1, c, LANES), lambda bi, i: (bi, blk(i), gcol)),
        ]

    const = lambda shape: pl.BlockSpec(shape, lambda bi, i: (0,) * len(shape))
    return pl.pallas_call(
        _mlstm_kernel,
        out_shape=(jax.ShapeDtypeStruct((b, t, W_G), MIXER_DTYPE), jax.ShapeDtypeStruct((b, t, W_G), MIXER_DTYPE)),
        grid=(b, nblk),
        in_specs=specs(False) + specs(True) + [const((3, 2 * W_G)), const((1, 2 * W_G)), const((1, LANES)),
                                               const((2, 2 * LANES, 8 * LANES))],
        out_specs=(
            pl.BlockSpec((1, c, W_G), lambda bi, i: (bi, i, 0)),
            pl.BlockSpec((1, c, W_G), lambda bi, i: (bi, nblk - 1 - i, 0)),
        ),
        scratch_shapes=[pltpu.VMEM((H_G, LANES, 2 * LANES), F32), pltpu.VMEM((ROW_ALIGN, LANES), F32)],
        compiler_params=_cparams(("parallel", "arbitrary")),
        name="mlstm",
    )(u3, u3, u3, u3, u3, u3, u3, u3, conv_w, conv_b, gate_bias, _mlstm_select_matrix())


def _rwkv_prep_kernel(u_ref, up_ref, un_ref, mu_ref, w0_ref, w2_ref, a0_ref, a2_ref, g2_ref, kks_ref, ka_ref, rk_ref,
                      r_ref, v_ref, kk_ref, g_ref, bonus_ref, lwf_ref, lwb_ref, kf_ref, kb_ref, bf_ref, bb_ref):
    i = pl.program_id(1)
    nblk = pl.num_programs(1)
    u = u_ref[0]
    prev_row = jnp.where(i == 0, 0.0, up_ref[0][ROW_ALIGN - 1:ROW_ALIGN, :])
    next_row = jnp.where(i == nblk - 1, 0.0, un_ref[0][0:1, :])
    prev, nxt = _shift_rows(u, prev_row, next_row)
    us = u + mu_ref[...] * (0.5 * (prev + nxt) - u)
    r = us[:, 0:W_G]
    k = us[:, W_G:2 * W_G]
    v = us[:, 2 * W_G:3 * W_G]
    xw = us[:, 3 * W_G:3 * W_G + 64]
    xa = us[:, 3 * W_G + 64:3 * W_G + 128]
    xg = us[:, 3 * W_G + 128:]
    bd = _head_mean_matrix(W_G)
    g = _mm(_sigmoid(xg), g2_ref[...])
    lw = jnp.tanh(xw)
    a_lr = _mm_x3(xa, a2_ref[...])
    kk = k * kks_ref[...]
    kk = kk * lax.rsqrt(_head_mean(kk * kk, bd) * HEAD_DIM + 1e-12)
    r_ref[0] = r.astype(r_ref.dtype)
    v_ref[0] = v.astype(v_ref.dtype)
    kk_ref[0] = kk.astype(kk_ref.dtype)
    g_ref[0] = g.astype(g_ref.dtype)
    bonus = jnp.zeros_like(r)
    for d, (lw_ref, k_ref, b_ref) in enumerate(((lwf_ref, kf_ref, bf_ref), (lwb_ref, kb_ref, bb_ref))):
        z = w0_ref[d:d + 1, :] + _mm_x3(lw, w2_ref[d])
        lw_ref[0] = -_sigmoid(z) * math.exp(-0.5)
        a = _sigmoid(a0_ref[d:d + 1, :] + a_lr)
        kd = k * (1.0 + (a - 1.0) * ka_ref[...])
        k_ref[0] = kd.astype(k_ref.dtype)
        b_ref[0] = (kk * a).astype(b_ref.dtype)
        bonus = bonus + _head_mean(r * kd * rk_ref[...], bd) * HEAD_DIM * v
    bonus_ref[0] = bonus.astype(bonus_ref.dtype)


def _rwkv_prep(u3, mu, w0, w2, a0, a2, g2, kks, ka, rk):
    b, t, _ = u3.shape
    tb = min(RWKV_BLOCK, t)
    nblk = t // tb
    rpb = tb // ROW_ALIGN
    n8 = t // ROW_ALIGN
    const = lambda shape: pl.BlockSpec(shape, lambda bi, i: (0,) * len(shape))
    out = lambda dtype: jax.ShapeDtypeStruct((b, t, W_G), dtype)
    ospec = pl.BlockSpec((1, tb, W_G), lambda bi, i: (bi, i, 0))
    return pl.pallas_call(
        _rwkv_prep_kernel,
        out_shape=(out(MIXER_DTYPE),) * 3 + (out(MIXER_DTYPE),) * 2 + (out(F32),) * 2 + (out(MIXER_DTYPE),) * 4,
        grid=(b, nblk),
        in_specs=[
            pl.BlockSpec((1, tb, 1024), lambda bi, i: (bi, i, 0)),
            pl.BlockSpec((1, ROW_ALIGN, 1024), lambda bi, i: (bi, jnp.maximum(i * rpb - 1, 0), 0)),
            pl.BlockSpec((1, ROW_ALIGN, 1024), lambda bi, i: (bi, jnp.minimum((i + 1) * rpb, n8 - 1), 0)),
            const((1, 1024)), const((2, W_G)), const((2, 64, W_G)), const((2, W_G)), const((64, W_G)),
            const((128, W_G)), const((1, W_G)), const((1, W_G)), const((1, W_G)),
        ],
        out_specs=(ospec,) * 11,
        compiler_params=_cparams(("parallel", "parallel")),
        name="rwkv_prep",
    )(u3, u3, u3, mu, w0, w2, a0, a2, g2, kks, ka, rk)


def _tri_inverse_all(lmats, n):
    r = _iota2((n, n), 0)
    c = _iota2((n, n), 1)
    eye = jnp.where(r == c, 1.0, 0.0)
    pair = (r // 2 == c // 2) & (r != c)
    invs = [eye + jnp.where(pair, lm, 0.0) for lm in lmats]
    s = 2
    while s < n:
        sel = (r // (2 * s) == c // (2 * s)) & (r // s != c // s)
        offs = [jnp.where(sel, -lm, 0.0) for lm in lmats]
        xs = [_mm(inv, off) for inv, off in zip(invs, offs)]
        invs = [inv - _mm(x, inv) for inv, x in zip(invs, xs)]
        s *= 2
    return invs


def _rwkv_tile_terms(r, k, v, kk, b, lw, reverse):
    c = r.shape[0]
    tt = _iota2((c, c), 0)
    ss = _iota2((c, c), 1)
    tri = jnp.where((tt <= ss) if reverse else (ss <= tt), 1.0, 0.0).astype(BF16)
    cum_in = _mm_r3(tri, lw)
    cum_all = jnp.sum(lw, axis=0, keepdims=True)
    e_neg = jnp.exp(-cum_in)
    e_end = jnp.exp(cum_all - cum_in)
    return dict(at=-kk * jnp.exp(cum_in - lw), rt=r * jnp.exp(cum_in), bt=b * e_neg, kt=k * e_neg,
                gb=b * e_end, gk=k * e_end, v=v, e_all=jnp.exp(cum_all))


def _rwkv_chunk_terms(tiles, reverses):
    c = RWKV_CHUNK
    tt = _iota2((c, c), 0)
    ss = _iota2((c, c), 1)
    heads = [(ti, h) for ti in range(len(tiles)) for h in range(H_G)]
    sl = lambda h: slice(h * HEAD_DIM, (h + 1) * HEAD_DIM)
    get = lambda name: [tiles[ti][name][:, sl(h)] for ti, h in heads]
    at, rt, bt, kt, gb, gk, v = (get(nm) for nm in ("at", "rt", "bt", "kt", "gb", "gk", "v"))
    strict = [(ss > tt) if reverses[ti] else (ss < tt) for ti, _ in heads]
    incl = [(ss >= tt) if reverses[ti] else (ss <= tt) for ti, _ in heads]
    ps = [_mm_nt(jnp.concatenate([a, r_], axis=0), jnp.concatenate([b_, k_], axis=0))
          for a, r_, b_, k_ in zip(at, rt, bt, kt)]
    l_ab = [jnp.where(m, p[:c, :c], 0.0) for m, p in zip(strict, ps)]
    l_ak = [jnp.where(m, p[:c, c:], 0.0) for m, p in zip(strict, ps)]
    m_r = [jnp.concatenate([jnp.where(m, p[c:, :c], 0.0), jnp.where(m, p[c:, c:], 0.0)], axis=1)
           for m, p in zip(incl, ps)]
    lakv = [_mm(l, v_) for l, v_ in zip(l_ak, v)]
    invs = _tri_inverse_all(l_ab, c)
    tw = [_mm(inv, jnp.concatenate([a, lv], axis=1)) for inv, a, lv in zip(invs, at, lakv)]
    zeros = jnp.zeros((c, HEAD_DIM), F32)
    mm2 = [_mm(m, jnp.concatenate([t_, jnp.concatenate([zeros, v_], axis=1)], axis=0))
           for m, t_, v_ in zip(m_r, tw, v)]
    r1 = [r_ + m[:, :HEAD_DIM] for r_, m in zip(rt, mm2)]
    y0 = [m[:, HEAD_DIM:] for m in mm2]
    twg = [_mm_tn(t_, g_) for t_, g_ in zip(tw, gb)]
    vgk = [_mm_tn(v_, g_) for v_, g_ in zip(v, gk)]
    mlow = [x[:HEAD_DIM] for x in twg]
    nadd = [x[HEAD_DIM:] + y for x, y in zip(twg, vgk)]
    e_all = [tiles[ti]["e_all"][:, sl(h)] for ti, h in heads]
    return r1, y0, mlow, nadd, e_all


def _rwkv_core_kernel(rf_ref, kf_ref, vf_ref, kkf_ref, bf_ref, lwf_ref, rb_ref, kb_ref, vb_ref, kkb_ref, bb_ref,
                      lwb_ref, yf_ref, yb_ref, s_scr, *, group):
    i = pl.program_id(1)
    c = RWKV_CHUNK
    nch = rf_ref.shape[1] // c
    dirs = ((rf_ref, kf_ref, vf_ref, kkf_ref, bf_ref, lwf_ref), (rb_ref, kb_ref, vb_ref, kkb_ref, bb_ref, lwb_ref))
    y_refs = (yf_ref, yb_ref)

    @pl.when(i == 0)
    def _():
        s_scr[...] = jnp.zeros_like(s_scr)

    def step(j, states):
        tiles, reverses, rows = [], [], []
        for q in range(group):
            for d in range(2):
                cj = j * group + q
                cj = cj if d == 0 else nch - 1 - cj
                rw = pl.ds(pl.multiple_of(cj * c, c), c)
                tiles.append(_rwkv_tile_terms(*(ref[0, rw, :].astype(F32) for ref in dirs[d]), d == 1))
                reverses.append(d == 1)
                rows.append(rw)
        r1, y0, mlow, nadd, e_all = _rwkv_chunk_terms(tiles, reverses)
        states = list(states)
        for q in range(group):
            ys = [[], []]
            for d in range(2):
                for h in range(H_G):
                    n = (q * 2 + d) * H_G + h
                    s = states[d * H_G + h]
                    ys[d].append(_mm_nt(r1[n], s) + y0[n])
                    states[d * H_G + h] = s * e_all[n] + _mm(s, mlow[n]) + nadd[n]
            for d in range(2):
                y_refs[d][0, rows[q * 2 + d], :] = jnp.concatenate(ys[d], axis=1).astype(y_refs[d].dtype)
        return tuple(states)

    init = tuple(s_scr[n] for n in range(2 * H_G))
    if nch == group:
        states = step(0, init)
    else:
        states = lax.fori_loop(0, nch // group, step, init)
    for n in range(2 * H_G):
        s_scr[n] = states[n]


def _rwkv_core(r, v, kk, lwf, lwb, kf, kb, bf, bb):
    b, t, _ = r.shape
    tb = min(RWKV_BLOCK, t)
    nblk = t // tb
    fwd = pl.BlockSpec((1, tb, W_G), lambda bi, i: (bi, i, 0))
    bwd = pl.BlockSpec((1, tb, W_G), lambda bi, i: (bi, nblk - 1 - i, 0))
    out = jax.ShapeDtypeStruct((b, t, W_G), MIXER_DTYPE)
    return pl.pallas_call(
        functools.partial(_rwkv_core_kernel, group=min(RWKV_GROUP, tb // RWKV_CHUNK)),
        out_shape=(out, out),
        grid=(b, nblk),
        in_specs=[fwd] * 6 + [bwd] * 6,
        out_specs=(fwd, bwd),
        scratch_shapes=[pltpu.VMEM((2 * H_G, HEAD_DIM, HEAD_DIM), F32)],
        compiler_params=_cparams(("parallel", "arbitrary")),
        name="rwkv_core",
    )(r, kf, v, kk, bf, lwf, r, kb, v, kk, bb, lwb)


def _group_norm(y, gain, eps, bd):
    mu = _head_mean(y, bd)
    d = y - mu
    var = _head_mean(d * d, bd)
    return d * lax.rsqrt(var + eps) * gain


N_MIX_REFS = 15


def _mix_residual(refs, tiles):
    x_refs = refs[:len(tiles)]
    (ryf_ref, ryb_ref, rbon_ref, rg_ref, at_ref, mhf_ref, mhb_ref, mo_ref, tof_ref, tob_ref,
     tg_ref, rln_ref, mln_ref, tln_ref, w_ref) = refs[len(tiles):]
    bd = _head_mean_matrix(W_G)
    f = lambda ref: ref[...].astype(F32)
    o_a = (_group_norm(f(ryf_ref) + f(ryb_ref), rln_ref[...], RWKV_GN_EPS, bd) + f(rbon_ref)) * f(rg_ref)
    o_c = _group_norm(f(mhf_ref) + f(mhb_ref), mln_ref[...], HEAD_NORM_EPS, bd) * _sigmoid(mo_ref[...])
    o_d = _group_norm(f(tof_ref) + f(tob_ref), tln_ref[...], HEAD_NORM_EPS, bd) * _silu(tg_ref[...])
    mix = jnp.concatenate([o_a, at_ref[...], o_c, o_d], axis=1).astype(BF16)
    return _part_tile(pl.program_id(0), x_refs, tiles) + jnp.dot(mix, w_ref[...], preferred_element_type=F32)


def _mix_call(kernel, x_parts, rw, at, ml, rt, u2, rln, mln, tln, w_out, tm, extra, extra_specs, out_shape, out_specs,
              scratch_shapes, name, lag=0):
    tiles = _part_tiles(x_parts, tm)
    nt = sum(tiles)
    row = lambda w: pl.BlockSpec((tm, w), lambda i: (jnp.minimum(i, nt - 1), 0))
    ucol = lambda off: pl.BlockSpec((tm, W_G), lambda i: (jnp.minimum(i, nt - 1), off // W_G))
    const = lambda shape: pl.BlockSpec(shape, lambda i: (0, 0))
    return pl.pallas_call(
        functools.partial(kernel, tiles=tiles),
        out_shape=out_shape,
        grid=(nt + lag,),
        in_specs=_part_specs(x_parts, tm, D_MODEL) + [row(W_G)] * 7 + [ucol(U_MLSTM + 3 * W_G)] + [row(W_G)] * 2
                 + [ucol(U_RET + 3 * W_G)] + [const((1, W_G))] * 3
                 + [pl.BlockSpec((D_MODEL, D_MODEL), lambda i: (0, 0), pipeline_mode=pl.Buffered(1))] + extra_specs,
        out_specs=out_specs,
        scratch_shapes=scratch_shapes,
        compiler_params=_cparams(("arbitrary",)),
        name=name,
    )(*x_parts, *rw, at, *ml, u2, *rt, u2, rln, mln, tln, w_out, *extra)


def _mix_ffn_kernel(*refs, tiles):
    n_in = len(tiles) + N_MIX_REFS
    g_ref, wg_ref, wu_ref, wd_ref, o_ref, x_scr = refs[n_in:]

    @pl.when(pl.program_id(0) == 0)
    def _():
        x_scr[...] = jnp.zeros_like(x_scr)

    x = x_scr[...]
    h = _rms_norm_rows(x, g_ref[...]).astype(BF16)
    a = jnp.dot(h, wg_ref[...], preferred_element_type=F32)
    b = jnp.dot(h, wu_ref[...], preferred_element_type=F32)
    z = (_silu(a) * b).astype(BF16)
    o_ref[...] = x + jnp.dot(z, wd_ref[...], preferred_element_type=F32)
    x_scr[...] = _mix_residual(refs[:n_in], tiles)


def _mix_ffn(mix_args, gain, wg, wu, wd, tm):
    n = sum(_part_tiles(mix_args[0], tm)) * tm
    res = lambda shape: pl.BlockSpec(shape, lambda i: (0, 0), pipeline_mode=pl.Buffered(1))
    return _mix_call(_mix_ffn_kernel, *mix_args, tm, (gain, wg, wu, wd),
                     [pl.BlockSpec((1, D_MODEL), lambda i: (0, 0)), res((D_MODEL, D_FF)), res((D_MODEL, D_FF)),
                      res((D_FF, D_MODEL))],
                     jax.ShapeDtypeStruct((n, D_MODEL), F32),
                     pl.BlockSpec((tm, D_MODEL), lambda i: (jnp.maximum(i - 1, 0), 0)),
                     [pltpu.VMEM((tm, D_MODEL), F32)], "mix_ffn", lag=1)


def _mix_router_kernel(*refs, tiles):
    n_in = len(tiles) + N_MIX_REFS
    g_ref, wr_ref, x_ref, h_ref, gate_ref, idx_ref, cnt_ref, cnt_scr = refs[n_in:]

    @pl.when(pl.program_id(0) == 0)
    def _():
        cnt_scr[...] = jnp.zeros_like(cnt_scr)

    x = _mix_residual(refs[:n_in], tiles)
    x_ref[...] = x
    h = _rms_norm_rows(x, g_ref[...])
    h_ref[...] = h.astype(BF16)
    logits = _mm_x3(h, wr_ref[...])
    tm = logits.shape[0]
    lt = logits.T[:N_EXPERTS, :]
    sub = _iota2(lt.shape, 0)
    e = jnp.exp(lt - jnp.max(lt, axis=0, keepdims=True))
    p = e / jnp.sum(e, axis=0, keepdims=True)
    m1 = jnp.max(p, axis=0, keepdims=True)
    i1 = jnp.min(jnp.where(p == m1, sub, N_EXPERTS), axis=0, keepdims=True)
    p2 = jnp.where(sub == i1, -1.0, p)
    m2 = jnp.max(p2, axis=0, keepdims=True)
    i2 = jnp.min(jnp.where(p2 == m2, sub, N_EXPERTS), axis=0, keepdims=True)
    tot = m1 + m2
    gates_t = jnp.where(sub == 0, m1 / tot, jnp.where(sub == 1, m2 / tot, 0.0))
    gate_ref[...] = jnp.concatenate([gates_t, jnp.zeros((LANES - N_EXPERTS, tm), F32)], axis=0).T
    chosen = jnp.where((sub == i1) | (sub == i2), 1.0, 0.0)
    triu = jnp.where(_iota2((tm, tm), 0) <= _iota2((tm, tm), 1), 1.0, 0.0).astype(BF16)
    incl = jnp.dot(chosen.astype(BF16), triu, preferred_element_type=F32)
    rank = jnp.tile(cnt_scr[...], (1, tm // LANES)) + incl - chosen
    r1 = jnp.sum(jnp.where(sub == i1, rank, 0.0), axis=0, keepdims=True)
    r2 = jnp.sum(jnp.where(sub == i2, rank, 0.0), axis=0, keepdims=True)
    idx_ref[...] = jnp.where(sub == 0, i1.astype(F32), jnp.where(
        sub == 1, i2.astype(F32), jnp.where(sub == 2, r1, jnp.where(sub == 3, r2, 0.0)))).astype(jnp.int32)
    cnt_scr[...] = cnt_scr[...] + jnp.sum(chosen, axis=1, keepdims=True)
    cnt_ref[...] = cnt_scr[...].astype(jnp.int32)


def _mix_router(mix_args, gain, wr_pad, tm):
    n = sum(_part_tiles(mix_args[0], tm)) * tm
    return _mix_call(
        _mix_router_kernel, *mix_args, tm, (gain, wr_pad),
        [pl.BlockSpec((1, D_MODEL), lambda i: (0, 0)), pl.BlockSpec((D_MODEL, LANES), lambda i: (0, 0))],
        (jax.ShapeDtypeStruct((n, D_MODEL), F32), jax.ShapeDtypeStruct((n, D_MODEL), BF16),
         jax.ShapeDtypeStruct((n, LANES), F32), jax.ShapeDtypeStruct((ROW_ALIGN, n), jnp.int32),
         jax.ShapeDtypeStruct((ROW_ALIGN, LANES), jnp.int32)),
        (pl.BlockSpec((tm, D_MODEL), lambda i: (i, 0)), pl.BlockSpec((tm, D_MODEL), lambda i: (i, 0)),
         pl.BlockSpec((tm, LANES), lambda i: (i, 0)), pl.BlockSpec((ROW_ALIGN, tm), lambda i: (0, i)),
         pl.BlockSpec((ROW_ALIGN, LANES), lambda i: (0, 0))),
        [pltpu.VMEM((N_EXPERTS, LANES), F32)], "mix_router")


def _expert_ffn_kernel(te_ref, nv_ref, xs_ref, wg_ref, wu_ref, wd_ref, o_ref):
    i = pl.program_id(0)

    @pl.when(i < nv_ref[0])
    def _():
        h = xs_ref[...]
        fw = D_FF // EXPERT_FF_SPLIT
        acc = None
        for f in range(EXPERT_FF_SPLIT):
            cols = slice(f * fw, (f + 1) * fw)
            a = jnp.dot(h, wg_ref[0, :, cols], preferred_element_type=F32)
            b = jnp.dot(h, wu_ref[0, :, cols], preferred_element_type=F32)
            z = (_silu(a) * b).astype(BF16)
            y = jnp.dot(z, wd_ref[0, cols, :], preferred_element_type=F32)
            acc = y if acc is None else acc + y
        o_ref[...] = acc.astype(o_ref.dtype)

    @pl.when(i >= nv_ref[0])
    def _():
        o_ref[...] = jnp.zeros_like(o_ref)


def _expert_ffn(xs, tile_expert, n_valid, wg, wu, wd, tm):
    rows = xs.shape[0]
    wspec = lambda shape: pl.BlockSpec((1,) + shape, lambda i, te, nv: (te[i], 0, 0))
    return pl.pallas_call(
        _expert_ffn_kernel,
        out_shape=jax.ShapeDtypeStruct((rows, D_MODEL), BF16),
        grid_spec=pltpu.PrefetchScalarGridSpec(
            num_scalar_prefetch=2,
            grid=(rows // tm,),
            in_specs=[pl.BlockSpec((tm, D_MODEL), lambda i, te, nv: (i, 0)),
                      wspec((D_MODEL, D_FF)), wspec((D_MODEL, D_FF)), wspec((D_FF, D_MODEL))],
            out_specs=pl.BlockSpec((tm, D_MODEL), lambda i, te, nv: (i, 0)),
        ),
        compiler_params=_cparams(("arbitrary",)),
        name="expert_ffn",
    )(tile_expert, n_valid, xs, wg, wu, wd)


def _moe_combine_kernel(x_ref, y1_ref, y2_ref, gate_ref, nf_ref, *o_refs, tiles):
    i = pl.program_id(0)
    g = gate_ref[...]
    y = x_ref[...] + g[:, 0:1] * y1_ref[...].astype(F32) + g[:, 1:2] * y2_ref[...].astype(F32)
    out = _rms_norm_rows(y, nf_ref[...])
    start = 0
    for o_ref, nt in zip(o_refs, tiles):
        @pl.when((i >= start) & (i < start + nt))
        def _(o_ref=o_ref):
            o_ref[...] = out
        start += nt


def _moe_combine(x2, y1, y2, gates, norm_final, tm, part_rows):
    n = x2.shape[0]
    row = lambda w: pl.BlockSpec((tm, w), lambda i: (i, 0))
    outs = tuple(jax.ShapeDtypeStruct((r, D_MODEL), F32) for r in part_rows)
    return pl.pallas_call(
        functools.partial(_moe_combine_kernel, tiles=_part_tiles(outs, tm)),
        out_shape=outs,
        grid=(n // tm,),
        in_specs=[row(D_MODEL), row(D_MODEL), row(D_MODEL), row(LANES), pl.BlockSpec((1, D_MODEL), lambda i: (0, 0))],
        out_specs=tuple(_part_specs(outs, tm, D_MODEL)),
        compiler_params=_cparams(("arbitrary",)),
        name="moe_combine",
    )(x2, y1, y2, gates, norm_final)


def _moe(x2, h, gates, idx, counts, wg, wu, wd, norm_final, tm, part_rows):
    n = x2.shape[0]
    tme = EXPERT_TILE
    n_tiles = (2 * n + N_EXPERTS * (tme - 1)) // tme + 1
    e1, e2, r1, r2 = idx[0], idx[1], idx[2], idx[3]
    cnt = counts[:N_EXPERTS, 0]
    padded = (cnt + tme - 1) // tme * tme
    group_end = jnp.cumsum(padded)
    group_off = group_end - padded
    dense_off = jnp.cumsum(cnt) - cnt
    lookup = lambda table, e: sum(jnp.where(e == k, table[k], 0) for k in range(N_EXPERTS))
    slot1 = lookup(group_off, e1) + r1
    slot2 = lookup(group_off, e2) + r2
    tok = jnp.arange(n, dtype=jnp.int32)
    sorted_tok = jnp.sort(jnp.concatenate([e1 * n + tok, e2 * n + tok])) % n
    tile_start = jnp.arange(n_tiles, dtype=jnp.int32) * tme
    tile_expert = jnp.minimum(jnp.searchsorted(group_end, tile_start, side='right'), N_EXPERTS - 1).astype(jnp.int32)
    n_valid = (group_end[-1:] // tme).astype(jnp.int32)
    rank = (tile_start - group_off[tile_expert])[:, None] + jnp.arange(tme, dtype=jnp.int32)[None, :]
    dense = jnp.clip(dense_off[tile_expert][:, None] + rank, 0, 2 * n - 1)
    take = functools.partial(jnp.take, axis=0, mode='clip')
    spread = (tile_start[:, None] + jnp.arange(tme, dtype=jnp.int32)[None, :]) % n
    src = jnp.where(rank < cnt[tile_expert][:, None], take(sorted_tok, dense.reshape(-1)).reshape(dense.shape), spread)
    xs = take(h, src.reshape(-1))
    ys = _expert_ffn(xs, tile_expert, n_valid, wg, wu, wd, tme)
    y1 = take(ys, slot1)
    y2 = take(ys, slot2)
    return _moe_combine(x2, y1, y2, gates, norm_final, tm, part_rows)


def _rope_tables(t):
    rows = t // GRID_W
    pos = np.arange(rows * GRID_W)
    row = (pos // GRID_W).astype(np.float32)
    col = (pos % GRID_W).astype(np.float32)
    nf = HEAD_DIM // 4
    inv = jnp.asarray(ROPE_THETA, F32) ** (-jnp.arange(nf, dtype=F32) / nf)
    ar = jnp.asarray(row)[:, None] * inv
    ac = jnp.asarray(col)[:, None] * inv
    cos = jnp.concatenate([jnp.cos(ar), jnp.cos(ar), jnp.cos(ac), jnp.cos(ac)], axis=-1)
    sin = jnp.concatenate([-jnp.sin(ar), jnp.sin(ar), -jnp.sin(ac), jnp.sin(ac)], axis=-1)
    return jnp.tile(cos, (1, H_G)), jnp.tile(sin, (1, H_G))


def _pad_w_in(w):
    a, b_, c, d = 1024, 512, 1040, 1024
    w_a, w_b, w_c, w_d = w[:, :a], w[:, a:a + b_], w[:, a + b_:a + b_ + c], w[:, a + b_ + c:]
    gates = jnp.pad(w_c[:, 1024:], ((0, 0), (0, LANES - 16)))
    return jnp.concatenate([w_a, w_c[:, :1024], w_d, w_b, gates], axis=1).astype(BF16)


def _row(v):
    return v.reshape(1, -1).astype(F32)


def _trunk(xs, p):
    t = xs[0].shape[1]
    part_rows = tuple(x.shape[0] * t for x in xs)
    b = sum(x.shape[0] for x in xs)
    n = b * t
    tm = 256
    cos, sin = _rope_tables(t)
    x_parts = tuple(x.reshape(-1, D_MODEL) for x in xs)
    depth = p['w_in'].shape[0]
    for l in range(depth):
        tm_in = 2 * tm if all(xp.shape[0] % (2 * tm) == 0 for xp in x_parts) else tm
        u2 = _in_proj(x_parts, _row(p['norm_mix'][l]), _pad_w_in(p['w_in'][l]), tm_in)
        u3 = u2.reshape(b, t, U_COLS)
        (r, v, kk, g, bonus, lwf, lwb, kf, kb, bf, bb) = _rwkv_prep(
            u3, _row(p['rwkv_mu'][l]), p['rwkv_w0'][l], p['rwkv_w2'][l], p['rwkv_a0'][l], p['rwkv_a2'][l],
            p['rwkv_g2'][l], _row(p['rwkv_kk'][l]), _row(p['rwkv_ka'][l]), _row(p['rwkv_rk'][l]))
        yf, yb = _rwkv_core(r, v, kk, lwf, lwb, kf, kb, bf, bb)
        at = _attention(u3, cos, sin, _row(jnp.tile(p['attn_q_norm'][l], H_G)),
                        _row(jnp.tile(p['attn_k_norm'][l], KV_ATTN)))
        gate_bias = jnp.pad(jnp.concatenate([p['mlstm_i_bias'][l].reshape(-1), p['mlstm_f_bias'][l].reshape(-1)]),
                            (0, LANES - 4 * H_G))
        hf, hb = _mlstm(u3, p['mlstm_conv_w'][l], _row(p['mlstm_conv_b'][l]), _row(gate_bias))
        of, ob = _retention(u3, cos, sin)
        flat = lambda z: z.reshape(n, W_G)
        mix_args = (x_parts, tuple(map(flat, (yf, yb, bonus, g))), flat(at), tuple(map(flat, (hf, hb))),
                    tuple(map(flat, (of, ob))), u2, _row(p['rwkv_ln'][l]), _row(p['mlstm_ln'][l]),
                    _row(p['ret_ln'][l]), p['w_out'][l].astype(BF16))
        j = l // 2
        if l % 2 == 0:
            x2 = _mix_ffn(mix_args, _row(p['norm_ffn'][l]), p['ffn_w_gate'][j].astype(BF16),
                          p['ffn_w_up'][j].astype(BF16), p['ffn_w_down'][j].astype(BF16), tm)
            x_parts = (x2,)
            if l == depth - 1:
                raise NotImplementedError("final norm after a dense FFN layer")
        else:
            wr = jnp.pad(p['moe_router'][j], ((0, 0), (0, LANES - N_EXPERTS)))
            x2, h, gates, idx, counts = _mix_router(mix_args, _row(p['norm_ffn'][l]), wr, tm)
            if l != depth - 1:
                raise NotImplementedError("expert layer that is not the last layer")
            outs = _moe(x2, h, gates, idx, counts, p['moe_w_gate'][j].astype(BF16), p['moe_w_up'][j].astype(BF16),
                        p['moe_w_down'][j].astype(BF16), _row(p['norm_final']), tm, part_rows)
    return tuple(o.reshape(x.shape) for o, x in zip(outs, xs))


def kernel(x_prompt, x_sample, norm_mix, norm_ffn, norm_final, w_in, w_out, rwkv_mu, rwkv_w0, rwkv_w2,
           rwkv_a0, rwkv_a2, rwkv_g2, rwkv_kk, rwkv_ka, rwkv_rk, rwkv_ln, attn_q_norm, attn_k_norm,
           mlstm_conv_w, mlstm_conv_b, mlstm_i_bias, mlstm_f_bias, mlstm_ln, ret_ln, ffn_w_gate, ffn_w_up,
           ffn_w_down, moe_router, moe_w_gate, moe_w_up, moe_w_down):
    p = dict(norm_mix=norm_mix, norm_ffn=norm_ffn, norm_final=norm_final, w_in=w_in, w_out=w_out,
             rwkv_mu=rwkv_mu, rwkv_w0=rwkv_w0, rwkv_w2=rwkv_w2, rwkv_a0=rwkv_a0, rwkv_a2=rwkv_a2,
             rwkv_g2=rwkv_g2, rwkv_kk=rwkv_kk, rwkv_ka=rwkv_ka, rwkv_rk=rwkv_rk, rwkv_ln=rwkv_ln,
             attn_q_norm=attn_q_norm, attn_k_norm=attn_k_norm, mlstm_conv_w=mlstm_conv_w,
             mlstm_conv_b=mlstm_conv_b, mlstm_i_bias=mlstm_i_bias, mlstm_f_bias=mlstm_f_bias,
             mlstm_ln=mlstm_ln, ret_ln=ret_ln, ffn_w_gate=ffn_w_gate, ffn_w_up=ffn_w_up,
             ffn_w_down=ffn_w_down, moe_router=moe_router, moe_w_gate=moe_w_gate, moe_w_up=moe_w_up,
             moe_w_down=moe_w_down)
    return _trunk((x_prompt, x_sample), p)
```

```python
import functools
import math

import numpy as np
import jax
import jax.numpy as jnp
from jax import lax
from jax.experimental import pallas as pl
from jax.experimental.pallas import tpu as pltpu

F32 = jnp.float32
BF16 = jnp.bfloat16
MIXER_DTYPE = BF16

D_MODEL = 1024
HEAD_DIM = 64
W_G = 256
H_G = 4
KV_ATTN = 2
D_FF = 2816
N_EXPERTS = 8
NORM_EPS = 1e-6
HEAD_NORM_EPS = 1e-5
RWKV_GN_EPS = 64e-5
NEG_INF = -1e30
ROPE_THETA = 10000.0
GRID_W = 64

LANES = 128
ROW_ALIGN = 8
VMEM_LIMIT_BYTES = 56 * 1024 * 1024

U_RWKV = 0
U_MLSTM = 1024
U_RET = 2048
U_ATTN = 3072
U_GATE = 3584
U_COLS = 3712

RWKV_CHUNK = 64
RWKV_BLOCK = 256
RWKV_GROUP = 4
MIX_CHUNK = 128
MIX_BLOCK = 256
EXPERT_TILE = 256
EXPERT_FF_SPLIT = 1


def _cparams(sem):
    return pltpu.CompilerParams(dimension_semantics=sem, vmem_limit_bytes=VMEM_LIMIT_BYTES)


def _bdot(a, b, dims):
    return lax.dot_general(a, b, (dims, ((), ())), preferred_element_type=F32)


def _mm(a, b):
    return _bdot(a.astype(BF16), b.astype(BF16), ((1,), (0,)))


def _mm_nt(a, b):
    return _bdot(a.astype(BF16), b.astype(BF16), ((1,), (1,)))


def _mm_tn(a, b):
    return _bdot(a.astype(BF16), b.astype(BF16), ((0,), (0,)))


def _split2(a):
    hi = a.astype(BF16)
    lo = (a - hi.astype(F32)).astype(BF16)
    return hi, lo


def _split3(a):
    hi = a.astype(BF16)
    r = a - hi.astype(F32)
    mid = r.astype(BF16)
    lo = (r - mid.astype(F32)).astype(BF16)
    return hi, mid, lo


def _mm_l2(a, b_exact):
    hi, lo = _split2(a)
    return _bdot(hi, b_exact, ((1,), (0,))) + _bdot(lo, b_exact, ((1,), (0,)))


def _mm_l3(a, b_exact):
    h, m, l = _split3(a)
    return _bdot(h, b_exact, ((1,), (0,))) + _bdot(m, b_exact, ((1,), (0,))) + _bdot(l, b_exact, ((1,), (0,)))


def _mm_r3(a_exact, b):
    h, m, l = _split3(b)
    return _bdot(a_exact, h, ((1,), (0,))) + _bdot(a_exact, m, ((1,), (0,))) + _bdot(a_exact, l, ((1,), (0,)))


def _mm_x3(a, b):
    ah, al = _split2(a)
    bh, bl = _split2(b)
    d = ((1,), (0,))
    return _bdot(ah, bh, d) + _bdot(ah, bl, d) + _bdot(al, bh, d)


def _iota2(shape, axis):
    return lax.broadcasted_iota(jnp.int32, shape, axis)


def _head_mean_matrix(width):
    r = _iota2((width, width), 0) // HEAD_DIM
    c = _iota2((width, width), 1) // HEAD_DIM
    return jnp.where(r == c, 1.0 / HEAD_DIM, 0.0).astype(BF16)


def _head_mean(z, bd):
    return _bdot(z.astype(BF16), bd, ((1,), (0,)))


def _sigmoid(x):
    return 1.0 / (1.0 + jnp.exp(-x))


def _silu(x):
    return x * _sigmoid(x)


def _log_sigmoid(x):
    return jnp.minimum(x, 0.0) - jnp.log(1.0 + jnp.exp(-jnp.abs(x)))


def _rms_norm_rows(x, gain):
    ms = jnp.mean(x * x, axis=-1, keepdims=True)
    return x * lax.rsqrt(ms + NORM_EPS) * gain


def _rope_swap(z):
    w = z.shape[-1]
    lane = _iota2(z.shape, z.ndim - 1)
    fwd = pltpu.roll(z, w - 16, z.ndim - 1)
    bwd = pltpu.roll(z, 16, z.ndim - 1)
    return jnp.where((lane % 32) < 16, fwd, bwd)


def _rope(z, cos, sin):
    return z * cos + _rope_swap(z) * sin


def _shift_rows(x, prev_row, next_row):
    n = x.shape[0]
    row = _iota2(x.shape, 0)
    prev = jnp.where(row == 0, prev_row, pltpu.roll(x, 1, 0))
    nxt = jnp.where(row == n - 1, next_row, pltpu.roll(x, n - 1, 0))
    return prev, nxt


def _part_tiles(parts, tm):
    return tuple(p.shape[0] // tm for p in parts)


def _part_specs(parts, tm, width):
    specs, start = [], 0
    for nt in _part_tiles(parts, tm):
        specs.append(pl.BlockSpec((tm, width), lambda i, s=start, nt=nt: (jnp.clip(i - s, 0, nt - 1), 0)))
        start += nt
    return specs


def _part_tile(i, refs, tiles):
    x = refs[0][...]
    start = tiles[0]
    for ref, nt in zip(refs[1:], tiles[1:]):
        x = jnp.where(i >= start, ref[...], x)
        start += nt
    return x


def _inproj_kernel(*refs, tiles):
    x_refs, (g_ref, w_ref, cos_ref, sin_ref, qg_ref, kg_ref, o_ref) = refs[:len(tiles)], refs[len(tiles):]
    h = _rms_norm_rows(_part_tile(pl.program_id(0), x_refs, tiles), g_ref[...])
    u = jnp.dot(h.astype(BF16), w_ref[...], preferred_element_type=F32)
    o_ref[...] = u
    cos = cos_ref[...]
    sin = sin_ref[...]
    kw = KV_ATTN * HEAD_DIM
    o_ref[:, U_RET:U_RET + W_G] = _rope(u[:, U_RET:U_RET + W_G], cos, sin)
    o_ref[:, U_RET + W_G:U_RET + 2 * W_G] = _rope(u[:, U_RET + W_G:U_RET + 2 * W_G], cos, sin) * HEAD_DIM ** -0.5
    q = u[:, U_ATTN:U_ATTN + W_G]
    k = u[:, U_ATTN + W_G:U_ATTN + W_G + kw]
    qn = q * lax.rsqrt(_head_mean(q * q, _head_mean_matrix(W_G)) + NORM_EPS) * qg_ref[...]
    kn = k * lax.rsqrt(_head_mean(k * k, _head_mean_matrix(kw)) + NORM_EPS) * kg_ref[...]
    o_ref[:, U_ATTN:U_ATTN + W_G] = _rope(qn, cos, sin) * (HEAD_DIM ** -0.5 * math.log2(math.e))
    o_ref[:, U_ATTN + W_G:U_ATTN + W_G + kw] = _rope(kn, cos[:, :kw], sin[:, :kw])


def _in_proj(x_parts, gain, w_pad, cos, sin, q_gain, k_gain, tm):
    tiles = _part_tiles(x_parts, tm)
    n = sum(tiles) * tm
    assert cos.shape[0] % tm == 0
    t_tiles = cos.shape[0] // tm
    const = lambda shape: pl.BlockSpec(shape, lambda i: (0, 0))
    table = pl.BlockSpec((tm, W_G), lambda i: (i % t_tiles, 0))
    return pl.pallas_call(
        functools.partial(_inproj_kernel, tiles=tiles),
        out_shape=jax.ShapeDtypeStruct((n, U_COLS), F32),
        grid=(n // tm,),
        in_specs=_part_specs(x_parts, tm, D_MODEL) + [
            const((1, D_MODEL)),
            pl.BlockSpec((D_MODEL, U_COLS), lambda i: (0, 0), pipeline_mode=pl.Buffered(1)),
            table, table, const((1, W_G)), const((1, KV_ATTN * HEAD_DIM)),
        ],
        out_specs=pl.BlockSpec((tm, U_COLS), lambda i: (i, 0)),
        compiler_params=_cparams(("arbitrary",)),
        name="in_proj",
    )(*x_parts, gain, w_pad, cos, sin, q_gain, k_gain)


def _attn_kernel(u_ref, o_ref, q_scr, k_scr, v_scr, *, tq):
    t = u_ref.shape[1]
    u = u_ref[0]
    v = u[:, W_G + KV_ATTN * HEAD_DIM:]
    q_scr[...] = u[:, :W_G].astype(BF16)
    k_scr[...] = u[:, W_G:W_G + KV_ATTN * HEAD_DIM].astype(BF16)
    ones = jnp.ones((t, HEAD_DIM), BF16)
    for j in range(KV_ATTN):
        vj = v[:, j * HEAD_DIM:(j + 1) * HEAD_DIM].astype(BF16)
        v_scr[:, j * LANES:(j + 1) * LANES] = jnp.concatenate([vj, ones], axis=1)
    group = H_G // KV_ATTN

    def q_tile(i, carry):
        rows = pl.ds(pl.multiple_of(i * tq, tq), tq)

        def scores(j):
            q = jnp.concatenate([q_scr[rows, (j * group + g) * HEAD_DIM:(j * group + g + 1) * HEAD_DIM]
                                 for g in range(group)], axis=0)
            return _bdot(q, k_scr[:, j * HEAD_DIM:(j + 1) * HEAD_DIM], ((1,), (1,)))

        s_next = scores(0)
        for j in range(KV_ATTN):
            s = s_next
            if j + 1 < KV_ATTN:
                s_next = scores(j + 1)
            m = jnp.max(s, axis=-1, keepdims=True)
            p = jnp.exp2(s - m)
            r = jnp.dot(p.astype(BF16), v_scr[:, j * LANES:(j + 1) * LANES], preferred_element_type=F32)
            o = r[:, :HEAD_DIM] / r[:, HEAD_DIM:]
            for g in range(group):
                h = j * group + g
                o_ref[0, rows, h * HEAD_DIM:(h + 1) * HEAD_DIM] = o[g * tq:(g + 1) * tq]
        return carry

    lax.fori_loop(0, t // tq, q_tile, 0)


def _attention(u3):
    b, t, _ = u3.shape
    tq = min(256, t)
    col = U_ATTN // 512
    return pl.pallas_call(
        functools.partial(_attn_kernel, tq=tq),
        out_shape=jax.ShapeDtypeStruct((b, t, W_G), F32),
        grid=(b,),
        in_specs=[
            pl.BlockSpec((1, t, 512), lambda i: (i, 0, col)),
        ],
        out_specs=pl.BlockSpec((1, t, W_G), lambda i: (i, 0, 0)),
        scratch_shapes=[
            pltpu.VMEM((t, W_G), BF16),
            pltpu.VMEM((t, KV_ATTN * HEAD_DIM), BF16),
            pltpu.VMEM((t, KV_ATTN * LANES), BF16),
        ],
        compiler_params=_cparams(("parallel",)),
        name="attention",
    )(u3)


def _ret_log_gamma(direction):
    return [math.log1p(-2.0 ** (-5.0 - (2 * h + direction) / 2.0)) for h in range(H_G)]


def _ret_kernel(uf_ref, ub_ref, of_ref, ob_ref, rf_scr, rb_scr):
    i = pl.program_id(1)
    c = MIX_CHUNK

    @pl.when(i == 0)
    def _():
        rf_scr[...] = jnp.zeros_like(rf_scr)
        rb_scr[...] = jnp.zeros_like(rb_scr)

    tt = _iota2((c, c), 0)
    ss = _iota2((c, c), 1)
    diff = (tt - ss).astype(F32)
    jcol = tt.astype(F32)
    first = ss < HEAD_DIM
    block_diag = (tt < HEAD_DIM) == first
    lg_f = _ret_log_gamma(0)
    lg_b = _ret_log_gamma(1)
    psl = lambda p: slice(p * LANES, (p + 1) * LANES)
    lane_lg = lambda lg, p: jnp.where(first, lg[2 * p], lg[2 * p + 1])
    row_decay = lambda lg, p: jnp.where(tt < HEAD_DIM, math.exp(c * lg[2 * p]), math.exp(c * lg[2 * p + 1]))
    own = lambda h, x: jnp.where(first, x, 0.0) if h % 2 == 0 else jnp.where(first, 0.0, x)
    pairs = range(H_G // 2)

    nch = uf_ref.shape[1] // c
    uf = uf_ref[0]
    qf_all, kf_all = uf[:, :W_G], uf[:, W_G:2 * W_G]
    ub = ub_ref[0]
    qb_all, kb_all = ub[:, :W_G], ub[:, W_G:2 * W_G]
    rf_prev = [rf_scr[p] for p in pairs]
    rb_prev = [rb_scr[p] for p in pairs]
    decay = [jnp.where(tt >= ss, jnp.exp(diff * lg_f[h]), 0.0) + jnp.where(ss >= tt, jnp.exp(-diff * lg_b[h]), 0.0)
             for h in range(H_G)]
    for n in range(nch):
        rf_rows = slice(n * c, (n + 1) * c)
        rb_rows = slice((nch - 1 - n) * c, (nch - n) * c)
        qf, kf, vf = qf_all[rf_rows], kf_all[rf_rows], uf[rf_rows, 2 * W_G:3 * W_G]
        qb, kb, vb = qb_all[rb_rows], kb_all[rb_rows], ub[rb_rows, 2 * W_G:3 * W_G]
        qk = [_mm_nt(qf[:, psl(h // 2)], own(h, kf[:, psl(h // 2)])) for h in range(H_G)]
        intra = [_mm(qk[h] * decay[h], own(h, vf[:, psl(h // 2)])) for h in range(H_G)]
        inter_f = [_mm(qf[:, psl(p)] * jnp.exp((jcol + 1.0) * lane_lg(lg_f, p)), rf_prev[p]) for p in pairs]
        inter_b = [_mm(qb[:, psl(p)] * jnp.exp((c - jcol) * lane_lg(lg_b, p)), rb_prev[p]) for p in pairs]
        upd_f = [_mm_tn(kf[:, psl(p)] * jnp.exp((c - 1.0 - jcol) * lane_lg(lg_f, p)), vf[:, psl(p)]) for p in pairs]
        upd_b = [_mm_tn(kb[:, psl(p)] * jnp.exp(jcol * lane_lg(lg_b, p)), vb[:, psl(p)]) for p in pairs]
        for p in pairs:
            of_ref[0, rf_rows, psl(p)] = (intra[2 * p] + intra[2 * p + 1] + inter_f[p]).astype(of_ref.dtype)
            ob_ref[0, rb_rows, psl(p)] = inter_b[p].astype(ob_ref.dtype)
        rf_prev = [row_decay(lg_f, p) * rf_prev[p] + jnp.where(block_diag, upd_f[p], 0.0) for p in pairs]
        rb_prev = [row_decay(lg_b, p) * rb_prev[p] + jnp.where(block_diag, upd_b[p], 0.0) for p in pairs]
    for p in pairs:
        rf_scr[p] = rf_prev[p]
        rb_scr[p] = rb_prev[p]


def _retention(u3):
    b, t, _ = u3.shape
    c = min(MIX_BLOCK, t)
    nblk = t // c
    col = U_RET // 1024
    return pl.pallas_call(
        _ret_kernel,
        out_shape=(jax.ShapeDtypeStruct((b, t, W_G), MIXER_DTYPE), jax.ShapeDtypeStruct((b, t, W_G), MIXER_DTYPE)),
        grid=(b, nblk),
        in_specs=[
            pl.BlockSpec((1, c, 1024), lambda bi, i: (bi, i, col)),
            pl.BlockSpec((1, c, 1024), lambda bi, i: (bi, nblk - 1 - i, col)),
        ],
        out_specs=(
            pl.BlockSpec((1, c, W_G), lambda bi, i: (bi, i, 0)),
            pl.BlockSpec((1, c, W_G), lambda bi, i: (bi, nblk - 1 - i, 0)),
        ),
        scratch_shapes=[pltpu.VMEM((H_G // 2, LANES, LANES), F32), pltpu.VMEM((H_G // 2, LANES, LANES), F32)],
        compiler_params=_cparams(("parallel", "arbitrary")),
        name="retention",
    )(u3, u3)


def _mlstm_tile(u_ref, up_ref, un_ref, g_ref, cw_ref, cb_ref, gb_ref, blk, nblk, direction):
    c = MIX_CHUNK
    reverse = direction == 1
    u = u_ref[0]
    qk = u[:, :2 * W_G]
    prev_row = jnp.where(blk == 0, 0.0, up_ref[0][ROW_ALIGN - 1:ROW_ALIGN, :])
    next_row = jnp.where(blk == nblk - 1, 0.0, un_ref[0][0:1, :])
    prev, nxt = _shift_rows(qk, prev_row, next_row)
    cw = cw_ref[...]
    qk = _silu(cw[0:1] * prev + cw[1:2] * qk + cw[2:3] * nxt + cb_ref[...])
    qa = qk[:, :W_G]
    ka = qk[:, W_G:] * HEAD_DIM ** -0.5
    va = u[:, 2 * W_G:3 * W_G]

    x = g_ref[0] + gb_ref[...]
    xt = x.T
    lf_c = _log_sigmoid(x)
    lf_r = _log_sigmoid(xt)
    tt = _iota2((c, c), 0)
    ss = _iota2((c, c), 1)
    lower = jnp.where(ss <= tt, 1.0, 0.0).astype(BF16)
    upper = jnp.where(tt <= ss, 1.0, 0.0).astype(BF16)
    if reverse:
        b_c = _mm_r3(upper, lf_c)
        b_r = _mm_l3(lf_r, lower)
        mask = ss >= tt
    else:
        b_c = _mm_r3(lower, lf_c)
        b_r = _mm_l3(lf_r, upper)
        mask = ss <= tt
    g_all = jnp.sum(lf_c, axis=0, keepdims=True)
    return dict(q=qa, k=ka, v=va, x=x, xt=xt, b_c=b_c, b_r=b_r, g_all=g_all, mask=mask)


def _mlstm_select_matrix():
    sel = np.zeros((2, 2 * LANES, 8 * LANES), np.float32)
    for d in range(2):
        for h in range(H_G):
            ci, cf = d * H_G + h, 2 * H_G + d * H_G + h
            p, j = divmod(h, 2)
            sel[d, cf, LANES * h:LANES * (h + 1)] = 1.0
            sel[d, cf, 4 * LANES + LANES * p + HEAD_DIM * j:4 * LANES + LANES * p + HEAD_DIM * (j + 1)] = 1.0
            sel[d, LANES + ci, 6 * LANES + LANES * p + HEAD_DIM * j:6 * LANES + LANES * p + HEAD_DIM * (j + 1)] = 1.0
    return jnp.asarray(sel, BF16)


def _mlstm_kernel(uf_ref, upf_ref, unf_ref, gf_ref, ub_ref, upb_ref, unb_ref, gbk_ref,
                  cw_ref, cb_ref, gb_ref, sel_ref, of_ref, ob_ref, st_scr, m_scr):
    i = pl.program_id(1)
    nblk = pl.num_programs(1)
    c = MIX_CHUNK

    @pl.when(i == 0)
    def _():
        st_scr[...] = jnp.zeros_like(st_scr)
        m_scr[...] = jnp.zeros_like(m_scr)

    tiles = (_mlstm_tile(uf_ref, upf_ref, unf_ref, gf_ref, cw_ref, cb_ref, gb_ref, i, nblk, 0),
             _mlstm_tile(ub_ref, upb_ref, unb_ref, gbk_ref, cw_ref, cb_ref, gb_ref, nblk - 1 - i, nblk, 1))
    o_refs = (of_ref, ob_ref)
    sel = [_mm_l2(jnp.concatenate([tiles[d]["b_c"], tiles[d]["x"]], axis=1), sel_ref[d]) for d in range(2)]
    first = _iota2((c, LANES), 1) < HEAD_DIM
    row_first = _iota2((LANES, 2 * LANES), 0) < HEAD_DIM
    lane2 = _iota2((LANES, 2 * LANES), 1) % LANES < HEAD_DIM
    block_diag = row_first == lane2
    ones = jnp.ones((c, LANES), F32)
    pairs = [(d, p) for d in range(2) for p in range(H_G // 2)]
    heads = [(d, h) for d in range(2) for h in range(H_G)]
    psl = lambda p: slice(p * LANES, (p + 1) * LANES)
    q_pair = {dp: tiles[dp[0]]["q"][:, psl(dp[1])] for dp in pairs}
    k_pair = {dp: tiles[dp[0]]["k"][:, psl(dp[1])] for dp in pairs}
    v_pair = {dp: tiles[dp[0]]["v"][:, psl(dp[1])] for dp in pairs}
    state = {dp: st_scr[n] for n, dp in enumerate(pairs)}
    m_row = {dp: m_scr[n:n + 1, :] for n, dp in enumerate(pairs)}

    def own(d, h, x):
        return jnp.where(first, x, 0.0) if h % 2 == 0 else jnp.where(first, 0.0, x)

    qk = [_mm_nt(q_pair[(d, h // 2)], own(d, h, k_pair[(d, h // 2)])) for d, h in heads]
    qs = {dp: _mm(q_pair[dp], state[dp]) for dp in pairs}
    bc = [sel[d][:, LANES * h:LANES * (h + 1)] for d, h in heads]
    m_prev = [m_row[(d, h // 2)][:, HEAD_DIM * (h % 2):HEAD_DIM * (h % 2) + 1] for d, h in heads]
    dlog = []
    for (d, h), bc_ in zip(heads, bc):
        ci, cf = d * H_G + h, 2 * H_G + d * H_G + h
        rowterm = tiles[d]["xt"][ci:ci + 1, :] - tiles[d]["b_r"][cf:cf + 1, :]
        dlog.append(jnp.where(tiles[d]["mask"], bc_ + rowterm, NEG_INF))
    inter_log = [bc_ + m_ for bc_, m_ in zip(bc, m_prev)]
    m_t = [jnp.maximum(il, jnp.max(dl, axis=-1, keepdims=True)) for il, dl in zip(inter_log, dlog)]
    sc = [qk_ * jnp.exp(dl - mt) for qk_, dl, mt in zip(qk, dlog, m_t)]
    w_inter = [jnp.exp(il - mt) for il, mt in zip(inter_log, m_t)]
    e_neg = [jnp.exp(-mt) for mt in m_t]
    res = []
    for n, (d, h) in enumerate(heads):
        v_aug = jnp.concatenate([own(d, h, v_pair[(d, h // 2)]), own(d, h, ones)], axis=1)
        res.append(_mm(sc[n], v_aug))
    for n, (d, p) in enumerate(pairs):
        a, b_ = 2 * n, 2 * n + 1
        tot = res[a] + res[b_] + jnp.tile(jnp.where(first, w_inter[a], w_inter[b_]), (1, 2)) * qs[(d, p)]
        den = jnp.maximum(jnp.abs(tot[:, LANES:]), jnp.where(first, e_neg[a], e_neg[b_]))
        o_refs[d][0, :, psl(p)] = (tot[:, :LANES] / den).astype(o_refs[d].dtype)

    for n, (d, p) in enumerate(pairs):
        bcp = sel[d][:, 4 * LANES + LANES * p:4 * LANES + LANES * (p + 1)]
        lip = sel[d][:, 6 * LANES + LANES * p:6 * LANES + LANES * (p + 1)]
        g_row = bcp[0:1, :] if d == 1 else bcp[c - 1:c, :]
        a_p = g_row - bcp + lip
        m_new = jnp.maximum(g_row + m_row[(d, p)], jnp.max(a_p, axis=0, keepdims=True))
        dec = jnp.exp(g_row + m_row[(d, p)] - m_new)
        kw_t = (k_pair[(d, p)] * jnp.exp(a_p - m_new)).T
        upd = _mm(kw_t, jnp.concatenate([v_pair[(d, p)], ones], axis=1))
        dec_tile = jnp.where(row_first, dec[:, 0:1], dec[:, HEAD_DIM:HEAD_DIM + 1])
        st_scr[n] = dec_tile * state[(d, p)] + jnp.where(block_diag, upd, 0.0)
        m_scr[n:n + 1, :] = m_new


def _mlstm(u3, conv_w, conv_b, gate_bias):
    b, t, _ = u3.shape
    c = MIX_CHUNK
    nblk = t // c
    rpb = c // ROW_ALIGN
    n8 = t // ROW_ALIGN
    col = U_MLSTM // 1024
    hcol = U_MLSTM // 512
    gcol = U_GATE // LANES

    def specs(rev):
        blk = (lambda i: nblk - 1 - i) if rev else (lambda i: i)
        return [
            pl.BlockSpec((1, c, 1024), lambda bi, i: (bi, blk(i), col)),
            pl.BlockSpec((1, ROW_ALIGN, 512), lambda bi, i: (bi, jnp.maximum(blk(i) * rpb - 1, 0), hcol)),
            pl.BlockSpec((1, ROW_ALIGN, 512), lambda bi, i: (bi, jnp.minimum((blk(i) + 1) * rpb, n8 - 1), hcol)),
            pl.BlockSpec((1, c, LANES), lambda bi, i: (bi, blk(i), gcol)),
        ]

    const = lambda shape: pl.BlockSpec(shape, lambda bi, i: (0,) * len(shape))
    return pl.pallas_call(
        _mlstm_kernel,
        out_shape=(jax.ShapeDtypeStruct((b, t, W_G), MIXER_DTYPE), jax.ShapeDtypeStruct((b, t, W_G), MIXER_DTYPE)),
        grid=(b, nblk),
        in_specs=specs(False) + specs(True) + [const((3, 2 * W_G)), const((1, 2 * W_G)), const((1, LANES)),
                                               const((2, 2 * LANES, 8 * LANES))],
        out_specs=(
            pl.BlockSpec((1, c, W_G), lambda bi, i: (bi, i, 0)),
            pl.BlockSpec((1, c, W_G), lambda bi, i: (bi, nblk - 1 - i, 0)),
        ),
        scratch_shapes=[pltpu.VMEM((H_G, LANES, 2 * LANES), F32), pltpu.VMEM((ROW_ALIGN, LANES), F32)],
        compiler_params=_cparams(("parallel", "arbitrary")),
        name="mlstm",
    )(u3, u3, u3, u3, u3, u3, u3, u3, conv_w, conv_b, gate_bias, _mlstm_select_matrix())


def _rwkv_prep_kernel(u_ref, up_ref, un_ref, mu_ref, w0_ref, w2_ref, a0_ref, a2_ref, g2_ref, kks_ref, ka_ref, rk_ref,
                      r_ref, v_ref, kk_ref, g_ref, bonus_ref, lwf_ref, lwb_ref, kf_ref, kb_ref, bf_ref, bb_ref):
    i = pl.program_id(1)
    nblk = pl.num_programs(1)
    u = u_ref[0]
    prev_row = jnp.where(i == 0, 0.0, up_ref[0][ROW_ALIGN - 1:ROW_ALIGN, :])
    next_row = jnp.where(i == nblk - 1, 0.0, un_ref[0][0:1, :])
    prev, nxt = _shift_rows(u, prev_row, next_row)
    us = u + mu_ref[...] * (0.5 * (prev + nxt) - u)
    r = us[:, 0:W_G]
    k = us[:, W_G:2 * W_G]
    v = us[:, 2 * W_G:3 * W_G]
    xw = us[:, 3 * W_G:3 * W_G + 64]
    xa = us[:, 3 * W_G + 64:3 * W_G + 128]
    xg = us[:, 3 * W_G + 128:]
    bd = _head_mean_matrix(W_G)
    g = _mm(_sigmoid(xg), g2_ref[...])
    lw = jnp.tanh(xw)
    a_lr = _mm_x3(xa, a2_ref[...])
    kk = k * kks_ref[...]
    kk = kk * lax.rsqrt(_head_mean(kk * kk, bd) * HEAD_DIM + 1e-12)
    r_ref[0] = r.astype(r_ref.dtype)
    v_ref[0] = v.astype(v_ref.dtype)
    kk_ref[0] = kk.astype(kk_ref.dtype)
    g_ref[0] = g.astype(g_ref.dtype)
    bonus = jnp.zeros_like(r)
    for d, (lw_ref, k_ref, b_ref) in enumerate(((lwf_ref, kf_ref, bf_ref), (lwb_ref, kb_ref, bb_ref))):
        z = w0_ref[d:d + 1, :] + _mm_x3(lw, w2_ref[d])
        lw_ref[0] = -_sigmoid(z) * math.exp(-0.5)
        a = _sigmoid(a0_ref[d:d + 1, :] + a_lr)
        kd = k * (1.0 + (a - 1.0) * ka_ref[...])
        k_ref[0] = kd.astype(k_ref.dtype)
        b_ref[0] = (kk * a).astype(b_ref.dtype)
        bonus = bonus + _head_mean(r * kd * rk_ref[...], bd) * HEAD_DIM * v
    bonus_ref[0] = bonus.astype(bonus_ref.dtype)


def _rwkv_prep(u3, mu, w0, w2, a0, a2, g2, kks, ka, rk):
    b, t, _ = u3.shape
    tb = min(RWKV_BLOCK, t)
    nblk = t // tb
    rpb = tb // ROW_ALIGN
    n8 = t // ROW_ALIGN
    const = lambda shape: pl.BlockSpec(shape, lambda bi, i: (0,) * len(shape))
    out = lambda dtype: jax.ShapeDtypeStruct((b, t, W_G), dtype)
    ospec = pl.BlockSpec((1, tb, W_G), lambda bi, i: (bi, i, 0))
    return pl.pallas_call(
        _rwkv_prep_kernel,
        out_shape=(out(MIXER_DTYPE),) * 3 + (out(MIXER_DTYPE),) * 2 + (out(F32),) * 2 + (out(MIXER_DTYPE),) * 4,
        grid=(b, nblk),
        in_specs=[
            pl.BlockSpec((1, tb, 1024), lambda bi, i: (bi, i, 0)),
            pl.BlockSpec((1, ROW_ALIGN, 1024), lambda bi, i: (bi, jnp.maximum(i * rpb - 1, 0), 0)),
            pl.BlockSpec((1, ROW_ALIGN, 1024), lambda bi, i: (bi, jnp.minimum((i + 1) * rpb, n8 - 1), 0)),
            const((1, 1024)), const((2, W_G)), const((2, 64, W_G)), const((2, W_G)), const((64, W_G)),
            const((128, W_G)), const((1, W_G)), const((1, W_G)), const((1, W_G)),
        ],
        out_specs=(ospec,) * 11,
        compiler_params=_cparams(("parallel", "parallel")),
        name="rwkv_prep",
    )(u3, u3, u3, mu, w0, w2, a0, a2, g2, kks, ka, rk)


def _tri_inverse_all(lmats, n):
    r = _iota2((n, n), 0)
    c = _iota2((n, n), 1)
    eye = jnp.where(r == c, 1.0, 0.0)
    pair = (r // 2 == c // 2) & (r != c)
    invs = [eye + jnp.where(pair, lm, 0.0) for lm in lmats]
    s = 2
    while s < n:
        sel = (r // (2 * s) == c // (2 * s)) & (r // s != c // s)
        offs = [jnp.where(sel, -lm, 0.0) for lm in lmats]
        xs = [_mm(inv, off) for inv, off in zip(invs, offs)]
        invs = [inv - _mm(x, inv) for inv, x in zip(invs, xs)]
        s *= 2
    return invs


def _rwkv_tile_terms(r, k, v, kk, b, lw, reverse):
    c = r.shape[0]
    tt = _iota2((c, c), 0)
    ss = _iota2((c, c), 1)
    tri = jnp.where((tt <= ss) if reverse else (ss <= tt), 1.0, 0.0).astype(BF16)
    cum_in = _mm_r3(tri, lw)
    cum_all = jnp.sum(lw, axis=0, keepdims=True)
    e_neg = jnp.exp(-cum_in)
    e_end = jnp.exp(cum_all - cum_in)
    return dict(at=-kk * jnp.exp(cum_in - lw), rt=r * jnp.exp(cum_in), bt=b * e_neg, kt=k * e_neg,
                gb=b * e_end, gk=k * e_end, v=v, e_all=jnp.exp(cum_all))


def _rwkv_chunk_terms(tiles, reverses):
    c = RWKV_CHUNK
    tt = _iota2((c, c), 0)
    ss = _iota2((c, c), 1)
    heads = [(ti, h) for ti in range(len(tiles)) for h in range(H_G)]
    sl = lambda h: slice(h * HEAD_DIM, (h + 1) * HEAD_DIM)
    get = lambda name: [tiles[ti][name][:, sl(h)] for ti, h in heads]
    at, rt, bt, kt, gb, gk, v = (get(nm) for nm in ("at", "rt", "bt", "kt", "gb", "gk", "v"))
    strict = [(ss > tt) if reverses[ti] else (ss < tt) for ti, _ in heads]
    incl = [(ss >= tt) if reverses[ti] else (ss <= tt) for ti, _ in heads]
    ps = [_mm_nt(jnp.concatenate([a, r_], axis=0), jnp.concatenate([b_, k_], axis=0))
          for a, r_, b_, k_ in zip(at, rt, bt, kt)]
    l_ab = [jnp.where(m, p[:c, :c], 0.0) for m, p in zip(strict, ps)]
    l_ak = [jnp.where(m, p[:c, c:], 0.0) for m, p in zip(strict, ps)]
    m_r = [jnp.concatenate([jnp.where(m, p[c:, :c], 0.0), jnp.where(m, p[c:, c:], 0.0)], axis=1)
           for m, p in zip(incl, ps)]
    lakv = [_mm(l, v_) for l, v_ in zip(l_ak, v)]
    invs = _tri_inverse_all(l_ab, c)
    tw = [_mm(inv, jnp.concatenate([a, lv], axis=1)) for inv, a, lv in zip(invs, at, lakv)]
    zeros = jnp.zeros((c, HEAD_DIM), F32)
    mm2 = [_mm(m, jnp.concatenate([t_, jnp.concatenate([zeros, v_], axis=1)], axis=0))
           for m, t_, v_ in zip(m_r, tw, v)]
    r1 = [r_ + m[:, :HEAD_DIM] for r_, m in zip(rt, mm2)]
    y0 = [m[:, HEAD_DIM:] for m in mm2]
    twg = [_mm_tn(t_, g_) for t_, g_ in zip(tw, gb)]
    vgk = [_mm_tn(v_, g_) for v_, g_ in zip(v, gk)]
    mlow = [x[:HEAD_DIM] for x in twg]
    nadd = [x[HEAD_DIM:] + y for x, y in zip(twg, vgk)]
    e_all = [tiles[ti]["e_all"][:, sl(h)] for ti, h in heads]
    return r1, y0, mlow, nadd, e_all


def _rwkv_core_kernel(rf_ref, kf_ref, vf_ref, kkf_ref, bf_ref, lwf_ref, rb_ref, kb_ref, vb_ref, kkb_ref, bb_ref,
                      lwb_ref, yf_ref, yb_ref, s_scr, *, group):
    i = pl.program_id(1)
    c = RWKV_CHUNK
    nch = rf_ref.shape[1] // c
    dirs = ((rf_ref, kf_ref, vf_ref, kkf_ref, bf_ref, lwf_ref), (rb_ref, kb_ref, vb_ref, kkb_ref, bb_ref, lwb_ref))
    y_refs = (yf_ref, yb_ref)

    @pl.when(i == 0)
    def _():
        s_scr[...] = jnp.zeros_like(s_scr)

    def step(j, states):
        tiles, reverses, rows = [], [], []
        for q in range(group):
            for d in range(2):
                cj = j * group + q
                cj = cj if d == 0 else nch - 1 - cj
                rw = pl.ds(pl.multiple_of(cj * c, c), c)
                tiles.append(_rwkv_tile_terms(*(ref[0, rw, :].astype(F32) for ref in dirs[d]), d == 1))
                reverses.append(d == 1)
                rows.append(rw)
        r1, y0, mlow, nadd, e_all = _rwkv_chunk_terms(tiles, reverses)
        states = list(states)
        for q in range(group):
            ys = [[], []]
            for d in range(2):
                for h in range(H_G):
                    n = (q * 2 + d) * H_G + h
                    s = states[d * H_G + h]
                    ys[d].append(_mm_nt(r1[n], s) + y0[n])
                    states[d * H_G + h] = s * e_all[n] + _mm(s, mlow[n]) + nadd[n]
            for d in range(2):
                y_refs[d][0, rows[q * 2 + d], :] = jnp.concatenate(ys[d], axis=1).astype(y_refs[d].dtype)
        return tuple(states)

    init = tuple(s_scr[n] for n in range(2 * H_G))
    if nch == group:
        states = step(0, init)
    else:
        states = lax.fori_loop(0, nch // group, step, init)
    for n in range(2 * H_G):
        s_scr[n] = states[n]


def _rwkv_core(r, v, kk, lwf, lwb, kf, kb, bf, bb):
    b, t, _ = r.shape
    tb = min(RWKV_BLOCK, t)
    nblk = t // tb
    fwd = pl.BlockSpec((1, tb, W_G), lambda bi, i: (bi, i, 0))
    bwd = pl.BlockSpec((1, tb, W_G), lambda bi, i: (bi, nblk - 1 - i, 0))
    out = jax.ShapeDtypeStruct((b, t, W_G), MIXER_DTYPE)
    return pl.pallas_call(
        functools.partial(_rwkv_core_kernel, group=min(RWKV_GROUP, tb // RWKV_CHUNK)),
        out_shape=(out, out),
        grid=(b, nblk),
        in_specs=[fwd] * 6 + [bwd] * 6,
        out_specs=(fwd, bwd),
        scratch_shapes=[pltpu.VMEM((2 * H_G, HEAD_DIM, HEAD_DIM), F32)],
        compiler_params=_cparams(("parallel", "arbitrary")),
        name="rwkv_core",
    )(r, kf, v, kk, bf, lwf, r, kb, v, kk, bb, lwb)


def _group_norm(y, gain, eps, bd):
    mu = _head_mean(y, bd)
    d = y - mu
    var = _head_mean(d * d, bd)
    return d * lax.rsqrt(var + eps) * gain


N_MIX_REFS = 15


def _mix_residual(refs, tiles):
    x_refs = refs[:len(tiles)]
    (ryf_ref, ryb_ref, rbon_ref, rg_ref, at_ref, mhf_ref, mhb_ref, mo_ref, tof_ref, tob_ref,
     tg_ref, rln_ref, mln_ref, tln_ref, w_ref) = refs[len(tiles):]
    bd = _head_mean_matrix(W_G)
    f = lambda ref: ref[...].astype(F32)
    o_a = (_group_norm(f(ryf_ref) + f(ryb_ref), rln_ref[...], RWKV_GN_EPS, bd) + f(rbon_ref)) * f(rg_ref)
    o_c = _group_norm(f(mhf_ref) + f(mhb_ref), mln_ref[...], HEAD_NORM_EPS, bd) * _sigmoid(mo_ref[...])
    o_d = _group_norm(f(tof_ref) + f(tob_ref), tln_ref[...], HEAD_NORM_EPS, bd) * _silu(tg_ref[...])
    mix = jnp.concatenate([o_a, at_ref[...], o_c, o_d], axis=1).astype(BF16)
    return _part_tile(pl.program_id(0), x_refs, tiles) + jnp.dot(mix, w_ref[...], preferred_element_type=F32)


def _mix_call(kernel, x_parts, rw, at, ml, rt, u2, rln, mln, tln, w_out, tm, extra, extra_specs, out_shape, out_specs,
              scratch_shapes, name, lag=0):
    tiles = _part_tiles(x_parts, tm)
    nt = sum(tiles)
    row = lambda w: pl.BlockSpec((tm, w), lambda i: (jnp.minimum(i, nt - 1), 0))
    ucol = lambda off: pl.BlockSpec((tm, W_G), lambda i: (jnp.minimum(i, nt - 1), off // W_G))
    const = lambda shape: pl.BlockSpec(shape, lambda i: (0, 0))
    return pl.pallas_call(
        functools.partial(kernel, tiles=tiles),
        out_shape=out_shape,
        grid=(nt + lag,),
        in_specs=_part_specs(x_parts, tm, D_MODEL) + [row(W_G)] * 7 + [ucol(U_MLSTM + 3 * W_G)] + [row(W_G)] * 2
                 + [ucol(U_RET + 3 * W_G)] + [const((1, W_G))] * 3
                 + [pl.BlockSpec((D_MODEL, D_MODEL), lambda i: (0, 0), pipeline_mode=pl.Buffered(1))] + extra_specs,
        out_specs=out_specs,
        scratch_shapes=scratch_shapes,
        compiler_params=_cparams(("arbitrary",)),
        name=name,
    )(*x_parts, *rw, at, *ml, u2, *rt, u2, rln, mln, tln, w_out, *extra)


def _mix_ffn_kernel(*refs, tiles):
    n_in = len(tiles) + N_MIX_REFS
    g_ref, wg_ref, wu_ref, wd_ref, o_ref, x_scr = refs[n_in:]

    @pl.when(pl.program_id(0) == 0)
    def _():
        x_scr[...] = jnp.zeros_like(x_scr)

    x = x_scr[...]
    x_next = _mix_residual(refs[:n_in], tiles)
    h = _rms_norm_rows(x, g_ref[...]).astype(BF16)
    a = jnp.dot(h, wg_ref[...], preferred_element_type=F32)
    b = jnp.dot(h, wu_ref[...], preferred_element_type=F32)
    z = (_silu(a) * b).astype(BF16)
    o_ref[...] = x + jnp.dot(z, wd_ref[...], preferred_element_type=F32)
    x_scr[...] = x_next


def _mix_ffn(mix_args, gain, wg, wu, wd, tm):
    n = sum(_part_tiles(mix_args[0], tm)) * tm
    res = lambda shape: pl.BlockSpec(shape, lambda i: (0, 0), pipeline_mode=pl.Buffered(1))
    return _mix_call(_mix_ffn_kernel, *mix_args, tm, (gain, wg, wu, wd),
                     [pl.BlockSpec((1, D_MODEL), lambda i: (0, 0)), res((D_MODEL, D_FF)), res((D_MODEL, D_FF)),
                      res((D_FF, D_MODEL))],
                     jax.ShapeDtypeStruct((n, D_MODEL), F32),
                     pl.BlockSpec((tm, D_MODEL), lambda i: (jnp.maximum(i - 1, 0), 0)),
                     [pltpu.VMEM((tm, D_MODEL), F32)], "mix_ffn", lag=1)


def _mix_router_kernel(*refs, tiles):
    n_in = len(tiles) + N_MIX_REFS
    g_ref, wr_ref, x_ref, h_ref, gate_ref, idx_ref, cnt_ref, cnt_scr = refs[n_in:]

    @pl.when(pl.program_id(0) == 0)
    def _():
        cnt_scr[...] = jnp.zeros_like(cnt_scr)

    x = _mix_residual(refs[:n_in], tiles)
    x_ref[...] = x
    h = _rms_norm_rows(x, g_ref[...])
    h_ref[...] = h.astype(BF16)
    logits = _mm_x3(h, wr_ref[...])
    tm = logits.shape[0]
    lt = logits.T[:N_EXPERTS, :]
    sub = _iota2(lt.shape, 0)
    e = jnp.exp(lt - jnp.max(lt, axis=0, keepdims=True))
    p = e / jnp.sum(e, axis=0, keepdims=True)
    m1 = jnp.max(p, axis=0, keepdims=True)
    i1 = jnp.min(jnp.where(p == m1, sub, N_EXPERTS), axis=0, keepdims=True)
    p2 = jnp.where(sub == i1, -1.0, p)
    m2 = jnp.max(p2, axis=0, keepdims=True)
    i2 = jnp.min(jnp.where(p2 == m2, sub, N_EXPERTS), axis=0, keepdims=True)
    tot = m1 + m2
    gates_t = jnp.where(sub == 0, m1 / tot, jnp.where(sub == 1, m2 / tot, 0.0))
    gate_ref[...] = jnp.concatenate([gates_t, jnp.zeros((LANES - N_EXPERTS, tm), F32)], axis=0).T
    chosen = jnp.where((sub == i1) | (sub == i2), 1.0, 0.0)
    triu = jnp.where(_iota2((tm, tm), 0) <= _iota2((tm, tm), 1), 1.0, 0.0).astype(BF16)
    incl = jnp.dot(chosen.astype(BF16), triu, preferred_element_type=F32)
    rank = jnp.tile(cnt_scr[...], (1, tm // LANES)) + incl - chosen
    r1 = jnp.sum(jnp.where(sub == i1, rank, 0.0), axis=0, keepdims=True)
    r2 = jnp.sum(jnp.where(sub == i2, rank, 0.0), axis=0, keepdims=True)
    idx_ref[...] = jnp.where(sub == 0, i1.astype(F32), jnp.where(
        sub == 1, i2.astype(F32), jnp.where(sub == 2, r1, jnp.where(sub == 3, r2, 0.0)))).astype(jnp.int32)
    cnt_scr[...] = cnt_scr[...] + jnp.sum(chosen, axis=1, keepdims=True)
    cnt_ref[...] = cnt_scr[...].astype(jnp.int32)


def _mix_router(mix_args, gain, wr_pad, tm):
    n = sum(_part_tiles(mix_args[0], tm)) * tm
    return _mix_call(
        _mix_router_kernel, *mix_args, tm, (gain, wr_pad),
        [pl.BlockSpec((1, D_MODEL), lambda i: (0, 0)), pl.BlockSpec((D_MODEL, LANES), lambda i: (0, 0))],
        (jax.ShapeDtypeStruct((n, D_MODEL), F32), jax.ShapeDtypeStruct((n, D_MODEL), BF16),
         jax.ShapeDtypeStruct((n, LANES), F32), jax.ShapeDtypeStruct((ROW_ALIGN, n), jnp.int32),
         jax.ShapeDtypeStruct((ROW_ALIGN, LANES), jnp.int32)),
        (pl.BlockSpec((tm, D_MODEL), lambda i: (i, 0)), pl.BlockSpec((tm, D_MODEL), lambda i: (i, 0)),
         pl.BlockSpec((tm, LANES), lambda i: (i, 0)), pl.BlockSpec((ROW_ALIGN, tm), lambda i: (0, i)),
         pl.BlockSpec((ROW_ALIGN, LANES), lambda i: (0, 0))),
        [pltpu.VMEM((N_EXPERTS, LANES), F32)], "mix_router")


def _expert_ffn_kernel(te_ref, nv_ref, xs_ref, wg_ref, wu_ref, wd_ref, o_ref):
    i = pl.program_id(0)

    @pl.when(i < nv_ref[0])
    def _():
        h = xs_ref[...]
        fw = D_FF // EXPERT_FF_SPLIT
        acc = None
        for f in range(EXPERT_FF_SPLIT):
            cols = slice(f * fw, (f + 1) * fw)
            a = jnp.dot(h, wg_ref[0, :, cols], preferred_element_type=F32)
            b = jnp.dot(h, wu_ref[0, :, cols], preferred_element_type=F32)
            z = (_silu(a) * b).astype(BF16)
            y = jnp.dot(z, wd_ref[0, cols, :], preferred_element_type=F32)
            acc = y if acc is None else acc + y
        o_ref[...] = acc.astype(o_ref.dtype)

    @pl.when(i >= nv_ref[0])
    def _():
        o_ref[...] = jnp.zeros_like(o_ref)


def _expert_ffn(xs, tile_expert, n_valid, wg, wu, wd, tm):
    rows = xs.shape[0]
    wspec = lambda shape: pl.BlockSpec((1,) + shape, lambda i, te, nv: (te[i], 0, 0))
    return pl.pallas_call(
        _expert_ffn_kernel,
        out_shape=jax.ShapeDtypeStruct((rows, D_MODEL), BF16),
        grid_spec=pltpu.PrefetchScalarGridSpec(
            num_scalar_prefetch=2,
            grid=(rows // tm,),
            in_specs=[pl.BlockSpec((tm, D_MODEL), lambda i, te, nv: (i, 0)),
                      wspec((D_MODEL, D_FF)), wspec((D_MODEL, D_FF)), wspec((D_FF, D_MODEL))],
            out_specs=pl.BlockSpec((tm, D_MODEL), lambda i, te, nv: (i, 0)),
        ),
        compiler_params=_cparams(("arbitrary",)),
        name="expert_ffn",
    )(tile_expert, n_valid, xs, wg, wu, wd)


def _moe_combine_kernel(x_ref, y1_ref, y2_ref, gate_ref, nf_ref, *o_refs, tiles):
    i = pl.program_id(0)
    g = gate_ref[...]
    y = x_ref[...] + g[:, 0:1] * y1_ref[...].astype(F32) + g[:, 1:2] * y2_ref[...].astype(F32)
    out = _rms_norm_rows(y, nf_ref[...])
    start = 0
    for o_ref, nt in zip(o_refs, tiles):
        @pl.when((i >= start) & (i < start + nt))
        def _(o_ref=o_ref):
            o_ref[...] = out
        start += nt


def _moe_combine(x2, y1, y2, gates, norm_final, tm, part_rows):
    n = x2.shape[0]
    row = lambda w: pl.BlockSpec((tm, w), lambda i: (i, 0))
    outs = tuple(jax.ShapeDtypeStruct((r, D_MODEL), F32) for r in part_rows)
    return pl.pallas_call(
        functools.partial(_moe_combine_kernel, tiles=_part_tiles(outs, tm)),
        out_shape=outs,
        grid=(n // tm,),
        in_specs=[row(D_MODEL), row(D_MODEL), row(D_MODEL), row(LANES), pl.BlockSpec((1, D_MODEL), lambda i: (0, 0))],
        out_specs=tuple(_part_specs(outs, tm, D_MODEL)),
        compiler_params=_cparams(("arbitrary",)),
        name="moe_combine",
    )(x2, y1, y2, gates, norm_final)


def _moe(x2, h, gates, idx, counts, wg, wu, wd, norm_final, tm, part_rows):
    n = x2.shape[0]
    tme = EXPERT_TILE
    n_tiles = (2 * n + N_EXPERTS * (tme - 1)) // tme + 1
    e1, e2, r1, r2 = idx[0], idx[1], idx[2], idx[3]
    cnt = counts[:N_EXPERTS, 0]
    padded = (cnt + tme - 1) // tme * tme
    group_end = jnp.cumsum(padded)
    group_off = group_end - padded
    dense_off = jnp.cumsum(cnt) - cnt
    lookup = lambda table, e: sum(jnp.where(e == k, table[k], 0) for k in range(N_EXPERTS))
    slot1 = lookup(group_off, e1) + r1
    slot2 = lookup(group_off, e2) + r2
    tok = jnp.arange(n, dtype=jnp.int32)
    sorted_tok = jnp.sort(jnp.concatenate([e1 * n + tok, e2 * n + tok])) % n
    tile_start = jnp.arange(n_tiles, dtype=jnp.int32) * tme
    tile_expert = jnp.minimum(jnp.searchsorted(group_end, tile_start, side='right'), N_EXPERTS - 1).astype(jnp.int32)
    n_valid = (group_end[-1:] // tme).astype(jnp.int32)
    rank = (tile_start - group_off[tile_expert])[:, None] + jnp.arange(tme, dtype=jnp.int32)[None, :]
    dense = jnp.clip(dense_off[tile_expert][:, None] + rank, 0, 2 * n - 1)
    take = functools.partial(jnp.take, axis=0, mode='clip')
    spread = (tile_start[:, None] + jnp.arange(tme, dtype=jnp.int32)[None, :]) % n
    src = jnp.where(rank < cnt[tile_expert][:, None], take(sorted_tok, dense.reshape(-1)).reshape(dense.shape), spread)
    xs = take(h, src.reshape(-1))
    ys = _expert_ffn(xs, tile_expert, n_valid, wg, wu, wd, tme)
    y1 = take(ys, slot1)
    y2 = take(ys, slot2)
    return _moe_combine(x2, y1, y2, gates, norm_final, tm, part_rows)


def _rope_tables(t):
    rows = t // GRID_W
    pos = np.arange(rows * GRID_W)
    row = (pos // GRID_W).astype(np.float32)
    col = (pos % GRID_W).astype(np.float32)
    nf = HEAD_DIM // 4
    inv = jnp.asarray(ROPE_THETA, F32) ** (-jnp.arange(nf, dtype=F32) / nf)
    ar = jnp.asarray(row)[:, None] * inv
    ac = jnp.asarray(col)[:, None] * inv
    cos = jnp.concatenate([jnp.cos(ar), jnp.cos(ar), jnp.cos(ac), jnp.cos(ac)], axis=-1)
    sin = jnp.concatenate([-jnp.sin(ar), jnp.sin(ar), -jnp.sin(ac), jnp.sin(ac)], axis=-1)
    return jnp.tile(cos, (1, H_G)), jnp.tile(sin, (1, H_G))


def _pad_w_in(w):
    a, b_, c, d = 1024, 512, 1040, 1024
    w_a, w_b, w_c, w_d = w[:, :a], w[:, a:a + b_], w[:, a + b_:a + b_ + c], w[:, a + b_ + c:]
    gates = jnp.pad(w_c[:, 1024:], ((0, 0), (0, LANES - 16)))
    return jnp.concatenate([w_a, w_c[:, :1024], w_d, w_b, gates], axis=1).astype(BF16)


def _row(v):
    return v.reshape(1, -1).astype(F32)


def _trunk(xs, p):
    t = xs[0].shape[1]
    part_rows = tuple(x.shape[0] * t for x in xs)
    b = sum(x.shape[0] for x in xs)
    n = b * t
    tm = 256
    cos, sin = _rope_tables(t)
    x_parts = tuple(x.reshape(-1, D_MODEL) for x in xs)
    depth = p['w_in'].shape[0]
    for l in range(depth):
        tm_in = 2 * tm if t % (2 * tm) == 0 else tm
        u2 = _in_proj(x_parts, _row(p['norm_mix'][l]), _pad_w_in(p['w_in'][l]), cos, sin,
                      _row(jnp.tile(p['attn_q_norm'][l], H_G)), _row(jnp.tile(p['attn_k_norm'][l], KV_ATTN)), tm_in)
        u3 = u2.reshape(b, t, U_COLS)
        (r, v, kk, g, bonus, lwf, lwb, kf, kb, bf, bb) = _rwkv_prep(
            u3, _row(p['rwkv_mu'][l]), p['rwkv_w0'][l], p['rwkv_w2'][l], p['rwkv_a0'][l], p['rwkv_a2'][l],
            p['rwkv_g2'][l], _row(p['rwkv_kk'][l]), _row(p['rwkv_ka'][l]), _row(p['rwkv_rk'][l]))
        yf, yb = _rwkv_core(r, v, kk, lwf, lwb, kf, kb, bf, bb)
        at = _attention(u3)
        gate_bias = jnp.pad(jnp.concatenate([p['mlstm_i_bias'][l].reshape(-1), p['mlstm_f_bias'][l].reshape(-1)]),
                            (0, LANES - 4 * H_G))
        hf, hb = _mlstm(u3, p['mlstm_conv_w'][l], _row(p['mlstm_conv_b'][l]), _row(gate_bias))
        of, ob = _retention(u3)
        flat = lambda z: z.reshape(n, W_G)
        mix_args = (x_parts, tuple(map(flat, (yf, yb, bonus, g))), flat(at), tuple(map(flat, (hf, hb))),
                    tuple(map(flat, (of, ob))), u2, _row(p['rwkv_ln'][l]), _row(p['mlstm_ln'][l]),
                    _row(p['ret_ln'][l]), p['w_out'][l].astype(BF16))
        j = l // 2
        if l % 2 == 0:
            x2 = _mix_ffn(mix_args, _row(p['norm_ffn'][l]), p['ffn_w_gate'][j].astype(BF16),
                          p['ffn_w_up'][j].astype(BF16), p['ffn_w_down'][j].astype(BF16), tm)
            x_parts = (x2,)
            if l == depth - 1:
                raise NotImplementedError("final norm after a dense FFN layer")
        else:
            wr = jnp.pad(p['moe_router'][j], ((0, 0), (0, LANES - N_EXPERTS)))
            x2, h, gates, idx, counts = _mix_router(mix_args, _row(p['norm_ffn'][l]), wr, tm)
            if l != depth - 1:
                raise NotImplementedError("expert layer that is not the last layer")
            outs = _moe(x2, h, gates, idx, counts, p['moe_w_gate'][j].astype(BF16), p['moe_w_up'][j].astype(BF16),
                        p['moe_w_down'][j].astype(BF16), _row(p['norm_final']), tm, part_rows)
    return tuple(o.reshape(x.shape) for o, x in zip(outs, xs))


def kernel(x_prompt, x_sample, norm_mix, norm_ffn, norm_final, w_in, w_out, rwkv_mu, rwkv_w0, rwkv_w2,
           rwkv_a0, rwkv_a2, rwkv_g2, rwkv_kk, rwkv_ka, rwkv_rk, rwkv_ln, attn_q_norm, attn_k_norm,
           mlstm_conv_w, mlstm_conv_b, mlstm_i_bias, mlstm_f_bias, mlstm_ln, ret_ln, ffn_w_gate, ffn_w_up,
           ffn_w_down, moe_router, moe_w_gate, moe_w_up, moe_w_down):
    p = dict(norm_mix=norm_mix, norm_ffn=norm_ffn, norm_final=norm_final, w_in=w_in, w_out=w_out,
             rwkv_mu=rwkv_mu, rwkv_w0=rwkv_w0, rwkv_w2=rwkv_w2, rwkv_a0=rwkv_a0, rwkv_a2=rwkv_a2,
             rwkv_g2=rwkv_g2, rwkv_kk=rwkv_kk, rwkv_ka=rwkv_ka, rwkv_rk=rwkv_rk, rwkv_ln=rwkv_ln,
             attn_q_norm=attn_q_norm, attn_k_norm=attn_k_norm, mlstm_conv_w=mlstm_conv_w,
             mlstm_conv_b=mlstm_conv_b, mlstm_i_bias=mlstm_i_bias, mlstm_f_bias=mlstm_f_bias,
             mlstm_ln=mlstm_ln, ret_ln=ret_ln, ffn_w_gate=ffn_w_gate, ffn_w_up=ffn_w_up,
             ffn_w_down=ffn_w_down, moe_router=moe_router, moe_w_gate=moe_w_gate, moe_w_up=moe_w_up,
             moe_w_down=moe_w_down)
    return _trunk((x_prompt, x_sample), p)
```

```python
import functools
import math

import numpy as np
import jax
import jax.numpy as jnp
from jax import lax
from jax.experimental import pallas as pl
from jax.experimental.pallas import tpu as pltpu

F32 = jnp.float32
BF16 = jnp.bfloat16
MIXER_DTYPE = BF16

D_MODEL = 1024
HEAD_DIM = 64
W_G = 256
H_G = 4
KV_ATTN = 2
D_FF = 2816
N_EXPERTS = 8
NORM_EPS = 1e-6
HEAD_NORM_EPS = 1e-5
RWKV_GN_EPS = 64e-5
NEG_INF = -1e30
ROPE_THETA = 10000.0
GRID_W = 64

LANES = 128
ROW_ALIGN = 8
VMEM_LIMIT_BYTES = 56 * 1024 * 1024

U_RWKV = 0
U_MLSTM = 1024
U_RET = 2048
U_ATTN = 3072
U_GATE = 3584
U_COLS = 3712

RWKV_CHUNK = 64
RWKV_BLOCK = 256
RWKV_GROUP = 4
MIX_CHUNK = 128
MIX_BLOCK = 256
EXPERT_TILE = 256
EXPERT_FF_SPLIT = 1


def _cparams(sem):
    return pltpu.CompilerParams(dimension_semantics=sem, vmem_limit_bytes=VMEM_LIMIT_BYTES)


def _bdot(a, b, dims):
    return lax.dot_general(a, b, (dims, ((), ())), preferred_element_type=F32)


def _mm(a, b):
    return _bdot(a.astype(BF16), b.astype(BF16), ((1,), (0,)))


def _mm_nt(a, b):
    return _bdot(a.astype(BF16), b.astype(BF16), ((1,), (1,)))


def _mm_tn(a, b):
    return _bdot(a.astype(BF16), b.astype(BF16), ((0,), (0,)))


def _split2(a):
    hi = a.astype(BF16)
    lo = (a - hi.astype(F32)).astype(BF16)
    return hi, lo


def _split3(a):
    hi = a.astype(BF16)
    r = a - hi.astype(F32)
    mid = r.astype(BF16)
    lo = (r - mid.astype(F32)).astype(BF16)
    return hi, mid, lo


def _mm_l2(a, b_exact):
    hi, lo = _split2(a)
    return _bdot(hi, b_exact, ((1,), (0,))) + _bdot(lo, b_exact, ((1,), (0,)))


def _mm_l3(a, b_exact):
    h, m, l = _split3(a)
    return _bdot(h, b_exact, ((1,), (0,))) + _bdot(m, b_exact, ((1,), (0,))) + _bdot(l, b_exact, ((1,), (0,)))


def _mm_r3(a_exact, b):
    h, m, l = _split3(b)
    return _bdot(a_exact, h, ((1,), (0,))) + _bdot(a_exact, m, ((1,), (0,))) + _bdot(a_exact, l, ((1,), (0,)))


def _mm_x3(a, b):
    ah, al = _split2(a)
    bh, bl = _split2(b)
    d = ((1,), (0,))
    return _bdot(ah, bh, d) + _bdot(ah, bl, d) + _bdot(al, bh, d)


def _iota2(shape, axis):
    return lax.broadcasted_iota(jnp.int32, shape, axis)


def _head_mean_matrix(width):
    r = _iota2((width, width), 0) // HEAD_DIM
    c = _iota2((width, width), 1) // HEAD_DIM
    return jnp.where(r == c, 1.0 / HEAD_DIM, 0.0).astype(BF16)


def _head_mean(z, bd):
    return _bdot(z.astype(BF16), bd, ((1,), (0,)))


def _sigmoid(x):
    return 1.0 / (1.0 + jnp.exp(-x))


def _silu(x):
    return x * _sigmoid(x)


def _log_sigmoid(x):
    return jnp.minimum(x, 0.0) - jnp.log(1.0 + jnp.exp(-jnp.abs(x)))


def _rms_norm_rows(x, gain):
    ms = jnp.mean(x * x, axis=-1, keepdims=True)
    return x * lax.rsqrt(ms + NORM_EPS) * gain


def _rope_swap(z):
    w = z.shape[-1]
    lane = _iota2(z.shape, z.ndim - 1)
    fwd = pltpu.roll(z, w - 16, z.ndim - 1)
    bwd = pltpu.roll(z, 16, z.ndim - 1)
    return jnp.where((lane % 32) < 16, fwd, bwd)


def _rope(z, cos, sin):
    return z * cos + _rope_swap(z) * sin


def _shift_rows(x, prev_row, next_row):
    n = x.shape[0]
    row = _iota2(x.shape, 0)
    prev = jnp.where(row == 0, prev_row, pltpu.roll(x, 1, 0))
    nxt = jnp.where(row == n - 1, next_row, pltpu.roll(x, n - 1, 0))
    return prev, nxt


def _part_tiles(parts, tm):
    return tuple(p.shape[0] // tm for p in parts)


def _part_specs(parts, tm, width):
    specs, start = [], 0
    for nt in _part_tiles(parts, tm):
        specs.append(pl.BlockSpec((tm, width), lambda i, s=start, nt=nt: (jnp.clip(i - s, 0, nt - 1), 0)))
        start += nt
    return specs


def _part_tile(i, refs, tiles):
    x = refs[0][...]
    start = tiles[0]
    for ref, nt in zip(refs[1:], tiles[1:]):
        x = jnp.where(i >= start, ref[...], x)
        start += nt
    return x


def _inproj_kernel(*refs, tiles):
    x_refs, (g_ref, w_ref, cos_ref, sin_ref, qg_ref, kg_ref, o_ref) = refs[:len(tiles)], refs[len(tiles):]
    h = _rms_norm_rows(_part_tile(pl.program_id(0), x_refs, tiles), g_ref[...])
    u = jnp.dot(h.astype(BF16), w_ref[...], preferred_element_type=F32)
    o_ref[...] = u
    cos = cos_ref[...]
    sin = sin_ref[...]
    kw = KV_ATTN * HEAD_DIM
    o_ref[:, U_RET:U_RET + W_G] = _rope(u[:, U_RET:U_RET + W_G], cos, sin)
    o_ref[:, U_RET + W_G:U_RET + 2 * W_G] = _rope(u[:, U_RET + W_G:U_RET + 2 * W_G], cos, sin) * HEAD_DIM ** -0.5
    q = u[:, U_ATTN:U_ATTN + W_G]
    k = u[:, U_ATTN + W_G:U_ATTN + W_G + kw]
    qn = q * lax.rsqrt(_head_mean(q * q, _head_mean_matrix(W_G)) + NORM_EPS) * qg_ref[...]
    kn = k * lax.rsqrt(_head_mean(k * k, _head_mean_matrix(kw)) + NORM_EPS) * kg_ref[...]
    o_ref[:, U_ATTN:U_ATTN + W_G] = _rope(qn, cos, sin) * (HEAD_DIM ** -0.5 * math.log2(math.e))
    o_ref[:, U_ATTN + W_G:U_ATTN + W_G + kw] = _rope(kn, cos[:, :kw], sin[:, :kw])


def _in_proj(x_parts, gain, w_pad, cos, sin, q_gain, k_gain, tm):
    tiles = _part_tiles(x_parts, tm)
    n = sum(tiles) * tm
    assert cos.shape[0] % tm == 0
    t_tiles = cos.shape[0] // tm
    const = lambda shape: pl.BlockSpec(shape, lambda i: (0, 0))
    table = pl.BlockSpec((tm, W_G), lambda i: (i % t_tiles, 0))
    return pl.pallas_call(
        functools.partial(_inproj_kernel, tiles=tiles),
        out_shape=jax.ShapeDtypeStruct((n, U_COLS), F32),
        grid=(n // tm,),
        in_specs=_part_specs(x_parts, tm, D_MODEL) + [
            const((1, D_MODEL)),
            pl.BlockSpec((D_MODEL, U_COLS), lambda i: (0, 0), pipeline_mode=pl.Buffered(1)),
            table, table, const((1, W_G)), const((1, KV_ATTN * HEAD_DIM)),
        ],
        out_specs=pl.BlockSpec((tm, U_COLS), lambda i: (i, 0)),
        compiler_params=_cparams(("arbitrary",)),
        name="in_proj",
    )(*x_parts, gain, w_pad, cos, sin, q_gain, k_gain)


def _attn_kernel(u_ref, o_ref, q_scr, k_scr, v_scr, *, tq):
    t = u_ref.shape[1]
    u = u_ref[0]
    v = u[:, W_G + KV_ATTN * HEAD_DIM:]
    q_scr[...] = u[:, :W_G].astype(BF16)
    k_scr[...] = u[:, W_G:W_G + KV_ATTN * HEAD_DIM].astype(BF16)
    ones = jnp.ones((t, HEAD_DIM), BF16)
    for j in range(KV_ATTN):
        vj = v[:, j * HEAD_DIM:(j + 1) * HEAD_DIM].astype(BF16)
        v_scr[:, j * LANES:(j + 1) * LANES] = jnp.concatenate([vj, ones], axis=1)
    group = H_G // KV_ATTN

    def q_tile(i, carry):
        rows = pl.ds(pl.multiple_of(i * tq, tq), tq)

        def scores(j):
            q = jnp.concatenate([q_scr[rows, (j * group + g) * HEAD_DIM:(j * group + g + 1) * HEAD_DIM]
                                 for g in range(group)], axis=0)
            return _bdot(q, k_scr[:, j * HEAD_DIM:(j + 1) * HEAD_DIM], ((1,), (1,)))

        s_next = scores(0)
        for j in range(KV_ATTN):
            s = s_next
            if j + 1 < KV_ATTN:
                s_next = scores(j + 1)
            m = jnp.max(s, axis=-1, keepdims=True)
            p = jnp.exp2(s - m)
            r = jnp.dot(p.astype(BF16), v_scr[:, j * LANES:(j + 1) * LANES], preferred_element_type=F32)
            o = r[:, :HEAD_DIM] / r[:, HEAD_DIM:]
            for g in range(group):
                h = j * group + g
                o_ref[0, rows, h * HEAD_DIM:(h + 1) * HEAD_DIM] = o[g * tq:(g + 1) * tq]
        return carry

    lax.fori_loop(0, t // tq, q_tile, 0)


def _attention(u3):
    b, t, _ = u3.shape
    tq = min(256, t)
    col = U_ATTN // 512
    return pl.pallas_call(
        functools.partial(_attn_kernel, tq=tq),
        out_shape=jax.ShapeDtypeStruct((b, t, W_G), F32),
        grid=(b,),
        in_specs=[
            pl.BlockSpec((1, t, 512), lambda i: (i, 0, col)),
        ],
        out_specs=pl.BlockSpec((1, t, W_G), lambda i: (i, 0, 0)),
        scratch_shapes=[
            pltpu.VMEM((t, W_G), BF16),
            pltpu.VMEM((t, KV_ATTN * HEAD_DIM), BF16),
            pltpu.VMEM((t, KV_ATTN * LANES), BF16),
        ],
        compiler_params=_cparams(("parallel",)),
        name="attention",
    )(u3)


def _ret_log_gamma(direction):
    return [math.log1p(-2.0 ** (-5.0 - (2 * h + direction) / 2.0)) for h in range(H_G)]


def _ret_kernel(uf_ref, ub_ref, of_ref, ob_ref, rf_scr, rb_scr):
    i = pl.program_id(1)
    c = MIX_CHUNK

    @pl.when(i == 0)
    def _():
        rf_scr[...] = jnp.zeros_like(rf_scr)
        rb_scr[...] = jnp.zeros_like(rb_scr)

    tt = _iota2((c, c), 0)
    ss = _iota2((c, c), 1)
    diff = (tt - ss).astype(F32)
    jcol = tt.astype(F32)
    first = ss < HEAD_DIM
    block_diag = (tt < HEAD_DIM) == first
    lg_f = _ret_log_gamma(0)
    lg_b = _ret_log_gamma(1)
    psl = lambda p: slice(p * LANES, (p + 1) * LANES)
    lane_lg = lambda lg, p: jnp.where(first, lg[2 * p], lg[2 * p + 1])
    row_decay = lambda lg, p: jnp.where(tt < HEAD_DIM, math.exp(c * lg[2 * p]), math.exp(c * lg[2 * p + 1]))
    own = lambda h, x: jnp.where(first, x, 0.0) if h % 2 == 0 else jnp.where(first, 0.0, x)
    pairs = range(H_G // 2)

    nch = uf_ref.shape[1] // c
    uf = uf_ref[0]
    qf_all, kf_all = uf[:, :W_G], uf[:, W_G:2 * W_G]
    ub = ub_ref[0]
    qb_all, kb_all = ub[:, :W_G], ub[:, W_G:2 * W_G]
    rf_prev = [rf_scr[p] for p in pairs]
    rb_prev = [rb_scr[p] for p in pairs]
    decay = [jnp.where(tt >= ss, jnp.exp(diff * lg_f[h]), 0.0) + jnp.where(ss >= tt, jnp.exp(-diff * lg_b[h]), 0.0)
             for h in range(H_G)]
    for n in range(nch):
        rf_rows = slice(n * c, (n + 1) * c)
        rb_rows = slice((nch - 1 - n) * c, (nch - n) * c)
        qf, kf, vf = qf_all[rf_rows], kf_all[rf_rows], uf[rf_rows, 2 * W_G:3 * W_G]
        qb, kb, vb = qb_all[rb_rows], kb_all[rb_rows], ub[rb_rows, 2 * W_G:3 * W_G]
        qk = [_mm_nt(qf[:, psl(h // 2)], own(h, kf[:, psl(h // 2)])) for h in range(H_G)]
        intra = [_mm(qk[h] * decay[h], own(h, vf[:, psl(h // 2)])) for h in range(H_G)]
        inter_f = [_mm(qf[:, psl(p)] * jnp.exp((jcol + 1.0) * lane_lg(lg_f, p)), rf_prev[p]) for p in pairs]
        inter_b = [_mm(qb[:, psl(p)] * jnp.exp((c - jcol) * lane_lg(lg_b, p)), rb_prev[p]) for p in pairs]
        upd_f = [_mm_tn(kf[:, psl(p)] * jnp.exp((c - 1.0 - jcol) * lane_lg(lg_f, p)), vf[:, psl(p)]) for p in pairs]
        upd_b = [_mm_tn(kb[:, psl(p)] * jnp.exp(jcol * lane_lg(lg_b, p)), vb[:, psl(p)]) for p in pairs]
        for p in pairs:
            of_ref[0, rf_rows, psl(p)] = (intra[2 * p] + intra[2 * p + 1] + inter_f[p]).astype(of_ref.dtype)
            ob_ref[0, rb_rows, psl(p)] = inter_b[p].astype(ob_ref.dtype)
        rf_prev = [row_decay(lg_f, p) * rf_prev[p] + jnp.where(block_diag, upd_f[p], 0.0) for p in pairs]
        rb_prev = [row_decay(lg_b, p) * rb_prev[p] + jnp.where(block_diag, upd_b[p], 0.0) for p in pairs]
    for p in pairs:
        rf_scr[p] = rf_prev[p]
        rb_scr[p] = rb_prev[p]


def _retention(u3):
    b, t, _ = u3.shape
    c = min(MIX_BLOCK, t)
    nblk = t // c
    col = U_RET // 1024
    return pl.pallas_call(
        _ret_kernel,
        out_shape=(jax.ShapeDtypeStruct((b, t, W_G), MIXER_DTYPE), jax.ShapeDtypeStruct((b, t, W_G), MIXER_DTYPE)),
        grid=(b, nblk),
        in_specs=[
            pl.BlockSpec((1, c, 1024), lambda bi, i: (bi, i, col)),
            pl.BlockSpec((1, c, 1024), lambda bi, i: (bi, nblk - 1 - i, col)),
        ],
        out_specs=(
            pl.BlockSpec((1, c, W_G), lambda bi, i: (bi, i, 0)),
            pl.BlockSpec((1, c, W_G), lambda bi, i: (bi, nblk - 1 - i, 0)),
        ),
        scratch_shapes=[pltpu.VMEM((H_G // 2, LANES, LANES), F32), pltpu.VMEM((H_G // 2, LANES, LANES), F32)],
        compiler_params=_cparams(("parallel", "arbitrary")),
        name="retention",
    )(u3, u3)


def _mlstm_tile(u_ref, up_ref, un_ref, g_ref, cw_ref, cb_ref, gb_ref, blk, nblk, direction):
    c = MIX_CHUNK
    reverse = direction == 1
    u = u_ref[0]
    qk = u[:, :2 * W_G]
    prev_row = jnp.where(blk == 0, 0.0, up_ref[0][ROW_ALIGN - 1:ROW_ALIGN, :])
    next_row = jnp.where(blk == nblk - 1, 0.0, un_ref[0][0:1, :])
    prev, nxt = _shift_rows(qk, prev_row, next_row)
    cw = cw_ref[...]
    qk = _silu(cw[0:1] * prev + cw[1:2] * qk + cw[2:3] * nxt + cb_ref[...])
    qa = qk[:, :W_G]
    ka = qk[:, W_G:] * HEAD_DIM ** -0.5
    va = u[:, 2 * W_G:3 * W_G]

    x = g_ref[0] + gb_ref[...]
    xt = x.T
    lf_c = _log_sigmoid(x)
    lf_rows = _log_sigmoid(xt[2 * H_G:4 * H_G, :])
    lf_r = jnp.concatenate([jnp.zeros((2 * H_G, c), F32), lf_rows, jnp.zeros((c - 4 * H_G, c), F32)], axis=0)
    tt = _iota2((c, c), 0)
    ss = _iota2((c, c), 1)
    lower = jnp.where(ss <= tt, 1.0, 0.0).astype(BF16)
    upper = jnp.where(tt <= ss, 1.0, 0.0).astype(BF16)
    if reverse:
        b_c = _mm_r3(upper, lf_c)
        b_r = _mm_l3(lf_r, lower)
        mask = ss >= tt
    else:
        b_c = _mm_r3(lower, lf_c)
        b_r = _mm_l3(lf_r, upper)
        mask = ss <= tt
    return dict(q=qa, k=ka, v=va, x=x, xt=xt, b_c=b_c, b_r=b_r, mask=mask)


def _mlstm_select_matrix():
    sel = np.zeros((2, 2 * LANES, 8 * LANES), np.float32)
    for d in range(2):
        for h in range(H_G):
            ci, cf = d * H_G + h, 2 * H_G + d * H_G + h
            p, j = divmod(h, 2)
            sel[d, cf, LANES * h:LANES * (h + 1)] = 1.0
            sel[d, cf, 4 * LANES + LANES * p + HEAD_DIM * j:4 * LANES + LANES * p + HEAD_DIM * (j + 1)] = 1.0
            sel[d, LANES + ci, 6 * LANES + LANES * p + HEAD_DIM * j:6 * LANES + LANES * p + HEAD_DIM * (j + 1)] = 1.0
    return jnp.asarray(sel, BF16)


def _mlstm_kernel(uf_ref, upf_ref, unf_ref, gf_ref, ub_ref, upb_ref, unb_ref, gbk_ref,
                  cw_ref, cb_ref, gb_ref, sel_ref, of_ref, ob_ref, st_scr, m_scr):
    i = pl.program_id(1)
    nblk = pl.num_programs(1)
    c = MIX_CHUNK

    @pl.when(i == 0)
    def _():
        st_scr[...] = jnp.zeros_like(st_scr)
        m_scr[...] = jnp.zeros_like(m_scr)

    tiles = (_mlstm_tile(uf_ref, upf_ref, unf_ref, gf_ref, cw_ref, cb_ref, gb_ref, i, nblk, 0),
             _mlstm_tile(ub_ref, upb_ref, unb_ref, gbk_ref, cw_ref, cb_ref, gb_ref, nblk - 1 - i, nblk, 1))
    o_refs = (of_ref, ob_ref)
    sel = [_mm_l2(jnp.concatenate([tiles[d]["b_c"], tiles[d]["x"]], axis=1), sel_ref[d]) for d in range(2)]
    first = _iota2((c, LANES), 1) < HEAD_DIM
    row_first = _iota2((LANES, 2 * LANES), 0) < HEAD_DIM
    lane2 = _iota2((LANES, 2 * LANES), 1) % LANES < HEAD_DIM
    block_diag = row_first == lane2
    ones = jnp.ones((c, LANES), F32)
    pairs = [(d, p) for d in range(2) for p in range(H_G // 2)]
    heads = [(d, h) for d in range(2) for h in range(H_G)]
    psl = lambda p: slice(p * LANES, (p + 1) * LANES)
    q_pair = {dp: tiles[dp[0]]["q"][:, psl(dp[1])] for dp in pairs}
    k_pair = {dp: tiles[dp[0]]["k"][:, psl(dp[1])] for dp in pairs}
    v_pair = {dp: tiles[dp[0]]["v"][:, psl(dp[1])] for dp in pairs}
    state = {dp: st_scr[n] for n, dp in enumerate(pairs)}
    m_row = {dp: m_scr[n:n + 1, :] for n, dp in enumerate(pairs)}

    def own(d, h, x):
        return jnp.where(first, x, 0.0) if h % 2 == 0 else jnp.where(first, 0.0, x)

    qk = [_mm_nt(q_pair[(d, h // 2)], own(d, h, k_pair[(d, h // 2)])) for d, h in heads]
    qs = {dp: _mm(q_pair[dp], state[dp]) for dp in pairs}
    bc = [sel[d][:, LANES * h:LANES * (h + 1)] for d, h in heads]
    m_prev = [m_row[(d, h // 2)][:, HEAD_DIM * (h % 2):HEAD_DIM * (h % 2) + 1] for d, h in heads]
    dlog = []
    for (d, h), bc_ in zip(heads, bc):
        ci, cf = d * H_G + h, 2 * H_G + d * H_G + h
        rowterm = tiles[d]["xt"][ci:ci + 1, :] - tiles[d]["b_r"][cf:cf + 1, :]
        dlog.append(jnp.where(tiles[d]["mask"], bc_ + rowterm, NEG_INF))
    inter_log = [bc_ + m_ for bc_, m_ in zip(bc, m_prev)]
    m_t = [jnp.maximum(il, jnp.max(dl, axis=-1, keepdims=True)) for il, dl in zip(inter_log, dlog)]
    sc = [qk_ * jnp.exp(dl - mt) for qk_, dl, mt in zip(qk, dlog, m_t)]
    w_inter = [jnp.exp(il - mt) for il, mt in zip(inter_log, m_t)]
    e_neg = [jnp.exp(-mt) for mt in m_t]
    res = []
    for n, (d, h) in enumerate(heads):
        v_aug = jnp.concatenate([own(d, h, v_pair[(d, h // 2)]), own(d, h, ones)], axis=1)
        res.append(_mm(sc[n], v_aug))
    for n, (d, p) in enumerate(pairs):
        a, b_ = 2 * n, 2 * n + 1
        tot = res[a] + res[b_] + jnp.tile(jnp.where(first, w_inter[a], w_inter[b_]), (1, 2)) * qs[(d, p)]
        den = jnp.maximum(jnp.abs(tot[:, LANES:]), jnp.where(first, e_neg[a], e_neg[b_]))
        o_refs[d][0, :, psl(p)] = (tot[:, :LANES] / den).astype(o_refs[d].dtype)

    for n, (d, p) in enumerate(pairs):
        bcp = sel[d][:, 4 * LANES + LANES * p:4 * LANES + LANES * (p + 1)]
        lip = sel[d][:, 6 * LANES + LANES * p:6 * LANES + LANES * (p + 1)]
        g_row = bcp[0:1, :] if d == 1 else bcp[c - 1:c, :]
        a_p = g_row - bcp + lip
        m_new = jnp.maximum(g_row + m_row[(d, p)], jnp.max(a_p, axis=0, keepdims=True))
        dec = jnp.exp(g_row + m_row[(d, p)] - m_new)
        kw_t = (k_pair[(d, p)] * jnp.exp(a_p - m_new)).T
        upd = _mm(kw_t, jnp.concatenate([v_pair[(d, p)], ones], axis=1))
        dec_tile = jnp.where(row_first, dec[:, 0:1], dec[:, HEAD_DIM:HEAD_DIM + 1])
        st_scr[n] = dec_tile * state[(d, p)] + jnp.where(block_diag, upd, 0.0)
        m_scr[n:n + 1, :] = m_new


def _mlstm(u3, conv_w, conv_b, gate_bias):
    b, t, _ = u3.shape
    c = MIX_CHUNK
    nblk = t // c
    rpb = c // ROW_ALIGN
    n8 = t // ROW_ALIGN
    col = U_MLSTM // 1024
    hcol = U_MLSTM // 512
    gcol = U_GATE // LANES

    def specs(rev):
        blk = (lambda i: nblk - 1 - i) if rev else (lambda i: i)
        return [
            pl.BlockSpec((1, c, 1024), lambda bi, i: (bi, blk(i), col)),
            pl.BlockSpec((1, ROW_ALIGN, 512), lambda bi, i: (bi, jnp.maximum(blk(i) * rpb - 1, 0), hcol)),
            pl.BlockSpec((1, ROW_ALIGN, 512), lambda bi, i: (bi, jnp.minimum((blk(i) + 1) * rpb, n8 - 1), hcol)),
            pl.BlockSpec((1, c, LANES), lambda bi, i: (bi, blk(i), gcol)),
        ]

    const = lambda shape: pl.BlockSpec(shape, lambda bi, i: (0,) * len(shape))
    return pl.pallas_call(
        _mlstm_kernel,
        out_shape=(jax.ShapeDtypeStruct((b, t, W_G), MIXER_DTYPE), jax.ShapeDtypeStruct((b, t, W_G), MIXER_DTYPE)),
        grid=(b, nblk),
        in_specs=specs(False) + specs(True) + [const((3, 2 * W_G)), const((1, 2 * W_G)), const((1, LANES)),
                                               const((2, 2 * LANES, 8 * LANES))],
        out_specs=(
            pl.BlockSpec((1, c, W_G), lambda bi, i: (bi, i, 0)),
            pl.BlockSpec((1, c, W_G), lambda bi, i: (bi, nblk - 1 - i, 0)),
        ),
        scratch_shapes=[pltpu.VMEM((H_G, LANES, 2 * LANES), F32), pltpu.VMEM((ROW_ALIGN, LANES), F32)],
        compiler_params=_cparams(("parallel", "arbitrary")),
        name="mlstm",
    )(u3, u3, u3, u3, u3, u3, u3, u3, conv_w, conv_b, gate_bias, _mlstm_select_matrix())


def _rwkv_prep_kernel(u_ref, up_ref, un_ref, mu_ref, w0_ref, w2_ref, a0_ref, a2_ref, g2_ref, kks_ref, ka_ref, rk_ref,
                      r_ref, v_ref, kk_ref, g_ref, bonus_ref, lwf_ref, lwb_ref, kf_ref, kb_ref, bf_ref, bb_ref):
    i = pl.program_id(1)
    nblk = pl.num_programs(1)
    u = u_ref[0]
    prev_row = jnp.where(i == 0, 0.0, up_ref[0][ROW_ALIGN - 1:ROW_ALIGN, :])
    next_row = jnp.where(i == nblk - 1, 0.0, un_ref[0][0:1, :])
    prev, nxt = _shift_rows(u, prev_row, next_row)
    us = u + mu_ref[...] * (0.5 * (prev + nxt) - u)
    r = us[:, 0:W_G]
    k = us[:, W_G:2 * W_G]
    v = us[:, 2 * W_G:3 * W_G]
    xw = us[:, 3 * W_G:3 * W_G + 64]
    xa = us[:, 3 * W_G + 64:3 * W_G + 128]
    xg = us[:, 3 * W_G + 128:]
    bd = _head_mean_matrix(W_G)
    g = _mm(_sigmoid(xg), g2_ref[...])
    lw = jnp.tanh(xw)
    a_lr = _mm_x3(xa, a2_ref[...])
    kk = k * kks_ref[...]
    kk = kk * lax.rsqrt(_head_mean(kk * kk, bd) * HEAD_DIM + 1e-12)
    r_ref[0] = r.astype(r_ref.dtype)
    v_ref[0] = v.astype(v_ref.dtype)
    kk_ref[0] = kk.astype(kk_ref.dtype)
    g_ref[0] = g.astype(g_ref.dtype)
    bonus = jnp.zeros_like(r)
    for d, (lw_ref, k_ref, b_ref) in enumerate(((lwf_ref, kf_ref, bf_ref), (lwb_ref, kb_ref, bb_ref))):
        z = w0_ref[d:d + 1, :] + _mm_x3(lw, w2_ref[d])
        lw_ref[0] = -_sigmoid(z) * math.exp(-0.5)
        a = _sigmoid(a0_ref[d:d + 1, :] + a_lr)
        kd = k * (1.0 + (a - 1.0) * ka_ref[...])
        k_ref[0] = kd.astype(k_ref.dtype)
        b_ref[0] = (kk * a).astype(b_ref.dtype)
        bonus = bonus + _head_mean(r * kd * rk_ref[...], bd) * HEAD_DIM * v
    bonus_ref[0] = bonus.astype(bonus_ref.dtype)


def _rwkv_prep(u3, mu, w0, w2, a0, a2, g2, kks, ka, rk):
    b, t, _ = u3.shape
    tb = min(RWKV_BLOCK, t)
    nblk = t // tb
    rpb = tb // ROW_ALIGN
    n8 = t // ROW_ALIGN
    const = lambda shape: pl.BlockSpec(shape, lambda bi, i: (0,) * len(shape))
    out = lambda dtype: jax.ShapeDtypeStruct((b, t, W_G), dtype)
    ospec = pl.BlockSpec((1, tb, W_G), lambda bi, i: (bi, i, 0))
    return pl.pallas_call(
        _rwkv_prep_kernel,
        out_shape=(out(MIXER_DTYPE),) * 3 + (out(MIXER_DTYPE),) * 2 + (out(F32),) * 2 + (out(MIXER_DTYPE),) * 4,
        grid=(b, nblk),
        in_specs=[
            pl.BlockSpec((1, tb, 1024), lambda bi, i: (bi, i, 0)),
            pl.BlockSpec((1, ROW_ALIGN, 1024), lambda bi, i: (bi, jnp.maximum(i * rpb - 1, 0), 0)),
            pl.BlockSpec((1, ROW_ALIGN, 1024), lambda bi, i: (bi, jnp.minimum((i + 1) * rpb, n8 - 1), 0)),
            const((1, 1024)), const((2, W_G)), const((2, 64, W_G)), const((2, W_G)), const((64, W_G)),
            const((128, W_G)), const((1, W_G)), const((1, W_G)), const((1, W_G)),
        ],
        out_specs=(ospec,) * 11,
        compiler_params=_cparams(("parallel", "parallel")),
        name="rwkv_prep",
    )(u3, u3, u3, mu, w0, w2, a0, a2, g2, kks, ka, rk)


def _tri_inverse_all(lmats, n):
    r = _iota2((n, n), 0)
    c = _iota2((n, n), 1)
    eye = jnp.where(r == c, 1.0, 0.0)
    pair = (r // 2 == c // 2) & (r != c)
    invs = [eye + jnp.where(pair, lm, 0.0) for lm in lmats]
    s = 2
    while s < n:
        sel = (r // (2 * s) == c // (2 * s)) & (r // s != c // s)
        offs = [jnp.where(sel, -lm, 0.0) for lm in lmats]
        xs = [_mm(inv, off) for inv, off in zip(invs, offs)]
        invs = [inv - _mm(x, inv) for inv, x in zip(invs, xs)]
        s *= 2
    return invs


def _rwkv_tile_terms(r, k, v, kk, b, lw, reverse):
    c = r.shape[0]
    tt = _iota2((c, c), 0)
    ss = _iota2((c, c), 1)
    tri = jnp.where((tt <= ss) if reverse else (ss <= tt), 1.0, 0.0).astype(BF16)
    cum_in = _mm_r3(tri, lw)
    cum_all = jnp.sum(lw, axis=0, keepdims=True)
    e_neg = jnp.exp(-cum_in)
    e_end = jnp.exp(cum_all - cum_in)
    return dict(at=-kk * jnp.exp(cum_in - lw), rt=r * jnp.exp(cum_in), bt=b * e_neg, kt=k * e_neg,
                gb=b * e_end, gk=k * e_end, v=v, e_all=jnp.exp(cum_all))


def _rwkv_chunk_terms(tiles, reverses):
    c = RWKV_CHUNK
    tt = _iota2((c, c), 0)
    ss = _iota2((c, c), 1)
    heads = [(ti, h) for ti in range(len(tiles)) for h in range(H_G)]
    sl = lambda h: slice(h * HEAD_DIM, (h + 1) * HEAD_DIM)
    get = lambda name: [tiles[ti][name][:, sl(h)] for ti, h in heads]
    at, rt, bt, kt, gb, gk, v = (get(nm) for nm in ("at", "rt", "bt", "kt", "gb", "gk", "v"))
    strict = [(ss > tt) if reverses[ti] else (ss < tt) for ti, _ in heads]
    incl = [(ss >= tt) if reverses[ti] else (ss <= tt) for ti, _ in heads]
    ps = [_mm_nt(jnp.concatenate([a, r_], axis=0), jnp.concatenate([b_, k_], axis=0))
          for a, r_, b_, k_ in zip(at, rt, bt, kt)]
    l_ab = [jnp.where(m, p[:c, :c], 0.0) for m, p in zip(strict, ps)]
    l_ak = [jnp.where(m, p[:c, c:], 0.0) for m, p in zip(strict, ps)]
    m_r = [jnp.concatenate([jnp.where(m, p[c:, :c], 0.0), jnp.where(m, p[c:, c:], 0.0)], axis=1)
           for m, p in zip(incl, ps)]
    lakv = [_mm(l, v_) for l, v_ in zip(l_ak, v)]
    invs = _tri_inverse_all(l_ab, c)
    tw = [_mm(inv, jnp.concatenate([a, lv], axis=1)) for inv, a, lv in zip(invs, at, lakv)]
    zeros = jnp.zeros((c, HEAD_DIM), F32)
    mm2 = [_mm(m, jnp.concatenate([t_, jnp.concatenate([zeros, v_], axis=1)], axis=0))
           for m, t_, v_ in zip(m_r, tw, v)]
    r1 = [r_ + m[:, :HEAD_DIM] for r_, m in zip(rt, mm2)]
    y0 = [m[:, HEAD_DIM:] for m in mm2]
    twg = [_mm_tn(t_, g_) for t_, g_ in zip(tw, gb)]
    vgk = [_mm_tn(v_, g_) for v_, g_ in zip(v, gk)]
    mlow = [x[:HEAD_DIM] for x in twg]
    nadd = [x[HEAD_DIM:] + y for x, y in zip(twg, vgk)]
    e_all = [tiles[ti]["e_all"][:, sl(h)] for ti, h in heads]
    return r1, y0, mlow, nadd, e_all


def _rwkv_core_kernel(rf_ref, kf_ref, vf_ref, kkf_ref, bf_ref, lwf_ref, rb_ref, kb_ref, vb_ref, kkb_ref, bb_ref,
                      lwb_ref, yf_ref, yb_ref, s_scr, *, group):
    i = pl.program_id(1)
    c = RWKV_CHUNK
    nch = rf_ref.shape[1] // c
    dirs = ((rf_ref, kf_ref, vf_ref, kkf_ref, bf_ref, lwf_ref), (rb_ref, kb_ref, vb_ref, kkb_ref, bb_ref, lwb_ref))
    y_refs = (yf_ref, yb_ref)

    @pl.when(i == 0)
    def _():
        s_scr[...] = jnp.zeros_like(s_scr)

    def step(j, states):
        tiles, reverses, rows = [], [], []
        for q in range(group):
            for d in range(2):
                cj = j * group + q
                cj = cj if d == 0 else nch - 1 - cj
                rw = pl.ds(pl.multiple_of(cj * c, c), c)
                tiles.append(_rwkv_tile_terms(*(ref[0, rw, :].astype(F32) for ref in dirs[d]), d == 1))
                reverses.append(d == 1)
                rows.append(rw)
        r1, y0, mlow, nadd, e_all = _rwkv_chunk_terms(tiles, reverses)
        states = list(states)
        for q in range(group):
            ys = [[], []]
            for d in range(2):
                for h in range(H_G):
                    n = (q * 2 + d) * H_G + h
                    s = states[d * H_G + h]
                    ys[d].append(_mm_nt(r1[n], s) + y0[n])
                    states[d * H_G + h] = s * e_all[n] + _mm(s, mlow[n]) + nadd[n]
            for d in range(2):
                y_refs[d][0, rows[q * 2 + d], :] = jnp.concatenate(ys[d], axis=1).astype(y_refs[d].dtype)
        return tuple(states)

    init = tuple(s_scr[n] for n in range(2 * H_G))
    if nch == group:
        states = step(0, init)
    else:
        states = lax.fori_loop(0, nch // group, step, init)
    for n in range(2 * H_G):
        s_scr[n] = states[n]


def _rwkv_core(r, v, kk, lwf, lwb, kf, kb, bf, bb):
    b, t, _ = r.shape
    tb = min(RWKV_BLOCK, t)
    nblk = t // tb
    fwd = pl.BlockSpec((1, tb, W_G), lambda bi, i: (bi, i, 0))
    bwd = pl.BlockSpec((1, tb, W_G), lambda bi, i: (bi, nblk - 1 - i, 0))
    out = jax.ShapeDtypeStruct((b, t, W_G), MIXER_DTYPE)
    return pl.pallas_call(
        functools.partial(_rwkv_core_kernel, group=min(RWKV_GROUP, tb // RWKV_CHUNK)),
        out_shape=(out, out),
        grid=(b, nblk),
        in_specs=[fwd] * 6 + [bwd] * 6,
        out_specs=(fwd, bwd),
        scratch_shapes=[pltpu.VMEM((2 * H_G, HEAD_DIM, HEAD_DIM), F32)],
        compiler_params=_cparams(("parallel", "arbitrary")),
        name="rwkv_core",
    )(r, kf, v, kk, bf, lwf, r, kb, v, kk, bb, lwb)


def _group_norm(y, gain, eps, bd):
    mu = _head_mean(y, bd)
    d = y - mu
    var = _head_mean(d * d, bd)
    return d * lax.rsqrt(var + eps) * gain


N_MIX_REFS = 15


def _mix_residual(refs, tiles):
    x_refs = refs[:len(tiles)]
    (ryf_ref, ryb_ref, rbon_ref, rg_ref, at_ref, mhf_ref, mhb_ref, mo_ref, tof_ref, tob_ref,
     tg_ref, rln_ref, mln_ref, tln_ref, w_ref) = refs[len(tiles):]
    bd = _head_mean_matrix(W_G)
    f = lambda ref: ref[...].astype(F32)
    o_a = (_group_norm(f(ryf_ref) + f(ryb_ref), rln_ref[...], RWKV_GN_EPS, bd) + f(rbon_ref)) * f(rg_ref)
    o_c = _group_norm(f(mhf_ref) + f(mhb_ref), mln_ref[...], HEAD_NORM_EPS, bd) * _sigmoid(mo_ref[...])
    o_d = _group_norm(f(tof_ref) + f(tob_ref), tln_ref[...], HEAD_NORM_EPS, bd) * _silu(tg_ref[...])
    mix = jnp.concatenate([o_a, at_ref[...], o_c, o_d], axis=1).astype(BF16)
    return _part_tile(pl.program_id(0), x_refs, tiles) + jnp.dot(mix, w_ref[...], preferred_element_type=F32)


def _mix_call(kernel, x_parts, rw, at, ml, rt, u2, rln, mln, tln, w_out, tm, extra, extra_specs, out_shape, out_specs,
              scratch_shapes, name, lag=0):
    tiles = _part_tiles(x_parts, tm)
    nt = sum(tiles)
    row = lambda w: pl.BlockSpec((tm, w), lambda i: (jnp.minimum(i, nt - 1), 0))
    ucol = lambda off: pl.BlockSpec((tm, W_G), lambda i: (jnp.minimum(i, nt - 1), off // W_G))
    const = lambda shape: pl.BlockSpec(shape, lambda i: (0, 0))
    return pl.pallas_call(
        functools.partial(kernel, tiles=tiles),
        out_shape=out_shape,
        grid=(nt + lag,),
        in_specs=_part_specs(x_parts, tm, D_MODEL) + [row(W_G)] * 7 + [ucol(U_MLSTM + 3 * W_G)] + [row(W_G)] * 2
                 + [ucol(U_RET + 3 * W_G)] + [const((1, W_G))] * 3
                 + [pl.BlockSpec((D_MODEL, D_MODEL), lambda i: (0, 0), pipeline_mode=pl.Buffered(1))] + extra_specs,
        out_specs=out_specs,
        scratch_shapes=scratch_shapes,
        compiler_params=_cparams(("arbitrary",)),
        name=name,
    )(*x_parts, *rw, at, *ml, u2, *rt, u2, rln, mln, tln, w_out, *extra)


def _mix_ffn_kernel(*refs, tiles):
    n_in = len(tiles) + N_MIX_REFS
    g_ref, wg_ref, wu_ref, wd_ref, o_ref, x_scr = refs[n_in:]

    @pl.when(pl.program_id(0) == 0)
    def _():
        x_scr[...] = jnp.zeros_like(x_scr)

    x = x_scr[...]
    x_next = _mix_residual(refs[:n_in], tiles)
    h = _rms_norm_rows(x, g_ref[...]).astype(BF16)
    a = jnp.dot(h, wg_ref[...], preferred_element_type=F32)
    b = jnp.dot(h, wu_ref[...], preferred_element_type=F32)
    z = (_silu(a) * b).astype(BF16)
    o_ref[...] = x + jnp.dot(z, wd_ref[...], preferred_element_type=F32)
    x_scr[...] = x_next


def _mix_ffn(mix_args, gain, wg, wu, wd, tm):
    n = sum(_part_tiles(mix_args[0], tm)) * tm
    res = lambda shape: pl.BlockSpec(shape, lambda i: (0, 0), pipeline_mode=pl.Buffered(1))
    return _mix_call(_mix_ffn_kernel, *mix_args, tm, (gain, wg, wu, wd),
                     [pl.BlockSpec((1, D_MODEL), lambda i: (0, 0)), res((D_MODEL, D_FF)), res((D_MODEL, D_FF)),
                      res((D_FF, D_MODEL))],
                     jax.ShapeDtypeStruct((n, D_MODEL), F32),
                     pl.BlockSpec((tm, D_MODEL), lambda i: (jnp.maximum(i - 1, 0), 0)),
                     [pltpu.VMEM((tm, D_MODEL), F32)], "mix_ffn", lag=1)


def _mix_router_kernel(*refs, tiles):
    n_in = len(tiles) + N_MIX_REFS
    g_ref, wr_ref, x_ref, h_ref, gate_ref, idx_ref, cnt_ref, cnt_scr = refs[n_in:]

    @pl.when(pl.program_id(0) == 0)
    def _():
        cnt_scr[...] = jnp.zeros_like(cnt_scr)

    x = _mix_residual(refs[:n_in], tiles)
    x_ref[...] = x
    h = _rms_norm_rows(x, g_ref[...])
    h_ref[...] = h.astype(BF16)
    logits = _mm_x3(h, wr_ref[...])
    tm = logits.shape[0]
    lt = logits.T[:N_EXPERTS, :]
    sub = _iota2(lt.shape, 0)
    e = jnp.exp(lt - jnp.max(lt, axis=0, keepdims=True))
    p = e / jnp.sum(e, axis=0, keepdims=True)
    m1 = jnp.max(p, axis=0, keepdims=True)
    i1 = jnp.min(jnp.where(p == m1, sub, N_EXPERTS), axis=0, keepdims=True)
    p2 = jnp.where(sub == i1, -1.0, p)
    m2 = jnp.max(p2, axis=0, keepdims=True)
    i2 = jnp.min(jnp.where(p2 == m2, sub, N_EXPERTS), axis=0, keepdims=True)
    tot = m1 + m2
    gates_t = jnp.where(sub == 0, m1 / tot, jnp.where(sub == 1, m2 / tot, 0.0))
    gate_ref[...] = jnp.concatenate([gates_t, jnp.zeros((LANES - N_EXPERTS, tm), F32)], axis=0).T
    chosen = jnp.where((sub == i1) | (sub == i2), 1.0, 0.0)
    triu = jnp.where(_iota2((tm, tm), 0) <= _iota2((tm, tm), 1), 1.0, 0.0).astype(BF16)
    incl = jnp.dot(chosen.astype(BF16), triu, preferred_element_type=F32)
    rank = jnp.tile(cnt_scr[...], (1, tm // LANES)) + incl - chosen
    r1 = jnp.sum(jnp.where(sub == i1, rank, 0.0), axis=0, keepdims=True)
    r2 = jnp.sum(jnp.where(sub == i2, rank, 0.0), axis=0, keepdims=True)
    idx_ref[...] = jnp.where(sub == 0, i1.astype(F32), jnp.where(
        sub == 1, i2.astype(F32), jnp.where(sub == 2, r1, jnp.where(sub == 3, r2, 0.0)))).astype(jnp.int32)
    cnt_scr[...] = cnt_scr[...] + jnp.sum(chosen, axis=1, keepdims=True)
    cnt_ref[...] = cnt_scr[...].astype(jnp.int32)


def _mix_router(mix_args, gain, wr_pad, tm):
    n = sum(_part_tiles(mix_args[0], tm)) * tm
    return _mix_call(
        _mix_router_kernel, *mix_args, tm, (gain, wr_pad),
        [pl.BlockSpec((1, D_MODEL), lambda i: (0, 0)), pl.BlockSpec((D_MODEL, LANES), lambda i: (0, 0))],
        (jax.ShapeDtypeStruct((n, D_MODEL), F32), jax.ShapeDtypeStruct((n, D_MODEL), BF16),
         jax.ShapeDtypeStruct((n, LANES), F32), jax.ShapeDtypeStruct((ROW_ALIGN, n), jnp.int32),
         jax.ShapeDtypeStruct((ROW_ALIGN, LANES), jnp.int32)),
        (pl.BlockSpec((tm, D_MODEL), lambda i: (i, 0)), pl.BlockSpec((tm, D_MODEL), lambda i: (i, 0)),
         pl.BlockSpec((tm, LANES), lambda i: (i, 0)), pl.BlockSpec((ROW_ALIGN, tm), lambda i: (0, i)),
         pl.BlockSpec((ROW_ALIGN, LANES), lambda i: (0, 0))),
        [pltpu.VMEM((N_EXPERTS, LANES), F32)], "mix_router")


def _expert_ffn_kernel(te_ref, nv_ref, xs_ref, wg_ref, wu_ref, wd_ref, o_ref):
    i = pl.program_id(0)

    @pl.when(i < nv_ref[0])
    def _():
        h = xs_ref[...]
        fw = D_FF // EXPERT_FF_SPLIT
        acc = None
        for f in range(EXPERT_FF_SPLIT):
            cols = slice(f * fw, (f + 1) * fw)
            a = jnp.dot(h, wg_ref[0, :, cols], preferred_element_type=F32)
            b = jnp.dot(h, wu_ref[0, :, cols], preferred_element_type=F32)
            z = (_silu(a) * b).astype(BF16)
            y = jnp.dot(z, wd_ref[0, cols, :], preferred_element_type=F32)
            acc = y if acc is None else acc + y
        o_ref[...] = acc.astype(o_ref.dtype)

    @pl.when(i >= nv_ref[0])
    def _():
        o_ref[...] = jnp.zeros_like(o_ref)


def _expert_ffn(xs, tile_expert, n_valid, wg, wu, wd, tm):
    rows = xs.shape[0]
    wspec = lambda shape: pl.BlockSpec((1,) + shape, lambda i, te, nv: (te[i], 0, 0))
    return pl.pallas_call(
        _expert_ffn_kernel,
        out_shape=jax.ShapeDtypeStruct((rows, D_MODEL), BF16),
        grid_spec=pltpu.PrefetchScalarGridSpec(
            num_scalar_prefetch=2,
            grid=(rows // tm,),
            in_specs=[pl.BlockSpec((tm, D_MODEL), lambda i, te, nv: (i, 0)),
                      wspec((D_MODEL, D_FF)), wspec((D_MODEL, D_FF)), wspec((D_FF, D_MODEL))],
            out_specs=pl.BlockSpec((tm, D_MODEL), lambda i, te, nv: (i, 0)),
        ),
        compiler_params=_cparams(("arbitrary",)),
        name="expert_ffn",
    )(tile_expert, n_valid, xs, wg, wu, wd)


def _moe_combine_kernel(x_ref, y1_ref, y2_ref, gate_ref, nf_ref, *o_refs, tiles):
    i = pl.program_id(0)
    g = gate_ref[...]
    y = x_ref[...] + g[:, 0:1] * y1_ref[...].astype(F32) + g[:, 1:2] * y2_ref[...].astype(F32)
    out = _rms_norm_rows(y, nf_ref[...])
    start = 0
    for o_ref, nt in zip(o_refs, tiles):
        @pl.when((i >= start) & (i < start + nt))
        def _(o_ref=o_ref):
            o_ref[...] = out
        start += nt


def _moe_combine(x2, y1, y2, gates, norm_final, tm, part_rows):
    n = x2.shape[0]
    row = lambda w: pl.BlockSpec((tm, w), lambda i: (i, 0))
    outs = tuple(jax.ShapeDtypeStruct((r, D_MODEL), F32) for r in part_rows)
    return pl.pallas_call(
        functools.partial(_moe_combine_kernel, tiles=_part_tiles(outs, tm)),
        out_shape=outs,
        grid=(n // tm,),
        in_specs=[row(D_MODEL), row(D_MODEL), row(D_MODEL), row(LANES), pl.BlockSpec((1, D_MODEL), lambda i: (0, 0))],
        out_specs=tuple(_part_specs(outs, tm, D_MODEL)),
        compiler_params=_cparams(("arbitrary",)),
        name="moe_combine",
    )(x2, y1, y2, gates, norm_final)


def _moe(x2, h, gates, idx, counts, wg, wu, wd, norm_final, tm, part_rows):
    n = x2.shape[0]
    tme = EXPERT_TILE
    n_tiles = (2 * n + N_EXPERTS * (tme - 1)) // tme + 1
    e1, e2, r1, r2 = idx[0], idx[1], idx[2], idx[3]
    cnt = counts[:N_EXPERTS, 0]
    padded = (cnt + tme - 1) // tme * tme
    group_end = jnp.cumsum(padded)
    group_off = group_end - padded
    dense_off = jnp.cumsum(cnt) - cnt
    lookup = lambda table, e: sum(jnp.where(e == k, table[k], 0) for k in range(N_EXPERTS))
    slot1 = lookup(group_off, e1) + r1
    slot2 = lookup(group_off, e2) + r2
    tok = jnp.arange(n, dtype=jnp.int32)
    sorted_tok = jnp.sort(jnp.concatenate([e1 * n + tok, e2 * n + tok])) % n
    tile_start = jnp.arange(n_tiles, dtype=jnp.int32) * tme
    tile_expert = jnp.minimum(jnp.searchsorted(group_end, tile_start, side='right'), N_EXPERTS - 1).astype(jnp.int32)
    n_valid = (group_end[-1:] // tme).astype(jnp.int32)
    rank = (tile_start - group_off[tile_expert])[:, None] + jnp.arange(tme, dtype=jnp.int32)[None, :]
    dense = jnp.clip(dense_off[tile_expert][:, None] + rank, 0, 2 * n - 1)
    take = functools.partial(jnp.take, axis=0, mode='clip')
    spread = (tile_start[:, None] + jnp.arange(tme, dtype=jnp.int32)[None, :]) % n
    src = jnp.where(rank < cnt[tile_expert][:, None], take(sorted_tok, dense.reshape(-1)).reshape(dense.shape), spread)
    xs = take(h, src.reshape(-1))
    ys = _expert_ffn(xs, tile_expert, n_valid, wg, wu, wd, tme)
    y1 = take(ys, slot1)
    y2 = take(ys, slot2)
    return _moe_combine(x2, y1, y2, gates, norm_final, tm, part_rows)


def _rope_tables(t):
    rows = t // GRID_W
    pos = np.arange(rows * GRID_W)
    row = (pos // GRID_W).astype(np.float32)
    col = (pos % GRID_W).astype(np.float32)
    nf = HEAD_DIM // 4
    inv = jnp.asarray(ROPE_THETA, F32) ** (-jnp.arange(nf, dtype=F32) / nf)
    ar = jnp.asarray(row)[:, None] * inv
    ac = jnp.asarray(col)[:, None] * inv
    cos = jnp.concatenate([jnp.cos(ar), jnp.cos(ar), jnp.cos(ac), jnp.cos(ac)], axis=-1)
    sin = jnp.concatenate([-jnp.sin(ar), jnp.sin(ar), -jnp.sin(ac), jnp.sin(ac)], axis=-1)
    return jnp.tile(cos, (1, H_G)), jnp.tile(sin, (1, H_G))


def _pad_w_in(w):
    a, b_, c, d = 1024, 512, 1040, 1024
    w_a, w_b, w_c, w_d = w[:, :a], w[:, a:a + b_], w[:, a + b_:a + b_ + c], w[:, a + b_ + c:]
    gates = jnp.pad(w_c[:, 1024:], ((0, 0), (0, LANES - 16)))
    return jnp.concatenate([w_a, w_c[:, :1024], w_d, w_b, gates], axis=1).astype(BF16)


def _row(v):
    return v.reshape(1, -1).astype(F32)


def _trunk(xs, p):
    t = xs[0].shape[1]
    part_rows = tuple(x.shape[0] * t for x in xs)
    b = sum(x.shape[0] for x in xs)
    n = b * t
    tm = 256
    cos, sin = _rope_tables(t)
    x_parts = tuple(x.reshape(-1, D_MODEL) for x in xs)
    depth = p['w_in'].shape[0]
    for l in range(depth):
        tm_in = 2 * tm if t % (2 * tm) == 0 else tm
        u2 = _in_proj(x_parts, _row(p['norm_mix'][l]), _pad_w_in(p['w_in'][l]), cos, sin,
                      _row(jnp.tile(p['attn_q_norm'][l], H_G)), _row(jnp.tile(p['attn_k_norm'][l], KV_ATTN)), tm_in)
        u3 = u2.reshape(b, t, U_COLS)
        (r, v, kk, g, bonus, lwf, lwb, kf, kb, bf, bb) = _rwkv_prep(
            u3, _row(p['rwkv_mu'][l]), p['rwkv_w0'][l], p['rwkv_w2'][l], p['rwkv_a0'][l], p['rwkv_a2'][l],
            p['rwkv_g2'][l], _row(p['rwkv_kk'][l]), _row(p['rwkv_ka'][l]), _row(p['rwkv_rk'][l]))
        yf, yb = _rwkv_core(r, v, kk, lwf, lwb, kf, kb, bf, bb)
        at = _attention(u3)
        gate_bias = jnp.pad(jnp.concatenate([p['mlstm_i_bias'][l].reshape(-1), p['mlstm_f_bias'][l].reshape(-1)]),
                            (0, LANES - 4 * H_G))
        hf, hb = _mlstm(u3, p['mlstm_conv_w'][l], _row(p['mlstm_conv_b'][l]), _row(gate_bias))
        of, ob = _retention(u3)
        flat = lambda z: z.reshape(n, W_G)
        mix_args = (x_parts, tuple(map(flat, (yf, yb, bonus, g))), flat(at), tuple(map(flat, (hf, hb))),
                    tuple(map(flat, (of, ob))), u2, _row(p['rwkv_ln'][l]), _row(p['mlstm_ln'][l]),
                    _row(p['ret_ln'][l]), p['w_out'][l].astype(BF16))
        j = l // 2
        if l % 2 == 0:
            x2 = _mix_ffn(mix_args, _row(p['norm_ffn'][l]), p['ffn_w_gate'][j].astype(BF16),
                          p['ffn_w_up'][j].astype(BF16), p['ffn_w_down'][j].astype(BF16), tm)
            x_parts = (x2,)
            if l == depth - 1:
                raise NotImplementedError("final norm after a dense FFN layer")
        else:
            wr = jnp.pad(p['moe_router'][j], ((0, 0), (0, LANES - N_EXPERTS)))
            x2, h, gates, idx, counts = _mix_router(mix_args, _row(p['norm_ffn'][l]), wr, tm)
            if l != depth - 1:
                raise NotImplementedError("expert layer that is not the last layer")
            outs = _moe(x2, h, gates, idx, counts, p['moe_w_gate'][j].astype(BF16), p['moe_w_up'][j].astype(BF16),
                        p['moe_w_down'][j].astype(BF16), _row(p['norm_final']), tm, part_rows)
    return tuple(o.reshape(x.shape) for o, x in zip(outs, xs))


def kernel(x_prompt, x_sample, norm_mix, norm_ffn, norm_final, w_in, w_out, rwkv_mu, rwkv_w0, rwkv_w2,
           rwkv_a0, rwkv_a2, rwkv_g2, rwkv_kk, rwkv_ka, rwkv_rk, rwkv_ln, attn_q_norm, attn_k_norm,
           mlstm_conv_w, mlstm_conv_b, mlstm_i_bias, mlstm_f_bias, mlstm_ln, ret_ln, ffn_w_gate, ffn_w_up,
           ffn_w_down, moe_router, moe_w_gate, moe_w_up, moe_w_down):
    p = dict(norm_mix=norm_mix, norm_ffn=norm_ffn, norm_final=norm_final, w_in=w_in, w_out=w_out,
             rwkv_mu=rwkv_mu, rwkv_w0=rwkv_w0, rwkv_w2=rwkv_w2, rwkv_a0=rwkv_a0, rwkv_a2=rwkv_a2,
             rwkv_g2=rwkv_g2, rwkv_kk=rwkv_kk, rwkv_ka=rwkv_ka, rwkv_rk=rwkv_rk, rwkv_ln=rwkv_ln,
             attn_q_norm=attn_q_norm, attn_k_norm=attn_k_norm, mlstm_conv_w=mlstm_conv_w,
             mlstm_conv_b=mlstm_conv_b, mlstm_i_bias=mlstm_i_bias, mlstm_f_bias=mlstm_f_bias,
             mlstm_ln=mlstm_ln, ret_ln=ret_ln, ffn_w_gate=ffn_w_gate, ffn_w_up=ffn_w_up,
             ffn_w_down=ffn_w_down, moe_router=moe_router, moe_w_gate=moe_w_gate, moe_w_up=moe_w_up,
             moe_w_down=moe_w_down)
    return _trunk((x_prompt, x_sample), p)
```
